```python
import math
import jax, jax.numpy as jnp
from jax import lax
import numpy as np

D_MODEL = 1024
BATCH = 8
SEQ = 2048
DEPTH = 2
DEC_BATCH = 128
DEC_SEQ = 8
PAST_LEN = 16384
PAGE_SIZE = 128

N_AB = (DEPTH + 1) // 2
N_CL = DEPTH // 2
D_A = D_MODEL // 2
S5_GROUP = 16
G_A = D_A // S5_GROUP
P_STATE = 64
D_B = D_MODEL // 2
H_B = 4
DK_B = 128
DV_B = D_B // H_B
CONV_W = 4
CHUNK = 64
AB_SPLITS = (D_A, D_A + H_B * DK_B, D_A + 2 * H_B * DK_B, D_A + 2 * H_B * DK_B + D_B,
             D_A + 2 * H_B * DK_B + D_B + H_B, D_A + 2 * H_B * DK_B + D_B + 2 * H_B)
D_IN_AB = AB_SPLITS[-1] + D_B
N_HEAD_C = 64
H_C = D_MODEL // N_HEAD_C
LORA_DECAY = 64
LORA_AAA = 64
LORA_GATE = 160
RWKV_GN_EPS = 64e-5
D_FF = ((-(-8 * D_MODEL // 3) + 255) // 256) * 256
NORM_EPS = 1e-6

kernel_name = "hybrid_s5_gdn_rwkv7_decode_step"


def rmsnorm(x, w):
    xf = x.astype(jnp.float32)
    y = xf * lax.rsqrt(jnp.mean(xf * xf, axis=-1, keepdims=True) + NORM_EPS)
    return (y * w.astype(jnp.float32)).astype(x.dtype)


def l2norm(x):
    xf = x.astype(jnp.float32)
    return xf * lax.rsqrt(jnp.sum(xf * xf, axis=-1, keepdims=True) + NORM_EPS)


def swiglu(h, w_gate, w_up, w_down):
    return (jax.nn.silu(h @ w_gate) * (h @ w_up)) @ w_down


def s5_mixer(u, h0_re, h0_im, lam_re, lam_im, log_step, b_re, b_im, c_re, c_im, d_skip, w_glu):
    bsz, seq = u.shape[0], u.shape[1]
    f32 = jnp.float32
    uf = u.astype(f32).reshape(bsz, seq, G_A, S5_GROUP)
    lr, li = lam_re.astype(f32), lam_im.astype(f32)
    dt = jnp.exp(log_step.astype(f32))[:, None]
    mag = jnp.exp(lr * dt)
    ab_re, ab_im = mag * jnp.cos(li * dt), mag * jnp.sin(li * dt)
    den = lr * lr + li * li
    nr, ni = ab_re - 1.0, ab_im
    cr, ci = (nr * lr + ni * li) / den, (ni * lr - nr * li) / den
    b_re, b_im = b_re.astype(f32), b_im.astype(f32)
    bb_re = cr[..., None] * b_re - ci[..., None] * b_im
    bb_im = cr[..., None] * b_im + ci[..., None] * b_re
    bu_re = jnp.einsum("gpc,blgc->blgp", bb_re, uf)
    bu_im = jnp.einsum("gpc,blgc->blgp", bb_im, uf)
    h0r, h0i = h0_re.astype(f32), h0_im.astype(f32)
    bu_re = bu_re.at[:, 0].add(ab_re * h0r - ab_im * h0i)
    bu_im = bu_im.at[:, 0].add(ab_re * h0i + ab_im * h0r)
    a_re = jnp.broadcast_to(ab_re, bu_re.shape)
    a_im = jnp.broadcast_to(ab_im, bu_im.shape)

    def combine(e1, e2):
        a1r, a1i, b1r, b1i = e1
        a2r, a2i, b2r, b2i = e2
        return (a1r * a2r - a1i * a2i, a1r * a2i + a1i * a2r,
                a2r * b1r - a2i * b1i + b2r, a2r * b1i + a2i * b1r + b2i)

    _, _, xr, xi = lax.associative_scan(combine, (a_re, a_im, bu_re, bu_im), axis=1)
    y = (jnp.einsum("gcp,blgp->blgc", c_re.astype(f32), xr)
         - jnp.einsum("gcp,blgp->blgc", c_im.astype(f32), xi)
         + d_skip.astype(f32).reshape(G_A, S5_GROUP) * uf).reshape(bsz, seq, D_A)
    yg = jax.nn.gelu(y)
    out = yg * jax.nn.sigmoid(yg @ w_glu.astype(f32))
    return out.astype(u.dtype), xr[:, -1], xi[:, -1]


def causal_conv_silu(x, buf, w):
    seq = x.shape[1]
    xp = jnp.concatenate([buf.astype(x.dtype), x], axis=1)
    y = sum(xp[:, j:j + seq] * w[j] for j in range(CONV_W))
    return jax.nn.silu(y), xp[:, -(CONV_W - 1):]


def gated_delta_chunked(q, k, v, g, beta, s0):
    bsz, seq, nh, dk = q.shape
    dv = v.shape[-1]
    n_chunks = -(-seq // CHUNK)
    pad = n_chunks * CHUNK - seq

    def blocks(t):
        t = t.astype(jnp.float32)
        t = jnp.pad(t, [(0, 0), (0, pad)] + [(0, 0)] * (t.ndim - 2))
        t = t.reshape((bsz, n_chunks, CHUNK) + t.shape[2:])
        return jnp.swapaxes(jnp.moveaxis(t, 3, 2), 0, 1)

    qb = blocks(q) * (dk ** -0.5)
    kb, vb, gb, bb = blocks(k), blocks(v), blocks(g), blocks(beta)
    gc = jnp.cumsum(gb, axis=-1)
    idx = jnp.arange(CHUNK)
    causal = idx[:, None] >= idx[None, :]
    strict = idx[:, None] > idx[None, :]
    decay = jnp.exp(jnp.where(causal, gc[..., :, None] - gc[..., None, :], -jnp.inf))
    k_beta = kb * bb[..., None]
    v_beta = vb * bb[..., None]
    lmat = jnp.where(strict, jnp.einsum("nbhid,nbhjd->nbhij", k_beta, kb) * decay, 0.0)
    eye = jnp.eye(CHUNK, dtype=jnp.float32)
    tmat = lax.linalg.triangular_solve(eye + lmat, jnp.broadcast_to(eye, lmat.shape),
                                       left_side=True, lower=True)
    u_c = tmat @ v_beta
    w_c = tmat @ (k_beta * jnp.exp(gc)[..., None])

    def step(state, inp):
        qc, kc, uc, wc, gcc, dc = inp
        attn = jnp.einsum("bhid,bhjd->bhij", qc, kc) * dc
        v_new = uc - jnp.einsum("bhcd,bhdv->bhcv", wc, state)
        o = (jnp.einsum("bhcd,bhdv->bhcv", qc * jnp.exp(gcc)[..., None], state)
             + jnp.einsum("bhij,bhjv->bhiv", attn, v_new))
        g_last = gcc[..., -1]
        state = (state * jnp.exp(g_last)[..., None, None]
                 + jnp.einsum("bhcd,bhcv->bhdv", kc * jnp.exp(g_last[..., None] - gcc)[..., None], v_new))
        return state, o

    s_fin, o = lax.scan(step, s0.astype(jnp.float32), (qb, kb, u_c, w_c, gc, decay))
    o = jnp.transpose(o, (1, 0, 3, 2, 4)).reshape(bsz, n_chunks * CHUNK, nh, dv)[:, :seq]
    return o, s_fin


def gdn_mixer(q, k, v, b_raw, a_raw, z, s0, conv0, conv_w, a_log, dt_bias, norm_w):
    bsz, seq = q.shape[0], q.shape[1]
    qkv, conv_new = causal_conv_silu(jnp.concatenate([q, k, v], axis=-1), conv0, conv_w)
    q, k, v = jnp.split(qkv, (H_B * DK_B, 2 * H_B * DK_B), axis=-1)
    q = l2norm(q.reshape(bsz, seq, H_B, DK_B))
    k = l2norm(k.reshape(bsz, seq, H_B, DK_B))
    v = v.reshape(bsz, seq, H_B, DV_B)
    beta = jax.nn.sigmoid(b_raw.astype(jnp.float32))
    g = -jnp.exp(a_log.astype(jnp.float32)) * jax.nn.softplus(
        a_raw.astype(jnp.float32) + dt_bias.astype(jnp.float32))
    o, s_fin = gated_delta_chunked(q, k, v, g, beta, s0)
    o = rmsnorm(o, norm_w) * jax.nn.silu(z.astype(jnp.float32).reshape(bsz, seq, H_B, DV_B))
    return o.reshape(bsz, seq, D_B).astype(q.dtype if q.dtype != jnp.float32 else z.dtype), s_fin, conv_new


def rwkv7_scan(r, decay, k, v, a, b, s0):
    def step(state, inp):
        r_t, w_t, k_t, v_t, a_t, b_t = inp
        sa = jnp.einsum("bhvk,bhk->bhv", state, a_t)
        state = (state * w_t[:, :, None, :] + sa[..., None] * b_t[:, :, None, :]
                 + v_t[..., None] * k_t[:, :, None, :])
        return state, jnp.einsum("bhvk,bhk->bhv", state, r_t)

    xs = tuple(jnp.swapaxes(t.astype(jnp.float32), 0, 1) for t in (r, decay, k, v, a, b))
    s_fin, ys = lax.scan(step, s0.astype(jnp.float32), xs)
    return jnp.swapaxes(ys, 0, 1), s_fin


def rwkv7_mixer(x, shift0, s0, maa, w_r, w_k, w_v, w_o, w0, w1, w2, a0, a1, a2, g1, g2,
                k_k, k_a, r_k, ln_w, ln_b):
    bsz, seq, _ = x.shape
    x_prev = jnp.concatenate([shift0[:, None].astype(x.dtype), x[:, :-1]], axis=1)
    xx = x_prev - x
    xr, xw, xk, xv, xa, xg = (x + xx * maa[j] for j in range(6))
    r = xr @ w_r
    k = xk @ w_k
    v = xv @ w_v
    w = -jax.nn.softplus(-(w0 + jnp.tanh(xw @ w1) @ w2).astype(jnp.float32)) - 0.5
    a = jax.nn.sigmoid((a0 + (xa @ a1) @ a2).astype(jnp.float32))
    g = jax.nn.sigmoid(xg @ g1) @ g2
    heads = lambda t: t.reshape(bsz, seq, H_C, N_HEAD_C)
    kk = l2norm(heads(k * k_k))
    k = k.astype(jnp.float32) * (1.0 + (a - 1.0) * k_a.astype(jnp.float32))
    decay = jnp.exp(-jnp.exp(w))
    rh, kh, vh, ah = heads(r).astype(jnp.float32), heads(k), heads(v).astype(jnp.float32), heads(a)
    y, s_fin = rwkv7_scan(rh, heads(decay), kh, vh, -kk, kk * ah, s0)
    mu = jnp.mean(y, axis=-1, keepdims=True)
    var = jnp.mean(jnp.square(y - mu), axis=-1, keepdims=True)
    y = (y - mu) * lax.rsqrt(var + RWKV_GN_EPS)
    y = y * ln_w.astype(jnp.float32).reshape(H_C, N_HEAD_C) + ln_b.astype(jnp.float32).reshape(H_C, N_HEAD_C)
    y = y + jnp.sum(rh * kh * r_k.astype(jnp.float32), axis=-1, keepdims=True) * vh
    out = (y.reshape(bsz, seq, D_MODEL) * g.astype(jnp.float32)).astype(x.dtype) @ w_o
    return out, s_fin, x[:, -1]


def trunk(x, s5_re0, s5_im0, gdn_s0, gdn_conv0, rw_s0, rw_shift0, prm):
    s5_re_n, s5_im_n, gdn_s_n, gdn_conv_n, rw_s_n, rw_shift_n = [], [], [], [], [], []
    for layer in range(DEPTH):
        i = layer // 2
        h = rmsnorm(x, prm["norm_mix"][layer])
        if layer % 2 == 0:
            proj = h @ prm["ab_w_in"][i]
            u_a, q, k, v, b_raw, a_raw, z = jnp.split(proj, AB_SPLITS, axis=-1)
            y_a, hr, hi = s5_mixer(u_a, s5_re0[i], s5_im0[i], prm["s5_lambda_re"][i], prm["s5_lambda_im"][i],
                                   prm["s5_log_step"][i], prm["s5_B_re"][i], prm["s5_B_im"][i],
                                   prm["s5_C_re"][i], prm["s5_C_im"][i], prm["s5_D"][i], prm["s5_w_glu"][i])
            y_b, s_b, conv_b = gdn_mixer(q, k, v, b_raw, a_raw, z, gdn_s0[i], gdn_conv0[i], prm["gdn_conv_w"][i],
                                         prm["gdn_A_log"][i], prm["gdn_dt_bias"][i], prm["gdn_norm_w"][i])
            mix = jnp.concatenate([y_a, y_b.astype(y_a.dtype)], axis=-1) @ prm["ab_w_out"][i]
            s5_re_n.append(hr)
            s5_im_n.append(hi)
            gdn_s_n.append(s_b)
            gdn_conv_n.append(conv_b)
        else:
            mix, s_c, shift_c = rwkv7_mixer(
                h, rw_shift0[i], rw_s0[i], prm["rw_maa"][i], prm["rw_w_r"][i], prm["rw_w_k"][i], prm["rw_w_v"][i],
                prm["rw_w_o"][i], prm["rw_w0"][i], prm["rw_w1"][i], prm["rw_w2"][i], prm["rw_a0"][i], prm["rw_a1"][i],
                prm["rw_a2"][i], prm["rw_g1"][i], prm["rw_g2"][i], prm["rw_k_k"][i], prm["rw_k_a"][i],
                prm["rw_r_k"][i], prm["rw_ln_w"][i], prm["rw_ln_b"][i])
            rw_s_n.append(s_c)
            rw_shift_n.append(shift_c)
        x = x + mix.astype(x.dtype)
        h = rmsnorm(x, prm["norm_ffn"][layer])
        x = x + swiglu(h, prm["ffn_w_gate"][layer], prm["ffn_w_up"][layer], prm["ffn_w_down"][layer]).astype(x.dtype)
    y = rmsnorm(x, prm["norm_final"])
    return (y, jnp.stack(s5_re_n), jnp.stack(s5_im_n), jnp.stack(gdn_s_n), jnp.stack(gdn_conv_n),
            jnp.stack(rw_s_n), jnp.stack(rw_shift_n))


def setup_inputs(seed: int = 0) -> dict:
    key = jax.random.key(seed)
    ks = jax.random.split(key, 64)
    counter = [0]

    def nxt():
        counter[0] += 1
        return ks[counter[0] - 1]

    def nrm(shape, scale=1.0):
        return scale * jax.random.normal(nxt(), shape, jnp.float32)

    def unif(shape, lo, hi):
        return jax.random.uniform(nxt(), shape, jnp.float32, lo, hi)

    D = D_MODEL
    inp = {}
    inp["x_prompt"] = nrm((BATCH, SEQ, D))
    inp["x_sample"] = nrm((DEC_BATCH, DEC_SEQ, D))
    inp["state_s5_re"] = nrm((N_AB, DEC_BATCH, G_A, P_STATE), 0.3)
    inp["state_s5_im"] = nrm((N_AB, DEC_BATCH, G_A, P_STATE), 0.3)
    inp["state_gdn"] = nrm((N_AB, DEC_BATCH, H_B, DK_B, DV_B), 0.1)
    inp["state_gdn_conv"] = nrm((N_AB, DEC_BATCH, CONV_W - 1, 2 * H_B * DK_B + D_B))
    inp["state_rwkv"] = nrm((N_CL, DEC_BATCH, H_C, N_HEAD_C, N_HEAD_C), 0.3)
    inp["state_rwkv_shift"] = nrm((N_CL, DEC_BATCH, D))
    inp["norm_mix"] = 1.0 + nrm((DEPTH, D), 0.02)
    inp["norm_ffn"] = 1.0 + nrm((DEPTH, D), 0.02)
    inp["norm_final"] = 1.0 + nrm((D,), 0.02)
    inp["ffn_w_gate"] = nrm((DEPTH, D, D_FF), D ** -0.5)
    inp["ffn_w_up"] = nrm((DEPTH, D, D_FF), D ** -0.5)
    inp["ffn_w_down"] = nrm((DEPTH, D_FF, D), D_FF ** -0.5)
    inp["ab_w_in"] = nrm((N_AB, D, D_IN_AB), D ** -0.5)
    inp["ab_w_out"] = nrm((N_AB, D_A + D_B, D), (D_A + D_B) ** -0.5)
    inp["s5_lambda_re"] = -0.5 * jnp.exp(nrm((N_AB, G_A, P_STATE), 0.05))
    inp["s5_lambda_im"] = jnp.pi * jnp.arange(P_STATE, dtype=jnp.float32) + nrm((N_AB, G_A, P_STATE), 0.01)
    inp["s5_log_step"] = unif((N_AB, G_A), math.log(1e-3), math.log(1e-1))
    inp["s5_B_re"] = nrm((N_AB, G_A, P_STATE, S5_GROUP), (2 * S5_GROUP) ** -0.5)
    inp["s5_B_im"] = nrm((N_AB, G_A, P_STATE, S5_GROUP), (2 * S5_GROUP) ** -0.5)
    inp["s5_C_re"] = nrm((N_AB, G_A, S5_GROUP, P_STATE), (2 * P_STATE) ** -0.5)
    inp["s5_C_im"] = nrm((N_AB, G_A, S5_GROUP, P_STATE), (2 * P_STATE) ** -0.5)
    inp["s5_D"] = nrm((N_AB, D_A))
    inp["s5_w_glu"] = nrm((N_AB, D_A, D_A), D_A ** -0.5)
    inp["gdn_conv_w"] = nrm((N_AB, CONV_W, 2 * H_B * DK_B + D_B), 0.5)
    inp["gdn_A_log"] = jnp.log(unif((N_AB, H_B), 1.0, 16.0))
    dt = jnp.exp(unif((N_AB, H_B), math.log(1e-3), math.log(1e-1)))
    inp["gdn_dt_bias"] = dt + jnp.log(-jnp.expm1(-dt))
    inp["gdn_norm_w"] = 1.0 + nrm((N_AB, DV_B), 0.05)
    inp["rw_maa"] = unif((N_CL, 6, D), 0.0, 1.0)
    inp["rw_w_r"] = nrm((N_CL, D, D), D ** -0.5)
    inp["rw_w_k"] = nrm((N_CL, D, D), D ** -0.5)
    inp["rw_w_v"] = nrm((N_CL, D, D), D ** -0.5)
    inp["rw_w_o"] = nrm((N_CL, D, D), D ** -0.5)
    ramp = jnp.arange(D, dtype=jnp.float32) / (D - 1)
    inp["rw_w0"] = -6.0 + 5.0 * ramp ** 0.85 + nrm((N_CL, D), 0.1)
    inp["rw_w1"] = nrm((N_CL, D, LORA_DECAY), D ** -0.5)
    inp["rw_w2"] = nrm((N_CL, LORA_DECAY, D), 0.1 * LORA_DECAY ** -0.5)
    inp["rw_a0"] = nrm((N_CL, D), 0.1)
    inp["rw_a1"] = nrm((N_CL, D, LORA_AAA), D ** -0.5)
    inp["rw_a2"] = nrm((N_CL, LORA_AAA, D), 0.1 * LORA_AAA ** -0.5)
    inp["rw_g1"] = nrm((N_CL, D, LORA_GATE), D ** -0.5)
    inp["rw_g2"] = nrm((N_CL, LORA_GATE, D), LORA_GATE ** -0.5)
    inp["rw_k_k"] = 0.85 + nrm((N_CL, D), 0.02)
    inp["rw_k_a"] = 1.0 + nrm((N_CL, D), 0.02)
    inp["rw_r_k"] = nrm((N_CL, H_C, N_HEAD_C), 0.1)
    inp["rw_ln_w"] = 1.0 + nrm((N_CL, D), 0.05)
    inp["rw_ln_b"] = nrm((N_CL, D), 0.02)
    return inp


def reference(x_prompt, x_sample, state_s5_re, state_s5_im, state_gdn, state_gdn_conv, state_rwkv,
              state_rwkv_shift, norm_mix, norm_ffn, norm_final, ffn_w_gate, ffn_w_up, ffn_w_down,
              ab_w_in, ab_w_out, s5_lambda_re, s5_lambda_im, s5_log_step, s5_B_re, s5_B_im, s5_C_re, s5_C_im,
              s5_D, s5_w_glu, gdn_conv_w, gdn_A_log, gdn_dt_bias, gdn_norm_w, rw_maa, rw_w_r, rw_w_k, rw_w_v,
              rw_w_o, rw_w0, rw_w1, rw_w2, rw_a0, rw_a1, rw_a2, rw_g1, rw_g2, rw_k_k, rw_k_a, rw_r_k,
              rw_ln_w, rw_ln_b):
    prm = dict(norm_mix=norm_mix, norm_ffn=norm_ffn, norm_final=norm_final, ffn_w_gate=ffn_w_gate,
               ffn_w_up=ffn_w_up, ffn_w_down=ffn_w_down, ab_w_in=ab_w_in, ab_w_out=ab_w_out,
               s5_lambda_re=s5_lambda_re, s5_lambda_im=s5_lambda_im, s5_log_step=s5_log_step,
               s5_B_re=s5_B_re, s5_B_im=s5_B_im, s5_C_re=s5_C_re, s5_C_im=s5_C_im, s5_D=s5_D, s5_w_glu=s5_w_glu,
               gdn_conv_w=gdn_conv_w, gdn_A_log=gdn_A_log, gdn_dt_bias=gdn_dt_bias, gdn_norm_w=gdn_norm_w,
               rw_maa=rw_maa, rw_w_r=rw_w_r, rw_w_k=rw_w_k, rw_w_v=rw_w_v, rw_w_o=rw_w_o, rw_w0=rw_w0,
               rw_w1=rw_w1, rw_w2=rw_w2, rw_a0=rw_a0, rw_a1=rw_a1, rw_a2=rw_a2, rw_g1=rw_g1, rw_g2=rw_g2,
               rw_k_k=rw_k_k, rw_k_a=rw_k_a, rw_r_k=rw_r_k, rw_ln_w=rw_ln_w, rw_ln_b=rw_ln_b)
    dt = x_prompt.dtype
    (y_prompt, p_s5_re, p_s5_im, p_gdn, p_gdn_conv, p_rwkv, p_rwkv_shift) = trunk(
        x_prompt,
        jnp.zeros((N_AB, BATCH) + state_s5_re.shape[2:], jnp.float32),
        jnp.zeros((N_AB, BATCH) + state_s5_im.shape[2:], jnp.float32),
        jnp.zeros((N_AB, BATCH) + state_gdn.shape[2:], jnp.float32),
        jnp.zeros((N_AB, BATCH) + state_gdn_conv.shape[2:], dt),
        jnp.zeros((N_CL, BATCH) + state_rwkv.shape[2:], jnp.float32),
        jnp.zeros((N_CL, BATCH) + state_rwkv_shift.shape[2:], dt),
        prm)
    (y_sample, s_s5_re, s_s5_im, s_gdn, s_gdn_conv, s_rwkv, s_rwkv_shift) = trunk(
        x_sample, state_s5_re, state_s5_im, state_gdn, state_gdn_conv, state_rwkv, state_rwkv_shift, prm)
    return (y_prompt, y_sample, p_s5_re, p_s5_im, p_gdn, p_gdn_conv, p_rwkv, p_rwkv_shift,
            s_s5_re, s_s5_im, s_gdn, s_gdn_conv, s_rwkv, s_rwkv_shift)
```

```python
import functools
import math

import jax
import jax.numpy as jnp
from jax import lax
from jax.experimental import pallas as pl
from jax.experimental.pallas import tpu as pltpu

F32 = jnp.float32
BF16 = jnp.bfloat16
HIGHEST = lax.Precision.HIGHEST

D_MODEL = 1024
D_A = 512
S5_GROUP = 16
G_A = 32
P_STATE = 64
N_S5 = G_A * P_STATE
D_B = 512
H_B = 4
DK_B = 128
DV_B = 128
D_QKV = 2 * H_B * DK_B + D_B
CONV_W = 4
GDN_CHUNK = 64
N_HEAD_C = 64
H_C = 16
RWKV_CHUNK = 64
RWKV_GN_EPS = 64e-5
D_FF = 2816
NORM_EPS = 1e-6

LANES = 128
SUBLANES = 8
FF_TILE = 256
N_FF_TILES = D_FF // FF_TILE
GATE_PAD = LANES
D_IN_PAD = 4 * 512 + 512 + GATE_PAD
S5_LANE_BLOCK = 512
N_S5_BLOCKS = N_S5 // S5_LANE_BLOCK
TOKEN_TILE = 512
RWKV_PROJ_TILE = 256
MIB = 1024 * 1024


def _cparams(semantics, vmem_mib):
    return pltpu.CompilerParams(dimension_semantics=semantics, vmem_limit_bytes=vmem_mib * MIB)


def _rmsnorm(x, w):
    return x * lax.rsqrt(jnp.mean(x * x, axis=-1, keepdims=True) + NORM_EPS) * w


def _sigmoid(x):
    return 1.0 / (1.0 + jnp.exp(-x))


def _silu(x):
    return x * _sigmoid(x)


def _softplus(x):
    return jnp.maximum(x, 0.0) + jnp.log1p(jnp.exp(-jnp.abs(x)))


def _bdot(a, b):
    return jnp.dot(a.astype(BF16), b.astype(BF16), preferred_element_type=F32)


def _bdot_nt(a, b):
    return lax.dot_general(a.astype(BF16), b.astype(BF16), (((1,), (1,)), ((), ())),
                           preferred_element_type=F32)


def _bdot_tn(a, b):
    return lax.dot_general(a.astype(BF16), b.astype(BF16), (((0,), (0,)), ((), ())),
                           preferred_element_type=F32)


def _hdot(a, b):
    return jnp.dot(a, b, preferred_element_type=F32, precision=HIGHEST)


def _hdot_nt(a, b):
    return lax.dot_general(a, b, (((1,), (1,)), ((), ())), preferred_element_type=F32,
                           precision=HIGHEST)


def _split_dot(a, ones_bf16):
    hi = a.astype(BF16)
    lo = (a - hi.astype(F32)).astype(BF16)
    return (jnp.dot(hi, ones_bf16, preferred_element_type=F32)
            + jnp.dot(lo, ones_bf16, preferred_element_type=F32))


def _iota2(shape):
    return (lax.broadcasted_iota(jnp.int32, shape, 0), lax.broadcasted_iota(jnp.int32, shape, 1))


def _inv_identity_plus(lm, t):
    row, col = _iota2((t, t))
    eye = (row == col).astype(F32)
    base = min(t, 16)
    if t > base:
        n = jnp.where((row >> 4) == (col >> 4), -lm, 0.0)
    else:
        n = -lm
    x = eye + n
    p = n
    k = 1
    while 2 * k < base:
        p = _hdot(p, p)
        x = x + _hdot(x, p)
        k *= 2
    shift = 4
    blk = base
    while blk < t:
        same_big = (row >> (shift + 1)) == (col >> (shift + 1))
        same_small = (row >> shift) == (col >> shift)
        l1 = jnp.where(same_big, jnp.where(same_small, 0.0, lm), 0.0)
        x = x - _hdot(x, _hdot(l1, x))
        blk *= 2
        shift += 1
    return x


def _ab_in_kernel(x_ref, nw_ref, w_ref, u_ref, qkv_ref, z_ref, ba_ref):
    h = _rmsnorm(x_ref[...], nw_ref[...])
    p = jnp.dot(h.astype(BF16), w_ref[...], preferred_element_type=F32)
    u_ref[...] = p[:, 0:D_A]
    qkv_ref[...] = p[:, D_A:D_A + D_QKV]
    z_ref[...] = p[:, D_A + D_QKV:D_A + D_QKV + D_B]
    ba_ref[...] = p[:, D_A + D_QKV + D_B:D_IN_PAD]


def _ab_in(x_bm, norm_w, w_in):
    bv, lv, _ = x_bm.shape
    tl = min(TOKEN_TILE, lv)
    grid = (bv, lv // tl)
    tm = lambda n: pl.BlockSpec((tl, n), lambda b, i: (i, b))
    const = lambda shape: pl.BlockSpec(shape, lambda b, i: (0,) * len(shape))
    return pl.pallas_call(
        _ab_in_kernel,
        grid=grid,
        in_specs=[pl.BlockSpec((None, tl, D_MODEL), lambda b, i: (b, i, 0)),
                  const((1, D_MODEL)), const((D_MODEL, D_IN_PAD))],
        out_specs=[tm(D_A), tm(D_QKV), tm(D_B), tm(GATE_PAD)],
        out_shape=[jax.ShapeDtypeStruct((lv, bv * D_A), F32),
                   jax.ShapeDtypeStruct((lv, bv * D_QKV), F32),
                   jax.ShapeDtypeStruct((lv, bv * D_B), F32),
                   jax.ShapeDtypeStruct((lv, bv * GATE_PAD), F32)],
        compiler_params=_cparams(("parallel", "parallel"), 48),
    )(x_bm, norm_w, w_in)


def _s5_kernel(u_ref, h0r_ref, h0i_ref, lr_ref, li_ref, ls_ref, bre_ref, bim_ref, cre_ref, cim_ref,
               d_ref, wglu_ref, y_ref, hr_out, hi_out,
               ar_s, ai_s, cr_s, ci_s, hr_s, hi_s, bur_s, bui_s, yg_s, *, tc):
    rows = tc * SUBLANES

    @pl.when(pl.program_id(1) == 0)
    def _():
        lr = lr_ref[...]
        li = li_ref[...]
        dt = jnp.exp(ls_ref[...])
        mag = jnp.exp(lr * dt)
        ar = mag * jnp.cos(li * dt)
        ai = mag * jnp.sin(li * dt)
        den = lr * lr + li * li
        nr = ar - 1.0
        cr = (nr * lr + ai * li) / den
        ci = (ai * lr - nr * li) / den
        ar_s[...] = jnp.broadcast_to(ar, (SUBLANES, N_S5))
        ai_s[...] = jnp.broadcast_to(ai, (SUBLANES, N_S5))
        cr_s[...] = jnp.broadcast_to(cr, (SUBLANES, N_S5))
        ci_s[...] = jnp.broadcast_to(ci, (SUBLANES, N_S5))
        hr_s[...] = h0r_ref[...]
        hi_s[...] = h0i_ref[...]

    u = u_ref[...].reshape(rows, D_A)
    ub = u.astype(BF16)
    for j in range(N_S5_BLOCKS):
        sl = slice(j * S5_LANE_BLOCK, (j + 1) * S5_LANE_BLOCK)
        uj = ub[:, j * LANES:(j + 1) * LANES]
        pr = jnp.dot(uj, bre_ref[j], preferred_element_type=F32)
        pi = jnp.dot(uj, bim_ref[j], preferred_element_type=F32)
        crj = cr_s[0:1, sl]
        cij = ci_s[0:1, sl]
        bur_s[:, :, sl] = (crj * pr - cij * pi).reshape(tc, SUBLANES, S5_LANE_BLOCK)
        bui_s[:, :, sl] = (crj * pi + cij * pr).reshape(tc, SUBLANES, S5_LANE_BLOCK)

    for j in range(N_S5_BLOCKS):
        sl = slice(j * S5_LANE_BLOCK, (j + 1) * S5_LANE_BLOCK)
        ar = ar_s[:, sl]
        ai = ai_s[:, sl]

        def step(t, carry, sl=sl, ar=ar, ai=ai):
            hr, hi = carry
            nr = ar * hr - ai * hi + bur_s[t, :, sl]
            ni = ar * hi + ai * hr + bui_s[t, :, sl]
            bur_s[t, :, sl] = nr
            bui_s[t, :, sl] = ni
            return nr, ni

        hr, hi = lax.fori_loop(0, tc, step, (hr_s[:, sl], hi_s[:, sl]), unroll=min(tc, 8))
        hr_s[:, sl] = hr
        hi_s[:, sl] = hi

    for j in range(N_S5_BLOCKS):
        sl = slice(j * S5_LANE_BLOCK, (j + 1) * S5_LANE_BLOCK)
        cl = slice(j * LANES, (j + 1) * LANES)
        xr = bur_s[:, :, sl].reshape(rows, S5_LANE_BLOCK).astype(BF16)
        xi = bui_s[:, :, sl].reshape(rows, S5_LANE_BLOCK).astype(BF16)
        yj = (jnp.dot(xr, cre_ref[j], preferred_element_type=F32)
              - jnp.dot(xi, cim_ref[j], preferred_element_type=F32)
              + d_ref[:, cl] * u[:, cl])
        yg_s[:, cl] = jax.nn.gelu(yj)

    yg = yg_s[...]
    out = yg * _sigmoid(jnp.dot(yg.astype(BF16), wglu_ref[...], preferred_element_type=F32))
    y_ref[...] = out.reshape(tc, SUBLANES, D_A)
    hr_out[...] = hr_s[...]
    hi_out[...] = hi_s[...]


def _s5(u_tm, h0r, h0i, prm, tc):
    seq, bsz, _ = u_tm.shape
    grid = (bsz // SUBLANES, seq // tc)
    const = lambda shape: pl.BlockSpec(shape, lambda b, i: (0,) * len(shape))
    act = pl.BlockSpec((tc, SUBLANES, D_A), lambda b, i: (i, b, 0))
    st = pl.BlockSpec((SUBLANES, N_S5), lambda b, i: (b, 0))
    vec = const((1, N_S5))
    small = pltpu.VMEM((SUBLANES, N_S5), F32)
    big = pltpu.VMEM((tc, SUBLANES, N_S5), F32)
    return pl.pallas_call(
        functools.partial(_s5_kernel, tc=tc),
        grid=grid,
        in_specs=[act, st, st, vec, vec, vec,
                  const((N_S5_BLOCKS, LANES, S5_LANE_BLOCK)), const((N_S5_BLOCKS, LANES, S5_LANE_BLOCK)),
                  const((N_S5_BLOCKS, S5_LANE_BLOCK, LANES)), const((N_S5_BLOCKS, S5_LANE_BLOCK, LANES)),
                  const((1, D_A)), const((D_A, D_A))],
        out_specs=[act, st, st],
        out_shape=[jax.ShapeDtypeStruct((seq, bsz, D_A), F32),
                   jax.ShapeDtypeStruct((bsz, N_S5), F32),
                   jax.ShapeDtypeStruct((bsz, N_S5), F32)],
        scratch_shapes=[small, small, small, small, small, small, big, big,
                        pltpu.VMEM((tc * SUBLANES, D_A), F32)],
        compiler_params=_cparams(("parallel", "arbitrary"), 48),
    )(u_tm, h0r, h0i, prm["lr"], prm["li"], prm["ls"], prm["bre"], prm["bim"], prm["cre"], prm["cim"],
      prm["d"], prm["wglu"])


def _gdn_kernel(qkv_ref, ba_ref, z_ref, conv0_ref, s0_ref, cw_ref, alog_ref, dtb_ref, nw_ref,
                o_ref, sfin_ref, convn_ref, xbuf, s_s, *, t):
    pad = SUBLANES
    hist = CONV_W - 1

    @pl.when(pl.program_id(1) == 0)
    def _():
        xbuf[pad - hist:pad, :] = conv0_ref[...]
        s_s[...] = s0_ref[...]

    x = qkv_ref[...]
    xbuf[pad:pad + t, :] = x
    cw = cw_ref[...]
    acc = x * cw[hist:hist + 1, :]
    for j in range(hist):
        acc = acc + xbuf[pad - hist + j:pad - hist + j + t, :] * cw[j:j + 1, :]
    last = xbuf[pad + t - hist:pad + t, :]
    convn_ref[...] = last
    xbuf[pad - hist:pad, :] = last
    y = _silu(acc)

    ba = ba_ref[...]
    beta = _sigmoid(ba)
    g = -jnp.exp(alog_ref[...]) * _softplus(ba + dtb_ref[...])
    row, col = _iota2((t, t))
    causal = row >= col
    strict = row > col
    gc = _hdot(causal.astype(F32), g)
    lane = lax.broadcasted_iota(jnp.int32, (t, LANES), 1)
    nw = nw_ref[...]
    z = z_ref[...]

    for h in range(H_B):
        bcol = beta[:, h:h + 1]
        gcol = gc[:, H_B + h:H_B + h + 1]
        dl = jnp.where(lane == 0, gcol, jnp.where(lane == 1, 1.0, 0.0))
        dr = jnp.where(lane == 0, 1.0, jnp.where(lane == 1, -gcol, 0.0))
        decay = jnp.exp(jnp.where(causal, _hdot_nt(dl, dr), -jnp.inf))
        eg = jnp.exp(gcol)
        glast = gcol[t - 1:t, :]
        q = y[:, h * DK_B:(h + 1) * DK_B]
        k = y[:, H_B * DK_B + h * DK_B:H_B * DK_B + (h + 1) * DK_B]
        v = y[:, 2 * H_B * DK_B + h * DV_B:2 * H_B * DK_B + (h + 1) * DV_B]
        q = q * lax.rsqrt(jnp.sum(q * q, axis=-1, keepdims=True) + NORM_EPS) * (DK_B ** -0.5)
        k = k * lax.rsqrt(jnp.sum(k * k, axis=-1, keepdims=True) + NORM_EPS)
        kb = k * bcol
        vb = v * bcol
        lm = jnp.where(strict, _bdot_nt(kb, k) * decay, 0.0)
        tinv = _inv_identity_plus(lm, t)
        u = _bdot(tinv, vb)
        w = _bdot(tinv, kb * eg)
        s = s_s[h]
        v_new = u - _bdot(w, s)
        attn = _bdot_nt(q, k) * decay
        o = _bdot(q * eg, s) + _bdot(attn, v_new)
        s_new = s * jnp.exp(glast) + _bdot_tn(k * jnp.exp(glast - gcol), v_new)
        s_s[h] = s_new
        sfin_ref[h] = s_new
        o = _rmsnorm(o, nw) * _silu(z[:, h * DV_B:(h + 1) * DV_B])
        o_ref[:, h * DV_B:(h + 1) * DV_B] = o


def _gdn(qkv_tm, ba_tm, z_tm, conv0, s0, prm, bsz, t):
    seq = qkv_tm.shape[0]
    grid = (bsz, seq // t)
    const = lambda shape: pl.BlockSpec(shape, lambda b, i: (0,) * len(shape))
    tm = lambda n: pl.BlockSpec((t, n), lambda b, i: (i, b))
    return pl.pallas_call(
        functools.partial(_gdn_kernel, t=t),
        grid=grid,
        in_specs=[tm(D_QKV), tm(GATE_PAD), tm(D_B),
                  pl.BlockSpec((None, CONV_W - 1, D_QKV), lambda b, i: (b, 0, 0)),
                  pl.BlockSpec((None, H_B, DK_B, DV_B), lambda b, i: (b, 0, 0, 0)),
                  const((CONV_W, D_QKV)), const((1, GATE_PAD)), const((1, GATE_PAD)), const((1, DV_B))],
        out_specs=[tm(D_B),
                   pl.BlockSpec((None, H_B, DK_B, DV_B), lambda b, i: (b, 0, 0, 0)),
                   pl.BlockSpec((None, CONV_W - 1, D_QKV), lambda b, i: (b, 0, 0))],
        out_shape=[jax.ShapeDtypeStruct((seq, bsz * D_B), F32),
                   jax.ShapeDtypeStruct((bsz, H_B, DK_B, DV_B), F32),
                   jax.ShapeDtypeStruct((bsz, CONV_W - 1, D_QKV), F32)],
        scratch_shapes=[pltpu.VMEM((t + SUBLANES, D_QKV), F32), pltpu.VMEM((H_B, DK_B, DV_B), F32)],
        compiler_params=_cparams(("parallel", "arbitrary"), 32),
    )(qkv_tm, ba_tm, z_tm, conv0, s0, prm["conv_w"], prm["alog"], prm["dtb"], prm["norm_w"])


def _mix_ffn_kernel(*refs, n_mix, final_norm):
    x_ref = refs[0]
    mix_refs = refs[1:1 + n_mix]
    wout_refs = refs[1 + n_mix:1 + 2 * n_mix]
    nffn_ref, wg_ref, wu_ref, wd_ref = refs[1 + 2 * n_mix:5 + 2 * n_mix]
    rest = refs[5 + 2 * n_mix:]
    if final_norm:
        nfin_ref, out_ref, acc_s = rest
    else:
        out_ref, acc_s = rest

    x1 = x_ref[...]
    for m_ref, w_ref in zip(mix_refs, wout_refs):
        x1 = x1 + jnp.dot(m_ref[...].astype(BF16), w_ref[...], preferred_element_type=F32)
    h = _rmsnorm(x1, nffn_ref[...]).astype(BF16)
    acc_s[...] = x1

    def ff_tile(c, carry):
        gate = jnp.dot(h, wg_ref[c], preferred_element_type=F32)
        up = jnp.dot(h, wu_ref[c], preferred_element_type=F32)
        act = (_silu(gate) * up).astype(BF16)
        acc_s[...] += jnp.dot(act, wd_ref[c], preferred_element_type=F32)
        return carry

    lax.fori_loop(0, N_FF_TILES, ff_tile, 0)
    x2 = acc_s[...]
    out_ref[...] = _rmsnorm(x2, nfin_ref[...]) if final_norm else x2


def _mix_ffn(x, x_bm, mixes, wouts, nffn, wg, wu, wd, nfin, bv, lv, out_bm):
    tl = min(TOKEN_TILE, lv)
    grid = (bv, lv // tl)
    const = lambda shape: pl.BlockSpec(shape, lambda b, i: (0,) * len(shape),
                                       pipeline_mode=pl.Buffered(1))
    tm = lambda n: pl.BlockSpec((tl, n), lambda b, i: (i, b))
    bm = pl.BlockSpec((None, tl, D_MODEL), lambda b, i: (b, i, 0))
    in_specs = [bm if x_bm else tm(D_MODEL)]
    in_specs += [tm(m.shape[1] // bv) for m in mixes]
    in_specs += [const(w.shape) for w in wouts]
    in_specs += [const((1, D_MODEL)), const(wg.shape), const(wu.shape), const(wd.shape)]
    args = [x, *mixes, *wouts, nffn, wg, wu, wd]
    if nfin is not None:
        in_specs.append(const((1, D_MODEL)))
        args.append(nfin)
    if out_bm:
        out_spec, out_shape = bm, jax.ShapeDtypeStruct((bv, lv, D_MODEL), F32)
    else:
        out_spec, out_shape = tm(D_MODEL), jax.ShapeDtypeStruct((lv, bv * D_MODEL), F32)
    return pl.pallas_call(
        functools.partial(_mix_ffn_kernel, n_mix=len(mixes), final_norm=nfin is not None),
        grid=grid,
        in_specs=in_specs,
        out_specs=out_spec,
        out_shape=out_shape,
        scratch_shapes=[pltpu.VMEM((tl, D_MODEL), F32)],
        compiler_params=_cparams(("parallel", "parallel"), 56),
    )(*args)


def _rwkv_proj_kernel(x_ref, shift0_ref, nw_ref, maa_ref, wr_ref, wk_ref, wv_ref, w0_ref, w1_ref, w2_ref,
                      a0_ref, a1_ref, a2_ref, g1_ref, g2_ref, kk_ref, ka_ref,
                      r_out, w_out, k_out, v_out, kk_out, a_out, g_out, shift_out, carry_s, *, bsz):
    @pl.when(pl.program_id(0) == 0)
    def _():
        carry_s[...] = shift0_ref[...]

    h = _rmsnorm(x_ref[...], nw_ref[...])
    rows = h.shape[0]
    if rows > bsz:
        prev = jnp.concatenate([carry_s[...], h[:rows - bsz]], axis=0)
    else:
        prev = carry_s[...]
    last = h[rows - bsz:]
    carry_s[...] = last
    shift_out[...] = last
    xx = prev - h
    maa = maa_ref[...]
    mixed = [h + xx * maa[j:j + 1, :] for j in range(6)]
    xr, xw, xk, xv, xa, xg = mixed
    r = _bdot(xr, wr_ref[...])
    k = _bdot(xk, wk_ref[...])
    v = _bdot(xv, wv_ref[...])
    w = -_softplus(-(w0_ref[...] + _bdot(jnp.tanh(_bdot(xw, w1_ref[...])), w2_ref[...]))) - 0.5
    a = _sigmoid(a0_ref[...] + _bdot(_bdot(xa, a1_ref[...]), a2_ref[...]))
    g = _bdot(_sigmoid(_bdot(xg, g1_ref[...])), g2_ref[...])
    r_out[...] = r
    w_out[...] = -jnp.exp(w)
    k_out[...] = k * (1.0 + (a - 1.0) * ka_ref[...])
    v_out[...] = v
    kk_out[...] = k * kk_ref[...]
    a_out[...] = a
    g_out[...] = g


def _rwkv_proj(x_flat, shift0, prm, bsz):
    rows = x_flat.shape[0]
    tr = max(bsz, min(RWKV_PROJ_TILE, rows))
    grid = (rows // tr,)
    const = lambda shape: pl.BlockSpec(shape, lambda i: (0,) * len(shape))
    act = pl.BlockSpec((tr, D_MODEL), lambda i: (i, 0))
    vec = const((1, D_MODEL))
    weights = [prm[n] for n in ("wr", "wk", "wv")]
    in_specs = [act, const((bsz, D_MODEL)), vec, const((6, D_MODEL))]
    in_specs += [const(w.shape) for w in weights]
    in_specs += [vec, const(prm["w1"].shape), const(prm["w2"].shape),
                 vec, const(prm["a1"].shape), const(prm["a2"].shape),
                 const(prm["g1"].shape), const(prm["g2"].shape), vec, vec]
    out_act = jax.ShapeDtypeStruct((rows, D_MODEL), F32)
    return pl.pallas_call(
        functools.partial(_rwkv_proj_kernel, bsz=bsz),
        grid=grid,
        in_specs=in_specs,
        out_specs=[act] * 7 + [const((bsz, D_MODEL))],
        out_shape=[out_act] * 7 + [jax.ShapeDtypeStruct((bsz, D_MODEL), F32)],
        scratch_shapes=[pltpu.VMEM((bsz, D_MODEL), F32)],
        compiler_params=_cparams(("arbitrary",), 56),
    )(x_flat, shift0, prm["norm_w"], prm["maa"], prm["wr"], prm["wk"], prm["wv"], prm["w0"], prm["w1"],
      prm["w2"], prm["a0"], prm["a1"], prm["a2"], prm["g1"], prm["g2"], prm["k_k"], prm["k_a"])


def _rwkv_scan_kernel(r_ref, w_ref, k_ref, v_ref, kk_ref, a_ref, g_ref, s0_ref, rk_ref, lnw_ref, lnb_ref,
                      y_ref, sfin_ref, s_s, *, t):
    n = N_HEAD_C
    zero_blk = jnp.zeros((n, n), F32)

    @pl.when(pl.program_id(1) == 0)
    def _():
        for j in range(H_C // 2):
            top = jnp.concatenate([s0_ref[2 * j], zero_blk], axis=1)
            bot = jnp.concatenate([zero_blk, s0_ref[2 * j + 1]], axis=1)
            s_s[j] = jnp.concatenate([top, bot], axis=0)

    row, col = _iota2((t, t))
    incl = row >= col
    strict = row > col
    tri = incl.astype(F32)
    lane = lax.broadcasted_iota(jnp.int32, (t, LANES), 1)
    head0 = lane < n
    prow, pcol = _iota2((LANES, LANES))
    same_head = (prow >> 6) == (pcol >> 6)
    ones_blk = same_head.astype(BF16)

    def pair(j, carry):
        sl = pl.ds(pl.multiple_of(j * LANES, LANES), LANES)
        r = r_ref[:, sl]
        w = w_ref[:, sl]
        k = k_ref[:, sl]
        v = v_ref[:, sl]
        kkr = kk_ref[:, sl]
        a = a_ref[:, sl]
        kk = kkr * lax.rsqrt(_split_dot(kkr * kkr, ones_blk) + NORM_EPS)
        av = -kk
        bv = kk * a
        gc = _hdot(tri, w)
        glast = gc[t - 1:t, :]
        pinv = jnp.exp(-gc)
        at = av * jnp.exp(gc - w)
        bt = bv * pinv
        kt = k * pinv
        rt = r * jnp.exp(gc)
        rem = jnp.exp(glast - gc)
        lhs = jnp.concatenate([jnp.where(head0, at, 0.0), jnp.where(head0, 0.0, at),
                               jnp.where(head0, rt, 0.0), jnp.where(head0, 0.0, rt)], axis=0)
        gb = _bdot_nt(lhs, bt)
        gk = _bdot_nt(lhs, kt)
        s = s_s[j]
        ars = _bdot_nt(jnp.concatenate([at, rt], axis=0), s)
        a_s0 = ars[:t]
        r_s0 = ars[t:]
        us = []
        for hh in range(2):
            aab = jnp.where(strict, gb[hh * t:(hh + 1) * t], 0.0)
            aak = jnp.where(strict, gk[hh * t:(hh + 1) * t], 0.0)
            tinv = _inv_identity_plus(-aab, t)
            us.append(_bdot(tinv, a_s0 + _bdot(aak, v)))
        u = jnp.where(head0, us[0], us[1])
        ys = []
        for hh in range(2):
            rb = jnp.where(incl, gb[(2 + hh) * t:(3 + hh) * t], 0.0)
            rkm = jnp.where(incl, gk[(2 + hh) * t:(3 + hh) * t], 0.0)
            ys.append(_bdot(rb, u) + _bdot(rkm, v))
        y = r_s0 + jnp.where(head0, ys[0], ys[1])
        upd = _bdot_tn(jnp.concatenate([u, v], axis=0), jnp.concatenate([bv * rem, k * rem], axis=0))
        s_new = s * jnp.exp(glast) + jnp.where(same_head, upd, 0.0)
        s_s[j] = s_new

        mu = _split_dot(y, ones_blk) * (1.0 / n)
        d = y - mu
        var = _split_dot(d * d, ones_blk) * (1.0 / n)
        yn = d * lax.rsqrt(var + RWKV_GN_EPS) * lnw_ref[:, sl] + lnb_ref[:, sl]
        bonus = _split_dot(r * k * rk_ref[:, sl], ones_blk)
        y_ref[:, sl] = (yn + bonus * v) * g_ref[:, sl]
        return carry

    lax.fori_loop(0, H_C // 2, pair, 0)

    for j in range(H_C // 2):
        sp = s_s[j]
        sfin_ref[2 * j] = sp[:n, :n]
        sfin_ref[2 * j + 1] = sp[n:, n:]


def _rwkv_scan(acts, s0, prm, bsz, t):
    seq = acts[0].shape[0]
    grid = (bsz, seq // t)
    const = lambda shape: pl.BlockSpec(shape, lambda b, i: (0,) * len(shape))
    tm = pl.BlockSpec((t, D_MODEL), lambda b, i: (i, b))
    st = pl.BlockSpec((None, H_C, N_HEAD_C, N_HEAD_C), lambda b, i: (b, 0, 0, 0))
    vec = const((1, D_MODEL))
    return pl.pallas_call(
        functools.partial(_rwkv_scan_kernel, t=t),
        grid=grid,
        in_specs=[tm] * 7 + [st, vec, vec, vec],
        out_specs=[tm, st],
        out_shape=[jax.ShapeDtypeStruct((seq, bsz * D_MODEL), F32),
                   jax.ShapeDtypeStruct((bsz, H_C, N_HEAD_C, N_HEAD_C), F32)],
        scratch_shapes=[pltpu.VMEM((H_C // 2, LANES, LANES), F32)],
        compiler_params=_cparams(("parallel", "arbitrary"), 32),
    )(*acts, s0, prm["r_k"], prm["ln_w"], prm["ln_b"])


def _block_diag_groups(w, rows_per_group, cols_per_group):
    g = w.shape[0]
    w = w.reshape(g // 8, 8, rows_per_group, cols_per_group)
    eye = jnp.eye(8, dtype=w.dtype)
    out = jnp.einsum("jgrc,gh->jgrhc", w, eye)
    return out.reshape(g // 8, 8 * rows_per_group, 8 * cols_per_group)


def _pad_to(w, axis, size):
    pad = [(0, 0)] * w.ndim
    pad[axis] = (0, size - w.shape[axis])
    return jnp.pad(w, pad)


def _row(v):
    return v.reshape(1, -1).astype(F32)


def _trunk(x_bm, bsz, seq, s5_re0, s5_im0, gdn_s0, gdn_conv0, rw_s0, rw_shift0, p, gdn_t, rwkv_t, s5_tc,
           out_bm):
    bv, lv, _ = x_bm.shape
    u, qkv, z, ba = _ab_in(x_bm, p["norm_mix0"], p["ab_w_in"])
    y_a, hr, hi = _s5(u.reshape(seq, bsz, D_A), s5_re0, s5_im0, p["s5"], s5_tc)
    y_b, gdn_s, gdn_conv = _gdn(qkv.reshape(seq, bsz * D_QKV), ba.reshape(seq, bsz * GATE_PAD),
                                z.reshape(seq, bsz * D_B), gdn_conv0, gdn_s0, p["gdn"], bsz, gdn_t)
    x1 = _mix_ffn(x_bm, True, [y_a.reshape(lv, bv * D_A), y_b.reshape(lv, bv * D_B)],
                  [p["ab_w_out_a"], p["ab_w_out_b"]], p["norm_ffn0"], p["wg0"], p["wu0"], p["wd0"],
                  None, bv, lv, False)
    acts = _rwkv_proj(x1.reshape(seq * bsz, D_MODEL), rw_shift0, p["rw"], bsz)
    shift = acts[7]
    yg, rw_s = _rwkv_scan([a.reshape(seq, bsz * D_MODEL) for a in acts[:7]], rw_s0, p["rw"], bsz, rwkv_t)
    y = _mix_ffn(x1, False, [yg.reshape(lv, bv * D_MODEL)], [p["rw"]["wo"]], p["norm_ffn1"], p["wg1"],
                 p["wu1"], p["wd1"], p["norm_final"], bv, lv, out_bm)
    return y, hr, hi, gdn_s, gdn_conv, rw_s, shift


def kernel(x_prompt, x_sample, state_s5_re, state_s5_im, state_gdn, state_gdn_conv, state_rwkv, state_rwkv_shift, norm_mix, norm_ffn, norm_final, ffn_w_gate, ffn_w_up, ffn_w_down, ab_w_in, ab_w_out, s5_lambda_re, s5_lambda_im, s5_log_step, s5_B_re, s5_B_im, s5_C_re, s5_C_im, s5_D, s5_w_glu, gdn_conv_w, gdn_A_log, gdn_dt_bias, gdn_norm_w, rw_maa, rw_w_r, rw_w_k, rw_w_v, rw_w_o, rw_w0, rw_w1, rw_w2, rw_a0, rw_a1, rw_a2, rw_g1, rw_g2, rw_k_k, rw_k_a, rw_r_k, rw_ln_w, rw_ln_b):
    bsz_p, seq_p, _ = x_prompt.shape
    bsz_s, seq_s, _ = x_sample.shape

    w_in = ab_w_in[0]
    n_main = D_A + D_QKV
    w_in = jnp.concatenate([w_in[:, :n_main], w_in[:, n_main + 2 * H_B:], w_in[:, n_main:n_main + 2 * H_B]],
                           axis=1)
    w_in = _pad_to(w_in, 1, D_IN_PAD).astype(BF16)

    def ff(w, layer):
        return jnp.transpose(w[layer].reshape(D_MODEL, N_FF_TILES, FF_TILE), (1, 0, 2)).astype(BF16)

    gate_lanes = lambda v: _pad_to(jnp.concatenate([jnp.zeros((H_B,), F32), v.astype(F32)]), 0,
                                   GATE_PAD).reshape(1, GATE_PAD)
    p = dict(
        norm_mix0=_row(norm_mix[0]), norm_ffn0=_row(norm_ffn[0]), norm_ffn1=_row(norm_ffn[1]),
        norm_final=_row(norm_final), ab_w_in=w_in,
        ab_w_out_a=ab_w_out[0][:D_A].astype(BF16), ab_w_out_b=ab_w_out[0][D_A:].astype(BF16),
        wg0=ff(ffn_w_gate, 0), wu0=ff(ffn_w_up, 0),
        wd0=ffn_w_down[0].reshape(N_FF_TILES, FF_TILE, D_MODEL).astype(BF16),
        wg1=ff(ffn_w_gate, 1), wu1=ff(ffn_w_up, 1),
        wd1=ffn_w_down[1].reshape(N_FF_TILES, FF_TILE, D_MODEL).astype(BF16),
        s5=dict(
            lr=_row(s5_lambda_re[0]), li=_row(s5_lambda_im[0]),
            ls=_row(jnp.repeat(s5_log_step[0], P_STATE)),
            bre=_block_diag_groups(jnp.swapaxes(s5_B_re[0], 1, 2), S5_GROUP, P_STATE).astype(BF16),
            bim=_block_diag_groups(jnp.swapaxes(s5_B_im[0], 1, 2), S5_GROUP, P_STATE).astype(BF16),
            cre=_block_diag_groups(jnp.swapaxes(s5_C_re[0], 1, 2), P_STATE, S5_GROUP).astype(BF16),
            cim=_block_diag_groups(jnp.swapaxes(s5_C_im[0], 1, 2), P_STATE, S5_GROUP).astype(BF16),
            d=_row(s5_D[0]), wglu=s5_w_glu[0].astype(BF16)),
        gdn=dict(conv_w=gdn_conv_w[0].astype(F32), alog=gate_lanes(gdn_A_log[0]),
                 dtb=gate_lanes(gdn_dt_bias[0]), norm_w=_row(gdn_norm_w[0])),
        rw=dict(
            norm_w=_row(norm_mix[1]), maa=rw_maa[0].astype(F32),
            wr=rw_w_r[0].astype(BF16), wk=rw_w_k[0].astype(BF16), wv=rw_w_v[0].astype(BF16),
            wo=rw_w_o[0].astype(BF16),
            w0=_row(rw_w0[0]), w1=_pad_to(rw_w1[0], 1, LANES).astype(BF16),
            w2=_pad_to(rw_w2[0], 0, LANES).astype(BF16),
            a0=_row(rw_a0[0]), a1=_pad_to(rw_a1[0], 1, LANES).astype(BF16),
            a2=_pad_to(rw_a2[0], 0, LANES).astype(BF16),
            g1=_pad_to(rw_g1[0], 1, 2 * LANES).astype(BF16), g2=_pad_to(rw_g2[0], 0, 2 * LANES).astype(BF16),
            k_k=_row(rw_k_k[0]), k_a=_row(rw_k_a[0]), r_k=_row(rw_r_k[0]),
            ln_w=_row(rw_ln_w[0]), ln_b=_row(rw_ln_b[0])),
    )

    zeros = lambda *shape: jnp.zeros(shape, F32)
    yp, p_hr, p_hi, p_gdn, p_conv, p_rw, p_shift = _trunk(
        x_prompt, bsz_p, seq_p, zeros(bsz_p, N_S5), zeros(bsz_p, N_S5), zeros(bsz_p, H_B, DK_B, DV_B),
        zeros(bsz_p, CONV_W - 1, D_QKV), zeros(bsz_p, H_C, N_HEAD_C, N_HEAD_C), zeros(bsz_p, D_MODEL), p,
        GDN_CHUNK, RWKV_CHUNK, 64, True)

    xs_tm = jnp.transpose(x_sample, (1, 0, 2)).reshape(1, seq_s * bsz_s, D_MODEL)
    ys, s_hr, s_hi, s_gdn, s_conv, s_rw, s_shift = _trunk(
        xs_tm, bsz_s, seq_s, state_s5_re[0].reshape(bsz_s, N_S5), state_s5_im[0].reshape(bsz_s, N_S5),
        state_gdn[0], state_gdn_conv[0], state_rwkv[0], state_rwkv_shift[0], p,
        seq_s, seq_s, seq_s, True)
    y_sample = jnp.transpose(ys.reshape(seq_s, bsz_s, D_MODEL), (1, 0, 2))

    s5_shape = lambda b: (1, b, G_A, P_STATE)
    return (yp, y_sample,
            p_hr.reshape(s5_shape(bsz_p)), p_hi.reshape(s5_shape(bsz_p)), p_gdn[None], p_conv[None],
            p_rw[None], p_shift[None],
            s_hr.reshape(s5_shape(bsz_s)), s_hi.reshape(s5_shape(bsz_s)), s_gdn[None], s_conv[None],
            s_rw[None], s_shift[None])
```

```python
import functools
import math

import jax
import jax.numpy as jnp
from jax import lax
from jax.experimental import pallas as pl
from jax.experimental.pallas import tpu as pltpu

F32 = jnp.float32
BF16 = jnp.bfloat16

D_MODEL = 1024
D_A = 512
S5_GROUP = 16
G_A = 32
P_STATE = 64
N_S5 = G_A * P_STATE
D_B = 512
H_B = 4
DK_B = 128
DV_B = 128
D_QKV = 2 * H_B * DK_B + D_B
CONV_W = 4
GDN_CHUNK = 64
N_HEAD_C = 64
H_C = 16
RWKV_CHUNK = 64
RWKV_PAIR_GROUP = 4
RWKV_GN_EPS = 64e-5
D_FF = 2816
NORM_EPS = 1e-6

LANES = 128
SUBLANES = 8
FF_TILE = 256
N_FF_TILES = D_FF // FF_TILE
GATE_PAD = LANES
D_IN_PAD = 4 * 512 + 512 + GATE_PAD
S5_LANE_BLOCK = 512
N_S5_BLOCKS = N_S5 // S5_LANE_BLOCK
TOKEN_TILE = 512
RWKV_PROJ_TILE = 256
MIB = 1024 * 1024


def _cparams(semantics, vmem_mib):
    return pltpu.CompilerParams(dimension_semantics=semantics, vmem_limit_bytes=vmem_mib * MIB)


def _rmsnorm(x, w):
    return x * lax.rsqrt(jnp.mean(x * x, axis=-1, keepdims=True) + NORM_EPS) * w


def _sigmoid(x):
    return 1.0 / (1.0 + jnp.exp(-x))


def _silu(x):
    return x * _sigmoid(x)


def _softplus(x):
    return jnp.maximum(x, 0.0) + jnp.log1p(jnp.exp(-jnp.abs(x)))


def _bdot(a, b):
    return jnp.dot(a.astype(BF16), b.astype(BF16), preferred_element_type=F32)


def _bdot_nt(a, b):
    return lax.dot_general(a.astype(BF16), b.astype(BF16), (((1,), (1,)), ((), ())),
                           preferred_element_type=F32)


def _bdot_tn(a, b):
    return lax.dot_general(a.astype(BF16), b.astype(BF16), (((0,), (0,)), ((), ())),
                           preferred_element_type=F32)


def _split(a):
    hi = a.astype(BF16)
    return hi, (a - hi.astype(F32)).astype(BF16)


def _split3(a):
    p1 = a.astype(BF16)
    r1 = a - p1.astype(F32)
    p2 = r1.astype(BF16)
    return p1, p2, (r1 - p2.astype(F32)).astype(BF16)


def _dot3(a, b):
    dot = lambda x, y: jnp.dot(x, y, preferred_element_type=F32)
    return dot(a[0], b[0]) + dot(a[0], b[1]) + dot(a[1], b[0])


def _split_dot(a, exact_bf16):
    hi, lo = _split(a)
    return (jnp.dot(hi, exact_bf16, preferred_element_type=F32)
            + jnp.dot(lo, exact_bf16, preferred_element_type=F32))


def _cumsum_rows(tri_bf16, w):
    return sum(jnp.dot(tri_bf16, part, preferred_element_type=F32) for part in _split3(w))


def _iota2(shape):
    return (lax.broadcasted_iota(jnp.int32, shape, 0), lax.broadcasted_iota(jnp.int32, shape, 1))


def _inv_identity_plus(lms, t):
    row, col = _iota2((t, t))
    eye = (row == col).astype(F32)
    base = min(t, 16)
    if t > base:
        same_base = (row >> 4) == (col >> 4)
        ns = [jnp.where(same_base, -lm, 0.0) for lm in lms]
    else:
        ns = [-lm for lm in lms]
    xs = [eye + n for n in ns]
    ps = ns
    k = 1
    while 2 * k < base:
        ps = [_dot3(sp, sp) for sp in [_split(p) for p in ps]]
        sps = [_split(p) for p in ps]
        xs = [x + _dot3(_split(x), sp) for x, sp in zip(xs, sps)]
        k *= 2
    shift = 4
    blk = base
    while blk < t:
        same_big = (row >> (shift + 1)) == (col >> (shift + 1))
        same_small = (row >> shift) == (col >> shift)
        off = [_split(jnp.where(same_big, jnp.where(same_small, 0.0, lm), 0.0)) for lm in lms]
        sxs = [_split(x) for x in xs]
        mids = [_split(_dot3(o, sx)) for o, sx in zip(off, sxs)]
        xs = [x - _dot3(sx, m) for x, sx, m in zip(xs, sxs, mids)]
        blk *= 2
        shift += 1
    return xs


def _ab_in_kernel(x_ref, nw_ref, w_ref, u_ref, qkv_ref, z_ref, ba_ref):
    h = _rmsnorm(x_ref[...], nw_ref[...])
    p = jnp.dot(h.astype(BF16), w_ref[...], preferred_element_type=F32)
    u_ref[...] = p[:, 0:D_A]
    qkv_ref[...] = p[:, D_A:D_A + D_QKV]
    z_ref[...] = p[:, D_A + D_QKV:D_A + D_QKV + D_B]
    ba_ref[...] = p[:, D_A + D_QKV + D_B:D_IN_PAD]


def _ab_in(x_bm, norm_w, w_in):
    bv, lv, _ = x_bm.shape
    tl = min(TOKEN_TILE, lv)
    grid = (bv, lv // tl)
    tm = lambda n: pl.BlockSpec((tl, n), lambda b, i: (i, b))
    const = lambda shape: pl.BlockSpec(shape, lambda b, i: (0,) * len(shape))
    return pl.pallas_call(
        _ab_in_kernel,
        grid=grid,
        in_specs=[pl.BlockSpec((None, tl, D_MODEL), lambda b, i: (b, i, 0)),
                  const((1, D_MODEL)), const((D_MODEL, D_IN_PAD))],
        out_specs=[tm(D_A), tm(D_QKV), tm(D_B), tm(GATE_PAD)],
        out_shape=[jax.ShapeDtypeStruct((lv, bv * D_A), F32),
                   jax.ShapeDtypeStruct((lv, bv * D_QKV), F32),
                   jax.ShapeDtypeStruct((lv, bv * D_B), F32),
                   jax.ShapeDtypeStruct((lv, bv * GATE_PAD), F32)],
        compiler_params=_cparams(("parallel", "parallel"), 48),
    )(x_bm, norm_w, w_in)


def _s5_kernel(u_ref, h0r_ref, h0i_ref, lr_ref, li_ref, ls_ref, bre_ref, bim_ref, cre_ref, cim_ref,
               d_ref, wglu_ref, y_ref, hr_out, hi_out,
               ar_s, ai_s, cr_s, ci_s, hr_s, hi_s, bur_s, bui_s, yg_s, *, tc):
    rows = tc * SUBLANES

    @pl.when(pl.program_id(1) == 0)
    def _():
        lr = lr_ref[...]
        li = li_ref[...]
        dt = jnp.exp(ls_ref[...])
        mag = jnp.exp(lr * dt)
        ar = mag * jnp.cos(li * dt)
        ai = mag * jnp.sin(li * dt)
        den = lr * lr + li * li
        nr = ar - 1.0
        cr = (nr * lr + ai * li) / den
        ci = (ai * lr - nr * li) / den
        ar_s[...] = jnp.broadcast_to(ar, (SUBLANES, N_S5))
        ai_s[...] = jnp.broadcast_to(ai, (SUBLANES, N_S5))
        cr_s[...] = jnp.broadcast_to(cr, (SUBLANES, N_S5))
        ci_s[...] = jnp.broadcast_to(ci, (SUBLANES, N_S5))
        hr_s[...] = h0r_ref[...]
        hi_s[...] = h0i_ref[...]

    u = u_ref[...].reshape(rows, D_A)
    ub = u.astype(BF16)
    for j in range(N_S5_BLOCKS):
        sl = slice(j * S5_LANE_BLOCK, (j + 1) * S5_LANE_BLOCK)
        uj = ub[:, j * LANES:(j + 1) * LANES]
        pr = jnp.dot(uj, bre_ref[j], preferred_element_type=F32)
        pi = jnp.dot(uj, bim_ref[j], preferred_element_type=F32)
        crj = cr_s[0:1, sl]
        cij = ci_s[0:1, sl]
        bur_s[:, :, sl] = (crj * pr - cij * pi).reshape(tc, SUBLANES, S5_LANE_BLOCK)
        bui_s[:, :, sl] = (crj * pi + cij * pr).reshape(tc, SUBLANES, S5_LANE_BLOCK)

    for j in range(N_S5_BLOCKS):
        sl = slice(j * S5_LANE_BLOCK, (j + 1) * S5_LANE_BLOCK)
        ar = ar_s[:, sl]
        ai = ai_s[:, sl]

        def step(t, carry, sl=sl, ar=ar, ai=ai):
            hr, hi = carry
            nr = ar * hr - ai * hi + bur_s[t, :, sl]
            ni = ar * hi + ai * hr + bui_s[t, :, sl]
            bur_s[t, :, sl] = nr
            bui_s[t, :, sl] = ni
            return nr, ni

        hr, hi = lax.fori_loop(0, tc, step, (hr_s[:, sl], hi_s[:, sl]), unroll=min(tc, 8))
        hr_s[:, sl] = hr
        hi_s[:, sl] = hi

    for j in range(N_S5_BLOCKS):
        sl = slice(j * S5_LANE_BLOCK, (j + 1) * S5_LANE_BLOCK)
        cl = slice(j * LANES, (j + 1) * LANES)
        xr = bur_s[:, :, sl].reshape(rows, S5_LANE_BLOCK).astype(BF16)
        xi = bui_s[:, :, sl].reshape(rows, S5_LANE_BLOCK).astype(BF16)
        yj = (jnp.dot(xr, cre_ref[j], preferred_element_type=F32)
              - jnp.dot(xi, cim_ref[j], preferred_element_type=F32)
              + d_ref[:, cl] * u[:, cl])
        yg_s[:, cl] = jax.nn.gelu(yj)

    yg = yg_s[...]
    out = yg * _sigmoid(jnp.dot(yg.astype(BF16), wglu_ref[...], preferred_element_type=F32))
    y_ref[...] = out.reshape(tc, SUBLANES, D_A)
    hr_out[...] = hr_s[...]
    hi_out[...] = hi_s[...]


def _s5(u_tm, h0r, h0i, prm, tc):
    seq, bsz, _ = u_tm.shape
    grid = (bsz // SUBLANES, seq // tc)
    const = lambda shape: pl.BlockSpec(shape, lambda b, i: (0,) * len(shape))
    act = pl.BlockSpec((tc, SUBLANES, D_A), lambda b, i: (i, b, 0))
    st = pl.BlockSpec((SUBLANES, N_S5), lambda b, i: (b, 0))
    vec = const((1, N_S5))
    small = pltpu.VMEM((SUBLANES, N_S5), F32)
    big = pltpu.VMEM((tc, SUBLANES, N_S5), F32)
    return pl.pallas_call(
        functools.partial(_s5_kernel, tc=tc),
        grid=grid,
        in_specs=[act, st, st, vec, vec, vec,
                  const((N_S5_BLOCKS, LANES, S5_LANE_BLOCK)), const((N_S5_BLOCKS, LANES, S5_LANE_BLOCK)),
                  const((N_S5_BLOCKS, S5_LANE_BLOCK, LANES)), const((N_S5_BLOCKS, S5_LANE_BLOCK, LANES)),
                  const((1, D_A)), const((D_A, D_A))],
        out_specs=[act, st, st],
        out_shape=[jax.ShapeDtypeStruct((seq, bsz, D_A), F32),
                   jax.ShapeDtypeStruct((bsz, N_S5), F32),
                   jax.ShapeDtypeStruct((bsz, N_S5), F32)],
        scratch_shapes=[small, small, small, small, small, small, big, big,
                        pltpu.VMEM((tc * SUBLANES, D_A), F32)],
        compiler_params=_cparams(("parallel", "arbitrary"), 48),
    )(u_tm, h0r, h0i, prm["lr"], prm["li"], prm["ls"], prm["bre"], prm["bim"], prm["cre"], prm["cim"],
      prm["d"], prm["wglu"])


def _gdn_kernel(qkv_ref, ba_ref, z_ref, conv0_ref, s0_ref, cw_ref, alog_ref, dtb_ref, nw_ref,
                o_ref, sfin_ref, convn_ref, xbuf, s_s, *, t):
    pad = SUBLANES
    hist = CONV_W - 1

    @pl.when(pl.program_id(1) == 0)
    def _():
        xbuf[pad - hist:pad, :] = conv0_ref[...]
        s_s[...] = s0_ref[...]

    x = qkv_ref[...]
    xbuf[pad:pad + t, :] = x
    cw = cw_ref[...]
    acc = x * cw[hist:hist + 1, :]
    for j in range(hist):
        acc = acc + xbuf[pad - hist + j:pad - hist + j + t, :] * cw[j:j + 1, :]
    last = xbuf[pad + t - hist:pad + t, :]
    convn_ref[...] = last
    xbuf[pad - hist:pad, :] = last
    y = _silu(acc)

    ba = ba_ref[...]
    beta = _sigmoid(ba)
    g = -jnp.exp(alog_ref[...]) * _softplus(ba + dtb_ref[...])
    row, col = _iota2((t, t))
    causal = row >= col
    strict = row > col
    gc = _cumsum_rows(causal.astype(BF16), g)
    lane = lax.broadcasted_iota(jnp.int32, (t, LANES), 1)
    nw = nw_ref[...]
    z = z_ref[...]
    heads = range(H_B)

    bcol = [beta[:, h:h + 1] for h in heads]
    gcol = [gc[:, H_B + h:H_B + h + 1] for h in heads]
    decay = []
    for h in heads:
        p1, p2, p3 = [p.astype(F32) for p in _split3(gcol[h])]
        dl = jnp.where(lane == 0, p1, jnp.where(lane == 1, p2, jnp.where(lane == 2, p3,
                                                                         jnp.where(lane < 6, 1.0, 0.0))))
        dr = jnp.where(lane < 3, 1.0, jnp.where(lane == 3, -p1, jnp.where(lane == 4, -p2,
                                                                        jnp.where(lane == 5, -p3, 0.0))))
        decay.append(jnp.exp(jnp.where(causal, _bdot_nt(dl, dr), -jnp.inf)))
    eg = [jnp.exp(gcol[h]) for h in heads]
    glast = [gcol[h][t - 1:t, :] for h in heads]
    q = [y[:, h * DK_B:(h + 1) * DK_B] for h in heads]
    k = [y[:, H_B * DK_B + h * DK_B:H_B * DK_B + (h + 1) * DK_B] for h in heads]
    v = [y[:, 2 * H_B * DK_B + h * DV_B:2 * H_B * DK_B + (h + 1) * DV_B] for h in heads]
    q = [x * lax.rsqrt(jnp.sum(x * x, axis=-1, keepdims=True) + NORM_EPS) * (DK_B ** -0.5) for x in q]
    k = [x * lax.rsqrt(jnp.sum(x * x, axis=-1, keepdims=True) + NORM_EPS) for x in k]
    kb = [k[h] * bcol[h] for h in heads]
    vb = [v[h] * bcol[h] for h in heads]
    lm = [jnp.where(strict, _bdot_nt(kb[h], k[h]) * decay[h], 0.0) for h in heads]
    attn = [_bdot_nt(q[h], k[h]) * decay[h] for h in heads]
    s = [s_s[h] for h in heads]
    qs = [_bdot(q[h] * eg[h], s[h]) for h in heads]
    tinv = _inv_identity_plus(lm, t)
    u = [_bdot(tinv[h], vb[h]) for h in heads]
    w = [_bdot(tinv[h], kb[h] * eg[h]) for h in heads]
    v_new = [u[h] - _bdot(w[h], s[h]) for h in heads]
    o = [qs[h] + _bdot(attn[h], v_new[h]) for h in heads]
    upd = [_bdot_tn(k[h] * jnp.exp(glast[h] - gcol[h]), v_new[h]) for h in heads]
    for h in heads:
        s_new = s[h] * jnp.exp(glast[h]) + upd[h]
        s_s[h] = s_new
        sfin_ref[h] = s_new
        o_ref[:, h * DV_B:(h + 1) * DV_B] = _rmsnorm(o[h], nw) * _silu(z[:, h * DV_B:(h + 1) * DV_B])


def _gdn(qkv_tm, ba_tm, z_tm, conv0, s0, prm, bsz, t):
    seq = qkv_tm.shape[0]
    grid = (bsz, seq // t)
    const = lambda shape: pl.BlockSpec(shape, lambda b, i: (0,) * len(shape))
    tm = lambda n: pl.BlockSpec((t, n), lambda b, i: (i, b))
    return pl.pallas_call(
        functools.partial(_gdn_kernel, t=t),
        grid=grid,
        in_specs=[tm(D_QKV), tm(GATE_PAD), tm(D_B),
                  pl.BlockSpec((None, CONV_W - 1, D_QKV), lambda b, i: (b, 0, 0)),
                  pl.BlockSpec((None, H_B, DK_B, DV_B), lambda b, i: (b, 0, 0, 0)),
                  const((CONV_W, D_QKV)), const((1, GATE_PAD)), const((1, GATE_PAD)), const((1, DV_B))],
        out_specs=[tm(D_B),
                   pl.BlockSpec((None, H_B, DK_B, DV_B), lambda b, i: (b, 0, 0, 0)),
                   pl.BlockSpec((None, CONV_W - 1, D_QKV), lambda b, i: (b, 0, 0))],
        out_shape=[jax.ShapeDtypeStruct((seq, bsz * D_B), F32),
                   jax.ShapeDtypeStruct((bsz, H_B, DK_B, DV_B), F32),
                   jax.ShapeDtypeStruct((bsz, CONV_W - 1, D_QKV), F32)],
        scratch_shapes=[pltpu.VMEM((t + SUBLANES, D_QKV), F32), pltpu.VMEM((H_B, DK_B, DV_B), F32)],
        compiler_params=_cparams(("parallel", "arbitrary"), 32),
    )(qkv_tm, ba_tm, z_tm, conv0, s0, prm["conv_w"], prm["alog"], prm["dtb"], prm["norm_w"])


def _mix_ffn_kernel(*refs, n_mix, final_norm):
    x_ref = refs[0]
    mix_refs = refs[1:1 + n_mix]
    wout_refs = refs[1 + n_mix:1 + 2 * n_mix]
    nffn_ref, wg_ref, wu_ref, wd_ref = refs[1 + 2 * n_mix:5 + 2 * n_mix]
    rest = refs[5 + 2 * n_mix:]
    if final_norm:
        nfin_ref, out_ref, acc_s = rest
    else:
        out_ref, acc_s = rest

    x1 = x_ref[...]
    for m_ref, w_ref in zip(mix_refs, wout_refs):
        x1 = x1 + jnp.dot(m_ref[...].astype(BF16), w_ref[...], preferred_element_type=F32)
    h = _rmsnorm(x1, nffn_ref[...]).astype(BF16)
    acc_s[...] = x1

    def ff_tile(c, carry):
        gate = jnp.dot(h, wg_ref[c], preferred_element_type=F32)
        up = jnp.dot(h, wu_ref[c], preferred_element_type=F32)
        act = (_silu(gate) * up).astype(BF16)
        acc_s[...] += jnp.dot(act, wd_ref[c], preferred_element_type=F32)
        return carry

    lax.fori_loop(0, N_FF_TILES, ff_tile, 0)
    x2 = acc_s[...]
    out_ref[...] = _rmsnorm(x2, nfin_ref[...]) if final_norm else x2


def _mix_ffn(x, x_bm, mixes, wouts, nffn, wg, wu, wd, nfin, bv, lv, out_bm):
    tl = min(TOKEN_TILE, lv)
    grid = (bv, lv // tl)
    const = lambda shape: pl.BlockSpec(shape, lambda b, i: (0,) * len(shape),
                                       pipeline_mode=pl.Buffered(1))
    tm = lambda n: pl.BlockSpec((tl, n), lambda b, i: (i, b))
    bm = pl.BlockSpec((None, tl, D_MODEL), lambda b, i: (b, i, 0))
    in_specs = [bm if x_bm else tm(D_MODEL)]
    in_specs += [tm(m.shape[1] // bv) for m in mixes]
    in_specs += [const(w.shape) for w in wouts]
    in_specs += [const((1, D_MODEL)), const(wg.shape), const(wu.shape), const(wd.shape)]
    args = [x, *mixes, *wouts, nffn, wg, wu, wd]
    if nfin is not None:
        in_specs.append(const((1, D_MODEL)))
        args.append(nfin)
    if out_bm:
        out_spec, out_shape = bm, jax.ShapeDtypeStruct((bv, lv, D_MODEL), F32)
    else:
        out_spec, out_shape = tm(D_MODEL), jax.ShapeDtypeStruct((lv, bv * D_MODEL), F32)
    return pl.pallas_call(
        functools.partial(_mix_ffn_kernel, n_mix=len(mixes), final_norm=nfin is not None),
        grid=grid,
        in_specs=in_specs,
        out_specs=out_spec,
        out_shape=out_shape,
        scratch_shapes=[pltpu.VMEM((tl, D_MODEL), F32)],
        compiler_params=_cparams(("parallel", "parallel"), 56),
    )(*args)


def _rwkv_proj_kernel(x_ref, shift0_ref, nw_ref, maa_ref, wr_ref, wk_ref, wv_ref, w0_ref, w1_ref, w2_ref,
                      a0_ref, a1_ref, a2_ref, g1_ref, g2_ref, kk_ref, ka_ref,
                      r_out, w_out, k_out, v_out, kk_out, a_out, g_out, shift_out, carry_s, *, bsz):
    @pl.when(pl.program_id(0) == 0)
    def _():
        carry_s[...] = shift0_ref[...]

    h = _rmsnorm(x_ref[...], nw_ref[...])
    rows = h.shape[0]
    if rows > bsz:
        prev = jnp.concatenate([carry_s[...], h[:rows - bsz]], axis=0)
    else:
        prev = carry_s[...]
    last = h[rows - bsz:]
    carry_s[...] = last
    shift_out[...] = last
    xx = prev - h
    maa = maa_ref[...]
    mixed = [h + xx * maa[j:j + 1, :] for j in range(6)]
    xr, xw, xk, xv, xa, xg = mixed
    r = _bdot(xr, wr_ref[...])
    k = _bdot(xk, wk_ref[...])
    v = _bdot(xv, wv_ref[...])
    w = -_softplus(-(w0_ref[...] + _bdot(jnp.tanh(_bdot(xw, w1_ref[...])), w2_ref[...]))) - 0.5
    a = _sigmoid(a0_ref[...] + _bdot(_bdot(xa, a1_ref[...]), a2_ref[...]))
    g = _bdot(_sigmoid(_bdot(xg, g1_ref[...])), g2_ref[...])
    r_out[...] = r
    w_out[...] = -jnp.exp(w)
    k_out[...] = k * (1.0 + (a - 1.0) * ka_ref[...])
    v_out[...] = v
    kk_out[...] = k * kk_ref[...]
    a_out[...] = a
    g_out[...] = g


def _rwkv_proj(x_flat, shift0, prm, bsz):
    rows = x_flat.shape[0]
    tr = max(bsz, min(RWKV_PROJ_TILE, rows))
    grid = (rows // tr,)
    const = lambda shape: pl.BlockSpec(shape, lambda i: (0,) * len(shape))
    act = pl.BlockSpec((tr, D_MODEL), lambda i: (i, 0))
    vec = const((1, D_MODEL))
    weights = [prm[n] for n in ("wr", "wk", "wv")]
    in_specs = [act, const((bsz, D_MODEL)), vec, const((6, D_MODEL))]
    in_specs += [const(w.shape) for w in weights]
    in_specs += [vec, const(prm["w1"].shape), const(prm["w2"].shape),
                 vec, const(prm["a1"].shape), const(prm["a2"].shape),
                 const(prm["g1"].shape), const(prm["g2"].shape), vec, vec]
    out_act = jax.ShapeDtypeStruct((rows, D_MODEL), F32)
    return pl.pallas_call(
        functools.partial(_rwkv_proj_kernel, bsz=bsz),
        grid=grid,
        in_specs=in_specs,
        out_specs=[act] * 7 + [const((bsz, D_MODEL))],
        out_shape=[out_act] * 7 + [jax.ShapeDtypeStruct((bsz, D_MODEL), F32)],
        scratch_shapes=[pltpu.VMEM((bsz, D_MODEL), F32)],
        compiler_params=_cparams(("arbitrary",), 56),
    )(x_flat, shift0, prm["norm_w"], prm["maa"], prm["wr"], prm["wk"], prm["wv"], prm["w0"], prm["w1"],
      prm["w2"], prm["a0"], prm["a1"], prm["a2"], prm["g1"], prm["g2"], prm["k_k"], prm["k_a"])


def _rwkv_scan_kernel(r_ref, w_ref, k_ref, v_ref, kk_ref, a_ref, g_ref, s0_ref, rk_ref, lnw_ref, lnb_ref,
                      y_ref, sfin_ref, s_s, *, t):
    n = N_HEAD_C
    zero_blk = jnp.zeros((n, n), F32)

    @pl.when(pl.program_id(1) == 0)
    def _():
        for j in range(H_C // 2):
            top = jnp.concatenate([s0_ref[2 * j], zero_blk], axis=1)
            bot = jnp.concatenate([zero_blk, s0_ref[2 * j + 1]], axis=1)
            s_s[j] = jnp.concatenate([top, bot], axis=0)

    row, col = _iota2((t, t))
    incl = row >= col
    strict = row > col
    tri = incl.astype(BF16)
    lane = lax.broadcasted_iota(jnp.int32, (t, LANES), 1)
    head0 = lane < n
    prow, pcol = _iota2((LANES, LANES))
    same_head = (prow >> 6) == (pcol >> 6)
    ones_blk = same_head.astype(BF16)

    def pair_group(js):
        ps = range(len(js))
        sl = [slice(j * LANES, (j + 1) * LANES) for j in js]
        r = [r_ref[:, x] for x in sl]
        w = [w_ref[:, x] for x in sl]
        k = [k_ref[:, x] for x in sl]
        v = [v_ref[:, x] for x in sl]
        kkr = [kk_ref[:, x] for x in sl]
        a = [a_ref[:, x] for x in sl]
        kk = [x * lax.rsqrt(_split_dot(x * x, ones_blk) + NORM_EPS) for x in kkr]
        bv = [kk[p] * a[p] for p in ps]
        gc = [_cumsum_rows(tri, x) for x in w]
        glast = [x[t - 1:t, :] for x in gc]
        pinv = [jnp.exp(-x) for x in gc]
        at = [-kk[p] * jnp.exp(gc[p] - w[p]) for p in ps]
        bt = [bv[p] * pinv[p] for p in ps]
        kt = [k[p] * pinv[p] for p in ps]
        rt = [r[p] * jnp.exp(gc[p]) for p in ps]
        rem = [jnp.exp(glast[p] - gc[p]) for p in ps]
        lhs = [jnp.concatenate([jnp.where(head0, at[p], 0.0), jnp.where(head0, 0.0, at[p]),
                                jnp.where(head0, rt[p], 0.0), jnp.where(head0, 0.0, rt[p])], axis=0)
               for p in ps]
        gb = [_bdot_nt(lhs[p], bt[p]) for p in ps]
        gk = [_bdot_nt(lhs[p], kt[p]) for p in ps]
        s = [s_s[j] for j in js]
        ars = [_bdot_nt(jnp.concatenate([at[p], rt[p]], axis=0), s[p]) for p in ps]
        chains = [(p, hh) for p in ps for hh in range(2)]
        aab = [jnp.where(strict, gb[p][hh * t:(hh + 1) * t], 0.0) for p, hh in chains]
        aak = [jnp.where(strict, gk[p][hh * t:(hh + 1) * t], 0.0) for p, hh in chains]
        rhs = [ars[p][:t] + _bdot(aak[c], v[p]) for c, (p, hh) in enumerate(chains)]
        tinv = _inv_identity_plus([-x for x in aab], t)
        us = [_bdot(tinv[c], rhs[c]) for c in range(len(chains))]
        u = [jnp.where(head0, us[2 * p], us[2 * p + 1]) for p in ps]
        rb = [jnp.where(incl, gb[p][(2 + hh) * t:(3 + hh) * t], 0.0) for p, hh in chains]
        rkm = [jnp.where(incl, gk[p][(2 + hh) * t:(3 + hh) * t], 0.0) for p, hh in chains]
        ys = [_bdot(rb[c], u[p]) + _bdot(rkm[c], v[p]) for c, (p, hh) in enumerate(chains)]
        y = [ars[p][t:] + jnp.where(head0, ys[2 * p], ys[2 * p + 1]) for p in ps]
        upd = [_bdot_tn(jnp.concatenate([u[p], v[p]], axis=0),
                        jnp.concatenate([bv[p] * rem[p], k[p] * rem[p]], axis=0)) for p in ps]
        for p, j in enumerate(js):
            s_s[j] = s[p] * jnp.exp(glast[p]) + jnp.where(same_head, upd[p], 0.0)

        mu = [_split_dot(x, ones_blk) * (1.0 / n) for x in y]
        d = [y[p] - mu[p] for p in ps]
        var = [_split_dot(x * x, ones_blk) * (1.0 / n) for x in d]
        bonus = [_split_dot(r[p] * k[p] * rk_ref[:, sl[p]], ones_blk) for p in ps]
        for p in ps:
            yn = d[p] * lax.rsqrt(var[p] + RWKV_GN_EPS) * lnw_ref[:, sl[p]] + lnb_ref[:, sl[p]]
            y_ref[:, sl[p]] = (yn + bonus[p] * v[p]) * g_ref[:, sl[p]]

    n_pairs = H_C // 2
    for first in range(0, n_pairs, RWKV_PAIR_GROUP):
        pair_group(list(range(first, first + RWKV_PAIR_GROUP)))

    for j in range(H_C // 2):
        sp = s_s[j]
        sfin_ref[2 * j] = sp[:n, :n]
        sfin_ref[2 * j + 1] = sp[n:, n:]


def _rwkv_scan(acts, s0, prm, bsz, t):
    seq = acts[0].shape[0]
    grid = (bsz, seq // t)
    const = lambda shape: pl.BlockSpec(shape, lambda b, i: (0,) * len(shape))
    tm = pl.BlockSpec((t, D_MODEL), lambda b, i: (i, b))
    st = pl.BlockSpec((None, H_C, N_HEAD_C, N_HEAD_C), lambda b, i: (b, 0, 0, 0))
    vec = const((1, D_MODEL))
    return pl.pallas_call(
        functools.partial(_rwkv_scan_kernel, t=t),
        grid=grid,
        in_specs=[tm] * 7 + [st, vec, vec, vec],
        out_specs=[tm, st],
        out_shape=[jax.ShapeDtypeStruct((seq, bsz * D_MODEL), F32),
                   jax.ShapeDtypeStruct((bsz, H_C, N_HEAD_C, N_HEAD_C), F32)],
        scratch_shapes=[pltpu.VMEM((H_C // 2, LANES, LANES), F32)],
        compiler_params=_cparams(("parallel", "arbitrary"), 32),
    )(*acts, s0, prm["r_k"], prm["ln_w"], prm["ln_b"])


def _block_diag_groups(w, rows_per_group, cols_per_group):
    g = w.shape[0]
    w = w.reshape(g // 8, 8, rows_per_group, cols_per_group)
    eye = jnp.eye(8, dtype=w.dtype)
    out = jnp.einsum("jgrc,gh->jgrhc", w, eye)
    return out.reshape(g // 8, 8 * rows_per_group, 8 * cols_per_group)


def _pad_to(w, axis, size):
    pad = [(0, 0)] * w.ndim
    pad[axis] = (0, size - w.shape[axis])
    return jnp.pad(w, pad)


def _row(v):
    return v.reshape(1, -1).astype(F32)


def _trunk(x_bm, bsz, seq, s5_re0, s5_im0, gdn_s0, gdn_conv0, rw_s0, rw_shift0, p, gdn_t, rwkv_t, s5_tc,
           out_bm):
    bv, lv, _ = x_bm.shape
    u, qkv, z, ba = _ab_in(x_bm, p["norm_mix0"], p["ab_w_in"])
    y_a, hr, hi = _s5(u.reshape(seq, bsz, D_A), s5_re0, s5_im0, p["s5"], s5_tc)
    y_b, gdn_s, gdn_conv = _gdn(qkv.reshape(seq, bsz * D_QKV), ba.reshape(seq, bsz * GATE_PAD),
                                z.reshape(seq, bsz * D_B), gdn_conv0, gdn_s0, p["gdn"], bsz, gdn_t)
    x1 = _mix_ffn(x_bm, True, [y_a.reshape(lv, bv * D_A), y_b.reshape(lv, bv * D_B)],
                  [p["ab_w_out_a"], p["ab_w_out_b"]], p["norm_ffn0"], p["wg0"], p["wu0"], p["wd0"],
                  None, bv, lv, False)
    acts = _rwkv_proj(x1.reshape(seq * bsz, D_MODEL), rw_shift0, p["rw"], bsz)
    shift = acts[7]
    yg, rw_s = _rwkv_scan([a.reshape(seq, bsz * D_MODEL) for a in acts[:7]], rw_s0, p["rw"], bsz, rwkv_t)
    y = _mix_ffn(x1, False, [yg.reshape(lv, bv * D_MODEL)], [p["rw"]["wo"]], p["norm_ffn1"], p["wg1"],
                 p["wu1"], p["wd1"], p["norm_final"], bv, lv, out_bm)
    return y, hr, hi, gdn_s, gdn_conv, rw_s, shift


def kernel(x_prompt, x_sample, state_s5_re, state_s5_im, state_gdn, state_gdn_conv, state_rwkv, state_rwkv_shift, norm_mix, norm_ffn, norm_final, ffn_w_gate, ffn_w_up, ffn_w_down, ab_w_in, ab_w_out, s5_lambda_re, s5_lambda_im, s5_log_step, s5_B_re, s5_B_im, s5_C_re, s5_C_im, s5_D, s5_w_glu, gdn_conv_w, gdn_A_log, gdn_dt_bias, gdn_norm_w, rw_maa, rw_w_r, rw_w_k, rw_w_v, rw_w_o, rw_w0, rw_w1, rw_w2, rw_a0, rw_a1, rw_a2, rw_g1, rw_g2, rw_k_k, rw_k_a, rw_r_k, rw_ln_w, rw_ln_b):
    bsz_p, seq_p, _ = x_prompt.shape
    bsz_s, seq_s, _ = x_sample.shape

    w_in = ab_w_in[0]
    n_main = D_A + D_QKV
    w_in = jnp.concatenate([w_in[:, :n_main], w_in[:, n_main + 2 * H_B:], w_in[:, n_main:n_main + 2 * H_B]],
                           axis=1)
    w_in = _pad_to(w_in, 1, D_IN_PAD).astype(BF16)

    def ff(w, layer):
        return jnp.transpose(w[layer].reshape(D_MODEL, N_FF_TILES, FF_TILE), (1, 0, 2)).astype(BF16)

    gate_lanes = lambda v: _pad_to(jnp.concatenate([jnp.zeros((H_B,), F32), v.astype(F32)]), 0,
                                   GATE_PAD).reshape(1, GATE_PAD)
    p = dict(
        norm_mix0=_row(norm_mix[0]), norm_ffn0=_row(norm_ffn[0]), norm_ffn1=_row(norm_ffn[1]),
        norm_final=_row(norm_final), ab_w_in=w_in,
        ab_w_out_a=ab_w_out[0][:D_A].astype(BF16), ab_w_out_b=ab_w_out[0][D_A:].astype(BF16),
        wg0=ff(ffn_w_gate, 0), wu0=ff(ffn_w_up, 0),
        wd0=ffn_w_down[0].reshape(N_FF_TILES, FF_TILE, D_MODEL).astype(BF16),
        wg1=ff(ffn_w_gate, 1), wu1=ff(ffn_w_up, 1),
        wd1=ffn_w_down[1].reshape(N_FF_TILES, FF_TILE, D_MODEL).astype(BF16),
        s5=dict(
            lr=_row(s5_lambda_re[0]), li=_row(s5_lambda_im[0]),
            ls=_row(jnp.repeat(s5_log_step[0], P_STATE)),
            bre=_block_diag_groups(jnp.swapaxes(s5_B_re[0], 1, 2), S5_GROUP, P_STATE).astype(BF16),
            bim=_block_diag_groups(jnp.swapaxes(s5_B_im[0], 1, 2), S5_GROUP, P_STATE).astype(BF16),
            cre=_block_diag_groups(jnp.swapaxes(s5_C_re[0], 1, 2), P_STATE, S5_GROUP).astype(BF16),
            cim=_block_diag_groups(jnp.swapaxes(s5_C_im[0], 1, 2), P_STATE, S5_GROUP).astype(BF16),
            d=_row(s5_D[0]), wglu=s5_w_glu[0].astype(BF16)),
        gdn=dict(conv_w=gdn_conv_w[0].astype(F32), alog=gate_lanes(gdn_A_log[0]),
                 dtb=gate_lanes(gdn_dt_bias[0]), norm_w=_row(gdn_norm_w[0])),
        rw=dict(
            norm_w=_row(norm_mix[1]), maa=rw_maa[0].astype(F32),
            wr=rw_w_r[0].astype(BF16), wk=rw_w_k[0].astype(BF16), wv=rw_w_v[0].astype(BF16),
            wo=rw_w_o[0].astype(BF16),
            w0=_row(rw_w0[0]), w1=_pad_to(rw_w1[0], 1, LANES).astype(BF16),
            w2=_pad_to(rw_w2[0], 0, LANES).astype(BF16),
            a0=_row(rw_a0[0]), a1=_pad_to(rw_a1[0], 1, LANES).astype(BF16),
            a2=_pad_to(rw_a2[0], 0, LANES).astype(BF16),
            g1=_pad_to(rw_g1[0], 1, 2 * LANES).astype(BF16), g2=_pad_to(rw_g2[0], 0, 2 * LANES).astype(BF16),
            k_k=_row(rw_k_k[0]), k_a=_row(rw_k_a[0]), r_k=_row(rw_r_k[0]),
            ln_w=_row(rw_ln_w[0]), ln_b=_row(rw_ln_b[0])),
    )

    zeros = lambda *shape: jnp.zeros(shape, F32)
    yp, p_hr, p_hi, p_gdn, p_conv, p_rw, p_shift = _trunk(
        x_prompt, bsz_p, seq_p, zeros(bsz_p, N_S5), zeros(bsz_p, N_S5), zeros(bsz_p, H_B, DK_B, DV_B),
        zeros(bsz_p, CONV_W - 1, D_QKV), zeros(bsz_p, H_C, N_HEAD_C, N_HEAD_C), zeros(bsz_p, D_MODEL), p,
        GDN_CHUNK, RWKV_CHUNK, 64, True)

    xs_tm = jnp.transpose(x_sample, (1, 0, 2)).reshape(1, seq_s * bsz_s, D_MODEL)
    ys, s_hr, s_hi, s_gdn, s_conv, s_rw, s_shift = _trunk(
        xs_tm, bsz_s, seq_s, state_s5_re[0].reshape(bsz_s, N_S5), state_s5_im[0].reshape(bsz_s, N_S5),
        state_gdn[0], state_gdn_conv[0], state_rwkv[0], state_rwkv_shift[0], p,
        seq_s, seq_s, seq_s, True)
    y_sample = jnp.transpose(ys.reshape(seq_s, bsz_s, D_MODEL), (1, 0, 2))

    s5_shape = lambda b: (1, b, G_A, P_STATE)
    return (yp, y_sample,
            p_hr.reshape(s5_shape(bsz_p)), p_hi.reshape(s5_shape(bsz_p)), p_gdn[None], p_conv[None],
            p_rw[None], p_shift[None],
            s_hr.reshape(s5_shape(bsz_s)), s_hi.reshape(s5_shape(bsz_s)), s_gdn[None], s_conv[None],
            s_rw[None], s_shift[None])
```

```python
import functools
import math

import jax
import jax.numpy as jnp
from jax import lax
from jax.experimental import pallas as pl
from jax.experimental.pallas import tpu as pltpu

F32 = jnp.float32
BF16 = jnp.bfloat16

D_MODEL = 1024
D_A = 512
S5_GROUP = 16
G_A = 32
P_STATE = 64
N_S5 = G_A * P_STATE
D_B = 512
H_B = 4
DK_B = 128
DV_B = 128
D_QKV = 2 * H_B * DK_B + D_B
CONV_W = 4
GDN_CHUNK = 64
N_HEAD_C = 64
H_C = 16
RWKV_CHUNK = 64
RWKV_PAIR_GROUP = 4
RWKV_GN_EPS = 64e-5
D_FF = 2816
NORM_EPS = 1e-6

LANES = 128
SUBLANES = 8
FF_TILE = 256
N_FF_TILES = D_FF // FF_TILE
GATE_PAD = LANES
D_IN_PAD = 4 * 512 + 512 + GATE_PAD
S5_LANE_BLOCK = 512
N_S5_BLOCKS = N_S5 // S5_LANE_BLOCK
TOKEN_TILE = 512
RWKV_PROJ_TILE = 256
MIB = 1024 * 1024


def _cparams(semantics, vmem_mib):
    return pltpu.CompilerParams(dimension_semantics=semantics, vmem_limit_bytes=vmem_mib * MIB)


def _rmsnorm(x, w):
    return x * lax.rsqrt(jnp.mean(x * x, axis=-1, keepdims=True) + NORM_EPS) * w


def _sigmoid(x):
    return 1.0 / (1.0 + jnp.exp(-x))


def _silu(x):
    return x * _sigmoid(x)


def _softplus(x):
    return jnp.maximum(x, 0.0) + jnp.log1p(jnp.exp(-jnp.abs(x)))


def _bdot(a, b):
    return jnp.dot(a.astype(BF16), b.astype(BF16), preferred_element_type=F32)


def _bdot_nt(a, b):
    return lax.dot_general(a.astype(BF16), b.astype(BF16), (((1,), (1,)), ((), ())),
                           preferred_element_type=F32)


def _bdot_tn(a, b):
    return lax.dot_general(a.astype(BF16), b.astype(BF16), (((0,), (0,)), ((), ())),
                           preferred_element_type=F32)


def _split(a):
    hi = a.astype(BF16)
    return hi, (a - hi.astype(F32)).astype(BF16)


def _split3(a):
    p1 = a.astype(BF16)
    r1 = a - p1.astype(F32)
    p2 = r1.astype(BF16)
    return p1, p2, (r1 - p2.astype(F32)).astype(BF16)


def _dot3(a, b):
    dot = lambda x, y: jnp.dot(x, y, preferred_element_type=F32)
    return dot(a[0], b[0]) + dot(a[0], b[1]) + dot(a[1], b[0])


def _split_dot(a, exact_bf16):
    hi, lo = _split(a)
    return (jnp.dot(hi, exact_bf16, preferred_element_type=F32)
            + jnp.dot(lo, exact_bf16, preferred_element_type=F32))


def _cumsum_rows(tri_bf16, w):
    return sum(jnp.dot(tri_bf16, part, preferred_element_type=F32) for part in _split3(w))


def _iota2(shape):
    return (lax.broadcasted_iota(jnp.int32, shape, 0), lax.broadcasted_iota(jnp.int32, shape, 1))


def _inv_identity_plus(lms, t):
    row, col = _iota2((t, t))
    eye = (row == col).astype(F32)
    base = min(t, 16)
    if t > base:
        same_base = (row >> 4) == (col >> 4)
        ns = [jnp.where(same_base, -lm, 0.0) for lm in lms]
    else:
        ns = [-lm for lm in lms]
    xs = [eye + n for n in ns]
    ps = ns
    k = 1
    while 2 * k < base:
        ps = [_dot3(sp, sp) for sp in [_split(p) for p in ps]]
        sps = [_split(p) for p in ps]
        xs = [x + _dot3(_split(x), sp) for x, sp in zip(xs, sps)]
        k *= 2
    shift = 4
    blk = base
    while blk < t:
        same_big = (row >> (shift + 1)) == (col >> (shift + 1))
        same_small = (row >> shift) == (col >> shift)
        off = [_split(jnp.where(same_big, jnp.where(same_small, 0.0, lm), 0.0)) for lm in lms]
        sxs = [_split(x) for x in xs]
        mids = [_split(_dot3(o, sx)) for o, sx in zip(off, sxs)]
        xs = [x - _dot3(sx, m) for x, sx, m in zip(xs, sxs, mids)]
        blk *= 2
        shift += 1
    return xs


def _ab_in_kernel(x_ref, nw_ref, w_ref, u_ref, qkv_ref, z_ref, ba_ref):
    h = _rmsnorm(x_ref[...], nw_ref[...])
    p = jnp.dot(h.astype(BF16), w_ref[...], preferred_element_type=F32)
    u_ref[...] = p[:, 0:D_A]
    qkv_ref[...] = p[:, D_A:D_A + D_QKV]
    z_ref[...] = p[:, D_A + D_QKV:D_A + D_QKV + D_B]
    ba_ref[...] = p[:, D_A + D_QKV + D_B:D_IN_PAD]


def _ab_in(x_bm, norm_w, w_in):
    bv, lv, _ = x_bm.shape
    tl = min(TOKEN_TILE, lv)
    grid = (bv, lv // tl)
    tm = lambda n: pl.BlockSpec((tl, n), lambda b, i: (i, b))
    const = lambda shape: pl.BlockSpec(shape, lambda b, i: (0,) * len(shape))
    return pl.pallas_call(
        _ab_in_kernel,
        grid=grid,
        in_specs=[pl.BlockSpec((None, tl, D_MODEL), lambda b, i: (b, i, 0)),
                  const((1, D_MODEL)), const((D_MODEL, D_IN_PAD))],
        out_specs=[tm(D_A), tm(D_QKV), tm(D_B), tm(GATE_PAD)],
        out_shape=[jax.ShapeDtypeStruct((lv, bv * D_A), F32),
                   jax.ShapeDtypeStruct((lv, bv * D_QKV), F32),
                   jax.ShapeDtypeStruct((lv, bv * D_B), F32),
                   jax.ShapeDtypeStruct((lv, bv * GATE_PAD), F32)],
        compiler_params=_cparams(("parallel", "parallel"), 48),
    )(x_bm, norm_w, w_in)


def _s5_kernel(u_ref, h0r_ref, h0i_ref, lr_ref, li_ref, ls_ref, bre_ref, bim_ref, cre_ref, cim_ref,
               d_ref, wglu_ref, y_ref, hr_out, hi_out,
               ar_s, ai_s, cr_s, ci_s, hr_s, hi_s, bur_s, bui_s, yg_s, *, tc):
    rows = tc * SUBLANES

    @pl.when(pl.program_id(1) == 0)
    def _():
        lr = lr_ref[...]
        li = li_ref[...]
        dt = jnp.exp(ls_ref[...])
        mag = jnp.exp(lr * dt)
        ar = mag * jnp.cos(li * dt)
        ai = mag * jnp.sin(li * dt)
        den = lr * lr + li * li
        nr = ar - 1.0
        cr = (nr * lr + ai * li) / den
        ci = (ai * lr - nr * li) / den
        ar_s[...] = jnp.broadcast_to(ar, (SUBLANES, N_S5))
        ai_s[...] = jnp.broadcast_to(ai, (SUBLANES, N_S5))
        cr_s[...] = jnp.broadcast_to(cr, (SUBLANES, N_S5))
        ci_s[...] = jnp.broadcast_to(ci, (SUBLANES, N_S5))
        hr_s[...] = h0r_ref[...]
        hi_s[...] = h0i_ref[...]

    u = u_ref[...].reshape(rows, D_A)
    ub = u.astype(BF16)
    for j in range(N_S5_BLOCKS):
        sl = slice(j * S5_LANE_BLOCK, (j + 1) * S5_LANE_BLOCK)
        uj = ub[:, j * LANES:(j + 1) * LANES]
        pr = jnp.dot(uj, bre_ref[j], preferred_element_type=F32)
        pi = jnp.dot(uj, bim_ref[j], preferred_element_type=F32)
        crj = cr_s[0:1, sl]
        cij = ci_s[0:1, sl]
        bur_s[:, :, sl] = (crj * pr - cij * pi).reshape(tc, SUBLANES, S5_LANE_BLOCK)
        bui_s[:, :, sl] = (crj * pi + cij * pr).reshape(tc, SUBLANES, S5_LANE_BLOCK)

    for j in range(N_S5_BLOCKS):
        sl = slice(j * S5_LANE_BLOCK, (j + 1) * S5_LANE_BLOCK)
        ar = ar_s[:, sl]
        ai = ai_s[:, sl]

        def step(t, carry, sl=sl, ar=ar, ai=ai):
            hr, hi = carry
            nr = ar * hr - ai * hi + bur_s[t, :, sl]
            ni = ar * hi + ai * hr + bui_s[t, :, sl]
            bur_s[t, :, sl] = nr
            bui_s[t, :, sl] = ni
            return nr, ni

        hr, hi = lax.fori_loop(0, tc, step, (hr_s[:, sl], hi_s[:, sl]), unroll=min(tc, 8))
        hr_s[:, sl] = hr
        hi_s[:, sl] = hi

    for j in range(N_S5_BLOCKS):
        sl = slice(j * S5_LANE_BLOCK, (j + 1) * S5_LANE_BLOCK)
        cl = slice(j * LANES, (j + 1) * LANES)
        xr = bur_s[:, :, sl].reshape(rows, S5_LANE_BLOCK).astype(BF16)
        xi = bui_s[:, :, sl].reshape(rows, S5_LANE_BLOCK).astype(BF16)
        yj = (jnp.dot(xr, cre_ref[j], preferred_element_type=F32)
              - jnp.dot(xi, cim_ref[j], preferred_element_type=F32)
              + d_ref[:, cl] * u[:, cl])
        yg_s[:, cl] = jax.nn.gelu(yj)

    yg = yg_s[...]
    out = yg * _sigmoid(jnp.dot(yg.astype(BF16), wglu_ref[...], preferred_element_type=F32))
    y_ref[...] = out.reshape(tc, SUBLANES, D_A)
    hr_out[...] = hr_s[...]
    hi_out[...] = hi_s[...]


def _s5(u_tm, h0r, h0i, prm, tc):
    seq, bsz, _ = u_tm.shape
    grid = (bsz // SUBLANES, seq // tc)
    const = lambda shape: pl.BlockSpec(shape, lambda b, i: (0,) * len(shape))
    act = pl.BlockSpec((tc, SUBLANES, D_A), lambda b, i: (i, b, 0))
    st = pl.BlockSpec((SUBLANES, N_S5), lambda b, i: (b, 0))
    vec = const((1, N_S5))
    small = pltpu.VMEM((SUBLANES, N_S5), F32)
    big = pltpu.VMEM((tc, SUBLANES, N_S5), F32)
    return pl.pallas_call(
        functools.partial(_s5_kernel, tc=tc),
        grid=grid,
        in_specs=[act, st, st, vec, vec, vec,
                  const((N_S5_BLOCKS, LANES, S5_LANE_BLOCK)), const((N_S5_BLOCKS, LANES, S5_LANE_BLOCK)),
                  const((N_S5_BLOCKS, S5_LANE_BLOCK, LANES)), const((N_S5_BLOCKS, S5_LANE_BLOCK, LANES)),
                  const((1, D_A)), const((D_A, D_A))],
        out_specs=[act, st, st],
        out_shape=[jax.ShapeDtypeStruct((seq, bsz, D_A), F32),
                   jax.ShapeDtypeStruct((bsz, N_S5), F32),
                   jax.ShapeDtypeStruct((bsz, N_S5), F32)],
        scratch_shapes=[small, small, small, small, small, small, big, big,
                        pltpu.VMEM((tc * SUBLANES, D_A), F32)],
        compiler_params=_cparams(("parallel", "arbitrary"), 48),
    )(u_tm, h0r, h0i, prm["lr"], prm["li"], prm["ls"], prm["bre"], prm["bim"], prm["cre"], prm["cim"],
      prm["d"], prm["wglu"])


def _gdn_kernel(qkv_ref, ba_ref, z_ref, conv0_ref, s0_ref, cw_ref, alog_ref, dtb_ref, nw_ref,
                o_ref, sfin_ref, convn_ref, xbuf, s_s, *, t, nb):
    pad = SUBLANES
    hist = CONV_W - 1

    @pl.when(pl.program_id(1) == 0)
    def _():
        for e in range(nb):
            xbuf[pad - hist:pad, e * D_QKV:(e + 1) * D_QKV] = conv0_ref[e]
        s_s[...] = s0_ref[...].reshape(nb * H_B, DK_B, DV_B)

    row, col = _iota2((t, t))
    causal = row >= col
    strict = row > col
    tri = causal.astype(BF16)
    lane = lax.broadcasted_iota(jnp.int32, (t, LANES), 1)
    nw = nw_ref[...]
    cw = cw_ref[...]
    ys, betas, gcs = [], [], []
    for e in range(nb):
        cols = slice(e * D_QKV, (e + 1) * D_QKV)
        x = qkv_ref[:, cols]
        xbuf[pad:pad + t, cols] = x
        acc = x * cw[hist:hist + 1, :]
        for j in range(hist):
            acc = acc + xbuf[pad - hist + j:pad - hist + j + t, cols] * cw[j:j + 1, :]
        last = xbuf[pad + t - hist:pad + t, cols]
        convn_ref[e] = last
        xbuf[pad - hist:pad, cols] = last
        ys.append(_silu(acc))
        ba = ba_ref[:, e * GATE_PAD:(e + 1) * GATE_PAD]
        betas.append(_sigmoid(ba))
        g = -jnp.exp(alog_ref[...]) * _softplus(ba + dtb_ref[...])
        gcs.append(_cumsum_rows(tri, g))

    chains = [(e, h) for e in range(nb) for h in range(H_B)]
    heads = range(len(chains))
    bcol = [betas[e][:, h:h + 1] for e, h in chains]
    gcol = [gcs[e][:, H_B + h:H_B + h + 1] for e, h in chains]
    y_of = [ys[e] for e, h in chains]
    h_of = [h for e, h in chains]
    decay = []
    for h in heads:
        p1, p2, p3 = [p.astype(F32) for p in _split3(gcol[h])]
        dl = jnp.where(lane == 0, p1, jnp.where(lane == 1, p2, jnp.where(lane == 2, p3,
                                                                         jnp.where(lane < 6, 1.0, 0.0))))
        dr = jnp.where(lane < 3, 1.0, jnp.where(lane == 3, -p1, jnp.where(lane == 4, -p2,
                                                                        jnp.where(lane == 5, -p3, 0.0))))
        decay.append(jnp.exp(jnp.where(causal, _bdot_nt(dl, dr), -jnp.inf)))
    eg = [jnp.exp(gcol[h]) for h in heads]
    glast = [gcol[h][t - 1:t, :] for h in heads]
    q = [y_of[c][:, h_of[c] * DK_B:(h_of[c] + 1) * DK_B] for c in heads]
    k = [y_of[c][:, (H_B + h_of[c]) * DK_B:(H_B + h_of[c] + 1) * DK_B] for c in heads]
    v = [y_of[c][:, 2 * H_B * DK_B + h_of[c] * DV_B:2 * H_B * DK_B + (h_of[c] + 1) * DV_B] for c in heads]
    q = [x * lax.rsqrt(jnp.sum(x * x, axis=-1, keepdims=True) + NORM_EPS) * (DK_B ** -0.5) for x in q]
    k = [x * lax.rsqrt(jnp.sum(x * x, axis=-1, keepdims=True) + NORM_EPS) for x in k]
    kb = [k[h] * bcol[h] for h in heads]
    vb = [v[h] * bcol[h] for h in heads]
    lm = [jnp.where(strict, _bdot_nt(kb[h], k[h]) * decay[h], 0.0) for h in heads]
    attn = [_bdot_nt(q[h], k[h]) * decay[h] for h in heads]
    s = [s_s[h] for h in heads]
    qs = [_bdot(q[h] * eg[h], s[h]) for h in heads]
    tinv = _inv_identity_plus(lm, t)
    u = [_bdot(tinv[h], vb[h]) for h in heads]
    w = [_bdot(tinv[h], kb[h] * eg[h]) for h in heads]
    v_new = [u[h] - _bdot(w[h], s[h]) for h in heads]
    o = [qs[h] + _bdot(attn[h], v_new[h]) for h in heads]
    upd = [_bdot_tn(k[h] * jnp.exp(glast[h] - gcol[h]), v_new[h]) for h in heads]
    for c, (e, h) in enumerate(chains):
        s_new = s[c] * jnp.exp(glast[c]) + upd[c]
        s_s[c] = s_new
        sfin_ref[e, h] = s_new
        cols = slice(e * D_B + h * DV_B, e * D_B + (h + 1) * DV_B)
        o_ref[:, cols] = _rmsnorm(o[c], nw) * _silu(z_ref[:, cols])


def _gdn(qkv_tm, ba_tm, z_tm, conv0, s0, prm, bsz, t, nb):
    seq = qkv_tm.shape[0]
    grid = (bsz // nb, seq // t)
    const = lambda shape: pl.BlockSpec(shape, lambda b, i: (0,) * len(shape))
    tm = lambda n: pl.BlockSpec((t, nb * n), lambda b, i: (i, b))
    state = pl.BlockSpec((nb, H_B, DK_B, DV_B), lambda b, i: (b, 0, 0, 0))
    conv = pl.BlockSpec((nb, CONV_W - 1, D_QKV), lambda b, i: (b, 0, 0))
    return pl.pallas_call(
        functools.partial(_gdn_kernel, t=t, nb=nb),
        grid=grid,
        in_specs=[tm(D_QKV), tm(GATE_PAD), tm(D_B), conv, state,
                  const((CONV_W, D_QKV)), const((1, GATE_PAD)), const((1, GATE_PAD)), const((1, DV_B))],
        out_specs=[tm(D_B), state, conv],
        out_shape=[jax.ShapeDtypeStruct((seq, bsz * D_B), F32),
                   jax.ShapeDtypeStruct((bsz, H_B, DK_B, DV_B), F32),
                   jax.ShapeDtypeStruct((bsz, CONV_W - 1, D_QKV), F32)],
        scratch_shapes=[pltpu.VMEM((t + SUBLANES, nb * D_QKV), F32),
                        pltpu.VMEM((nb * H_B, DK_B, DV_B), F32)],
        compiler_params=_cparams(("parallel", "arbitrary"), 48),
    )(qkv_tm, ba_tm, z_tm, conv0, s0, prm["conv_w"], prm["alog"], prm["dtb"], prm["norm_w"])


def _mix_ffn_kernel(*refs, n_mix, final_norm):
    x_ref = refs[0]
    mix_refs = refs[1:1 + n_mix]
    wout_refs = refs[1 + n_mix:1 + 2 * n_mix]
    nffn_ref, wg_ref, wu_ref, wd_ref = refs[1 + 2 * n_mix:5 + 2 * n_mix]
    rest = refs[5 + 2 * n_mix:]
    if final_norm:
        nfin_ref, out_ref, acc_s = rest
    else:
        out_ref, acc_s = rest

    x1 = x_ref[...]
    for m_ref, w_ref in zip(mix_refs, wout_refs):
        x1 = x1 + jnp.dot(m_ref[...].astype(BF16), w_ref[...], preferred_element_type=F32)
    h = _rmsnorm(x1, nffn_ref[...]).astype(BF16)
    acc_s[...] = x1

    def ff_tile(c, carry):
        gate = jnp.dot(h, wg_ref[c], preferred_element_type=F32)
        up = jnp.dot(h, wu_ref[c], preferred_element_type=F32)
        act = (_silu(gate) * up).astype(BF16)
        acc_s[...] += jnp.dot(act, wd_ref[c], preferred_element_type=F32)
        return carry

    lax.fori_loop(0, N_FF_TILES, ff_tile, 0)
    x2 = acc_s[...]
    out_ref[...] = _rmsnorm(x2, nfin_ref[...]) if final_norm else x2


def _mix_ffn(x, x_bm, mixes, wouts, nffn, wg, wu, wd, nfin, bv, lv, out_bm):
    tl = min(TOKEN_TILE, lv)
    grid = (bv, lv // tl)
    const = lambda shape: pl.BlockSpec(shape, lambda b, i: (0,) * len(shape),
                                       pipeline_mode=pl.Buffered(1))
    tm = lambda n: pl.BlockSpec((tl, n), lambda b, i: (i, b))
    bm = pl.BlockSpec((None, tl, D_MODEL), lambda b, i: (b, i, 0))
    in_specs = [bm if x_bm else tm(D_MODEL)]
    in_specs += [tm(m.shape[1] // bv) for m in mixes]
    in_specs += [const(w.shape) for w in wouts]
    in_specs += [const((1, D_MODEL)), const(wg.shape), const(wu.shape), const(wd.shape)]
    args = [x, *mixes, *wouts, nffn, wg, wu, wd]
    if nfin is not None:
        in_specs.append(const((1, D_MODEL)))
        args.append(nfin)
    if out_bm:
        out_spec, out_shape = bm, jax.ShapeDtypeStruct((bv, lv, D_MODEL), F32)
    else:
        out_spec, out_shape = tm(D_MODEL), jax.ShapeDtypeStruct((lv, bv * D_MODEL), F32)
    return pl.pallas_call(
        functools.partial(_mix_ffn_kernel, n_mix=len(mixes), final_norm=nfin is not None),
        grid=grid,
        in_specs=in_specs,
        out_specs=out_spec,
        out_shape=out_shape,
        scratch_shapes=[pltpu.VMEM((tl, D_MODEL), F32)],
        compiler_params=_cparams(("parallel", "parallel"), 56),
    )(*args)


def _rwkv_proj_kernel(x_ref, shift0_ref, nw_ref, maa_ref, wr_ref, wk_ref, wv_ref, w0_ref, w1_ref, w2_ref,
                      a0_ref, a1_ref, a2_ref, g1_ref, g2_ref, kk_ref, ka_ref,
                      r_out, w_out, k_out, v_out, kk_out, a_out, g_out, shift_out, hbuf, *, shift):
    pad = -(-shift // SUBLANES) * SUBLANES

    @pl.when(pl.program_id(1) == 0)
    def _():
        hbuf[pad - shift:pad, :] = shift0_ref[...]

    h = _rmsnorm(x_ref[...], nw_ref[...])
    rows = h.shape[0]
    hbuf[pad:pad + rows, :] = h
    prev = hbuf[pad - shift:pad - shift + rows, :]
    last = h[rows - shift:]
    hbuf[pad - shift:pad, :] = last
    shift_out[...] = last
    xx = prev - h
    maa = maa_ref[...]
    mixed = [h + xx * maa[j:j + 1, :] for j in range(6)]
    xr, xw, xk, xv, xa, xg = mixed
    r = _bdot(xr, wr_ref[...])
    k = _bdot(xk, wk_ref[...])
    v = _bdot(xv, wv_ref[...])
    w = -_softplus(-(w0_ref[...] + _bdot(jnp.tanh(_bdot(xw, w1_ref[...])), w2_ref[...]))) - 0.5
    a = _sigmoid(a0_ref[...] + _bdot(_bdot(xa, a1_ref[...]), a2_ref[...]))
    g = _bdot(_sigmoid(_bdot(xg, g1_ref[...])), g2_ref[...])
    r_out[...] = r
    w_out[...] = -jnp.exp(w)
    k_out[...] = k * (1.0 + (a - 1.0) * ka_ref[...])
    v_out[...] = v
    kk_out[...] = k * kk_ref[...]
    a_out[...] = a
    g_out[...] = g


def _rwkv_proj(x_tm, shift0, prm):
    bv, shift, _ = shift0.shape
    lv = x_tm.shape[0]
    tr = max(shift, min(RWKV_PROJ_TILE, lv))
    grid = (bv, lv // tr)
    const = lambda shape: pl.BlockSpec(shape, lambda b, i: (0,) * len(shape))
    act = pl.BlockSpec((tr, D_MODEL), lambda b, i: (i, b))
    carry = pl.BlockSpec((None, shift, D_MODEL), lambda b, i: (b, 0, 0))
    vec = const((1, D_MODEL))
    weights = [prm[n] for n in ("wr", "wk", "wv")]
    in_specs = [act, carry, vec, const((6, D_MODEL))]
    in_specs += [const(w.shape) for w in weights]
    in_specs += [vec, const(prm["w1"].shape), const(prm["w2"].shape),
                 vec, const(prm["a1"].shape), const(prm["a2"].shape),
                 const(prm["g1"].shape), const(prm["g2"].shape), vec, vec]
    out_act = jax.ShapeDtypeStruct((lv, bv * D_MODEL), F32)
    pad = -(-shift // SUBLANES) * SUBLANES
    return pl.pallas_call(
        functools.partial(_rwkv_proj_kernel, shift=shift),
        grid=grid,
        in_specs=in_specs,
        out_specs=[act] * 7 + [carry],
        out_shape=[out_act] * 7 + [jax.ShapeDtypeStruct((bv, shift, D_MODEL), F32)],
        scratch_shapes=[pltpu.VMEM((pad + tr, D_MODEL), F32)],
        compiler_params=_cparams(("parallel", "arbitrary"), 56),
    )(x_tm, shift0, prm["norm_w"], prm["maa"], prm["wr"], prm["wk"], prm["wv"], prm["w0"], prm["w1"],
      prm["w2"], prm["a0"], prm["a1"], prm["a2"], prm["g1"], prm["g2"], prm["k_k"], prm["k_a"])


def _rwkv_scan_kernel(r_ref, w_ref, k_ref, v_ref, kk_ref, a_ref, g_ref, s0_ref, rk_ref, lnw_ref, lnb_ref,
                      y_ref, sfin_ref, s_s, *, t, nb, group):
    n = N_HEAD_C
    zero_blk = jnp.zeros((n, n), F32)

    pairs_per_seq = H_C // 2

    @pl.when(pl.program_id(1) == 0)
    def _():
        for jj in range(nb * pairs_per_seq):
            e, j = divmod(jj, pairs_per_seq)
            top = jnp.concatenate([s0_ref[e, 2 * j], zero_blk], axis=1)
            bot = jnp.concatenate([zero_blk, s0_ref[e, 2 * j + 1]], axis=1)
            s_s[jj] = jnp.concatenate([top, bot], axis=0)

    row, col = _iota2((t, t))
    incl = row >= col
    strict = row > col
    tri = incl.astype(BF16)
    lane = lax.broadcasted_iota(jnp.int32, (t, LANES), 1)
    head0 = lane < n
    prow, pcol = _iota2((LANES, LANES))
    same_head = (prow >> 6) == (pcol >> 6)
    ones_blk = same_head.astype(BF16)

    def pair_group(js):
        ps = range(len(js))
        sl = [slice(j * LANES, (j + 1) * LANES) for j in js]
        psl = [slice((j % pairs_per_seq) * LANES, (j % pairs_per_seq + 1) * LANES) for j in js]
        r = [r_ref[:, x] for x in sl]
        w = [w_ref[:, x] for x in sl]
        k = [k_ref[:, x] for x in sl]
        v = [v_ref[:, x] for x in sl]
        kkr = [kk_ref[:, x] for x in sl]
        a = [a_ref[:, x] for x in sl]
        kk = [x * lax.rsqrt(_split_dot(x * x, ones_blk) + NORM_EPS) for x in kkr]
        bv = [kk[p] * a[p] for p in ps]
        gc = [_cumsum_rows(tri, x) for x in w]
        glast = [x[t - 1:t, :] for x in gc]
        pinv = [jnp.exp(-x) for x in gc]
        at = [-kk[p] * jnp.exp(gc[p] - w[p]) for p in ps]
        bt = [bv[p] * pinv[p] for p in ps]
        kt = [k[p] * pinv[p] for p in ps]
        rt = [r[p] * jnp.exp(gc[p]) for p in ps]
        rem = [jnp.exp(glast[p] - gc[p]) for p in ps]
        lhs = [jnp.concatenate([jnp.where(head0, at[p], 0.0), jnp.where(head0, 0.0, at[p]),
                                jnp.where(head0, rt[p], 0.0), jnp.where(head0, 0.0, rt[p])], axis=0)
               for p in ps]
        gb = [_bdot_nt(lhs[p], bt[p]) for p in ps]
        gk = [_bdot_nt(lhs[p], kt[p]) for p in ps]
        s = [s_s[j] for j in js]
        ars = [_bdot_nt(jnp.concatenate([at[p], rt[p]], axis=0), s[p]) for p in ps]
        chains = [(p, hh) for p in ps for hh in range(2)]
        aab = [jnp.where(strict, gb[p][hh * t:(hh + 1) * t], 0.0) for p, hh in chains]
        aak = [jnp.where(strict, gk[p][hh * t:(hh + 1) * t], 0.0) for p, hh in chains]
        rhs = [ars[p][:t] + _bdot(aak[c], v[p]) for c, (p, hh) in enumerate(chains)]
        tinv = _inv_identity_plus([-x for x in aab], t)
        us = [_bdot(tinv[c], rhs[c]) for c in range(len(chains))]
        u = [jnp.where(head0, us[2 * p], us[2 * p + 1]) for p in ps]
        rb = [jnp.where(incl, gb[p][(2 + hh) * t:(3 + hh) * t], 0.0) for p, hh in chains]
        rkm = [jnp.where(incl, gk[p][(2 + hh) * t:(3 + hh) * t], 0.0) for p, hh in chains]
        ys = [_bdot(rb[c], u[p]) + _bdot(rkm[c], v[p]) for c, (p, hh) in enumerate(chains)]
        y = [ars[p][t:] + jnp.where(head0, ys[2 * p], ys[2 * p + 1]) for p in ps]
        upd = [_bdot_tn(jnp.concatenate([u[p], v[p]], axis=0),
                        jnp.concatenate([bv[p] * rem[p], k[p] * rem[p]], axis=0)) for p in ps]
        for p, j in enumerate(js):
            s_s[j] = s[p] * jnp.exp(glast[p]) + jnp.where(same_head, upd[p], 0.0)

        mu = [_split_dot(x, ones_blk) * (1.0 / n) for x in y]
        d = [y[p] - mu[p] for p in ps]
        var = [_split_dot(x * x, ones_blk) * (1.0 / n) for x in d]
        bonus = [_split_dot(r[p] * k[p] * rk_ref[:, psl[p]], ones_blk) for p in ps]
        for p in ps:
            yn = d[p] * lax.rsqrt(var[p] + RWKV_GN_EPS) * lnw_ref[:, psl[p]] + lnb_ref[:, psl[p]]
            y_ref[:, sl[p]] = (yn + bonus[p] * v[p]) * g_ref[:, sl[p]]

    for first in range(0, nb * pairs_per_seq, group):
        pair_group(list(range(first, first + group)))

    for jj in range(nb * pairs_per_seq):
        e, j = divmod(jj, pairs_per_seq)
        sp = s_s[jj]
        sfin_ref[e, 2 * j] = sp[:n, :n]
        sfin_ref[e, 2 * j + 1] = sp[n:, n:]


def _rwkv_scan(acts, s0, prm, bsz, t, nb, group):
    seq = acts[0].shape[0]
    grid = (bsz // nb, seq // t)
    const = lambda shape: pl.BlockSpec(shape, lambda b, i: (0,) * len(shape))
    tm = pl.BlockSpec((t, nb * D_MODEL), lambda b, i: (i, b))
    st = pl.BlockSpec((nb, H_C, N_HEAD_C, N_HEAD_C), lambda b, i: (b, 0, 0, 0))
    vec = const((1, D_MODEL))
    return pl.pallas_call(
        functools.partial(_rwkv_scan_kernel, t=t, nb=nb, group=group),
        grid=grid,
        in_specs=[tm] * 7 + [st, vec, vec, vec],
        out_specs=[tm, st],
        out_shape=[jax.ShapeDtypeStruct((seq, bsz * D_MODEL), F32),
                   jax.ShapeDtypeStruct((bsz, H_C, N_HEAD_C, N_HEAD_C), F32)],
        scratch_shapes=[pltpu.VMEM((nb * H_C // 2, LANES, LANES), F32)],
        compiler_params=_cparams(("parallel", "arbitrary"), 48),
    )(*acts, s0, prm["r_k"], prm["ln_w"], prm["ln_b"])


def _block_diag_groups(w, rows_per_group, cols_per_group):
    g = w.shape[0]
    w = w.reshape(g // 8, 8, rows_per_group, cols_per_group)
    eye = jnp.eye(8, dtype=w.dtype)
    out = jnp.einsum("jgrc,gh->jgrhc", w, eye)
    return out.reshape(g // 8, 8 * rows_per_group, 8 * cols_per_group)


def _pad_to(w, axis, size):
    pad = [(0, 0)] * w.ndim
    pad[axis] = (0, size - w.shape[axis])
    return jnp.pad(w, pad)


def _row(v):
    return v.reshape(1, -1).astype(F32)


def _recurrence_tiling(seq):
    if seq >= GDN_CHUNK:
        return dict(s5_tc=64, gdn_t=GDN_CHUNK, gdn_nb=2, rwkv_t=RWKV_CHUNK, rwkv_nb=1,
                    rwkv_group=RWKV_PAIR_GROUP)
    return dict(s5_tc=seq, gdn_t=seq, gdn_nb=8, rwkv_t=seq, rwkv_nb=4, rwkv_group=2 * RWKV_PAIR_GROUP)


def _trunk(x_bm, bsz, seq, s5_re0, s5_im0, gdn_s0, gdn_conv0, rw_s0, rw_shift0, p, out_bm):
    bv, lv, _ = x_bm.shape
    cfg = _recurrence_tiling(seq)
    u, qkv, z, ba = _ab_in(x_bm, p["norm_mix0"], p["ab_w_in"])
    y_a, hr, hi = _s5(u.reshape(seq, bsz, D_A), s5_re0, s5_im0, p["s5"], cfg["s5_tc"])
    y_b, gdn_s, gdn_conv = _gdn(qkv.reshape(seq, bsz * D_QKV), ba.reshape(seq, bsz * GATE_PAD),
                                z.reshape(seq, bsz * D_B), gdn_conv0, gdn_s0, p["gdn"], bsz,
                                cfg["gdn_t"], cfg["gdn_nb"])
    x1 = _mix_ffn(x_bm, True, [y_a.reshape(lv, bv * D_A), y_b.reshape(lv, bv * D_B)],
                  [p["ab_w_out_a"], p["ab_w_out_b"]], p["norm_ffn0"], p["wg0"], p["wu0"], p["wd0"],
                  None, bv, lv, False)
    acts = _rwkv_proj(x1, rw_shift0.reshape(bv, bsz // bv, D_MODEL), p["rw"])
    shift = acts[7].reshape(bsz, D_MODEL)
    yg, rw_s = _rwkv_scan([a.reshape(seq, bsz * D_MODEL) for a in acts[:7]], rw_s0, p["rw"], bsz,
                          cfg["rwkv_t"], cfg["rwkv_nb"], cfg["rwkv_group"])
    y = _mix_ffn(x1, False, [yg.reshape(lv, bv * D_MODEL)], [p["rw"]["wo"]], p["norm_ffn1"], p["wg1"],
                 p["wu1"], p["wd1"], p["norm_final"], bv, lv, out_bm)
    return y, hr, hi, gdn_s, gdn_conv, rw_s, shift


def kernel(x_prompt, x_sample, state_s5_re, state_s5_im, state_gdn, state_gdn_conv, state_rwkv, state_rwkv_shift, norm_mix, norm_ffn, norm_final, ffn_w_gate, ffn_w_up, ffn_w_down, ab_w_in, ab_w_out, s5_lambda_re, s5_lambda_im, s5_log_step, s5_B_re, s5_B_im, s5_C_re, s5_C_im, s5_D, s5_w_glu, gdn_conv_w, gdn_A_log, gdn_dt_bias, gdn_norm_w, rw_maa, rw_w_r, rw_w_k, rw_w_v, rw_w_o, rw_w0, rw_w1, rw_w2, rw_a0, rw_a1, rw_a2, rw_g1, rw_g2, rw_k_k, rw_k_a, rw_r_k, rw_ln_w, rw_ln_b):
    bsz_p, seq_p, _ = x_prompt.shape
    bsz_s, seq_s, _ = x_sample.shape

    w_in = ab_w_in[0]
    n_main = D_A + D_QKV
    w_in = jnp.concatenate([w_in[:, :n_main], w_in[:, n_main + 2 * H_B:], w_in[:, n_main:n_main + 2 * H_B]],
                           axis=1)
    w_in = _pad_to(w_in, 1, D_IN_PAD).astype(BF16)

    def ff(w, layer):
        return jnp.transpose(w[layer].reshape(D_MODEL, N_FF_TILES, FF_TILE), (1, 0, 2)).astype(BF16)

    gate_lanes = lambda v: _pad_to(jnp.concatenate([jnp.zeros((H_B,), F32), v.astype(F32)]), 0,
                                   GATE_PAD).reshape(1, GATE_PAD)
    p = dict(
        norm_mix0=_row(norm_mix[0]), norm_ffn0=_row(norm_ffn[0]), norm_ffn1=_row(norm_ffn[1]),
        norm_final=_row(norm_final), ab_w_in=w_in,
        ab_w_out_a=ab_w_out[0][:D_A].astype(BF16), ab_w_out_b=ab_w_out[0][D_A:].astype(BF16),
        wg0=ff(ffn_w_gate, 0), wu0=ff(ffn_w_up, 0),
        wd0=ffn_w_down[0].reshape(N_FF_TILES, FF_TILE, D_MODEL).astype(BF16),
        wg1=ff(ffn_w_gate, 1), wu1=ff(ffn_w_up, 1),
        wd1=ffn_w_down[1].reshape(N_FF_TILES, FF_TILE, D_MODEL).astype(BF16),
        s5=dict(
            lr=_row(s5_lambda_re[0]), li=_row(s5_lambda_im[0]),
            ls=_row(jnp.repeat(s5_log_step[0], P_STATE)),
            bre=_block_diag_groups(jnp.swapaxes(s5_B_re[0], 1, 2), S5_GROUP, P_STATE).astype(BF16),
            bim=_block_diag_groups(jnp.swapaxes(s5_B_im[0], 1, 2), S5_GROUP, P_STATE).astype(BF16),
            cre=_block_diag_groups(jnp.swapaxes(s5_C_re[0], 1, 2), P_STATE, S5_GROUP).astype(BF16),
            cim=_block_diag_groups(jnp.swapaxes(s5_C_im[0], 1, 2), P_STATE, S5_GROUP).astype(BF16),
            d=_row(s5_D[0]), wglu=s5_w_glu[0].astype(BF16)),
        gdn=dict(conv_w=gdn_conv_w[0].astype(F32), alog=gate_lanes(gdn_A_log[0]),
                 dtb=gate_lanes(gdn_dt_bias[0]), norm_w=_row(gdn_norm_w[0])),
        rw=dict(
            norm_w=_row(norm_mix[1]), maa=rw_maa[0].astype(F32),
            wr=rw_w_r[0].astype(BF16), wk=rw_w_k[0].astype(BF16), wv=rw_w_v[0].astype(BF16),
            wo=rw_w_o[0].astype(BF16),
            w0=_row(rw_w0[0]), w1=_pad_to(rw_w1[0], 1, LANES).astype(BF16),
            w2=_pad_to(rw_w2[0], 0, LANES).astype(BF16),
            a0=_row(rw_a0[0]), a1=_pad_to(rw_a1[0], 1, LANES).astype(BF16),
            a2=_pad_to(rw_a2[0], 0, LANES).astype(BF16),
            g1=_pad_to(rw_g1[0], 1, 2 * LANES).astype(BF16), g2=_pad_to(rw_g2[0], 0, 2 * LANES).astype(BF16),
            k_k=_row(rw_k_k[0]), k_a=_row(rw_k_a[0]), r_k=_row(rw_r_k[0]),
            ln_w=_row(rw_ln_w[0]), ln_b=_row(rw_ln_b[0])),
    )

    zeros = lambda *shape: jnp.zeros(shape, F32)
    yp, p_hr, p_hi, p_gdn, p_conv, p_rw, p_shift = _trunk(
        x_prompt, bsz_p, seq_p, zeros(bsz_p, N_S5), zeros(bsz_p, N_S5), zeros(bsz_p, H_B, DK_B, DV_B),
        zeros(bsz_p, CONV_W - 1, D_QKV), zeros(bsz_p, H_C, N_HEAD_C, N_HEAD_C), zeros(bsz_p, D_MODEL), p,
        True)

    xs_tm = jnp.transpose(x_sample, (1, 0, 2)).reshape(1, seq_s * bsz_s, D_MODEL)
    ys, s_hr, s_hi, s_gdn, s_conv, s_rw, s_shift = _trunk(
        xs_tm, bsz_s, seq_s, state_s5_re[0].reshape(bsz_s, N_S5), state_s5_im[0].reshape(bsz_s, N_S5),
        state_gdn[0], state_gdn_conv[0], state_rwkv[0], state_rwkv_shift[0], p, True)
    y_sample = jnp.transpose(ys.reshape(seq_s, bsz_s, D_MODEL), (1, 0, 2))

    s5_shape = lambda b: (1, b, G_A, P_STATE)
    return (yp, y_sample,
            p_hr.reshape(s5_shape(bsz_p)), p_hi.reshape(s5_shape(bsz_p)), p_gdn[None], p_conv[None],
            p_rw[None], p_shift[None],
            s_hr.reshape(s5_shape(bsz_s)), s_hi.reshape(s5_shape(bsz_s)), s_gdn[None], s_conv[None],
            s_rw[None], s_shift[None])
```

```python
import functools
import math

import jax
import jax.numpy as jnp
from jax import lax
from jax.experimental import pallas as pl
from jax.experimental.pallas import tpu as pltpu

F32 = jnp.float32
BF16 = jnp.bfloat16

D_MODEL = 1024
D_A = 512
S5_GROUP = 16
G_A = 32
P_STATE = 64
N_S5 = G_A * P_STATE
D_B = 512
H_B = 4
DK_B = 128
DV_B = 128
D_QKV = 2 * H_B * DK_B + D_B
CONV_W = 4
GDN_CHUNK = 64
N_HEAD_C = 64
H_C = 16
RWKV_CHUNK = 64
RWKV_PAIR_GROUP = 8
RWKV_GN_EPS = 64e-5
D_FF = 2816
NORM_EPS = 1e-6

LANES = 128
SUBLANES = 8
FF_TILE = 256
N_FF_TILES = D_FF // FF_TILE
GATE_PAD = LANES
D_IN_PAD = 4 * 512 + 512 + GATE_PAD
S5_LANE_BLOCK = 512
N_S5_BLOCKS = N_S5 // S5_LANE_BLOCK
TOKEN_TILE = 512
RWKV_PROJ_TILE = 256
MIB = 1024 * 1024


def _cparams(semantics, vmem_mib):
    return pltpu.CompilerParams(dimension_semantics=semantics, vmem_limit_bytes=vmem_mib * MIB)


def _rmsnorm(x, w):
    return x * lax.rsqrt(jnp.mean(x * x, axis=-1, keepdims=True) + NORM_EPS) * w


def _sigmoid(x):
    return 1.0 / (1.0 + jnp.exp(-x))


def _silu(x):
    return x * _sigmoid(x)


def _softplus(x):
    return jnp.maximum(x, 0.0) + jnp.log1p(jnp.exp(-jnp.abs(x)))


def _bdot(a, b):
    return jnp.dot(a.astype(BF16), b.astype(BF16), preferred_element_type=F32)


def _bdot_nt(a, b):
    return lax.dot_general(a.astype(BF16), b.astype(BF16), (((1,), (1,)), ((), ())),
                           preferred_element_type=F32)


def _bdot_tn(a, b):
    return lax.dot_general(a.astype(BF16), b.astype(BF16), (((0,), (0,)), ((), ())),
                           preferred_element_type=F32)


def _split(a):
    hi = a.astype(BF16)
    return hi, (a - hi.astype(F32)).astype(BF16)


def _split3(a):
    p1 = a.astype(BF16)
    r1 = a - p1.astype(F32)
    p2 = r1.astype(BF16)
    return p1, p2, (r1 - p2.astype(F32)).astype(BF16)


def _dot3(a, b):
    dot = lambda x, y: jnp.dot(x, y, preferred_element_type=F32)
    return dot(a[0], b[0]) + dot(a[0], b[1]) + dot(a[1], b[0])


def _split_dot(a, exact_bf16):
    hi, lo = _split(a)
    return (jnp.dot(hi, exact_bf16, preferred_element_type=F32)
            + jnp.dot(lo, exact_bf16, preferred_element_type=F32))


def _cumsum_rows(tri_bf16, w):
    return sum(jnp.dot(tri_bf16, part, preferred_element_type=F32) for part in _split3(w))


def _iota2(shape):
    return (lax.broadcasted_iota(jnp.int32, shape, 0), lax.broadcasted_iota(jnp.int32, shape, 1))


def _inv_identity_plus(lms, t):
    row, col = _iota2((t, t))
    eye = (row == col).astype(F32)
    base = min(t, 16)
    if t > base:
        same_base = (row >> 4) == (col >> 4)
        ns = [jnp.where(same_base, -lm, 0.0) for lm in lms]
    else:
        ns = [-lm for lm in lms]
    xs = [eye + n for n in ns]
    ps = ns
    k = 1
    while 2 * k < base:
        ps = [_bdot(p, p) for p in ps]
        xs = [x + _bdot(x, p) for x, p in zip(xs, ps)]
        k *= 2
    shift = 4
    blk = base
    while blk < t:
        same_big = (row >> (shift + 1)) == (col >> (shift + 1))
        same_small = (row >> shift) == (col >> shift)
        off = [jnp.where(same_big, jnp.where(same_small, 0.0, lm), 0.0) for lm in lms]
        mids = [_bdot(o, x) for o, x in zip(off, xs)]
        xs = [x - _bdot(x, m) for x, m in zip(xs, mids)]
        blk *= 2
        shift += 1
    return xs


def _ab_in_kernel(x_ref, nw_ref, w_ref, u_ref, qkv_ref, z_ref, ba_ref):
    h = _rmsnorm(x_ref[...], nw_ref[...])
    p = jnp.dot(h.astype(BF16), w_ref[...], preferred_element_type=F32)
    u_ref[...] = p[:, 0:D_A]
    qkv_ref[...] = p[:, D_A:D_A + D_QKV]
    z_ref[...] = p[:, D_A + D_QKV:D_A + D_QKV + D_B]
    ba_ref[...] = p[:, D_A + D_QKV + D_B:D_IN_PAD]


def _ab_in(x_bm, norm_w, w_in):
    bv, lv, _ = x_bm.shape
    tl = min(TOKEN_TILE, lv)
    grid = (bv, lv // tl)
    tm = lambda n: pl.BlockSpec((tl, n), lambda b, i: (i, b))
    const = lambda shape: pl.BlockSpec(shape, lambda b, i: (0,) * len(shape))
    return pl.pallas_call(
        _ab_in_kernel,
        grid=grid,
        in_specs=[pl.BlockSpec((None, tl, D_MODEL), lambda b, i: (b, i, 0)),
                  const((1, D_MODEL)), const((D_MODEL, D_IN_PAD))],
        out_specs=[tm(D_A), tm(D_QKV), tm(D_B), tm(GATE_PAD)],
        out_shape=[jax.ShapeDtypeStruct((lv, bv * D_A), F32),
                   jax.ShapeDtypeStruct((lv, bv * D_QKV), F32),
                   jax.ShapeDtypeStruct((lv, bv * D_B), F32),
                   jax.ShapeDtypeStruct((lv, bv * GATE_PAD), F32)],
        compiler_params=_cparams(("parallel", "parallel"), 48),
    )(x_bm, norm_w, w_in)


def _s5_kernel(u_ref, h0r_ref, h0i_ref, lr_ref, li_ref, ls_ref, bre_ref, bim_ref, cre_ref, cim_ref,
               d_ref, wglu_ref, y_ref, hr_out, hi_out,
               ar_s, ai_s, cr_s, ci_s, hr_s, hi_s, bur_s, bui_s, yg_s, *, tc):
    rows = tc * SUBLANES

    @pl.when(pl.program_id(1) == 0)
    def _():
        lr = lr_ref[...]
        li = li_ref[...]
        dt = jnp.exp(ls_ref[...])
        mag = jnp.exp(lr * dt)
        ar = mag * jnp.cos(li * dt)
        ai = mag * jnp.sin(li * dt)
        den = lr * lr + li * li
        nr = ar - 1.0
        cr = (nr * lr + ai * li) / den
        ci = (ai * lr - nr * li) / den
        ar_s[...] = jnp.broadcast_to(ar, (SUBLANES, N_S5))
        ai_s[...] = jnp.broadcast_to(ai, (SUBLANES, N_S5))
        cr_s[...] = jnp.broadcast_to(cr, (SUBLANES, N_S5))
        ci_s[...] = jnp.broadcast_to(ci, (SUBLANES, N_S5))
        hr_s[...] = h0r_ref[...]
        hi_s[...] = h0i_ref[...]

    u = u_ref[...].reshape(rows, D_A)
    ub = u.astype(BF16)
    for j in range(N_S5_BLOCKS):
        sl = slice(j * S5_LANE_BLOCK, (j + 1) * S5_LANE_BLOCK)
        uj = ub[:, j * LANES:(j + 1) * LANES]
        pr = jnp.dot(uj, bre_ref[j], preferred_element_type=F32)
        pi = jnp.dot(uj, bim_ref[j], preferred_element_type=F32)
        crj = cr_s[0:1, sl]
        cij = ci_s[0:1, sl]
        bur_s[:, :, sl] = (crj * pr - cij * pi).reshape(tc, SUBLANES, S5_LANE_BLOCK)
        bui_s[:, :, sl] = (crj * pi + cij * pr).reshape(tc, SUBLANES, S5_LANE_BLOCK)

    for j in range(N_S5_BLOCKS):
        sl = slice(j * S5_LANE_BLOCK, (j + 1) * S5_LANE_BLOCK)
        ar = ar_s[:, sl]
        ai = ai_s[:, sl]

        def step(t, carry, sl=sl, ar=ar, ai=ai):
            hr, hi = carry
            nr = ar * hr - ai * hi + bur_s[t, :, sl]
            ni = ar * hi + ai * hr + bui_s[t, :, sl]
            bur_s[t, :, sl] = nr
            bui_s[t, :, sl] = ni
            return nr, ni

        hr, hi = lax.fori_loop(0, tc, step, (hr_s[:, sl], hi_s[:, sl]), unroll=min(tc, 8))
        hr_s[:, sl] = hr
        hi_s[:, sl] = hi

    for j in range(N_S5_BLOCKS):
        sl = slice(j * S5_LANE_BLOCK, (j + 1) * S5_LANE_BLOCK)
        cl = slice(j * LANES, (j + 1) * LANES)
        xr = bur_s[:, :, sl].reshape(rows, S5_LANE_BLOCK).astype(BF16)
        xi = bui_s[:, :, sl].reshape(rows, S5_LANE_BLOCK).astype(BF16)
        yj = (jnp.dot(xr, cre_ref[j], preferred_element_type=F32)
              - jnp.dot(xi, cim_ref[j], preferred_element_type=F32)
              + d_ref[:, cl] * u[:, cl])
        yg_s[:, cl] = jax.nn.gelu(yj)

    yg = yg_s[...]
    out = yg * _sigmoid(jnp.dot(yg.astype(BF16), wglu_ref[...], preferred_element_type=F32))
    y_ref[...] = out.reshape(tc, SUBLANES, D_A)
    hr_out[...] = hr_s[...]
    hi_out[...] = hi_s[...]


def _s5(u_tm, h0r, h0i, prm, tc):
    seq, bsz, _ = u_tm.shape
    grid = (bsz // SUBLANES, seq // tc)
    const = lambda shape: pl.BlockSpec(shape, lambda b, i: (0,) * len(shape))
    act = pl.BlockSpec((tc, SUBLANES, D_A), lambda b, i: (i, b, 0))
    st = pl.BlockSpec((SUBLANES, N_S5), lambda b, i: (b, 0))
    vec = const((1, N_S5))
    small = pltpu.VMEM((SUBLANES, N_S5), F32)
    big = pltpu.VMEM((tc, SUBLANES, N_S5), F32)
    return pl.pallas_call(
        functools.partial(_s5_kernel, tc=tc),
        grid=grid,
        in_specs=[act, st, st, vec, vec, vec,
                  const((N_S5_BLOCKS, LANES, S5_LANE_BLOCK)), const((N_S5_BLOCKS, LANES, S5_LANE_BLOCK)),
                  const((N_S5_BLOCKS, S5_LANE_BLOCK, LANES)), const((N_S5_BLOCKS, S5_LANE_BLOCK, LANES)),
                  const((1, D_A)), const((D_A, D_A))],
        out_specs=[act, st, st],
        out_shape=[jax.ShapeDtypeStruct((seq, bsz, D_A), F32),
                   jax.ShapeDtypeStruct((bsz, N_S5), F32),
                   jax.ShapeDtypeStruct((bsz, N_S5), F32)],
        scratch_shapes=[small, small, small, small, small, small, big, big,
                        pltpu.VMEM((tc * SUBLANES, D_A), F32)],
        compiler_params=_cparams(("parallel", "arbitrary"), 48),
    )(u_tm, h0r, h0i, prm["lr"], prm["li"], prm["ls"], prm["bre"], prm["bim"], prm["cre"], prm["cim"],
      prm["d"], prm["wglu"])


def _gdn_kernel(qkv_ref, ba_ref, z_ref, conv0_ref, s0_ref, cw_ref, alog_ref, dtb_ref, nw_ref,
                o_ref, sfin_ref, convn_ref, xbuf, s_s, *, t, nb):
    pad = SUBLANES
    hist = CONV_W - 1

    @pl.when(pl.program_id(1) == 0)
    def _():
        for e in range(nb):
            xbuf[pad - hist:pad, e * D_QKV:(e + 1) * D_QKV] = conv0_ref[e]
        s_s[...] = s0_ref[...].reshape(nb * H_B, DK_B, DV_B)

    row, col = _iota2((t, t))
    causal = row >= col
    strict = row > col
    tri = causal.astype(BF16)
    lane = lax.broadcasted_iota(jnp.int32, (t, LANES), 1)
    nw = nw_ref[...]
    cw = cw_ref[...]
    ys, betas, gcs = [], [], []
    for e in range(nb):
        cols = slice(e * D_QKV, (e + 1) * D_QKV)
        x = qkv_ref[:, cols]
        xbuf[pad:pad + t, cols] = x
        acc = x * cw[hist:hist + 1, :]
        for j in range(hist):
            acc = acc + xbuf[pad - hist + j:pad - hist + j + t, cols] * cw[j:j + 1, :]
        last = xbuf[pad + t - hist:pad + t, cols]
        convn_ref[e] = last
        xbuf[pad - hist:pad, cols] = last
        ys.append(_silu(acc))
        ba = ba_ref[:, e * GATE_PAD:(e + 1) * GATE_PAD]
        betas.append(_sigmoid(ba))
        g = -jnp.exp(alog_ref[...]) * _softplus(ba + dtb_ref[...])
        gcs.append(_cumsum_rows(tri, g))

    chains = [(e, h) for e in range(nb) for h in range(H_B)]
    heads = range(len(chains))
    bcol = [betas[e][:, h:h + 1] for e, h in chains]
    gcol = [gcs[e][:, H_B + h:H_B + h + 1] for e, h in chains]
    y_of = [ys[e] for e, h in chains]
    h_of = [h for e, h in chains]
    decay = []
    for h in heads:
        p1, p2, p3 = [p.astype(F32) for p in _split3(gcol[h])]
        dl = jnp.where(lane == 0, p1, jnp.where(lane == 1, p2, jnp.where(lane == 2, p3,
                                                                         jnp.where(lane < 6, 1.0, 0.0))))
        dr = jnp.where(lane < 3, 1.0, jnp.where(lane == 3, -p1, jnp.where(lane == 4, -p2,
                                                                        jnp.where(lane == 5, -p3, 0.0))))
        decay.append(jnp.exp(jnp.where(causal, _bdot_nt(dl, dr), -jnp.inf)))
    eg = [jnp.exp(gcol[h]) for h in heads]
    glast = [gcol[h][t - 1:t, :] for h in heads]
    q = [y_of[c][:, h_of[c] * DK_B:(h_of[c] + 1) * DK_B] for c in heads]
    k = [y_of[c][:, (H_B + h_of[c]) * DK_B:(H_B + h_of[c] + 1) * DK_B] for c in heads]
    v = [y_of[c][:, 2 * H_B * DK_B + h_of[c] * DV_B:2 * H_B * DK_B + (h_of[c] + 1) * DV_B] for c in heads]
    q = [x * lax.rsqrt(jnp.sum(x * x, axis=-1, keepdims=True) + NORM_EPS) * (DK_B ** -0.5) for x in q]
    k = [x * lax.rsqrt(jnp.sum(x * x, axis=-1, keepdims=True) + NORM_EPS) for x in k]
    kb = [k[h] * bcol[h] for h in heads]
    vb = [v[h] * bcol[h] for h in heads]
    lm = [jnp.where(strict, _bdot_nt(kb[h], k[h]) * decay[h], 0.0) for h in heads]
    attn = [_bdot_nt(q[h], k[h]) * decay[h] for h in heads]
    s = [s_s[h] for h in heads]
    qs = [_bdot(q[h] * eg[h], s[h]) for h in heads]
    tinv = _inv_identity_plus(lm, t)
    u = [_bdot(tinv[h], vb[h]) for h in heads]
    w = [_bdot(tinv[h], kb[h] * eg[h]) for h in heads]
    v_new = [u[h] - _bdot(w[h], s[h]) for h in heads]
    o = [qs[h] + _bdot(attn[h], v_new[h]) for h in heads]
    upd = [_bdot_tn(k[h] * jnp.exp(glast[h] - gcol[h]), v_new[h]) for h in heads]
    for c, (e, h) in enumerate(chains):
        s_new = s[c] * jnp.exp(glast[c]) + upd[c]
        s_s[c] = s_new
        sfin_ref[e, h] = s_new
        cols = slice(e * D_B + h * DV_B, e * D_B + (h + 1) * DV_B)
        o_ref[:, cols] = _rmsnorm(o[c], nw) * _silu(z_ref[:, cols])


def _gdn(qkv_tm, ba_tm, z_tm, conv0, s0, prm, bsz, t, nb):
    seq = qkv_tm.shape[0]
    grid = (bsz // nb, seq // t)
    const = lambda shape: pl.BlockSpec(shape, lambda b, i: (0,) * len(shape))
    tm = lambda n: pl.BlockSpec((t, nb * n), lambda b, i: (i, b))
    state = pl.BlockSpec((nb, H_B, DK_B, DV_B), lambda b, i: (b, 0, 0, 0))
    conv = pl.BlockSpec((nb, CONV_W - 1, D_QKV), lambda b, i: (b, 0, 0))
    return pl.pallas_call(
        functools.partial(_gdn_kernel, t=t, nb=nb),
        grid=grid,
        in_specs=[tm(D_QKV), tm(GATE_PAD), tm(D_B), conv, state,
                  const((CONV_W, D_QKV)), const((1, GATE_PAD)), const((1, GATE_PAD)), const((1, DV_B))],
        out_specs=[tm(D_B), state, conv],
        out_shape=[jax.ShapeDtypeStruct((seq, bsz * D_B), F32),
                   jax.ShapeDtypeStruct((bsz, H_B, DK_B, DV_B), F32),
                   jax.ShapeDtypeStruct((bsz, CONV_W - 1, D_QKV), F32)],
        scratch_shapes=[pltpu.VMEM((t + SUBLANES, nb * D_QKV), F32),
                        pltpu.VMEM((nb * H_B, DK_B, DV_B), F32)],
        compiler_params=_cparams(("parallel", "arbitrary"), 48),
    )(qkv_tm, ba_tm, z_tm, conv0, s0, prm["conv_w"], prm["alog"], prm["dtb"], prm["norm_w"])


def _mix_ffn_kernel(*refs, n_mix, final_norm):
    x_ref = refs[0]
    mix_refs = refs[1:1 + n_mix]
    wout_refs = refs[1 + n_mix:1 + 2 * n_mix]
    nffn_ref, wg_ref, wu_ref, wd_ref = refs[1 + 2 * n_mix:5 + 2 * n_mix]
    rest = refs[5 + 2 * n_mix:]
    if final_norm:
        nfin_ref, out_ref, acc_s = rest
    else:
        out_ref, acc_s = rest

    x1 = x_ref[...]
    for m_ref, w_ref in zip(mix_refs, wout_refs):
        x1 = x1 + jnp.dot(m_ref[...].astype(BF16), w_ref[...], preferred_element_type=F32)
    h = _rmsnorm(x1, nffn_ref[...]).astype(BF16)
    acc_s[...] = x1

    def ff_tile(c, carry):
        gate = jnp.dot(h, wg_ref[c], preferred_element_type=F32)
        up = jnp.dot(h, wu_ref[c], preferred_element_type=F32)
        act = (_silu(gate) * up).astype(BF16)
        acc_s[...] += jnp.dot(act, wd_ref[c], preferred_element_type=F32)
        return carry

    lax.fori_loop(0, N_FF_TILES, ff_tile, 0)
    x2 = acc_s[...]
    out_ref[...] = _rmsnorm(x2, nfin_ref[...]) if final_norm else x2


def _mix_ffn(x, x_bm, mixes, wouts, nffn, wg, wu, wd, nfin, bv, lv, out_bm):
    tl = min(TOKEN_TILE, lv)
    grid = (bv, lv // tl)
    const = lambda shape: pl.BlockSpec(shape, lambda b, i: (0,) * len(shape),
                                       pipeline_mode=pl.Buffered(1))
    tm = lambda n: pl.BlockSpec((tl, n), lambda b, i: (i, b))
    bm = pl.BlockSpec((None, tl, D_MODEL), lambda b, i: (b, i, 0))
    in_specs = [bm if x_bm else tm(D_MODEL)]
    in_specs += [tm(m.shape[1] // bv) for m in mixes]
    in_specs += [const(w.shape) for w in wouts]
    in_specs += [const((1, D_MODEL)), const(wg.shape), const(wu.shape), const(wd.shape)]
    args = [x, *mixes, *wouts, nffn, wg, wu, wd]
    if nfin is not None:
        in_specs.append(const((1, D_MODEL)))
        args.append(nfin)
    if out_bm:
        out_spec, out_shape = bm, jax.ShapeDtypeStruct((bv, lv, D_MODEL), F32)
    else:
        out_spec, out_shape = tm(D_MODEL), jax.ShapeDtypeStruct((lv, bv * D_MODEL), F32)
    return pl.pallas_call(
        functools.partial(_mix_ffn_kernel, n_mix=len(mixes), final_norm=nfin is not None),
        grid=grid,
        in_specs=in_specs,
        out_specs=out_spec,
        out_shape=out_shape,
        scratch_shapes=[pltpu.VMEM((tl, D_MODEL), F32)],
        compiler_params=_cparams(("parallel", "parallel"), 56),
    )(*args)


def _rwkv_proj_kernel(x_ref, shift0_ref, nw_ref, maa_ref, wr_ref, wk_ref, wv_ref, w0_ref, w1_ref, w2_ref,
                      a0_ref, a1_ref, a2_ref, g1_ref, g2_ref, kk_ref, ka_ref,
                      r_out, w_out, k_out, v_out, kk_out, a_out, g_out, shift_out, hbuf, *, shift):
    pad = -(-shift // SUBLANES) * SUBLANES

    @pl.when(pl.program_id(1) == 0)
    def _():
        hbuf[pad - shift:pad, :] = shift0_ref[...]

    h = _rmsnorm(x_ref[...], nw_ref[...])
    rows = h.shape[0]
    hbuf[pad:pad + rows, :] = h
    prev = hbuf[pad - shift:pad - shift + rows, :]
    last = h[rows - shift:]
    hbuf[pad - shift:pad, :] = last
    shift_out[...] = last
    xx = prev - h
    maa = maa_ref[...]
    mixed = [h + xx * maa[j:j + 1, :] for j in range(6)]
    xr, xw, xk, xv, xa, xg = mixed
    r = _bdot(xr, wr_ref[...])
    k = _bdot(xk, wk_ref[...])
    v = _bdot(xv, wv_ref[...])
    w = -_softplus(-(w0_ref[...] + _bdot(jnp.tanh(_bdot(xw, w1_ref[...])), w2_ref[...]))) - 0.5
    a = _sigmoid(a0_ref[...] + _bdot(_bdot(xa, a1_ref[...]), a2_ref[...]))
    g = _bdot(_sigmoid(_bdot(xg, g1_ref[...])), g2_ref[...])
    r_out[...] = r
    w_out[...] = -jnp.exp(w)
    k_out[...] = k * (1.0 + (a - 1.0) * ka_ref[...])
    v_out[...] = v
    kk_out[...] = k * kk_ref[...]
    a_out[...] = a
    g_out[...] = g


def _rwkv_proj(x_tm, shift0, prm):
    bv, shift, _ = shift0.shape
    lv = x_tm.shape[0]
    tr = max(shift, min(RWKV_PROJ_TILE, lv))
    grid = (bv, lv // tr)
    const = lambda shape: pl.BlockSpec(shape, lambda b, i: (0,) * len(shape))
    act = pl.BlockSpec((tr, D_MODEL), lambda b, i: (i, b))
    carry = pl.BlockSpec((None, shift, D_MODEL), lambda b, i: (b, 0, 0))
    vec = const((1, D_MODEL))
    weights = [prm[n] for n in ("wr", "wk", "wv")]
    in_specs = [act, carry, vec, const((6, D_MODEL))]
    in_specs += [const(w.shape) for w in weights]
    in_specs += [vec, const(prm["w1"].shape), const(prm["w2"].shape),
                 vec, const(prm["a1"].shape), const(prm["a2"].shape),
                 const(prm["g1"].shape), const(prm["g2"].shape), vec, vec]
    out_act = jax.ShapeDtypeStruct((lv, bv * D_MODEL), F32)
    pad = -(-shift // SUBLANES) * SUBLANES
    return pl.pallas_call(
        functools.partial(_rwkv_proj_kernel, shift=shift),
        grid=grid,
        in_specs=in_specs,
        out_specs=[act] * 7 + [carry],
        out_shape=[out_act] * 7 + [jax.ShapeDtypeStruct((bv, shift, D_MODEL), F32)],
        scratch_shapes=[pltpu.VMEM((pad + tr, D_MODEL), F32)],
        compiler_params=_cparams(("parallel", "arbitrary"), 56),
    )(x_tm, shift0, prm["norm_w"], prm["maa"], prm["wr"], prm["wk"], prm["wv"], prm["w0"], prm["w1"],
      prm["w2"], prm["a0"], prm["a1"], prm["a2"], prm["g1"], prm["g2"], prm["k_k"], prm["k_a"])


def _rwkv_scan_kernel(r_ref, w_ref, k_ref, v_ref, kk_ref, a_ref, g_ref, s0_ref, rk_ref, lnw_ref, lnb_ref,
                      y_ref, sfin_ref, s_s, *, t, nb, group):
    n = N_HEAD_C
    zero_blk = jnp.zeros((n, n), F32)

    pairs_per_seq = H_C // 2

    @pl.when(pl.program_id(1) == 0)
    def _():
        for jj in range(nb * pairs_per_seq):
            e, j = divmod(jj, pairs_per_seq)
            top = jnp.concatenate([s0_ref[e, 2 * j], zero_blk], axis=1)
            bot = jnp.concatenate([zero_blk, s0_ref[e, 2 * j + 1]], axis=1)
            s_s[jj] = jnp.concatenate([top, bot], axis=0)

    row, col = _iota2((t, t))
    incl = row >= col
    strict = row > col
    tri = incl.astype(BF16)
    lane = lax.broadcasted_iota(jnp.int32, (t, LANES), 1)
    head0 = lane < n
    prow, pcol = _iota2((LANES, LANES))
    same_head = (prow >> 6) == (pcol >> 6)
    ones_blk = same_head.astype(BF16)

    def pair_group(js):
        ps = range(len(js))
        sl = [slice(j * LANES, (j + 1) * LANES) for j in js]
        psl = [slice((j % pairs_per_seq) * LANES, (j % pairs_per_seq + 1) * LANES) for j in js]
        r = [r_ref[:, x] for x in sl]
        w = [w_ref[:, x] for x in sl]
        k = [k_ref[:, x] for x in sl]
        v = [v_ref[:, x] for x in sl]
        kkr = [kk_ref[:, x] for x in sl]
        a = [a_ref[:, x] for x in sl]
        kk = [x * lax.rsqrt(_split_dot(x * x, ones_blk) + NORM_EPS) for x in kkr]
        bv = [kk[p] * a[p] for p in ps]
        gc = [_cumsum_rows(tri, x) for x in w]
        glast = [x[t - 1:t, :] for x in gc]
        pinv = [jnp.exp(-x) for x in gc]
        at = [-kk[p] * jnp.exp(gc[p] - w[p]) for p in ps]
        bt = [bv[p] * pinv[p] for p in ps]
        kt = [k[p] * pinv[p] for p in ps]
        rt = [r[p] * jnp.exp(gc[p]) for p in ps]
        rem = [jnp.exp(glast[p] - gc[p]) for p in ps]
        lhs = [jnp.concatenate([jnp.where(head0, at[p], 0.0), jnp.where(head0, 0.0, at[p]),
                                jnp.where(head0, rt[p], 0.0), jnp.where(head0, 0.0, rt[p])], axis=0)
               for p in ps]
        gb = [_bdot_nt(lhs[p], bt[p]) for p in ps]
        gk = [_bdot_nt(lhs[p], kt[p]) for p in ps]
        s = [s_s[j] for j in js]
        ars = [_bdot_nt(jnp.concatenate([at[p], rt[p]], axis=0), s[p]) for p in ps]
        chains = [(p, hh) for p in ps for hh in range(2)]
        aab = [jnp.where(strict, gb[p][hh * t:(hh + 1) * t], 0.0) for p, hh in chains]
        aak = [jnp.where(strict, gk[p][hh * t:(hh + 1) * t], 0.0) for p, hh in chains]
        rhs = [ars[p][:t] + _bdot(aak[c], v[p]) for c, (p, hh) in enumerate(chains)]
        tinv = _inv_identity_plus([-x for x in aab], t)
        us = [_bdot(tinv[c], rhs[c]) for c in range(len(chains))]
        u = [jnp.where(head0, us[2 * p], us[2 * p + 1]) for p in ps]
        rb = [jnp.where(incl, gb[p][(2 + hh) * t:(3 + hh) * t], 0.0) for p, hh in chains]
        rkm = [jnp.where(incl, gk[p][(2 + hh) * t:(3 + hh) * t], 0.0) for p, hh in chains]
        ys = [_bdot(rb[c], u[p]) + _bdot(rkm[c], v[p]) for c, (p, hh) in enumerate(chains)]
        y = [ars[p][t:] + jnp.where(head0, ys[2 * p], ys[2 * p + 1]) for p in ps]
        upd = [_bdot_tn(jnp.concatenate([u[p], v[p]], axis=0),
                        jnp.concatenate([bv[p] * rem[p], k[p] * rem[p]], axis=0)) for p in ps]
        for p, j in enumerate(js):
            s_s[j] = s[p] * jnp.exp(glast[p]) + jnp.where(same_head, upd[p], 0.0)

        mu = [_split_dot(x, ones_blk) * (1.0 / n) for x in y]
        d = [y[p] - mu[p] for p in ps]
        var = [_split_dot(x * x, ones_blk) * (1.0 / n) for x in d]
        bonus = [_split_dot(r[p] * k[p] * rk_ref[:, psl[p]], ones_blk) for p in ps]
        for p in ps:
            yn = d[p] * lax.rsqrt(var[p] + RWKV_GN_EPS) * lnw_ref[:, psl[p]] + lnb_ref[:, psl[p]]
            y_ref[:, sl[p]] = (yn + bonus[p] * v[p]) * g_ref[:, sl[p]]

    for first in range(0, nb * pairs_per_seq, group):
        pair_group(list(range(first, first + group)))

    for jj in range(nb * pairs_per_seq):
        e, j = divmod(jj, pairs_per_seq)
        sp = s_s[jj]
        sfin_ref[e, 2 * j] = sp[:n, :n]
        sfin_ref[e, 2 * j + 1] = sp[n:, n:]


def _rwkv_scan(acts, s0, prm, bsz, t, nb, group):
    seq = acts[0].shape[0]
    grid = (bsz // nb, seq // t)
    const = lambda shape: pl.BlockSpec(shape, lambda b, i: (0,) * len(shape))
    tm = pl.BlockSpec((t, nb * D_MODEL), lambda b, i: (i, b))
    st = pl.BlockSpec((nb, H_C, N_HEAD_C, N_HEAD_C), lambda b, i: (b, 0, 0, 0))
    vec = const((1, D_MODEL))
    return pl.pallas_call(
        functools.partial(_rwkv_scan_kernel, t=t, nb=nb, group=group),
        grid=grid,
        in_specs=[tm] * 7 + [st, vec, vec, vec],
        out_specs=[tm, st],
        out_shape=[jax.ShapeDtypeStruct((seq, bsz * D_MODEL), F32),
                   jax.ShapeDtypeStruct((bsz, H_C, N_HEAD_C, N_HEAD_C), F32)],
        scratch_shapes=[pltpu.VMEM((nb * H_C // 2, LANES, LANES), F32)],
        compiler_params=_cparams(("parallel", "arbitrary"), 48),
    )(*acts, s0, prm["r_k"], prm["ln_w"], prm["ln_b"])


def _block_diag_groups(w, rows_per_group, cols_per_group):
    g = w.shape[0]
    w = w.reshape(g // 8, 8, rows_per_group, cols_per_group)
    eye = jnp.eye(8, dtype=w.dtype)
    out = jnp.einsum("jgrc,gh->jgrhc", w, eye)
    return out.reshape(g // 8, 8 * rows_per_group, 8 * cols_per_group)


def _pad_to(w, axis, size):
    pad = [(0, 0)] * w.ndim
    pad[axis] = (0, size - w.shape[axis])
    return jnp.pad(w, pad)


def _row(v):
    return v.reshape(1, -1).astype(F32)


def _recurrence_tiling(seq):
    if seq >= GDN_CHUNK:
        return dict(s5_tc=64, gdn_t=GDN_CHUNK, gdn_nb=4, rwkv_t=RWKV_CHUNK, rwkv_nb=1,
                    rwkv_group=RWKV_PAIR_GROUP)
    return dict(s5_tc=seq, gdn_t=seq, gdn_nb=8, rwkv_t=seq, rwkv_nb=4, rwkv_group=RWKV_PAIR_GROUP)


def _trunk(x_bm, bsz, seq, s5_re0, s5_im0, gdn_s0, gdn_conv0, rw_s0, rw_shift0, p, out_bm):
    bv, lv, _ = x_bm.shape
    cfg = _recurrence_tiling(seq)
    u, qkv, z, ba = _ab_in(x_bm, p["norm_mix0"], p["ab_w_in"])
    y_a, hr, hi = _s5(u.reshape(seq, bsz, D_A), s5_re0, s5_im0, p["s5"], cfg["s5_tc"])
    y_b, gdn_s, gdn_conv = _gdn(qkv.reshape(seq, bsz * D_QKV), ba.reshape(seq, bsz * GATE_PAD),
                                z.reshape(seq, bsz * D_B), gdn_conv0, gdn_s0, p["gdn"], bsz,
                                cfg["gdn_t"], cfg["gdn_nb"])
    x1 = _mix_ffn(x_bm, True, [y_a.reshape(lv, bv * D_A), y_b.reshape(lv, bv * D_B)],
                  [p["ab_w_out_a"], p["ab_w_out_b"]], p["norm_ffn0"], p["wg0"], p["wu0"], p["wd0"],
                  None, bv, lv, False)
    acts = _rwkv_proj(x1, rw_shift0.reshape(bv, bsz // bv, D_MODEL), p["rw"])
    shift = acts[7].reshape(bsz, D_MODEL)
    yg, rw_s = _rwkv_scan([a.reshape(seq, bsz * D_MODEL) for a in acts[:7]], rw_s0, p["rw"], bsz,
                          cfg["rwkv_t"], cfg["rwkv_nb"], cfg["rwkv_group"])
    y = _mix_ffn(x1, False, [yg.reshape(lv, bv * D_MODEL)], [p["rw"]["wo"]], p["norm_ffn1"], p["wg1"],
                 p["wu1"], p["wd1"], p["norm_final"], bv, lv, out_bm)
    return y, hr, hi, gdn_s, gdn_conv, rw_s, shift


def kernel(x_prompt, x_sample, state_s5_re, state_s5_im, state_gdn, state_gdn_conv, state_rwkv, state_rwkv_shift, norm_mix, norm_ffn, norm_final, ffn_w_gate, ffn_w_up, ffn_w_down, ab_w_in, ab_w_out, s5_lambda_re, s5_lambda_im, s5_log_step, s5_B_re, s5_B_im, s5_C_re, s5_C_im, s5_D, s5_w_glu, gdn_conv_w, gdn_A_log, gdn_dt_bias, gdn_norm_w, rw_maa, rw_w_r, rw_w_k, rw_w_v, rw_w_o, rw_w0, rw_w1, rw_w2, rw_a0, rw_a1, rw_a2, rw_g1, rw_g2, rw_k_k, rw_k_a, rw_r_k, rw_ln_w, rw_ln_b):
    bsz_p, seq_p, _ = x_prompt.shape
    bsz_s, seq_s, _ = x_sample.shape

    w_in = ab_w_in[0]
    n_main = D_A + D_QKV
    w_in = jnp.concatenate([w_in[:, :n_main], w_in[:, n_main + 2 * H_B:], w_in[:, n_main:n_main + 2 * H_B]],
                           axis=1)
    w_in = _pad_to(w_in, 1, D_IN_PAD).astype(BF16)

    def ff(w, layer):
        return jnp.transpose(w[layer].reshape(D_MODEL, N_FF_TILES, FF_TILE), (1, 0, 2)).astype(BF16)

    gate_lanes = lambda v: _pad_to(jnp.concatenate([jnp.zeros((H_B,), F32), v.astype(F32)]), 0,
                                   GATE_PAD).reshape(1, GATE_PAD)
    p = dict(
        norm_mix0=_row(norm_mix[0]), norm_ffn0=_row(norm_ffn[0]), norm_ffn1=_row(norm_ffn[1]),
        norm_final=_row(norm_final), ab_w_in=w_in,
        ab_w_out_a=ab_w_out[0][:D_A].astype(BF16), ab_w_out_b=ab_w_out[0][D_A:].astype(BF16),
        wg0=ff(ffn_w_gate, 0), wu0=ff(ffn_w_up, 0),
        wd0=ffn_w_down[0].reshape(N_FF_TILES, FF_TILE, D_MODEL).astype(BF16),
        wg1=ff(ffn_w_gate, 1), wu1=ff(ffn_w_up, 1),
        wd1=ffn_w_down[1].reshape(N_FF_TILES, FF_TILE, D_MODEL).astype(BF16),
        s5=dict(
            lr=_row(s5_lambda_re[0]), li=_row(s5_lambda_im[0]),
            ls=_row(jnp.repeat(s5_log_step[0], P_STATE)),
            bre=_block_diag_groups(jnp.swapaxes(s5_B_re[0], 1, 2), S5_GROUP, P_STATE).astype(BF16),
            bim=_block_diag_groups(jnp.swapaxes(s5_B_im[0], 1, 2), S5_GROUP, P_STATE).astype(BF16),
            cre=_block_diag_groups(jnp.swapaxes(s5_C_re[0], 1, 2), P_STATE, S5_GROUP).astype(BF16),
            cim=_block_diag_groups(jnp.swapaxes(s5_C_im[0], 1, 2), P_STATE, S5_GROUP).astype(BF16),
            d=_row(s5_D[0]), wglu=s5_w_glu[0].astype(BF16)),
        gdn=dict(conv_w=gdn_conv_w[0].astype(F32), alog=gate_lanes(gdn_A_log[0]),
                 dtb=gate_lanes(gdn_dt_bias[0]), norm_w=_row(gdn_norm_w[0])),
        rw=dict(
            norm_w=_row(norm_mix[1]), maa=rw_maa[0].astype(F32),
            wr=rw_w_r[0].astype(BF16), wk=rw_w_k[0].astype(BF16), wv=rw_w_v[0].astype(BF16),
            wo=rw_w_o[0].astype(BF16),
            w0=_row(rw_w0[0]), w1=_pad_to(rw_w1[0], 1, LANES).astype(BF16),
            w2=_pad_to(rw_w2[0], 0, LANES).astype(BF16),
            a0=_row(rw_a0[0]), a1=_pad_to(rw_a1[0], 1, LANES).astype(BF16),
            a2=_pad_to(rw_a2[0], 0, LANES).astype(BF16),
            g1=_pad_to(rw_g1[0], 1, 2 * LANES).astype(BF16), g2=_pad_to(rw_g2[0], 0, 2 * LANES).astype(BF16),
            k_k=_row(rw_k_k[0]), k_a=_row(rw_k_a[0]), r_k=_row(rw_r_k[0]),
            ln_w=_row(rw_ln_w[0]), ln_b=_row(rw_ln_b[0])),
    )

    zeros = lambda *shape: jnp.zeros(shape, F32)
    yp, p_hr, p_hi, p_gdn, p_conv, p_rw, p_shift = _trunk(
        x_prompt, bsz_p, seq_p, zeros(bsz_p, N_S5), zeros(bsz_p, N_S5), zeros(bsz_p, H_B, DK_B, DV_B),
        zeros(bsz_p, CONV_W - 1, D_QKV), zeros(bsz_p, H_C, N_HEAD_C, N_HEAD_C), zeros(bsz_p, D_MODEL), p,
        True)

    xs_tm = jnp.transpose(x_sample, (1, 0, 2)).reshape(1, seq_s * bsz_s, D_MODEL)
    ys, s_hr, s_hi, s_gdn, s_conv, s_rw, s_shift = _trunk(
        xs_tm, bsz_s, seq_s, state_s5_re[0].reshape(bsz_s, N_S5), state_s5_im[0].reshape(bsz_s, N_S5),
        state_gdn[0], state_gdn_conv[0], state_rwkv[0], state_rwkv_shift[0], p, True)
    y_sample = jnp.transpose(ys.reshape(seq_s, bsz_s, D_MODEL), (1, 0, 2))

    s5_shape = lambda b: (1, b, G_A, P_STATE)
    return (yp, y_sample,
            p_hr.reshape(s5_shape(bsz_p)), p_hi.reshape(s5_shape(bsz_p)), p_gdn[None], p_conv[None],
            p_rw[None], p_shift[None],
            s_hr.reshape(s5_shape(bsz_s)), s_hi.reshape(s5_shape(bsz_s)), s_gdn[None], s_conv[None],
            s_rw[None], s_shift[None])
```

```python
import functools
import math

import jax
import jax.numpy as jnp
from jax import lax
from jax.experimental import pallas as pl
from jax.experimental.pallas import tpu as pltpu

F32 = jnp.float32
BF16 = jnp.bfloat16

D_MODEL = 1024
D_A = 512
S5_GROUP = 16
G_A = 32
P_STATE = 64
N_S5 = G_A * P_STATE
D_B = 512
H_B = 4
DK_B = 128
DV_B = 128
D_QKV = 2 * H_B * DK_B + D_B
CONV_W = 4
GDN_CHUNK = 64
N_HEAD_C = 64
H_C = 16
RWKV_CHUNK = 64
RWKV_PAIR_GROUP = 8
RWKV_GN_EPS = 64e-5
D_FF = 2816
NORM_EPS = 1e-6

LANES = 128
SUBLANES = 8
FF_TILE = 256
N_FF_TILES = D_FF // FF_TILE
GATE_PAD = LANES
D_IN_PAD = 4 * 512 + 512 + GATE_PAD
S5_LANE_BLOCK = 512
N_S5_BLOCKS = N_S5 // S5_LANE_BLOCK
TOKEN_TILE = 512
RWKV_PROJ_TILE = 256
MIB = 1024 * 1024


def _cparams(semantics, vmem_mib):
    return pltpu.CompilerParams(dimension_semantics=semantics, vmem_limit_bytes=vmem_mib * MIB)


def _rmsnorm(x, w):
    return x * lax.rsqrt(jnp.mean(x * x, axis=-1, keepdims=True) + NORM_EPS) * w


def _sigmoid(x):
    return 1.0 / (1.0 + jnp.exp(-x))


def _silu(x):
    return x * _sigmoid(x)


def _softplus(x):
    return jnp.maximum(x, 0.0) + jnp.log1p(jnp.exp(-jnp.abs(x)))


def _bdot(a, b):
    return jnp.dot(a.astype(BF16), b.astype(BF16), preferred_element_type=F32)


def _bdot_nt(a, b):
    return lax.dot_general(a.astype(BF16), b.astype(BF16), (((1,), (1,)), ((), ())),
                           preferred_element_type=F32)


def _bdot_tn(a, b):
    return lax.dot_general(a.astype(BF16), b.astype(BF16), (((0,), (0,)), ((), ())),
                           preferred_element_type=F32)


def _split(a):
    hi = a.astype(BF16)
    return hi, (a - hi.astype(F32)).astype(BF16)


def _split3(a):
    p1 = a.astype(BF16)
    r1 = a - p1.astype(F32)
    p2 = r1.astype(BF16)
    return p1, p2, (r1 - p2.astype(F32)).astype(BF16)


def _dot3(a, b):
    dot = lambda x, y: jnp.dot(x, y, preferred_element_type=F32)
    return dot(a[0], b[0]) + dot(a[0], b[1]) + dot(a[1], b[0])


def _split_dot(a, exact_bf16):
    hi, lo = _split(a)
    return (jnp.dot(hi, exact_bf16, preferred_element_type=F32)
            + jnp.dot(lo, exact_bf16, preferred_element_type=F32))


def _cumsum_rows(tri_bf16, w):
    return sum(jnp.dot(tri_bf16, part, preferred_element_type=F32) for part in _split3(w))


def _iota2(shape):
    return (lax.broadcasted_iota(jnp.int32, shape, 0), lax.broadcasted_iota(jnp.int32, shape, 1))


def _inv_identity_plus(lms, t):
    row, col = _iota2((t, t))
    eye = (row == col).astype(F32)
    base = min(t, 16)
    if t > base:
        same_base = (row >> 4) == (col >> 4)
        ns = [jnp.where(same_base, -lm, 0.0) for lm in lms]
    else:
        ns = [-lm for lm in lms]
    xs = [eye + n for n in ns]
    ps = ns
    k = 1
    while 2 * k < base:
        ps = [_bdot(p, p) for p in ps]
        xs = [x + _bdot(x, p) for x, p in zip(xs, ps)]
        k *= 2
    shift = 4
    blk = base
    while blk < t:
        same_big = (row >> (shift + 1)) == (col >> (shift + 1))
        same_small = (row >> shift) == (col >> shift)
        off = [jnp.where(same_big, jnp.where(same_small, 0.0, lm), 0.0) for lm in lms]
        mids = [_bdot(o, x) for o, x in zip(off, xs)]
        xs = [x - _bdot(x, m) for x, m in zip(xs, mids)]
        blk *= 2
        shift += 1
    return xs


def _inv_identity_plus_wide(lmws, t):
    row, col = _iota2((t, 2 * t))
    left = col < t
    col = jnp.where(left, col, col - t)
    eye = (row == col).astype(F32)
    base = min(t, 16)

    def blockdiag(xw):
        return jnp.concatenate([jnp.where(left, xw, 0.0), jnp.where(left, 0.0, xw)], axis=0)

    if t > base:
        same_base = (row >> 4) == (col >> 4)
        ns = [jnp.where(same_base, -lmw, 0.0) for lmw in lmws]
    else:
        ns = [-lmw for lmw in lmws]
    xs = [eye + n for n in ns]
    ps = [_bdot(n, blockdiag(n)) for n in ns]
    k = 2
    while k < base:
        bds = [blockdiag(p) for p in ps]
        if 2 * k < base:
            both = [_bdot(jnp.concatenate([p, x], axis=0), bd) for p, x, bd in zip(ps, xs, bds)]
            ps = [r[:t] for r in both]
            xs = [x + r[t:] for x, r in zip(xs, both)]
        else:
            xs = [x + _bdot(x, bd) for x, bd in zip(xs, bds)]
        k *= 2
    shift = 4
    blk = base
    while blk < t:
        same_big = (row >> (shift + 1)) == (col >> (shift + 1))
        same_small = (row >> shift) == (col >> shift)
        off = [jnp.where(same_big, jnp.where(same_small, 0.0, lmw), 0.0) for lmw in lmws]
        mids = [_bdot(o, blockdiag(x)) for o, x in zip(off, xs)]
        xs = [x - _bdot(x, blockdiag(m)) for x, m in zip(xs, mids)]
        blk *= 2
        shift += 1
    return xs


def _ab_in_kernel(x_ref, nw_ref, w_ref, u_ref, qkv_ref, z_ref, ba_ref):
    h = _rmsnorm(x_ref[...], nw_ref[...])
    p = jnp.dot(h.astype(BF16), w_ref[...], preferred_element_type=F32)
    u_ref[...] = p[:, 0:D_A]
    qkv_ref[...] = p[:, D_A:D_A + D_QKV]
    z_ref[...] = p[:, D_A + D_QKV:D_A + D_QKV + D_B]
    ba_ref[...] = p[:, D_A + D_QKV + D_B:D_IN_PAD]


def _ab_in(x_bm, norm_w, w_in):
    bv, lv, _ = x_bm.shape
    tl = min(TOKEN_TILE, lv)
    grid = (bv, lv // tl)
    tm = lambda n: pl.BlockSpec((tl, n), lambda b, i: (i, b))
    const = lambda shape: pl.BlockSpec(shape, lambda b, i: (0,) * len(shape))
    return pl.pallas_call(
        _ab_in_kernel,
        grid=grid,
        in_specs=[pl.BlockSpec((None, tl, D_MODEL), lambda b, i: (b, i, 0)),
                  const((1, D_MODEL)), const((D_MODEL, D_IN_PAD))],
        out_specs=[tm(D_A), tm(D_QKV), tm(D_B), tm(GATE_PAD)],
        out_shape=[jax.ShapeDtypeStruct((lv, bv * D_A), F32),
                   jax.ShapeDtypeStruct((lv, bv * D_QKV), F32),
                   jax.ShapeDtypeStruct((lv, bv * D_B), F32),
                   jax.ShapeDtypeStruct((lv, bv * GATE_PAD), F32)],
        compiler_params=_cparams(("parallel", "parallel"), 48),
    )(x_bm, norm_w, w_in)


def _s5_kernel(u_ref, h0r_ref, h0i_ref, lr_ref, li_ref, ls_ref, bre_ref, bim_ref, cre_ref, cim_ref,
               d_ref, wglu_ref, y_ref, hr_out, hi_out,
               ar_s, ai_s, cr_s, ci_s, hr_s, hi_s, bur_s, bui_s, yg_s, *, tc):
    rows = tc * SUBLANES

    @pl.when(pl.program_id(1) == 0)
    def _():
        lr = lr_ref[...]
        li = li_ref[...]
        dt = jnp.exp(ls_ref[...])
        mag = jnp.exp(lr * dt)
        ar = mag * jnp.cos(li * dt)
        ai = mag * jnp.sin(li * dt)
        den = lr * lr + li * li
        nr = ar - 1.0
        cr = (nr * lr + ai * li) / den
        ci = (ai * lr - nr * li) / den
        ar_s[...] = jnp.broadcast_to(ar, (SUBLANES, N_S5))
        ai_s[...] = jnp.broadcast_to(ai, (SUBLANES, N_S5))
        cr_s[...] = jnp.broadcast_to(cr, (SUBLANES, N_S5))
        ci_s[...] = jnp.broadcast_to(ci, (SUBLANES, N_S5))
        hr_s[...] = h0r_ref[...]
        hi_s[...] = h0i_ref[...]

    u = u_ref[...].reshape(rows, D_A)
    ub = u.astype(BF16)
    for j in range(N_S5_BLOCKS):
        sl = slice(j * S5_LANE_BLOCK, (j + 1) * S5_LANE_BLOCK)
        uj = ub[:, j * LANES:(j + 1) * LANES]
        pr = jnp.dot(uj, bre_ref[j], preferred_element_type=F32)
        pi = jnp.dot(uj, bim_ref[j], preferred_element_type=F32)
        crj = cr_s[0:1, sl]
        cij = ci_s[0:1, sl]
        bur_s[:, :, sl] = (crj * pr - cij * pi).reshape(tc, SUBLANES, S5_LANE_BLOCK)
        bui_s[:, :, sl] = (crj * pi + cij * pr).reshape(tc, SUBLANES, S5_LANE_BLOCK)

    for j in range(N_S5_BLOCKS):
        sl = slice(j * S5_LANE_BLOCK, (j + 1) * S5_LANE_BLOCK)
        ar = ar_s[:, sl]
        ai = ai_s[:, sl]

        def step(t, carry, sl=sl, ar=ar, ai=ai):
            hr, hi = carry
            nr = ar * hr - ai * hi + bur_s[t, :, sl]
            ni = ar * hi + ai * hr + bui_s[t, :, sl]
            bur_s[t, :, sl] = nr
            bui_s[t, :, sl] = ni
            return nr, ni

        hr, hi = lax.fori_loop(0, tc, step, (hr_s[:, sl], hi_s[:, sl]), unroll=min(tc, 8))
        hr_s[:, sl] = hr
        hi_s[:, sl] = hi

    for j in range(N_S5_BLOCKS):
        sl = slice(j * S5_LANE_BLOCK, (j + 1) * S5_LANE_BLOCK)
        cl = slice(j * LANES, (j + 1) * LANES)
        xr = bur_s[:, :, sl].reshape(rows, S5_LANE_BLOCK).astype(BF16)
        xi = bui_s[:, :, sl].reshape(rows, S5_LANE_BLOCK).astype(BF16)
        yj = (jnp.dot(xr, cre_ref[j], preferred_element_type=F32)
              - jnp.dot(xi, cim_ref[j], preferred_element_type=F32)
              + d_ref[:, cl] * u[:, cl])
        yg_s[:, cl] = jax.nn.gelu(yj)

    yg = yg_s[...]
    out = yg * _sigmoid(jnp.dot(yg.astype(BF16), wglu_ref[...], preferred_element_type=F32))
    y_ref[...] = out.reshape(tc, SUBLANES, D_A)
    hr_out[...] = hr_s[...]
    hi_out[...] = hi_s[...]


def _s5(u_tm, h0r, h0i, prm, tc):
    seq, bsz, _ = u_tm.shape
    grid = (bsz // SUBLANES, seq // tc)
    const = lambda shape: pl.BlockSpec(shape, lambda b, i: (0,) * len(shape))
    act = pl.BlockSpec((tc, SUBLANES, D_A), lambda b, i: (i, b, 0))
    st = pl.BlockSpec((SUBLANES, N_S5), lambda b, i: (b, 0))
    vec = const((1, N_S5))
    small = pltpu.VMEM((SUBLANES, N_S5), F32)
    big = pltpu.VMEM((tc, SUBLANES, N_S5), F32)
    return pl.pallas_call(
        functools.partial(_s5_kernel, tc=tc),
        grid=grid,
        in_specs=[act, st, st, vec, vec, vec,
                  const((N_S5_BLOCKS, LANES, S5_LANE_BLOCK)), const((N_S5_BLOCKS, LANES, S5_LANE_BLOCK)),
                  const((N_S5_BLOCKS, S5_LANE_BLOCK, LANES)), const((N_S5_BLOCKS, S5_LANE_BLOCK, LANES)),
                  const((1, D_A)), const((D_A, D_A))],
        out_specs=[act, st, st],
        out_shape=[jax.ShapeDtypeStruct((seq, bsz, D_A), F32),
                   jax.ShapeDtypeStruct((bsz, N_S5), F32),
                   jax.ShapeDtypeStruct((bsz, N_S5), F32)],
        scratch_shapes=[small, small, small, small, small, small, big, big,
                        pltpu.VMEM((tc * SUBLANES, D_A), F32)],
        compiler_params=_cparams(("parallel", "arbitrary"), 48),
    )(u_tm, h0r, h0i, prm["lr"], prm["li"], prm["ls"], prm["bre"], prm["bim"], prm["cre"], prm["cim"],
      prm["d"], prm["wglu"])


def _gdn_kernel(qkv_ref, ba_ref, z_ref, conv0_ref, s0_ref, cw_ref, alog_ref, dtb_ref, nw_ref,
                o_ref, sfin_ref, convn_ref, xbuf, s_s, *, t, nb):
    pad = SUBLANES
    hist = CONV_W - 1

    @pl.when(pl.program_id(1) == 0)
    def _():
        for e in range(nb):
            xbuf[pad - hist:pad, e * D_QKV:(e + 1) * D_QKV] = conv0_ref[e]
        s_s[...] = s0_ref[...].reshape(nb * H_B, DK_B, DV_B)

    row, col = _iota2((t, t))
    causal = row >= col
    strict = row > col
    tri = causal.astype(BF16)
    lane = lax.broadcasted_iota(jnp.int32, (t, LANES), 1)
    nw = nw_ref[...]
    cw = cw_ref[...]
    ys, betas, gcs = [], [], []
    for e in range(nb):
        cols = slice(e * D_QKV, (e + 1) * D_QKV)
        x = qkv_ref[:, cols]
        xbuf[pad:pad + t, cols] = x
        acc = x * cw[hist:hist + 1, :]
        for j in range(hist):
            acc = acc + xbuf[pad - hist + j:pad - hist + j + t, cols] * cw[j:j + 1, :]
        last = xbuf[pad + t - hist:pad + t, cols]
        convn_ref[e] = last
        xbuf[pad - hist:pad, cols] = last
        ys.append(_silu(acc))
        ba = ba_ref[:, e * GATE_PAD:(e + 1) * GATE_PAD]
        betas.append(_sigmoid(ba))
        g = -jnp.exp(alog_ref[...]) * _softplus(ba + dtb_ref[...])
        gcs.append(_cumsum_rows(tri, g))

    chains = [(e, h) for e in range(nb) for h in range(H_B)]
    heads = range(len(chains))
    bcol = [betas[e][:, h:h + 1] for e, h in chains]
    gcol = [gcs[e][:, H_B + h:H_B + h + 1] for e, h in chains]
    y_of = [ys[e] for e, h in chains]
    h_of = [h for e, h in chains]
    decay = []
    for h in heads:
        p1, p2, p3 = [p.astype(F32) for p in _split3(gcol[h])]
        dl = jnp.where(lane == 0, p1, jnp.where(lane == 1, p2, jnp.where(lane == 2, p3,
                                                                         jnp.where(lane < 6, 1.0, 0.0))))
        dr = jnp.where(lane < 3, 1.0, jnp.where(lane == 3, -p1, jnp.where(lane == 4, -p2,
                                                                        jnp.where(lane == 5, -p3, 0.0))))
        decay.append(jnp.exp(jnp.where(causal, _bdot_nt(dl, dr), -jnp.inf)))
    eg = [jnp.exp(gcol[h]) for h in heads]
    glast = [gcol[h][t - 1:t, :] for h in heads]
    q = [y_of[c][:, h_of[c] * DK_B:(h_of[c] + 1) * DK_B] for c in heads]
    k = [y_of[c][:, (H_B + h_of[c]) * DK_B:(H_B + h_of[c] + 1) * DK_B] for c in heads]
    v = [y_of[c][:, 2 * H_B * DK_B + h_of[c] * DV_B:2 * H_B * DK_B + (h_of[c] + 1) * DV_B] for c in heads]
    q = [x * lax.rsqrt(jnp.sum(x * x, axis=-1, keepdims=True) + NORM_EPS) * (DK_B ** -0.5) for x in q]
    k = [x * lax.rsqrt(jnp.sum(x * x, axis=-1, keepdims=True) + NORM_EPS) for x in k]
    kb = [k[h] * bcol[h] for h in heads]
    vb = [v[h] * bcol[h] for h in heads]
    lm = [jnp.where(strict, _bdot_nt(kb[h], k[h]) * decay[h], 0.0) for h in heads]
    attn = [_bdot_nt(q[h], k[h]) * decay[h] for h in heads]
    s = [s_s[h] for h in heads]
    qs = [_bdot(q[h] * eg[h], s[h]) for h in heads]
    tinv = _inv_identity_plus(lm, t)
    u = [_bdot(tinv[h], vb[h]) for h in heads]
    w = [_bdot(tinv[h], kb[h] * eg[h]) for h in heads]
    v_new = [u[h] - _bdot(w[h], s[h]) for h in heads]
    o = [qs[h] + _bdot(attn[h], v_new[h]) for h in heads]
    upd = [_bdot_tn(k[h] * jnp.exp(glast[h] - gcol[h]), v_new[h]) for h in heads]
    for c, (e, h) in enumerate(chains):
        s_new = s[c] * jnp.exp(glast[c]) + upd[c]
        s_s[c] = s_new
        sfin_ref[e, h] = s_new
        cols = slice(e * D_B + h * DV_B, e * D_B + (h + 1) * DV_B)
        o_ref[:, cols] = _rmsnorm(o[c], nw) * _silu(z_ref[:, cols])


def _gdn(qkv_tm, ba_tm, z_tm, conv0, s0, prm, bsz, t, nb):
    seq = qkv_tm.shape[0]
    grid = (bsz // nb, seq // t)
    const = lambda shape: pl.BlockSpec(shape, lambda b, i: (0,) * len(shape))
    tm = lambda n: pl.BlockSpec((t, nb * n), lambda b, i: (i, b))
    state = pl.BlockSpec((nb, H_B, DK_B, DV_B), lambda b, i: (b, 0, 0, 0))
    conv = pl.BlockSpec((nb, CONV_W - 1, D_QKV), lambda b, i: (b, 0, 0))
    return pl.pallas_call(
        functools.partial(_gdn_kernel, t=t, nb=nb),
        grid=grid,
        in_specs=[tm(D_QKV), tm(GATE_PAD), tm(D_B), conv, state,
                  const((CONV_W, D_QKV)), const((1, GATE_PAD)), const((1, GATE_PAD)), const((1, DV_B))],
        out_specs=[tm(D_B), state, conv],
        out_shape=[jax.ShapeDtypeStruct((seq, bsz * D_B), F32),
                   jax.ShapeDtypeStruct((bsz, H_B, DK_B, DV_B), F32),
                   jax.ShapeDtypeStruct((bsz, CONV_W - 1, D_QKV), F32)],
        scratch_shapes=[pltpu.VMEM((t + SUBLANES, nb * D_QKV), F32),
                        pltpu.VMEM((nb * H_B, DK_B, DV_B), F32)],
        compiler_params=_cparams(("parallel", "arbitrary"), 48),
    )(qkv_tm, ba_tm, z_tm, conv0, s0, prm["conv_w"], prm["alog"], prm["dtb"], prm["norm_w"])


def _mix_ffn_kernel(*refs, n_mix, final_norm):
    x_ref = refs[0]
    mix_refs = refs[1:1 + n_mix]
    wout_refs = refs[1 + n_mix:1 + 2 * n_mix]
    nffn_ref, wg_ref, wu_ref, wd_ref = refs[1 + 2 * n_mix:5 + 2 * n_mix]
    rest = refs[5 + 2 * n_mix:]
    if final_norm:
        nfin_ref, out_ref, acc_s = rest
    else:
        out_ref, acc_s = rest

    x1 = x_ref[...]
    for m_ref, w_ref in zip(mix_refs, wout_refs):
        x1 = x1 + jnp.dot(m_ref[...].astype(BF16), w_ref[...], preferred_element_type=F32)
    h = _rmsnorm(x1, nffn_ref[...]).astype(BF16)
    acc_s[...] = x1

    def ff_tile(c, carry):
        gate = jnp.dot(h, wg_ref[c], preferred_element_type=F32)
        up = jnp.dot(h, wu_ref[c], preferred_element_type=F32)
        act = (_silu(gate) * up).astype(BF16)
        acc_s[...] += jnp.dot(act, wd_ref[c], preferred_element_type=F32)
        return carry

    lax.fori_loop(0, N_FF_TILES, ff_tile, 0)
    x2 = acc_s[...]
    out_ref[...] = _rmsnorm(x2, nfin_ref[...]) if final_norm else x2


def _mix_ffn(x, x_bm, mixes, wouts, nffn, wg, wu, wd, nfin, bv, lv, out_bm):
    tl = min(TOKEN_TILE, lv)
    grid = (bv, lv // tl)
    const = lambda shape: pl.BlockSpec(shape, lambda b, i: (0,) * len(shape),
                                       pipeline_mode=pl.Buffered(1))
    tm = lambda n: pl.BlockSpec((tl, n), lambda b, i: (i, b))
    bm = pl.BlockSpec((None, tl, D_MODEL), lambda b, i: (b, i, 0))
    in_specs = [bm if x_bm else tm(D_MODEL)]
    in_specs += [tm(m.shape[1] // bv) for m in mixes]
    in_specs += [const(w.shape) for w in wouts]
    in_specs += [const((1, D_MODEL)), const(wg.shape), const(wu.shape), const(wd.shape)]
    args = [x, *mixes, *wouts, nffn, wg, wu, wd]
    if nfin is not None:
        in_specs.append(const((1, D_MODEL)))
        args.append(nfin)
    if out_bm:
        out_spec, out_shape = bm, jax.ShapeDtypeStruct((bv, lv, D_MODEL), F32)
    else:
        out_spec, out_shape = tm(D_MODEL), jax.ShapeDtypeStruct((lv, bv * D_MODEL), F32)
    return pl.pallas_call(
        functools.partial(_mix_ffn_kernel, n_mix=len(mixes), final_norm=nfin is not None),
        grid=grid,
        in_specs=in_specs,
        out_specs=out_spec,
        out_shape=out_shape,
        scratch_shapes=[pltpu.VMEM((tl, D_MODEL), F32)],
        compiler_params=_cparams(("parallel", "parallel"), 56),
    )(*args)


def _rwkv_proj_kernel(x_ref, shift0_ref, nw_ref, maa_ref, wr_ref, wk_ref, wv_ref, w0_ref, w1_ref, w2_ref,
                      a0_ref, a1_ref, a2_ref, g1_ref, g2_ref, kk_ref, ka_ref,
                      r_out, w_out, k_out, v_out, kk_out, a_out, g_out, shift_out, hbuf, *, shift):
    pad = -(-shift // SUBLANES) * SUBLANES

    @pl.when(pl.program_id(1) == 0)
    def _():
        hbuf[pad - shift:pad, :] = shift0_ref[...]

    h = _rmsnorm(x_ref[...], nw_ref[...])
    rows = h.shape[0]
    hbuf[pad:pad + rows, :] = h
    prev = hbuf[pad - shift:pad - shift + rows, :]
    last = h[rows - shift:]
    hbuf[pad - shift:pad, :] = last
    shift_out[...] = last
    xx = prev - h
    maa = maa_ref[...]
    mixed = [h + xx * maa[j:j + 1, :] for j in range(6)]
    xr, xw, xk, xv, xa, xg = mixed
    r = _bdot(xr, wr_ref[...])
    k = _bdot(xk, wk_ref[...])
    v = _bdot(xv, wv_ref[...])
    w = -_softplus(-(w0_ref[...] + _bdot(jnp.tanh(_bdot(xw, w1_ref[...])), w2_ref[...]))) - 0.5
    a = _sigmoid(a0_ref[...] + _bdot(_bdot(xa, a1_ref[...]), a2_ref[...]))
    g = _bdot(_sigmoid(_bdot(xg, g1_ref[...])), g2_ref[...])
    r_out[...] = r
    w_out[...] = -jnp.exp(w)
    k_out[...] = k * (1.0 + (a - 1.0) * ka_ref[...])
    v_out[...] = v
    kk_out[...] = k * kk_ref[...]
    a_out[...] = a
    g_out[...] = g


def _rwkv_proj(x_tm, shift0, prm):
    bv, shift, _ = shift0.shape
    lv = x_tm.shape[0]
    tr = max(shift, min(RWKV_PROJ_TILE, lv))
    grid = (bv, lv // tr)
    const = lambda shape: pl.BlockSpec(shape, lambda b, i: (0,) * len(shape))
    act = pl.BlockSpec((tr, D_MODEL), lambda b, i: (i, b))
    carry = pl.BlockSpec((None, shift, D_MODEL), lambda b, i: (b, 0, 0))
    vec = const((1, D_MODEL))
    weights = [prm[n] for n in ("wr", "wk", "wv")]
    in_specs = [act, carry, vec, const((6, D_MODEL))]
    in_specs += [const(w.shape) for w in weights]
    in_specs += [vec, const(prm["w1"].shape), const(prm["w2"].shape),
                 vec, const(prm["a1"].shape), const(prm["a2"].shape),
                 const(prm["g1"].shape), const(prm["g2"].shape), vec, vec]
    out_act = jax.ShapeDtypeStruct((lv, bv * D_MODEL), F32)
    pad = -(-shift // SUBLANES) * SUBLANES
    return pl.pallas_call(
        functools.partial(_rwkv_proj_kernel, shift=shift),
        grid=grid,
        in_specs=in_specs,
        out_specs=[act] * 7 + [carry],
        out_shape=[out_act] * 7 + [jax.ShapeDtypeStruct((bv, shift, D_MODEL), F32)],
        scratch_shapes=[pltpu.VMEM((pad + tr, D_MODEL), F32)],
        compiler_params=_cparams(("parallel", "arbitrary"), 56),
    )(x_tm, shift0, prm["norm_w"], prm["maa"], prm["wr"], prm["wk"], prm["wv"], prm["w0"], prm["w1"],
      prm["w2"], prm["a0"], prm["a1"], prm["a2"], prm["g1"], prm["g2"], prm["k_k"], prm["k_a"])


def _rwkv_scan_kernel(r_ref, w_ref, k_ref, v_ref, kk_ref, a_ref, g_ref, s0_ref, rk_ref, lnw_ref, lnb_ref,
                      y_ref, sfin_ref, s_s, *, t, nb, group):
    n = N_HEAD_C
    zero_blk = jnp.zeros((n, n), F32)

    pairs_per_seq = H_C // 2

    @pl.when(pl.program_id(1) == 0)
    def _():
        for jj in range(nb * pairs_per_seq):
            e, j = divmod(jj, pairs_per_seq)
            top = jnp.concatenate([s0_ref[e, 2 * j], zero_blk], axis=1)
            bot = jnp.concatenate([zero_blk, s0_ref[e, 2 * j + 1]], axis=1)
            s_s[jj] = jnp.concatenate([top, bot], axis=0)

    row, col = _iota2((t, t))
    tri = (row >= col).astype(BF16)
    roww, colw = _iota2((t, 2 * t))
    colw = jnp.where(colw >= t, colw - t, colw)
    incl_w = roww >= colw
    strict_w = roww > colw
    lane = lax.broadcasted_iota(jnp.int32, (t, LANES), 1)
    head0 = lane < n

    def by_head(x):
        return jnp.concatenate([jnp.where(head0, x, 0.0), jnp.where(head0, 0.0, x)], axis=0)

    prow, pcol = _iota2((LANES, LANES))
    same_head = (prow >> 6) == (pcol >> 6)
    ones_blk = same_head.astype(BF16)

    def pair_group(js):
        ps = range(len(js))
        sl = [slice(j * LANES, (j + 1) * LANES) for j in js]
        psl = [slice((j % pairs_per_seq) * LANES, (j % pairs_per_seq + 1) * LANES) for j in js]
        r = [r_ref[:, x] for x in sl]
        w = [w_ref[:, x] for x in sl]
        k = [k_ref[:, x] for x in sl]
        v = [v_ref[:, x] for x in sl]
        kkr = [kk_ref[:, x] for x in sl]
        a = [a_ref[:, x] for x in sl]
        kk = [x * lax.rsqrt(_bdot(x * x, ones_blk) + NORM_EPS) for x in kkr]
        bv = [kk[p] * a[p] for p in ps]
        gc = [_cumsum_rows(tri, x) for x in w]
        glast = [x[t - 1:t, :] for x in gc]
        pinv = [jnp.exp(-x) for x in gc]
        at = [-kk[p] * jnp.exp(gc[p] - w[p]) for p in ps]
        bt = [bv[p] * pinv[p] for p in ps]
        kt = [k[p] * pinv[p] for p in ps]
        rt = [r[p] * jnp.exp(gc[p]) for p in ps]
        rem = [jnp.exp(glast[p] - gc[p]) for p in ps]
        lhs = [jnp.concatenate([at[p], rt[p]], axis=0) for p in ps]
        gm = [_bdot_nt(lhs[p], jnp.concatenate([by_head(bt[p]), by_head(kt[p])], axis=0)) for p in ps]
        s = [s_s[j] for j in js]
        ars = [_bdot_nt(lhs[p], s[p]) for p in ps]
        v2 = [by_head(x) for x in v]
        aab = [jnp.where(strict_w, gm[p][:t, :2 * t], 0.0) for p in ps]
        aak = [jnp.where(strict_w, gm[p][:t, 2 * t:], 0.0) for p in ps]
        rhs = [ars[p][:t] + _bdot(aak[p], v2[p]) for p in ps]
        tinv = _inv_identity_plus_wide([-x for x in aab], t)
        u = [_bdot(tinv[p], by_head(rhs[p])) for p in ps]
        rb = [jnp.where(incl_w, gm[p][t:, :2 * t], 0.0) for p in ps]
        rkm = [jnp.where(incl_w, gm[p][t:, 2 * t:], 0.0) for p in ps]
        y = [ars[p][t:] + _bdot(rb[p], by_head(u[p])) + _bdot(rkm[p], v2[p]) for p in ps]
        upd = [_bdot_tn(jnp.concatenate([u[p], v[p]], axis=0),
                        jnp.concatenate([bv[p] * rem[p], k[p] * rem[p]], axis=0)) for p in ps]
        for p, j in enumerate(js):
            s_s[j] = s[p] * jnp.exp(glast[p]) + jnp.where(same_head, upd[p], 0.0)

        mu = [_split_dot(x, ones_blk) * (1.0 / n) for x in y]
        d = [y[p] - mu[p] for p in ps]
        var = [_bdot(x * x, ones_blk) * (1.0 / n) for x in d]
        bonus = [_split_dot(r[p] * k[p] * rk_ref[:, psl[p]], ones_blk) for p in ps]
        for p in ps:
            yn = d[p] * lax.rsqrt(var[p] + RWKV_GN_EPS) * lnw_ref[:, psl[p]] + lnb_ref[:, psl[p]]
            y_ref[:, sl[p]] = (yn + bonus[p] * v[p]) * g_ref[:, sl[p]]

    for first in range(0, nb * pairs_per_seq, group):
        pair_group(list(range(first, first + group)))

    for jj in range(nb * pairs_per_seq):
        e, j = divmod(jj, pairs_per_seq)
        sp = s_s[jj]
        sfin_ref[e, 2 * j] = sp[:n, :n]
        sfin_ref[e, 2 * j + 1] = sp[n:, n:]


def _rwkv_scan(acts, s0, prm, bsz, t, nb, group):
    seq = acts[0].shape[0]
    grid = (bsz // nb, seq // t)
    const = lambda shape: pl.BlockSpec(shape, lambda b, i: (0,) * len(shape))
    tm = pl.BlockSpec((t, nb * D_MODEL), lambda b, i: (i, b))
    st = pl.BlockSpec((nb, H_C, N_HEAD_C, N_HEAD_C), lambda b, i: (b, 0, 0, 0))
    vec = const((1, D_MODEL))
    return pl.pallas_call(
        functools.partial(_rwkv_scan_kernel, t=t, nb=nb, group=group),
        grid=grid,
        in_specs=[tm] * 7 + [st, vec, vec, vec],
        out_specs=[tm, st],
        out_shape=[jax.ShapeDtypeStruct((seq, bsz * D_MODEL), F32),
                   jax.ShapeDtypeStruct((bsz, H_C, N_HEAD_C, N_HEAD_C), F32)],
        scratch_shapes=[pltpu.VMEM((nb * H_C // 2, LANES, LANES), F32)],
        compiler_params=_cparams(("parallel", "arbitrary"), 48),
    )(*acts, s0, prm["r_k"], prm["ln_w"], prm["ln_b"])


def _block_diag_groups(w, rows_per_group, cols_per_group):
    g = w.shape[0]
    w = w.reshape(g // 8, 8, rows_per_group, cols_per_group)
    eye = jnp.eye(8, dtype=w.dtype)
    out = jnp.einsum("jgrc,gh->jgrhc", w, eye)
    return out.reshape(g // 8, 8 * rows_per_group, 8 * cols_per_group)


def _pad_to(w, axis, size):
    pad = [(0, 0)] * w.ndim
    pad[axis] = (0, size - w.shape[axis])
    return jnp.pad(w, pad)


def _row(v):
    return v.reshape(1, -1).astype(F32)


def _recurrence_tiling(seq):
    if seq >= GDN_CHUNK:
        return dict(s5_tc=64, gdn_t=GDN_CHUNK, gdn_nb=4, rwkv_t=RWKV_CHUNK, rwkv_nb=2,
                    rwkv_group=2 * RWKV_PAIR_GROUP)
    return dict(s5_tc=seq, gdn_t=seq, gdn_nb=8, rwkv_t=seq, rwkv_nb=4, rwkv_group=2 * RWKV_PAIR_GROUP)


def _trunk(x_bm, bsz, seq, s5_re0, s5_im0, gdn_s0, gdn_conv0, rw_s0, rw_shift0, p, out_bm):
    bv, lv, _ = x_bm.shape
    cfg = _recurrence_tiling(seq)
    u, qkv, z, ba = _ab_in(x_bm, p["norm_mix0"], p["ab_w_in"])
    y_a, hr, hi = _s5(u.reshape(seq, bsz, D_A), s5_re0, s5_im0, p["s5"], cfg["s5_tc"])
    y_b, gdn_s, gdn_conv = _gdn(qkv.reshape(seq, bsz * D_QKV), ba.reshape(seq, bsz * GATE_PAD),
                                z.reshape(seq, bsz * D_B), gdn_conv0, gdn_s0, p["gdn"], bsz,
                                cfg["gdn_t"], cfg["gdn_nb"])
    x1 = _mix_ffn(x_bm, True, [y_a.reshape(lv, bv * D_A), y_b.reshape(lv, bv * D_B)],
                  [p["ab_w_out_a"], p["ab_w_out_b"]], p["norm_ffn0"], p["wg0"], p["wu0"], p["wd0"],
                  None, bv, lv, False)
    acts = _rwkv_proj(x1, rw_shift0.reshape(bv, bsz // bv, D_MODEL), p["rw"])
    shift = acts[7].reshape(bsz, D_MODEL)
    yg, rw_s = _rwkv_scan([a.reshape(seq, bsz * D_MODEL) for a in acts[:7]], rw_s0, p["rw"], bsz,
                          cfg["rwkv_t"], cfg["rwkv_nb"], cfg["rwkv_group"])
    y = _mix_ffn(x1, False, [yg.reshape(lv, bv * D_MODEL)], [p["rw"]["wo"]], p["norm_ffn1"], p["wg1"],
                 p["wu1"], p["wd1"], p["norm_final"], bv, lv, out_bm)
    return y, hr, hi, gdn_s, gdn_conv, rw_s, shift


def kernel(x_prompt, x_sample, state_s5_re, state_s5_im, state_gdn, state_gdn_conv, state_rwkv, state_rwkv_shift, norm_mix, norm_ffn, norm_final, ffn_w_gate, ffn_w_up, ffn_w_down, ab_w_in, ab_w_out, s5_lambda_re, s5_lambda_im, s5_log_step, s5_B_re, s5_B_im, s5_C_re, s5_C_im, s5_D, s5_w_glu, gdn_conv_w, gdn_A_log, gdn_dt_bias, gdn_norm_w, rw_maa, rw_w_r, rw_w_k, rw_w_v, rw_w_o, rw_w0, rw_w1, rw_w2, rw_a0, rw_a1, rw_a2, rw_g1, rw_g2, rw_k_k, rw_k_a, rw_r_k, rw_ln_w, rw_ln_b):
    bsz_p, seq_p, _ = x_prompt.shape
    bsz_s, seq_s, _ = x_sample.shape

    w_in = ab_w_in[0]
    n_main = D_A + D_QKV
    w_in = jnp.concatenate([w_in[:, :n_main], w_in[:, n_main + 2 * H_B:], w_in[:, n_main:n_main + 2 * H_B]],
                           axis=1)
    w_in = _pad_to(w_in, 1, D_IN_PAD).astype(BF16)

    def ff(w, layer):
        return jnp.transpose(w[layer].reshape(D_MODEL, N_FF_TILES, FF_TILE), (1, 0, 2)).astype(BF16)

    gate_lanes = lambda v: _pad_to(jnp.concatenate([jnp.zeros((H_B,), F32), v.astype(F32)]), 0,
                                   GATE_PAD).reshape(1, GATE_PAD)
    p = dict(
        norm_mix0=_row(norm_mix[0]), norm_ffn0=_row(norm_ffn[0]), norm_ffn1=_row(norm_ffn[1]),
        norm_final=_row(norm_final), ab_w_in=w_in,
        ab_w_out_a=ab_w_out[0][:D_A].astype(BF16), ab_w_out_b=ab_w_out[0][D_A:].astype(BF16),
        wg0=ff(ffn_w_gate, 0), wu0=ff(ffn_w_up, 0),
        wd0=ffn_w_down[0].reshape(N_FF_TILES, FF_TILE, D_MODEL).astype(BF16),
        wg1=ff(ffn_w_gate, 1), wu1=ff(ffn_w_up, 1),
        wd1=ffn_w_down[1].reshape(N_FF_TILES, FF_TILE, D_MODEL).astype(BF16),
        s5=dict(
            lr=_row(s5_lambda_re[0]), li=_row(s5_lambda_im[0]),
            ls=_row(jnp.repeat(s5_log_step[0], P_STATE)),
            bre=_block_diag_groups(jnp.swapaxes(s5_B_re[0], 1, 2), S5_GROUP, P_STATE).astype(BF16),
            bim=_block_diag_groups(jnp.swapaxes(s5_B_im[0], 1, 2), S5_GROUP, P_STATE).astype(BF16),
            cre=_block_diag_groups(jnp.swapaxes(s5_C_re[0], 1, 2), P_STATE, S5_GROUP).astype(BF16),
            cim=_block_diag_groups(jnp.swapaxes(s5_C_im[0], 1, 2), P_STATE, S5_GROUP).astype(BF16),
            d=_row(s5_D[0]), wglu=s5_w_glu[0].astype(BF16)),
        gdn=dict(conv_w=gdn_conv_w[0].astype(F32), alog=gate_lanes(gdn_A_log[0]),
                 dtb=gate_lanes(gdn_dt_bias[0]), norm_w=_row(gdn_norm_w[0])),
        rw=dict(
            norm_w=_row(norm_mix[1]), maa=rw_maa[0].astype(F32),
            wr=rw_w_r[0].astype(BF16), wk=rw_w_k[0].astype(BF16), wv=rw_w_v[0].astype(BF16),
            wo=rw_w_o[0].astype(BF16),
            w0=_row(rw_w0[0]), w1=_pad_to(rw_w1[0], 1, LANES).astype(BF16),
            w2=_pad_to(rw_w2[0], 0, LANES).astype(BF16),
            a0=_row(rw_a0[0]), a1=_pad_to(rw_a1[0], 1, LANES).astype(BF16),
            a2=_pad_to(rw_a2[0], 0, LANES).astype(BF16),
            g1=_pad_to(rw_g1[0], 1, 2 * LANES).astype(BF16), g2=_pad_to(rw_g2[0], 0, 2 * LANES).astype(BF16),
            k_k=_row(rw_k_k[0]), k_a=_row(rw_k_a[0]), r_k=_row(rw_r_k[0]),
            ln_w=_row(rw_ln_w[0]), ln_b=_row(rw_ln_b[0])),
    )

    zeros = lambda *shape: jnp.zeros(shape, F32)
    yp, p_hr, p_hi, p_gdn, p_conv, p_rw, p_shift = _trunk(
        x_prompt, bsz_p, seq_p, zeros(bsz_p, N_S5), zeros(bsz_p, N_S5), zeros(bsz_p, H_B, DK_B, DV_B),
        zeros(bsz_p, CONV_W - 1, D_QKV), zeros(bsz_p, H_C, N_HEAD_C, N_HEAD_C), zeros(bsz_p, D_MODEL), p,
        True)

    xs_tm = jnp.transpose(x_sample, (1, 0, 2)).reshape(1, seq_s * bsz_s, D_MODEL)
    ys, s_hr, s_hi, s_gdn, s_conv, s_rw, s_shift = _trunk(
        xs_tm, bsz_s, seq_s, state_s5_re[0].reshape(bsz_s, N_S5), state_s5_im[0].reshape(bsz_s, N_S5),
        state_gdn[0], state_gdn_conv[0], state_rwkv[0], state_rwkv_shift[0], p, True)
    y_sample = jnp.transpose(ys.reshape(seq_s, bsz_s, D_MODEL), (1, 0, 2))

    s5_shape = lambda b: (1, b, G_A, P_STATE)
    return (yp, y_sample,
            p_hr.reshape(s5_shape(bsz_p)), p_hi.reshape(s5_shape(bsz_p)), p_gdn[None], p_conv[None],
            p_rw[None], p_shift[None],
            s_hr.reshape(s5_shape(bsz_s)), s_hi.reshape(s5_shape(bsz_s)), s_gdn[None], s_conv[None],
            s_rw[None], s_shift[None])
```

```python
import functools
import math

import jax
import jax.numpy as jnp
from jax import lax
from jax.experimental import pallas as pl
from jax.experimental.pallas import tpu as pltpu

F32 = jnp.float32
BF16 = jnp.bfloat16

D_MODEL = 1024
D_A = 512
S5_GROUP = 16
G_A = 32
P_STATE = 64
N_S5 = G_A * P_STATE
D_B = 512
H_B = 4
DK_B = 128
DV_B = 128
D_QKV = 2 * H_B * DK_B + D_B
CONV_W = 4
GDN_CHUNK = 64
N_HEAD_C = 64
H_C = 16
RWKV_CHUNK = 64
RWKV_PAIR_GROUP = 8
RWKV_GN_EPS = 64e-5
D_FF = 2816
NORM_EPS = 1e-6

LANES = 128
SUBLANES = 8
FF_TILE = 256
N_FF_TILES = D_FF // FF_TILE
GATE_PAD = LANES
D_IN_PAD = 4 * 512 + 512 + GATE_PAD
S5_LANE_BLOCK = 512
N_S5_BLOCKS = N_S5 // S5_LANE_BLOCK
TOKEN_TILE = 512
RWKV_PROJ_TILE = 256
MIB = 1024 * 1024


def _cparams(semantics, vmem_mib):
    return pltpu.CompilerParams(dimension_semantics=semantics, vmem_limit_bytes=vmem_mib * MIB)


def _rmsnorm(x, w):
    return x * lax.rsqrt(jnp.mean(x * x, axis=-1, keepdims=True) + NORM_EPS) * w


def _sigmoid(x):
    return 0.5 + 0.5 * jnp.tanh(0.5 * x)


def _silu(x):
    half = 0.5 * x
    return half + half * jnp.tanh(half)


def _softplus(x):
    return jnp.maximum(x, 0.0) + jnp.log1p(jnp.exp(-jnp.abs(x)))


def _bdot(a, b):
    return jnp.dot(a.astype(BF16), b.astype(BF16), preferred_element_type=F32)


def _bdot_nt(a, b):
    return lax.dot_general(a.astype(BF16), b.astype(BF16), (((1,), (1,)), ((), ())),
                           preferred_element_type=F32)


def _bdot_tn(a, b):
    return lax.dot_general(a.astype(BF16), b.astype(BF16), (((0,), (0,)), ((), ())),
                           preferred_element_type=F32)


def _split(a):
    hi = a.astype(BF16)
    return hi, (a - hi.astype(F32)).astype(BF16)


def _split3(a):
    p1 = a.astype(BF16)
    r1 = a - p1.astype(F32)
    p2 = r1.astype(BF16)
    return p1, p2, (r1 - p2.astype(F32)).astype(BF16)


def _split_dot(a, exact_bf16):
    hi, lo = _split(a)
    return (jnp.dot(hi, exact_bf16, preferred_element_type=F32)
            + jnp.dot(lo, exact_bf16, preferred_element_type=F32))


def _cumsum_rows(tri_bf16, w, parts=3):
    split = _split3(w) if parts == 3 else _split(w)
    return sum(jnp.dot(tri_bf16, part, preferred_element_type=F32) for part in split)


def _iota2(shape):
    return (lax.broadcasted_iota(jnp.int32, shape, 0), lax.broadcasted_iota(jnp.int32, shape, 1))


def _inv_identity_plus_wide(lmws, t):
    row, col = _iota2((t, 2 * t))
    left = col < t
    col = jnp.where(left, col, col - t)
    eye = (row == col).astype(F32)
    base = min(t, 16)

    def blockdiag(xw):
        return jnp.concatenate([jnp.where(left, xw, 0.0), jnp.where(left, 0.0, xw)], axis=0)

    if t > base:
        same_base = (row >> 4) == (col >> 4)
        ns = [jnp.where(same_base, -lmw, 0.0) for lmw in lmws]
    else:
        ns = [-lmw for lmw in lmws]
    xs = [eye + n for n in ns]
    ps = [_bdot(n, blockdiag(n)) for n in ns]
    k = 2
    while k < base:
        bds = [blockdiag(p) for p in ps]
        if 2 * k < base:
            both = [_bdot(jnp.concatenate([p, x], axis=0), bd) for p, x, bd in zip(ps, xs, bds)]
            ps = [r[:t] for r in both]
            xs = [x + r[t:] for x, r in zip(xs, both)]
        else:
            xs = [x + _bdot(x, bd) for x, bd in zip(xs, bds)]
        k *= 2
    shift = 4
    blk = base
    while blk < t:
        same_big = (row >> (shift + 1)) == (col >> (shift + 1))
        same_small = (row >> shift) == (col >> shift)
        off = [jnp.where(same_big, jnp.where(same_small, 0.0, lmw), 0.0) for lmw in lmws]
        mids = [_bdot(o, blockdiag(x)) for o, x in zip(off, xs)]
        xs = [x - _bdot(x, blockdiag(m)) for x, m in zip(xs, mids)]
        blk *= 2
        shift += 1
    return xs


def _ab_in_kernel(x_ref, nw_ref, w_ref, u_ref, qkv_ref, z_ref, ba_ref):
    h = _rmsnorm(x_ref[...], nw_ref[...])
    p = jnp.dot(h.astype(BF16), w_ref[...], preferred_element_type=F32)
    u_ref[...] = p[:, 0:D_A]
    qkv_ref[...] = p[:, D_A:D_A + D_QKV]
    z_ref[...] = p[:, D_A + D_QKV:D_A + D_QKV + D_B]
    ba_ref[...] = p[:, D_A + D_QKV + D_B:D_IN_PAD]


def _ab_in(x_bm, norm_w, w_in):
    bv, lv, _ = x_bm.shape
    tl = min(TOKEN_TILE, lv)
    grid = (bv, lv // tl)
    tm = lambda n: pl.BlockSpec((tl, n), lambda b, i: (i, b))
    const = lambda shape: pl.BlockSpec(shape, lambda b, i: (0,) * len(shape))
    return pl.pallas_call(
        _ab_in_kernel,
        grid=grid,
        in_specs=[pl.BlockSpec((None, tl, D_MODEL), lambda b, i: (b, i, 0)),
                  const((1, D_MODEL)), const((D_MODEL, D_IN_PAD))],
        out_specs=[tm(D_A), tm(D_QKV), tm(D_B), tm(GATE_PAD)],
        out_shape=[jax.ShapeDtypeStruct((lv, bv * D_A), F32),
                   jax.ShapeDtypeStruct((lv, bv * D_QKV), F32),
                   jax.ShapeDtypeStruct((lv, bv * D_B), F32),
                   jax.ShapeDtypeStruct((lv, bv * GATE_PAD), F32)],
        compiler_params=_cparams(("parallel", "parallel"), 48),
    )(x_bm, norm_w, w_in)


def _s5_kernel(u_ref, h0r_ref, h0i_ref, lr_ref, li_ref, ls_ref, bre_ref, bim_ref, cre_ref, cim_ref,
               d_ref, wglu_ref, y_ref, hr_out, hi_out,
               ar_s, ai_s, cr_s, ci_s, hr_s, hi_s, bur_s, bui_s, yg_s, *, tc):
    rows = tc * SUBLANES

    @pl.when(pl.program_id(1) == 0)
    def _():
        lr = lr_ref[...]
        li = li_ref[...]
        dt = jnp.exp(ls_ref[...])
        mag = jnp.exp(lr * dt)
        ar = mag * jnp.cos(li * dt)
        ai = mag * jnp.sin(li * dt)
        den = lr * lr + li * li
        nr = ar - 1.0
        cr = (nr * lr + ai * li) / den
        ci = (ai * lr - nr * li) / den
        ar_s[...] = jnp.broadcast_to(ar, (SUBLANES, N_S5))
        ai_s[...] = jnp.broadcast_to(ai, (SUBLANES, N_S5))
        cr_s[...] = jnp.broadcast_to(cr, (SUBLANES, N_S5))
        ci_s[...] = jnp.broadcast_to(ci, (SUBLANES, N_S5))
        hr_s[...] = h0r_ref[...]
        hi_s[...] = h0i_ref[...]

    u = u_ref[...].reshape(rows, D_A)
    ub = u.astype(BF16)
    for j in range(N_S5_BLOCKS):
        sl = slice(j * S5_LANE_BLOCK, (j + 1) * S5_LANE_BLOCK)
        uj = ub[:, j * LANES:(j + 1) * LANES]
        pr = jnp.dot(uj, bre_ref[j], preferred_element_type=F32)
        pi = jnp.dot(uj, bim_ref[j], preferred_element_type=F32)
        crj = cr_s[0:1, sl]
        cij = ci_s[0:1, sl]
        bur_s[:, :, sl] = (crj * pr - cij * pi).reshape(tc, SUBLANES, S5_LANE_BLOCK)
        bui_s[:, :, sl] = (crj * pi + cij * pr).reshape(tc, SUBLANES, S5_LANE_BLOCK)

    for j in range(N_S5_BLOCKS):
        sl = slice(j * S5_LANE_BLOCK, (j + 1) * S5_LANE_BLOCK)
        ar = ar_s[:, sl]
        ai = ai_s[:, sl]

        def step(t, carry, sl=sl, ar=ar, ai=ai):
            hr, hi = carry
            nr = ar * hr - ai * hi + bur_s[t, :, sl]
            ni = ar * hi + ai * hr + bui_s[t, :, sl]
            bur_s[t, :, sl] = nr
            bui_s[t, :, sl] = ni
            return nr, ni

        hr, hi = lax.fori_loop(0, tc, step, (hr_s[:, sl], hi_s[:, sl]), unroll=min(tc, 8))
        hr_s[:, sl] = hr
        hi_s[:, sl] = hi

    for j in range(N_S5_BLOCKS):
        sl = slice(j * S5_LANE_BLOCK, (j + 1) * S5_LANE_BLOCK)
        cl = slice(j * LANES, (j + 1) * LANES)
        xr = bur_s[:, :, sl].reshape(rows, S5_LANE_BLOCK).astype(BF16)
        xi = bui_s[:, :, sl].reshape(rows, S5_LANE_BLOCK).astype(BF16)
        yj = (jnp.dot(xr, cre_ref[j], preferred_element_type=F32)
              - jnp.dot(xi, cim_ref[j], preferred_element_type=F32)
              + d_ref[:, cl] * u[:, cl])
        yg_s[:, cl] = jax.nn.gelu(yj)

    yg = yg_s[...]
    out = yg * _sigmoid(jnp.dot(yg.astype(BF16), wglu_ref[...], preferred_element_type=F32))
    y_ref[...] = out.reshape(tc, SUBLANES, D_A)
    hr_out[...] = hr_s[...]
    hi_out[...] = hi_s[...]


def _s5(u_tm, h0r, h0i, prm, tc):
    seq, bsz, _ = u_tm.shape
    grid = (bsz // SUBLANES, seq // tc)
    const = lambda shape: pl.BlockSpec(shape, lambda b, i: (0,) * len(shape))
    act = pl.BlockSpec((tc, SUBLANES, D_A), lambda b, i: (i, b, 0))
    st = pl.BlockSpec((SUBLANES, N_S5), lambda b, i: (b, 0))
    vec = const((1, N_S5))
    small = pltpu.VMEM((SUBLANES, N_S5), F32)
    big = pltpu.VMEM((tc, SUBLANES, N_S5), F32)
    return pl.pallas_call(
        functools.partial(_s5_kernel, tc=tc),
        grid=grid,
        in_specs=[act, st, st, vec, vec, vec,
                  const((N_S5_BLOCKS, LANES, S5_LANE_BLOCK)), const((N_S5_BLOCKS, LANES, S5_LANE_BLOCK)),
                  const((N_S5_BLOCKS, S5_LANE_BLOCK, LANES)), const((N_S5_BLOCKS, S5_LANE_BLOCK, LANES)),
                  const((1, D_A)), const((D_A, D_A))],
        out_specs=[act, st, st],
        out_shape=[jax.ShapeDtypeStruct((seq, bsz, D_A), F32),
                   jax.ShapeDtypeStruct((bsz, N_S5), F32),
                   jax.ShapeDtypeStruct((bsz, N_S5), F32)],
        scratch_shapes=[small, small, small, small, small, small, big, big,
                        pltpu.VMEM((tc * SUBLANES, D_A), F32)],
        compiler_params=_cparams(("parallel", "arbitrary"), 48),
    )(u_tm, h0r, h0i, prm["lr"], prm["li"], prm["ls"], prm["bre"], prm["bim"], prm["cre"], prm["cim"],
      prm["d"], prm["wglu"])


def _gdn_kernel(qkv_ref, ba_ref, z_ref, conv0_ref, s0_ref, cw_ref, alog_ref, dtb_ref, nw_ref,
                o_ref, sfin_ref, convn_ref, xbuf, s_s, *, t, nb):
    pad = SUBLANES
    hist = CONV_W - 1

    @pl.when(pl.program_id(1) == 0)
    def _():
        for e in range(nb):
            xbuf[pad - hist:pad, e * D_QKV:(e + 1) * D_QKV] = conv0_ref[e]
        s_s[...] = s0_ref[...].reshape(nb * H_B, DK_B, DV_B)

    row, col = _iota2((t, t))
    causal = row >= col
    strict = row > col
    tri = causal.astype(BF16)
    lane = lax.broadcasted_iota(jnp.int32, (t, LANES), 1)
    nw = nw_ref[...]
    cw = cw_ref[...]
    ys, betas, gcs = [], [], []
    for e in range(nb):
        cols = slice(e * D_QKV, (e + 1) * D_QKV)
        x = qkv_ref[:, cols]
        xbuf[pad:pad + t, cols] = x
        acc = x * cw[hist:hist + 1, :]
        for j in range(hist):
            acc = acc + xbuf[pad - hist + j:pad - hist + j + t, cols] * cw[j:j + 1, :]
        last = xbuf[pad + t - hist:pad + t, cols]
        convn_ref[e] = last
        xbuf[pad - hist:pad, cols] = last
        ys.append(_silu(acc))
        ba = ba_ref[:, e * GATE_PAD:(e + 1) * GATE_PAD]
        betas.append(_sigmoid(ba))
        g = -jnp.exp(alog_ref[...]) * _softplus(ba + dtb_ref[...])
        gcs.append(_cumsum_rows(tri, g))

    chains = [(e, h) for e in range(nb) for h in range(H_B)]
    heads = range(len(chains))
    bcol = [betas[e][:, h:h + 1] for e, h in chains]
    gcol = [gcs[e][:, H_B + h:H_B + h + 1] for e, h in chains]
    y_of = [ys[e] for e, h in chains]
    h_of = [h for e, h in chains]
    decay = []
    gc_parts = [[p.astype(F32) for p in _split3(gc)] for gc in gcs]
    for e, hh in chains:
        p1, p2, p3 = [p[:, H_B + hh:H_B + hh + 1] for p in gc_parts[e]]
        dl = jnp.where(lane == 0, p1, jnp.where(lane == 1, p2, jnp.where(lane == 2, p3,
                                                                         jnp.where(lane < 6, 1.0, 0.0))))
        dr = jnp.where(lane < 3, 1.0, jnp.where(lane == 3, -p1, jnp.where(lane == 4, -p2,
                                                                        jnp.where(lane == 5, -p3, 0.0))))
        decay.append(jnp.exp(jnp.where(causal, _bdot_nt(dl, dr), -jnp.inf)))
    eg = [jnp.exp(gcol[h]) for h in heads]
    glast = [gcol[h][t - 1:t, :] for h in heads]
    q = [y_of[c][:, h_of[c] * DK_B:(h_of[c] + 1) * DK_B] for c in heads]
    k = [y_of[c][:, (H_B + h_of[c]) * DK_B:(H_B + h_of[c] + 1) * DK_B] for c in heads]
    v = [y_of[c][:, 2 * H_B * DK_B + h_of[c] * DV_B:2 * H_B * DK_B + (h_of[c] + 1) * DV_B] for c in heads]
    ones_sq = jnp.ones((DK_B, DK_B), BF16)
    q = [x * (lax.rsqrt(_bdot(x * x, ones_sq) + NORM_EPS) * (DK_B ** -0.5)) for x in q]
    k = [x * lax.rsqrt(_bdot(x * x, ones_sq) + NORM_EPS) for x in k]
    kb = [k[h] * bcol[h] for h in heads]
    vb = [v[h] * bcol[h] for h in heads]
    lm = [jnp.where(strict, _bdot_nt(kb[h], k[h]) * decay[h], 0.0) for h in heads]
    attn = [_bdot_nt(q[h], k[h]) * decay[h] for h in heads]
    s = [s_s[h] for h in heads]
    qs = [_bdot(q[h] * eg[h], s[h]) for h in heads]
    pairs = [(c, c + 1) for c in range(0, len(chains), 2)]
    zeros_wide = jnp.zeros((t, DK_B + DV_B), F32)
    tinv = _inv_identity_plus_wide([jnp.concatenate([lm[c0], lm[c1]], axis=1) for c0, c1 in pairs], t)
    kbg = [kb[h] * eg[h] for h in heads]
    uw = [_bdot(tinv[i], jnp.concatenate(
        [jnp.concatenate([vb[c0], kbg[c0], zeros_wide], axis=1),
         jnp.concatenate([zeros_wide, vb[c1], kbg[c1]], axis=1)], axis=0)) for i, (c0, c1) in enumerate(pairs)]
    u = [uw[c // 2][:, (c % 2) * (DK_B + DV_B):(c % 2) * (DK_B + DV_B) + DV_B] for c in heads]
    w = [uw[c // 2][:, (c % 2) * (DK_B + DV_B) + DV_B:(c % 2 + 1) * (DK_B + DV_B)] for c in heads]
    v_new = [u[h] - _bdot(w[h], s[h]) for h in heads]
    zeros_v = jnp.zeros((t, DV_B), F32)
    av = [_bdot(jnp.concatenate([attn[c0], attn[c1]], axis=1), jnp.concatenate(
        [jnp.concatenate([v_new[c0], zeros_v], axis=1),
         jnp.concatenate([zeros_v, v_new[c1]], axis=1)], axis=0)) for c0, c1 in pairs]
    o = [qs[c] + av[c // 2][:, (c % 2) * DV_B:(c % 2 + 1) * DV_B] for c in heads]
    upd = [_bdot_tn(k[h] * jnp.exp(glast[h] - gcol[h]), v_new[h]) for h in heads]
    for c, (e, h) in enumerate(chains):
        s_new = s[c] * jnp.exp(glast[c]) + upd[c]
        s_s[c] = s_new
        sfin_ref[e, h] = s_new
        cols = slice(e * D_B + h * DV_B, e * D_B + (h + 1) * DV_B)
        o_ref[:, cols] = _rmsnorm(o[c], nw) * _silu(z_ref[:, cols])


def _gdn(qkv_tm, ba_tm, z_tm, conv0, s0, prm, bsz, t, nb):
    seq = qkv_tm.shape[0]
    grid = (bsz // nb, seq // t)
    const = lambda shape: pl.BlockSpec(shape, lambda b, i: (0,) * len(shape))
    tm = lambda n: pl.BlockSpec((t, nb * n), lambda b, i: (i, b))
    state = pl.BlockSpec((nb, H_B, DK_B, DV_B), lambda b, i: (b, 0, 0, 0))
    conv = pl.BlockSpec((nb, CONV_W - 1, D_QKV), lambda b, i: (b, 0, 0))
    return pl.pallas_call(
        functools.partial(_gdn_kernel, t=t, nb=nb),
        grid=grid,
        in_specs=[tm(D_QKV), tm(GATE_PAD), tm(D_B), conv, state,
                  const((CONV_W, D_QKV)), const((1, GATE_PAD)), const((1, GATE_PAD)), const((1, DV_B))],
        out_specs=[tm(D_B), state, conv],
        out_shape=[jax.ShapeDtypeStruct((seq, bsz * D_B), F32),
                   jax.ShapeDtypeStruct((bsz, H_B, DK_B, DV_B), F32),
                   jax.ShapeDtypeStruct((bsz, CONV_W - 1, D_QKV), F32)],
        scratch_shapes=[pltpu.VMEM((t + SUBLANES, nb * D_QKV), F32),
                        pltpu.VMEM((nb * H_B, DK_B, DV_B), F32)],
        compiler_params=_cparams(("parallel", "arbitrary"), 48),
    )(qkv_tm, ba_tm, z_tm, conv0, s0, prm["conv_w"], prm["alog"], prm["dtb"], prm["norm_w"])


def _mix_ffn_kernel(*refs, n_mix, final_norm):
    x_ref = refs[0]
    mix_refs = refs[1:1 + n_mix]
    wout_refs = refs[1 + n_mix:1 + 2 * n_mix]
    nffn_ref, wg_ref, wu_ref, wd_ref = refs[1 + 2 * n_mix:5 + 2 * n_mix]
    rest = refs[5 + 2 * n_mix:]
    if final_norm:
        nfin_ref, out_ref = rest
    else:
        (out_ref,) = rest

    x1 = x_ref[...]
    for m_ref, w_ref in zip(mix_refs, wout_refs):
        x1 = x1 + jnp.dot(m_ref[...].astype(BF16), w_ref[...], preferred_element_type=F32)
    h = _rmsnorm(x1, nffn_ref[...]).astype(BF16)
    out_ref[...] = x1

    for c in range(N_FF_TILES):
        cols = slice(c * FF_TILE, (c + 1) * FF_TILE)
        gate = jnp.dot(h, wg_ref[:, cols], preferred_element_type=F32)
        up = jnp.dot(h, wu_ref[:, cols], preferred_element_type=F32)
        act = (_silu(gate) * up).astype(BF16)
        out_ref[...] += jnp.dot(act, wd_ref[cols, :], preferred_element_type=F32)
    if final_norm:
        out_ref[...] = _rmsnorm(out_ref[...], nfin_ref[...])


def _mix_ffn(x, x_bm, mixes, wouts, nffn, wg, wu, wd, nfin, bv, lv, out_bm):
    tl = min(TOKEN_TILE, lv)
    grid = (bv, lv // tl)
    const = lambda shape: pl.BlockSpec(shape, lambda b, i: (0,) * len(shape),
                                       pipeline_mode=pl.Buffered(1))
    tm = lambda n: pl.BlockSpec((tl, n), lambda b, i: (i, b))
    bm = pl.BlockSpec((None, tl, D_MODEL), lambda b, i: (b, i, 0))
    in_specs = [bm if x_bm else tm(D_MODEL)]
    in_specs += [tm(m.shape[1] // bv) for m in mixes]
    in_specs += [const(w.shape) for w in wouts]
    in_specs += [const((1, D_MODEL)), const(wg.shape), const(wu.shape), const(wd.shape)]
    args = [x, *mixes, *wouts, nffn, wg, wu, wd]
    if nfin is not None:
        in_specs.append(const((1, D_MODEL)))
        args.append(nfin)
    if out_bm:
        out_spec, out_shape = bm, jax.ShapeDtypeStruct((bv, lv, D_MODEL), F32)
    else:
        out_spec, out_shape = tm(D_MODEL), jax.ShapeDtypeStruct((lv, bv * D_MODEL), F32)
    return pl.pallas_call(
        functools.partial(_mix_ffn_kernel, n_mix=len(mixes), final_norm=nfin is not None),
        grid=grid,
        in_specs=in_specs,
        out_specs=out_spec,
        out_shape=out_shape,
        compiler_params=_cparams(("parallel", "parallel"), 56),
    )(*args)


def _rwkv_proj_kernel(x_ref, shift0_ref, nw_ref, maa_ref, wr_ref, wk_ref, wv_ref, w0_ref, w1_ref, w2_ref,
                      a0_ref, a1_ref, a2_ref, g1_ref, g2_ref, kk_ref, ka_ref,
                      r_out, w_out, k_out, v_out, kk_out, a_out, g_out, shift_out, hbuf, *, shift):
    pad = -(-shift // SUBLANES) * SUBLANES

    @pl.when(pl.program_id(1) == 0)
    def _():
        hbuf[pad - shift:pad, :] = shift0_ref[...]

    h = _rmsnorm(x_ref[...], nw_ref[...])
    rows = h.shape[0]
    hbuf[pad:pad + rows, :] = h
    prev = hbuf[pad - shift:pad - shift + rows, :]
    last = h[rows - shift:]
    hbuf[pad - shift:pad, :] = last
    shift_out[...] = last
    hb = h.astype(BF16)
    xxb = (prev - h).astype(BF16)
    maa = maa_ref[...].astype(BF16)
    xr, xw, xk, xv, xa, xg = [hb + xxb * maa[j:j + 1, :] for j in range(6)]
    r = _bdot(xr, wr_ref[...])
    k = _bdot(xk, wk_ref[...])
    v = _bdot(xv, wv_ref[...])
    u = w0_ref[...] + _bdot(jnp.tanh(_bdot(xw, w1_ref[...])), w2_ref[...])
    a = _sigmoid(a0_ref[...] + _bdot(_bdot(xa, a1_ref[...]), a2_ref[...]))
    g = _bdot(_sigmoid(_bdot(xg, g1_ref[...])), g2_ref[...])
    r_out[...] = r
    w_out[...] = -math.exp(-0.5) * _sigmoid(u)
    k_out[...] = k * (1.0 + (a - 1.0) * ka_ref[...])
    v_out[...] = v
    kk_out[...] = k * kk_ref[...]
    a_out[...] = a
    g_out[...] = g


def _rwkv_proj(x_tm, shift0, prm):
    bv, shift, _ = shift0.shape
    lv = x_tm.shape[0]
    tr = max(shift, min(RWKV_PROJ_TILE, lv))
    grid = (bv, lv // tr)
    const = lambda shape: pl.BlockSpec(shape, lambda b, i: (0,) * len(shape))
    act = pl.BlockSpec((tr, D_MODEL), lambda b, i: (i, b))
    carry = pl.BlockSpec((None, shift, D_MODEL), lambda b, i: (b, 0, 0))
    vec = const((1, D_MODEL))
    weights = [prm[n] for n in ("wr", "wk", "wv")]
    in_specs = [act, carry, vec, const((6, D_MODEL))]
    in_specs += [const(w.shape) for w in weights]
    in_specs += [vec, const(prm["w1"].shape), const(prm["w2"].shape),
                 vec, const(prm["a1"].shape), const(prm["a2"].shape),
                 const(prm["g1"].shape), const(prm["g2"].shape), vec, vec]
    out_act = jax.ShapeDtypeStruct((lv, bv * D_MODEL), F32)
    pad = -(-shift // SUBLANES) * SUBLANES
    return pl.pallas_call(
        functools.partial(_rwkv_proj_kernel, shift=shift),
        grid=grid,
        in_specs=in_specs,
        out_specs=[act] * 7 + [carry],
        out_shape=[out_act] * 7 + [jax.ShapeDtypeStruct((bv, shift, D_MODEL), F32)],
        scratch_shapes=[pltpu.VMEM((pad + tr, D_MODEL), F32)],
        compiler_params=_cparams(("parallel", "arbitrary"), 56),
    )(x_tm, shift0, prm["norm_w"], prm["maa"], prm["wr"], prm["wk"], prm["wv"], prm["w0"], prm["w1"],
      prm["w2"], prm["a0"], prm["a1"], prm["a2"], prm["g1"], prm["g2"], prm["k_k"], prm["k_a"])


def _rwkv_scan_kernel(r_ref, w_ref, k_ref, v_ref, kk_ref, a_ref, g_ref, s0_ref, rk_ref, lnw_ref, lnb_ref,
                      y_ref, sfin_ref, s_s, *, t, nb, group):
    n = N_HEAD_C
    zero_blk = jnp.zeros((n, n), F32)

    pairs_per_seq = H_C // 2

    @pl.when(pl.program_id(1) == 0)
    def _():
        for jj in range(nb * pairs_per_seq):
            e, j = divmod(jj, pairs_per_seq)
            top = jnp.concatenate([s0_ref[e, 2 * j], zero_blk], axis=1)
            bot = jnp.concatenate([zero_blk, s0_ref[e, 2 * j + 1]], axis=1)
            s_s[jj] = jnp.concatenate([top, bot], axis=0)

    row, col = _iota2((t, t))
    tri = (row >= col).astype(BF16)
    roww, colw = _iota2((t, 2 * t))
    colw = jnp.where(colw >= t, colw - t, colw)
    incl_w = roww >= colw
    strict_w = roww > colw
    lane = lax.broadcasted_iota(jnp.int32, (t, LANES), 1)
    head0 = lane < n

    def by_head(x):
        return jnp.concatenate([jnp.where(head0, x, 0.0), jnp.where(head0, 0.0, x)], axis=0)

    prow, pcol = _iota2((LANES, LANES))
    same_head = (prow >> 6) == (pcol >> 6)
    ones_blk = same_head.astype(BF16)

    def pair_group(js):
        ps = range(len(js))
        sl = [slice(j * LANES, (j + 1) * LANES) for j in js]
        psl = [slice((j % pairs_per_seq) * LANES, (j % pairs_per_seq + 1) * LANES) for j in js]
        r = [r_ref[:, x] for x in sl]
        w = [w_ref[:, x] for x in sl]
        k = [k_ref[:, x] for x in sl]
        v = [v_ref[:, x] for x in sl]
        kkr = [kk_ref[:, x] for x in sl]
        a = [a_ref[:, x] for x in sl]
        kk = [x * lax.rsqrt(_bdot(x * x, ones_blk) + NORM_EPS) for x in kkr]
        bv = [kk[p] * a[p] for p in ps]
        gc = [_cumsum_rows(tri, x, parts=2) for x in w]
        glast = [x[t - 1:t, :] for x in gc]
        pinv = [jnp.exp(-x) for x in gc]
        at = [-kk[p] * jnp.exp(gc[p] - w[p]) for p in ps]
        bt = [bv[p] * pinv[p] for p in ps]
        kt = [k[p] * pinv[p] for p in ps]
        rt = [r[p] * jnp.exp(gc[p]) for p in ps]
        rem = [jnp.exp(glast[p] - gc[p]) for p in ps]
        lhs = [jnp.concatenate([at[p], rt[p]], axis=0) for p in ps]
        gm = [_bdot_nt(lhs[p], jnp.concatenate([by_head(bt[p]), by_head(kt[p])], axis=0)) for p in ps]
        s = [s_s[j] for j in js]
        ars = [_bdot_nt(lhs[p], s[p]) for p in ps]
        v2 = [by_head(x) for x in v]
        aab = [jnp.where(strict_w, gm[p][:t, :2 * t], 0.0) for p in ps]
        aak = [jnp.where(strict_w, gm[p][:t, 2 * t:], 0.0) for p in ps]
        rhs = [ars[p][:t] + _bdot(aak[p], v2[p]) for p in ps]
        tinv = _inv_identity_plus_wide([-x for x in aab], t)
        u = [_bdot(tinv[p], by_head(rhs[p])) for p in ps]
        rb = [jnp.where(incl_w, gm[p][t:, :2 * t], 0.0) for p in ps]
        rkm = [jnp.where(incl_w, gm[p][t:, 2 * t:], 0.0) for p in ps]
        y = [ars[p][t:] + _bdot(rb[p], by_head(u[p])) + _bdot(rkm[p], v2[p]) for p in ps]
        upd = [_bdot_tn(jnp.concatenate([u[p], v[p]], axis=0),
                        jnp.concatenate([bv[p] * rem[p], k[p] * rem[p]], axis=0)) for p in ps]
        for p, j in enumerate(js):
            s_s[j] = s[p] * jnp.exp(glast[p]) + jnp.where(same_head, upd[p], 0.0)

        mu = [_split_dot(x, ones_blk) * (1.0 / n) for x in y]
        d = [y[p] - mu[p] for p in ps]
        var = [_bdot(x * x, ones_blk) * (1.0 / n) for x in d]
        bonus = [_split_dot(r[p] * k[p] * rk_ref[:, psl[p]], ones_blk) for p in ps]
        for p in ps:
            yn = d[p] * lax.rsqrt(var[p] + RWKV_GN_EPS) * lnw_ref[:, psl[p]] + lnb_ref[:, psl[p]]
            y_ref[:, sl[p]] = (yn + bonus[p] * v[p]) * g_ref[:, sl[p]]

    for first in range(0, nb * pairs_per_seq, group):
        pair_group(list(range(first, first + group)))

    for jj in range(nb * pairs_per_seq):
        e, j = divmod(jj, pairs_per_seq)
        sp = s_s[jj]
        sfin_ref[e, 2 * j] = sp[:n, :n]
        sfin_ref[e, 2 * j + 1] = sp[n:, n:]


def _rwkv_scan(acts, s0, prm, bsz, t, nb, group):
    seq = acts[0].shape[0]
    grid = (bsz // nb, seq // t)
    const = lambda shape: pl.BlockSpec(shape, lambda b, i: (0,) * len(shape))
    tm = pl.BlockSpec((t, nb * D_MODEL), lambda b, i: (i, b))
    st = pl.BlockSpec((nb, H_C, N_HEAD_C, N_HEAD_C), lambda b, i: (b, 0, 0, 0))
    vec = const((1, D_MODEL))
    return pl.pallas_call(
        functools.partial(_rwkv_scan_kernel, t=t, nb=nb, group=group),
        grid=grid,
        in_specs=[tm] * 7 + [st, vec, vec, vec],
        out_specs=[tm, st],
        out_shape=[jax.ShapeDtypeStruct((seq, bsz * D_MODEL), F32),
                   jax.ShapeDtypeStruct((bsz, H_C, N_HEAD_C, N_HEAD_C), F32)],
        scratch_shapes=[pltpu.VMEM((nb * H_C // 2, LANES, LANES), F32)],
        compiler_params=_cparams(("parallel", "arbitrary"), 48),
    )(*acts, s0, prm["r_k"], prm["ln_w"], prm["ln_b"])


def _block_diag_groups(w, rows_per_group, cols_per_group):
    g = w.shape[0]
    w = w.reshape(g // 8, 8, rows_per_group, cols_per_group)
    eye = jnp.eye(8, dtype=w.dtype)
    out = jnp.einsum("jgrc,gh->jgrhc", w, eye)
    return out.reshape(g // 8, 8 * rows_per_group, 8 * cols_per_group)


def _pad_to(w, axis, size):
    pad = [(0, 0)] * w.ndim
    pad[axis] = (0, size - w.shape[axis])
    return jnp.pad(w, pad)


def _row(v):
    return v.reshape(1, -1).astype(F32)


def _recurrence_tiling(seq):
    if seq >= GDN_CHUNK:
        return dict(s5_tc=64, gdn_t=GDN_CHUNK, gdn_nb=4, rwkv_t=RWKV_CHUNK, rwkv_nb=2,
                    rwkv_group=2 * RWKV_PAIR_GROUP)
    return dict(s5_tc=seq, gdn_t=seq, gdn_nb=8, rwkv_t=seq, rwkv_nb=4, rwkv_group=2 * RWKV_PAIR_GROUP)


def _trunk(x_bm, bsz, seq, s5_re0, s5_im0, gdn_s0, gdn_conv0, rw_s0, rw_shift0, p, out_bm):
    bv, lv, _ = x_bm.shape
    cfg = _recurrence_tiling(seq)
    u, qkv, z, ba = _ab_in(x_bm, p["norm_mix0"], p["ab_w_in"])
    y_a, hr, hi = _s5(u.reshape(seq, bsz, D_A), s5_re0, s5_im0, p["s5"], cfg["s5_tc"])
    y_b, gdn_s, gdn_conv = _gdn(qkv.reshape(seq, bsz * D_QKV), ba.reshape(seq, bsz * GATE_PAD),
                                z.reshape(seq, bsz * D_B), gdn_conv0, gdn_s0, p["gdn"], bsz,
                                cfg["gdn_t"], cfg["gdn_nb"])
    x1 = _mix_ffn(x_bm, True, [y_a.reshape(lv, bv * D_A), y_b.reshape(lv, bv * D_B)],
                  [p["ab_w_out_a"], p["ab_w_out_b"]], p["norm_ffn0"], p["wg0"], p["wu0"], p["wd0"],
                  None, bv, lv, False)
    acts = _rwkv_proj(x1, rw_shift0.reshape(bv, bsz // bv, D_MODEL), p["rw"])
    shift = acts[7].reshape(bsz, D_MODEL)
    yg, rw_s = _rwkv_scan([a.reshape(seq, bsz * D_MODEL) for a in acts[:7]], rw_s0, p["rw"], bsz,
                          cfg["rwkv_t"], cfg["rwkv_nb"], cfg["rwkv_group"])
    y = _mix_ffn(x1, False, [yg.reshape(lv, bv * D_MODEL)], [p["rw"]["wo"]], p["norm_ffn1"], p["wg1"],
                 p["wu1"], p["wd1"], p["norm_final"], bv, lv, out_bm)
    return y, hr, hi, gdn_s, gdn_conv, rw_s, shift


def kernel(x_prompt, x_sample, state_s5_re, state_s5_im, state_gdn, state_gdn_conv, state_rwkv, state_rwkv_shift, norm_mix, norm_ffn, norm_final, ffn_w_gate, ffn_w_up, ffn_w_down, ab_w_in, ab_w_out, s5_lambda_re, s5_lambda_im, s5_log_step, s5_B_re, s5_B_im, s5_C_re, s5_C_im, s5_D, s5_w_glu, gdn_conv_w, gdn_A_log, gdn_dt_bias, gdn_norm_w, rw_maa, rw_w_r, rw_w_k, rw_w_v, rw_w_o, rw_w0, rw_w1, rw_w2, rw_a0, rw_a1, rw_a2, rw_g1, rw_g2, rw_k_k, rw_k_a, rw_r_k, rw_ln_w, rw_ln_b):
    bsz_p, seq_p, _ = x_prompt.shape
    bsz_s, seq_s, _ = x_sample.shape

    w_in = ab_w_in[0]
    n_main = D_A + D_QKV
    w_in = jnp.concatenate([w_in[:, :n_main], w_in[:, n_main + 2 * H_B:], w_in[:, n_main:n_main + 2 * H_B]],
                           axis=1)
    w_in = _pad_to(w_in, 1, D_IN_PAD).astype(BF16)

    ff = lambda w, layer: w[layer].astype(BF16)

    gate_lanes = lambda v: _pad_to(jnp.concatenate([jnp.zeros((H_B,), F32), v.astype(F32)]), 0,
                                   GATE_PAD).reshape(1, GATE_PAD)
    p = dict(
        norm_mix0=_row(norm_mix[0]), norm_ffn0=_row(norm_ffn[0]), norm_ffn1=_row(norm_ffn[1]),
        norm_final=_row(norm_final), ab_w_in=w_in,
        ab_w_out_a=ab_w_out[0][:D_A].astype(BF16), ab_w_out_b=ab_w_out[0][D_A:].astype(BF16),
        wg0=ff(ffn_w_gate, 0), wu0=ff(ffn_w_up, 0),
        wd0=ff(ffn_w_down, 0),
        wg1=ff(ffn_w_gate, 1), wu1=ff(ffn_w_up, 1),
        wd1=ff(ffn_w_down, 1),
        s5=dict(
            lr=_row(s5_lambda_re[0]), li=_row(s5_lambda_im[0]),
            ls=_row(jnp.repeat(s5_log_step[0], P_STATE)),
            bre=_block_diag_groups(jnp.swapaxes(s5_B_re[0], 1, 2), S5_GROUP, P_STATE).astype(BF16),
            bim=_block_diag_groups(jnp.swapaxes(s5_B_im[0], 1, 2), S5_GROUP, P_STATE).astype(BF16),
            cre=_block_diag_groups(jnp.swapaxes(s5_C_re[0], 1, 2), P_STATE, S5_GROUP).astype(BF16),
            cim=_block_diag_groups(jnp.swapaxes(s5_C_im[0], 1, 2), P_STATE, S5_GROUP).astype(BF16),
            d=_row(s5_D[0]), wglu=s5_w_glu[0].astype(BF16)),
        gdn=dict(conv_w=gdn_conv_w[0].astype(F32), alog=gate_lanes(gdn_A_log[0]),
                 dtb=gate_lanes(gdn_dt_bias[0]), norm_w=_row(gdn_norm_w[0])),
        rw=dict(
            norm_w=_row(norm_mix[1]), maa=rw_maa[0].astype(F32),
            wr=rw_w_r[0].astype(BF16), wk=rw_w_k[0].astype(BF16), wv=rw_w_v[0].astype(BF16),
            wo=rw_w_o[0].astype(BF16),
            w0=_row(rw_w0[0]), w1=_pad_to(rw_w1[0], 1, LANES).astype(BF16),
            w2=_pad_to(rw_w2[0], 0, LANES).astype(BF16),
            a0=_row(rw_a0[0]), a1=_pad_to(rw_a1[0], 1, LANES).astype(BF16),
            a2=_pad_to(rw_a2[0], 0, LANES).astype(BF16),
            g1=_pad_to(rw_g1[0], 1, 2 * LANES).astype(BF16), g2=_pad_to(rw_g2[0], 0, 2 * LANES).astype(BF16),
            k_k=_row(rw_k_k[0]), k_a=_row(rw_k_a[0]), r_k=_row(rw_r_k[0]),
            ln_w=_row(rw_ln_w[0]), ln_b=_row(rw_ln_b[0])),
    )

    zeros = lambda *shape: jnp.zeros(shape, F32)
    yp, p_hr, p_hi, p_gdn, p_conv, p_rw, p_shift = _trunk(
        x_prompt, bsz_p, seq_p, zeros(bsz_p, N_S5), zeros(bsz_p, N_S5), zeros(bsz_p, H_B, DK_B, DV_B),
        zeros(bsz_p, CONV_W - 1, D_QKV), zeros(bsz_p, H_C, N_HEAD_C, N_HEAD_C), zeros(bsz_p, D_MODEL), p,
        True)

    xs_tm = jnp.transpose(x_sample, (1, 0, 2)).reshape(1, seq_s * bsz_s, D_MODEL)
    ys, s_hr, s_hi, s_gdn, s_conv, s_rw, s_shift = _trunk(
        xs_tm, bsz_s, seq_s, state_s5_re[0].reshape(bsz_s, N_S5), state_s5_im[0].reshape(bsz_s, N_S5),
        state_gdn[0], state_gdn_conv[0], state_rwkv[0], state_rwkv_shift[0], p, True)
    y_sample = jnp.transpose(ys.reshape(seq_s, bsz_s, D_MODEL), (1, 0, 2))

    s5_shape = lambda b: (1, b, G_A, P_STATE)
    return (yp, y_sample,
            p_hr.reshape(s5_shape(bsz_p)), p_hi.reshape(s5_shape(bsz_p)), p_gdn[None], p_conv[None],
            p_rw[None], p_shift[None],
            s_hr.reshape(s5_shape(bsz_s)), s_hi.reshape(s5_shape(bsz_s)), s_gdn[None], s_conv[None],
            s_rw[None], s_shift[None])
```

```python
import functools
import math

import jax
import jax.numpy as jnp
from jax import lax
from jax.experimental import pallas as pl
from jax.experimental.pallas import tpu as pltpu

F32 = jnp.float32
BF16 = jnp.bfloat16

D_MODEL = 1024
D_A = 512
S5_GROUP = 16
G_A = 32
P_STATE = 64
N_S5 = G_A * P_STATE
D_B = 512
H_B = 4
DK_B = 128
DV_B = 128
D_QKV = 2 * H_B * DK_B + D_B
CONV_W = 4
GDN_CHUNK = 64
N_HEAD_C = 64
H_C = 16
RWKV_CHUNK = 64
RWKV_PAIR_GROUP = 8
RWKV_STAGGER = 7
RWKV_GN_EPS = 64e-5
D_FF = 2816
NORM_EPS = 1e-6

LANES = 128
SUBLANES = 8
FF_TILE = 256
N_FF_TILES = D_FF // FF_TILE
GATE_PAD = LANES
D_IN_PAD = 4 * 512 + 512 + GATE_PAD
S5_LANE_BLOCK = 512
N_S5_BLOCKS = N_S5 // S5_LANE_BLOCK
TOKEN_TILE = 512
RWKV_PROJ_TILE = 256
MIB = 1024 * 1024


def _cparams(semantics, vmem_mib):
    return pltpu.CompilerParams(dimension_semantics=semantics, vmem_limit_bytes=vmem_mib * MIB)


def _rmsnorm(x, w):
    return x * lax.rsqrt(jnp.mean(x * x, axis=-1, keepdims=True) + NORM_EPS) * w


def _sigmoid(x):
    return 0.5 + 0.5 * jnp.tanh(0.5 * x)


def _silu(x):
    half = 0.5 * x
    return half + half * jnp.tanh(half)


def _softplus(x):
    return jnp.maximum(x, 0.0) + jnp.log1p(jnp.exp(-jnp.abs(x)))


def _bdot(a, b):
    return jnp.dot(a.astype(BF16), b.astype(BF16), preferred_element_type=F32)


def _bdot_nt(a, b):
    return lax.dot_general(a.astype(BF16), b.astype(BF16), (((1,), (1,)), ((), ())),
                           preferred_element_type=F32)


def _bdot_tn(a, b):
    return lax.dot_general(a.astype(BF16), b.astype(BF16), (((0,), (0,)), ((), ())),
                           preferred_element_type=F32)


def _split(a):
    hi = a.astype(BF16)
    return hi, (a - hi.astype(F32)).astype(BF16)


def _split3(a):
    p1 = a.astype(BF16)
    r1 = a - p1.astype(F32)
    p2 = r1.astype(BF16)
    return p1, p2, (r1 - p2.astype(F32)).astype(BF16)


def _split_dot(a, exact_bf16):
    hi, lo = _split(a)
    return (jnp.dot(hi, exact_bf16, preferred_element_type=F32)
            + jnp.dot(lo, exact_bf16, preferred_element_type=F32))


def _cumsum_rows(tri_bf16, w, parts=3):
    split = _split3(w) if parts == 3 else _split(w)
    return sum(jnp.dot(tri_bf16, part, preferred_element_type=F32) for part in split)


def _iota2(shape):
    return (lax.broadcasted_iota(jnp.int32, shape, 0), lax.broadcasted_iota(jnp.int32, shape, 1))


def _run_staggered(gens, lag):
    pending, active, tick = list(gens), [], 0
    while pending or active:
        if pending and tick % lag == 0:
            active.append(pending.pop(0))
        for g in list(active):
            try:
                next(g)
            except StopIteration:
                active.remove(g)
        tick += 1


def _inv_identity_plus_wide(lmws, t):
    gen = _inv_identity_plus_wide_stages(lmws, t)
    while True:
        try:
            next(gen)
        except StopIteration as done:
            return done.value


def _inv_identity_plus_wide_stages(lmws, t):
    row, col = _iota2((t, 2 * t))
    left = col < t
    col = jnp.where(left, col, col - t)
    eye = (row == col).astype(F32)
    base = min(t, 16)

    def blockdiag(xw):
        return jnp.concatenate([jnp.where(left, xw, 0.0), jnp.where(left, 0.0, xw)], axis=0)

    if t > base:
        same_base = (row >> 4) == (col >> 4)
        ns = [jnp.where(same_base, -lmw, 0.0) for lmw in lmws]
    else:
        ns = [-lmw for lmw in lmws]
    xs = [eye + n for n in ns]
    ps = [_bdot(n, blockdiag(n)) for n in ns]
    yield
    k = 2
    while k < base:
        bds = [blockdiag(p) for p in ps]
        if 2 * k < base:
            both = [_bdot(jnp.concatenate([p, x], axis=0), bd) for p, x, bd in zip(ps, xs, bds)]
            ps = [r[:t] for r in both]
            xs = [x + r[t:] for x, r in zip(xs, both)]
        else:
            xs = [x + _bdot(x, bd) for x, bd in zip(xs, bds)]
        yield
        k *= 2
    shift = 4
    blk = base
    while blk < t:
        same_big = (row >> (shift + 1)) == (col >> (shift + 1))
        same_small = (row >> shift) == (col >> shift)
        off = [jnp.where(same_big, jnp.where(same_small, 0.0, lmw), 0.0) for lmw in lmws]
        mids = [_bdot(o, blockdiag(x)) for o, x in zip(off, xs)]
        yield
        xs = [x - _bdot(x, blockdiag(m)) for x, m in zip(xs, mids)]
        yield
        blk *= 2
        shift += 1
    return xs


def _ab_in_kernel(x_ref, nw_ref, w_ref, u_ref, qkv_ref, z_ref, ba_ref):
    h = _rmsnorm(x_ref[...], nw_ref[...])
    p = jnp.dot(h.astype(BF16), w_ref[...], preferred_element_type=F32)
    u_ref[...] = p[:, 0:D_A]
    qkv_ref[...] = p[:, D_A:D_A + D_QKV]
    z_ref[...] = p[:, D_A + D_QKV:D_A + D_QKV + D_B]
    ba_ref[...] = p[:, D_A + D_QKV + D_B:D_IN_PAD]


def _ab_in(x_bm, norm_w, w_in):
    bv, lv, _ = x_bm.shape
    tl = min(TOKEN_TILE, lv)
    grid = (bv, lv // tl)
    tm = lambda n: pl.BlockSpec((tl, n), lambda b, i: (i, b))
    const = lambda shape: pl.BlockSpec(shape, lambda b, i: (0,) * len(shape))
    return pl.pallas_call(
        _ab_in_kernel,
        grid=grid,
        in_specs=[pl.BlockSpec((None, tl, D_MODEL), lambda b, i: (b, i, 0)),
                  const((1, D_MODEL)), const((D_MODEL, D_IN_PAD))],
        out_specs=[tm(D_A), tm(D_QKV), tm(D_B), tm(GATE_PAD)],
        out_shape=[jax.ShapeDtypeStruct((lv, bv * D_A), F32),
                   jax.ShapeDtypeStruct((lv, bv * D_QKV), F32),
                   jax.ShapeDtypeStruct((lv, bv * D_B), F32),
                   jax.ShapeDtypeStruct((lv, bv * GATE_PAD), F32)],
        compiler_params=_cparams(("parallel", "parallel"), 48),
    )(x_bm, norm_w, w_in)


def _s5_kernel(u_ref, h0r_ref, h0i_ref, lr_ref, li_ref, ls_ref, bre_ref, bim_ref, cre_ref, cim_ref,
               d_ref, wglu_ref, y_ref, hr_out, hi_out,
               ar_s, ai_s, cr_s, ci_s, hr_s, hi_s, bur_s, bui_s, yg_s, *, tc):
    rows = tc * SUBLANES

    @pl.when(pl.program_id(1) == 0)
    def _():
        lr = lr_ref[...]
        li = li_ref[...]
        dt = jnp.exp(ls_ref[...])
        mag = jnp.exp(lr * dt)
        ar = mag * jnp.cos(li * dt)
        ai = mag * jnp.sin(li * dt)
        den = lr * lr + li * li
        nr = ar - 1.0
        cr = (nr * lr + ai * li) / den
        ci = (ai * lr - nr * li) / den
        ar_s[...] = jnp.broadcast_to(ar, (SUBLANES, N_S5))
        ai_s[...] = jnp.broadcast_to(ai, (SUBLANES, N_S5))
        cr_s[...] = jnp.broadcast_to(cr, (SUBLANES, N_S5))
        ci_s[...] = jnp.broadcast_to(ci, (SUBLANES, N_S5))
        hr_s[...] = h0r_ref[...]
        hi_s[...] = h0i_ref[...]

    u = u_ref[...].reshape(rows, D_A)
    ub = u.astype(BF16)
    for j in range(N_S5_BLOCKS):
        sl = slice(j * S5_LANE_BLOCK, (j + 1) * S5_LANE_BLOCK)
        uj = ub[:, j * LANES:(j + 1) * LANES]
        pr = jnp.dot(uj, bre_ref[j], preferred_element_type=F32)
        pi = jnp.dot(uj, bim_ref[j], preferred_element_type=F32)
        crj = cr_s[0:1, sl]
        cij = ci_s[0:1, sl]
        bur_s[:, :, sl] = (crj * pr - cij * pi).reshape(tc, SUBLANES, S5_LANE_BLOCK)
        bui_s[:, :, sl] = (crj * pi + cij * pr).reshape(tc, SUBLANES, S5_LANE_BLOCK)

    for j in range(N_S5_BLOCKS):
        sl = slice(j * S5_LANE_BLOCK, (j + 1) * S5_LANE_BLOCK)
        ar = ar_s[:, sl]
        ai = ai_s[:, sl]

        def step(t, carry, sl=sl, ar=ar, ai=ai):
            hr, hi = carry
            nr = ar * hr - ai * hi + bur_s[t, :, sl]
            ni = ar * hi + ai * hr + bui_s[t, :, sl]
            bur_s[t, :, sl] = nr
            bui_s[t, :, sl] = ni
            return nr, ni

        hr, hi = lax.fori_loop(0, tc, step, (hr_s[:, sl], hi_s[:, sl]), unroll=min(tc, 8))
        hr_s[:, sl] = hr
        hi_s[:, sl] = hi

    for j in range(N_S5_BLOCKS):
        sl = slice(j * S5_LANE_BLOCK, (j + 1) * S5_LANE_BLOCK)
        cl = slice(j * LANES, (j + 1) * LANES)
        xr = bur_s[:, :, sl].reshape(rows, S5_LANE_BLOCK).astype(BF16)
        xi = bui_s[:, :, sl].reshape(rows, S5_LANE_BLOCK).astype(BF16)
        yj = (jnp.dot(xr, cre_ref[j], preferred_element_type=F32)
              - jnp.dot(xi, cim_ref[j], preferred_element_type=F32)
              + d_ref[:, cl] * u[:, cl])
        yg_s[:, cl] = jax.nn.gelu(yj)

    yg = yg_s[...]
    out = yg * _sigmoid(jnp.dot(yg.astype(BF16), wglu_ref[...], preferred_element_type=F32))
    y_ref[...] = out.reshape(tc, SUBLANES, D_A)
    hr_out[...] = hr_s[...]
    hi_out[...] = hi_s[...]


def _s5(u_tm, h0r, h0i, prm, tc):
    seq, bsz, _ = u_tm.shape
    grid = (bsz // SUBLANES, seq // tc)
    const = lambda shape: pl.BlockSpec(shape, lambda b, i: (0,) * len(shape))
    act = pl.BlockSpec((tc, SUBLANES, D_A), lambda b, i: (i, b, 0))
    st = pl.BlockSpec((SUBLANES, N_S5), lambda b, i: (b, 0))
    vec = const((1, N_S5))
    small = pltpu.VMEM((SUBLANES, N_S5), F32)
    big = pltpu.VMEM((tc, SUBLANES, N_S5), F32)
    return pl.pallas_call(
        functools.partial(_s5_kernel, tc=tc),
        grid=grid,
        in_specs=[act, st, st, vec, vec, vec,
                  const((N_S5_BLOCKS, LANES, S5_LANE_BLOCK)), const((N_S5_BLOCKS, LANES, S5_LANE_BLOCK)),
                  const((N_S5_BLOCKS, S5_LANE_BLOCK, LANES)), const((N_S5_BLOCKS, S5_LANE_BLOCK, LANES)),
                  const((1, D_A)), const((D_A, D_A))],
        out_specs=[act, st, st],
        out_shape=[jax.ShapeDtypeStruct((seq, bsz, D_A), F32),
                   jax.ShapeDtypeStruct((bsz, N_S5), F32),
                   jax.ShapeDtypeStruct((bsz, N_S5), F32)],
        scratch_shapes=[small, small, small, small, small, small, big, big,
                        pltpu.VMEM((tc * SUBLANES, D_A), F32)],
        compiler_params=_cparams(("parallel", "arbitrary"), 48),
    )(u_tm, h0r, h0i, prm["lr"], prm["li"], prm["ls"], prm["bre"], prm["bim"], prm["cre"], prm["cim"],
      prm["d"], prm["wglu"])


def _gdn_kernel(qkv_ref, ba_ref, z_ref, conv0_ref, s0_ref, cw_ref, alog_ref, dtb_ref, nw_ref,
                o_ref, sfin_ref, convn_ref, xbuf, s_s, *, t, nb):
    pad = SUBLANES
    hist = CONV_W - 1

    @pl.when(pl.program_id(1) == 0)
    def _():
        for e in range(nb):
            xbuf[pad - hist:pad, e * D_QKV:(e + 1) * D_QKV] = conv0_ref[e]
        s_s[...] = s0_ref[...].reshape(nb * H_B, DK_B, DV_B)

    row, col = _iota2((t, t))
    causal = row >= col
    strict = row > col
    tri = causal.astype(BF16)
    lane = lax.broadcasted_iota(jnp.int32, (t, LANES), 1)
    nw = nw_ref[...]
    cw = cw_ref[...]
    ys, betas, gcs = [], [], []
    for e in range(nb):
        cols = slice(e * D_QKV, (e + 1) * D_QKV)
        x = qkv_ref[:, cols]
        xbuf[pad:pad + t, cols] = x
        acc = x * cw[hist:hist + 1, :]
        for j in range(hist):
            acc = acc + xbuf[pad - hist + j:pad - hist + j + t, cols] * cw[j:j + 1, :]
        last = xbuf[pad + t - hist:pad + t, cols]
        convn_ref[e] = last
        xbuf[pad - hist:pad, cols] = last
        ys.append(_silu(acc))
        ba = ba_ref[:, e * GATE_PAD:(e + 1) * GATE_PAD]
        betas.append(_sigmoid(ba))
        g = -jnp.exp(alog_ref[...]) * _softplus(ba + dtb_ref[...])
        gcs.append(_cumsum_rows(tri, g))

    chains = [(e, h) for e in range(nb) for h in range(H_B)]
    heads = range(len(chains))
    bcol = [betas[e][:, h:h + 1] for e, h in chains]
    gcol = [gcs[e][:, H_B + h:H_B + h + 1] for e, h in chains]
    y_of = [ys[e] for e, h in chains]
    h_of = [h for e, h in chains]
    decay = []
    gc_parts = [[p.astype(F32) for p in _split3(gc)] for gc in gcs]
    for e, hh in chains:
        p1, p2, p3 = [p[:, H_B + hh:H_B + hh + 1] for p in gc_parts[e]]
        dl = jnp.where(lane == 0, p1, jnp.where(lane == 1, p2, jnp.where(lane == 2, p3,
                                                                         jnp.where(lane < 6, 1.0, 0.0))))
        dr = jnp.where(lane < 3, 1.0, jnp.where(lane == 3, -p1, jnp.where(lane == 4, -p2,
                                                                        jnp.where(lane == 5, -p3, 0.0))))
        decay.append(jnp.exp(jnp.where(causal, _bdot_nt(dl, dr), -jnp.inf)))
    eg = [jnp.exp(gcol[h]) for h in heads]
    glast = [gcol[h][t - 1:t, :] for h in heads]
    q = [y_of[c][:, h_of[c] * DK_B:(h_of[c] + 1) * DK_B] for c in heads]
    k = [y_of[c][:, (H_B + h_of[c]) * DK_B:(H_B + h_of[c] + 1) * DK_B] for c in heads]
    v = [y_of[c][:, 2 * H_B * DK_B + h_of[c] * DV_B:2 * H_B * DK_B + (h_of[c] + 1) * DV_B] for c in heads]
    ones_sq = jnp.ones((DK_B, DK_B), BF16)
    q = [x * (lax.rsqrt(_bdot(x * x, ones_sq) + NORM_EPS) * (DK_B ** -0.5)) for x in q]
    k = [x * lax.rsqrt(_bdot(x * x, ones_sq) + NORM_EPS) for x in k]
    kb = [k[h] * bcol[h] for h in heads]
    vb = [v[h] * bcol[h] for h in heads]
    lm = [jnp.where(strict, _bdot_nt(kb[h], k[h]) * decay[h], 0.0) for h in heads]
    attn = [_bdot_nt(q[h], k[h]) * decay[h] for h in heads]
    s = [s_s[h] for h in heads]
    qs = [_bdot(q[h] * eg[h], s[h]) for h in heads]
    pairs = [(c, c + 1) for c in range(0, len(chains), 2)]
    zeros_wide = jnp.zeros((t, DK_B + DV_B), F32)
    tinv = _inv_identity_plus_wide([jnp.concatenate([lm[c0], lm[c1]], axis=1) for c0, c1 in pairs], t)
    kbg = [kb[h] * eg[h] for h in heads]
    uw = [_bdot(tinv[i], jnp.concatenate(
        [jnp.concatenate([vb[c0], kbg[c0], zeros_wide], axis=1),
         jnp.concatenate([zeros_wide, vb[c1], kbg[c1]], axis=1)], axis=0)) for i, (c0, c1) in enumerate(pairs)]
    u = [uw[c // 2][:, (c % 2) * (DK_B + DV_B):(c % 2) * (DK_B + DV_B) + DV_B] for c in heads]
    w = [uw[c // 2][:, (c % 2) * (DK_B + DV_B) + DV_B:(c % 2 + 1) * (DK_B + DV_B)] for c in heads]
    v_new = [u[h] - _bdot(w[h], s[h]) for h in heads]
    zeros_v = jnp.zeros((t, DV_B), F32)
    av = [_bdot(jnp.concatenate([attn[c0], attn[c1]], axis=1), jnp.concatenate(
        [jnp.concatenate([v_new[c0], zeros_v], axis=1),
         jnp.concatenate([zeros_v, v_new[c1]], axis=1)], axis=0)) for c0, c1 in pairs]
    o = [qs[c] + av[c // 2][:, (c % 2) * DV_B:(c % 2 + 1) * DV_B] for c in heads]
    upd = [_bdot_tn(k[h] * jnp.exp(glast[h] - gcol[h]), v_new[h]) for h in heads]
    for c, (e, h) in enumerate(chains):
        s_new = s[c] * jnp.exp(glast[c]) + upd[c]
        s_s[c] = s_new
        sfin_ref[e, h] = s_new
        cols = slice(e * D_B + h * DV_B, e * D_B + (h + 1) * DV_B)
        o_ref[:, cols] = _rmsnorm(o[c], nw) * _silu(z_ref[:, cols])


def _gdn(qkv_tm, ba_tm, z_tm, conv0, s0, prm, bsz, t, nb):
    seq = qkv_tm.shape[0]
    grid = (bsz // nb, seq // t)
    const = lambda shape: pl.BlockSpec(shape, lambda b, i: (0,) * len(shape))
    tm = lambda n: pl.BlockSpec((t, nb * n), lambda b, i: (i, b))
    state = pl.BlockSpec((nb, H_B, DK_B, DV_B), lambda b, i: (b, 0, 0, 0))
    conv = pl.BlockSpec((nb, CONV_W - 1, D_QKV), lambda b, i: (b, 0, 0))
    return pl.pallas_call(
        functools.partial(_gdn_kernel, t=t, nb=nb),
        grid=grid,
        in_specs=[tm(D_QKV), tm(GATE_PAD), tm(D_B), conv, state,
                  const((CONV_W, D_QKV)), const((1, GATE_PAD)), const((1, GATE_PAD)), const((1, DV_B))],
        out_specs=[tm(D_B), state, conv],
        out_shape=[jax.ShapeDtypeStruct((seq, bsz * D_B), F32),
                   jax.ShapeDtypeStruct((bsz, H_B, DK_B, DV_B), F32),
                   jax.ShapeDtypeStruct((bsz, CONV_W - 1, D_QKV), F32)],
        scratch_shapes=[pltpu.VMEM((t + SUBLANES, nb * D_QKV), F32),
                        pltpu.VMEM((nb * H_B, DK_B, DV_B), F32)],
        compiler_params=_cparams(("parallel", "arbitrary"), 48),
    )(qkv_tm, ba_tm, z_tm, conv0, s0, prm["conv_w"], prm["alog"], prm["dtb"], prm["norm_w"])


def _mix_ffn_kernel(*refs, n_mix, final_norm):
    x_ref = refs[0]
    mix_refs = refs[1:1 + n_mix]
    wout_refs = refs[1 + n_mix:1 + 2 * n_mix]
    nffn_ref, wg_ref, wu_ref, wd_ref = refs[1 + 2 * n_mix:5 + 2 * n_mix]
    rest = refs[5 + 2 * n_mix:]
    if final_norm:
        nfin_ref, out_ref = rest
    else:
        (out_ref,) = rest

    x1 = x_ref[...]
    for m_ref, w_ref in zip(mix_refs, wout_refs):
        x1 = x1 + jnp.dot(m_ref[...].astype(BF16), w_ref[...], preferred_element_type=F32)
    h = _rmsnorm(x1, nffn_ref[...]).astype(BF16)
    out_ref[...] = x1

    for c in range(N_FF_TILES):
        cols = slice(c * FF_TILE, (c + 1) * FF_TILE)
        gate = jnp.dot(h, wg_ref[:, cols], preferred_element_type=F32)
        up = jnp.dot(h, wu_ref[:, cols], preferred_element_type=F32)
        act = (_silu(gate) * up).astype(BF16)
        out_ref[...] += jnp.dot(act, wd_ref[cols, :], preferred_element_type=F32)
    if final_norm:
        out_ref[...] = _rmsnorm(out_ref[...], nfin_ref[...])


def _mix_ffn(x, x_bm, mixes, wouts, nffn, wg, wu, wd, nfin, bv, lv, out_bm):
    tl = min(TOKEN_TILE, lv)
    grid = (bv, lv // tl)
    const = lambda shape: pl.BlockSpec(shape, lambda b, i: (0,) * len(shape),
                                       pipeline_mode=pl.Buffered(1))
    tm = lambda n: pl.BlockSpec((tl, n), lambda b, i: (i, b))
    bm = pl.BlockSpec((None, tl, D_MODEL), lambda b, i: (b, i, 0))
    in_specs = [bm if x_bm else tm(D_MODEL)]
    in_specs += [tm(m.shape[1] // bv) for m in mixes]
    in_specs += [const(w.shape) for w in wouts]
    in_specs += [const((1, D_MODEL)), const(wg.shape), const(wu.shape), const(wd.shape)]
    args = [x, *mixes, *wouts, nffn, wg, wu, wd]
    if nfin is not None:
        in_specs.append(const((1, D_MODEL)))
        args.append(nfin)
    if out_bm:
        out_spec, out_shape = bm, jax.ShapeDtypeStruct((bv, lv, D_MODEL), F32)
    else:
        out_spec, out_shape = tm(D_MODEL), jax.ShapeDtypeStruct((lv, bv * D_MODEL), F32)
    return pl.pallas_call(
        functools.partial(_mix_ffn_kernel, n_mix=len(mixes), final_norm=nfin is not None),
        grid=grid,
        in_specs=in_specs,
        out_specs=out_spec,
        out_shape=out_shape,
        compiler_params=_cparams(("parallel", "parallel"), 56),
    )(*args)


def _rwkv_proj_kernel(x_ref, shift0_ref, nw_ref, maa_ref, wr_ref, wk_ref, wv_ref, w0_ref, w1_ref, w2_ref,
                      a0_ref, a1_ref, a2_ref, g1_ref, g2_ref, kk_ref, ka_ref,
                      r_out, w_out, k_out, v_out, kk_out, a_out, g_out, shift_out, hbuf, *, shift):
    pad = -(-shift // SUBLANES) * SUBLANES

    @pl.when(pl.program_id(1) == 0)
    def _():
        hbuf[pad - shift:pad, :] = shift0_ref[...]

    h = _rmsnorm(x_ref[...], nw_ref[...])
    rows = h.shape[0]
    hbuf[pad:pad + rows, :] = h
    prev = hbuf[pad - shift:pad - shift + rows, :]
    last = h[rows - shift:]
    hbuf[pad - shift:pad, :] = last
    shift_out[...] = last
    hb = h.astype(BF16)
    xxb = (prev - h).astype(BF16)
    maa = maa_ref[...].astype(BF16)
    xr, xw, xk, xv, xa, xg = [hb + xxb * maa[j:j + 1, :] for j in range(6)]
    r = _bdot(xr, wr_ref[...])
    k = _bdot(xk, wk_ref[...])
    v = _bdot(xv, wv_ref[...])
    u = w0_ref[...] + _bdot(jnp.tanh(_bdot(xw, w1_ref[...])), w2_ref[...])
    a = _sigmoid(a0_ref[...] + _bdot(_bdot(xa, a1_ref[...]), a2_ref[...]))
    g = _bdot(_sigmoid(_bdot(xg, g1_ref[...])), g2_ref[...])
    r_out[...] = r
    w_out[...] = -math.exp(-0.5) * _sigmoid(u)
    k_out[...] = k * (1.0 + (a - 1.0) * ka_ref[...])
    v_out[...] = v
    kk_out[...] = k * kk_ref[...]
    a_out[...] = a
    g_out[...] = g


def _rwkv_proj(x_tm, shift0, prm):
    bv, shift, _ = shift0.shape
    lv = x_tm.shape[0]
    tr = max(shift, min(RWKV_PROJ_TILE, lv))
    grid = (bv, lv // tr)
    const = lambda shape: pl.BlockSpec(shape, lambda b, i: (0,) * len(shape))
    act = pl.BlockSpec((tr, D_MODEL), lambda b, i: (i, b))
    carry = pl.BlockSpec((None, shift, D_MODEL), lambda b, i: (b, 0, 0))
    vec = const((1, D_MODEL))
    weights = [prm[n] for n in ("wr", "wk", "wv")]
    in_specs = [act, carry, vec, const((6, D_MODEL))]
    in_specs += [const(w.shape) for w in weights]
    in_specs += [vec, const(prm["w1"].shape), const(prm["w2"].shape),
                 vec, const(prm["a1"].shape), const(prm["a2"].shape),
                 const(prm["g1"].shape), const(prm["g2"].shape), vec, vec]
    out_act = jax.ShapeDtypeStruct((lv, bv * D_MODEL), F32)
    pad = -(-shift // SUBLANES) * SUBLANES
    return pl.pallas_call(
        functools.partial(_rwkv_proj_kernel, shift=shift),
        grid=grid,
        in_specs=in_specs,
        out_specs=[act] * 7 + [carry],
        out_shape=[out_act] * 7 + [jax.ShapeDtypeStruct((bv, shift, D_MODEL), F32)],
        scratch_shapes=[pltpu.VMEM((pad + tr, D_MODEL), F32)],
        compiler_params=_cparams(("parallel", "arbitrary"), 56),
    )(x_tm, shift0, prm["norm_w"], prm["maa"], prm["wr"], prm["wk"], prm["wv"], prm["w0"], prm["w1"],
      prm["w2"], prm["a0"], prm["a1"], prm["a2"], prm["g1"], prm["g2"], prm["k_k"], prm["k_a"])


def _rwkv_scan_kernel(r_ref, w_ref, k_ref, v_ref, kk_ref, a_ref, g_ref, s0_ref, rk_ref, lnw_ref, lnb_ref,
                      y_ref, sfin_ref, s_s, *, t, nb, group):
    n = N_HEAD_C
    zero_blk = jnp.zeros((n, n), F32)

    pairs_per_seq = H_C // 2

    @pl.when(pl.program_id(1) == 0)
    def _():
        for jj in range(nb * pairs_per_seq):
            e, j = divmod(jj, pairs_per_seq)
            top = jnp.concatenate([s0_ref[e, 2 * j], zero_blk], axis=1)
            bot = jnp.concatenate([zero_blk, s0_ref[e, 2 * j + 1]], axis=1)
            s_s[jj] = jnp.concatenate([top, bot], axis=0)

    row, col = _iota2((t, t))
    tri = (row >= col).astype(BF16)
    roww, colw = _iota2((t, 2 * t))
    colw = jnp.where(colw >= t, colw - t, colw)
    incl_w = roww >= colw
    strict_w = roww > colw
    lane = lax.broadcasted_iota(jnp.int32, (t, LANES), 1)
    head0 = lane < n

    def by_head(x):
        return jnp.concatenate([jnp.where(head0, x, 0.0), jnp.where(head0, 0.0, x)], axis=0)

    prow, pcol = _iota2((LANES, LANES))
    same_head = (prow >> 6) == (pcol >> 6)
    ones_blk = same_head.astype(BF16)

    def head_sums(xs, dot):
        tot = dot(jnp.concatenate(xs, axis=0), ones_blk)
        return [tot[i * t:(i + 1) * t] for i in range(len(xs))]

    def pair_group(js):
        ps = range(len(js))
        sl = [slice(j * LANES, (j + 1) * LANES) for j in js]
        psl = [slice((j % pairs_per_seq) * LANES, (j % pairs_per_seq + 1) * LANES) for j in js]
        r = [r_ref[:, x] for x in sl]
        w = [w_ref[:, x] for x in sl]
        k = [k_ref[:, x] for x in sl]
        v = [v_ref[:, x] for x in sl]
        kkr = [kk_ref[:, x] for x in sl]
        a = [a_ref[:, x] for x in sl]
        kk = [x * lax.rsqrt(ss + NORM_EPS) for x, ss in zip(kkr, head_sums([x * x for x in kkr], _bdot))]
        bv = [kk[p] * a[p] for p in ps]
        yield
        gc = [_cumsum_rows(tri, x, parts=2) for x in w]
        glast = [x[t - 1:t, :] for x in gc]
        yield
        pinv = [jnp.exp(-x) for x in gc]
        at = [-kk[p] * jnp.exp(gc[p] - w[p]) for p in ps]
        bt = [bv[p] * pinv[p] for p in ps]
        kt = [k[p] * pinv[p] for p in ps]
        rt = [r[p] * jnp.exp(gc[p]) for p in ps]
        rem = [jnp.exp(glast[p] - gc[p]) for p in ps]
        yield
        lhs = [jnp.concatenate([at[p], rt[p]], axis=0) for p in ps]
        gm = [_bdot_nt(lhs[p], jnp.concatenate([by_head(bt[p]), by_head(kt[p])], axis=0)) for p in ps]
        yield
        s = [s_s[j] for j in js]
        ars = [_bdot_nt(lhs[p], s[p]) for p in ps]
        v2 = [by_head(x) for x in v]
        yield
        aab = [jnp.where(strict_w, gm[p][:t, :2 * t], 0.0) for p in ps]
        aak = [jnp.where(strict_w, gm[p][:t, 2 * t:], 0.0) for p in ps]
        rkm = [jnp.where(incl_w, gm[p][t:, 2 * t:], 0.0) for p in ps]
        rb = [jnp.where(incl_w, gm[p][t:, :2 * t], 0.0) for p in ps]
        yield
        kv = [_bdot(jnp.concatenate([aak[p], rkm[p]], axis=0), v2[p]) for p in ps]
        rhs = [ars[p][:t] + kv[p][:t] for p in ps]
        yield
        tinv = yield from _inv_identity_plus_wide_stages([-x for x in aab], t)
        u = [_bdot(tinv[p], by_head(rhs[p])) for p in ps]
        yield
        y = [ars[p][t:] + _bdot(rb[p], by_head(u[p])) + kv[p][t:] for p in ps]
        yield
        upd = [_bdot_tn(jnp.concatenate([u[p], v[p]], axis=0),
                        jnp.concatenate([bv[p] * rem[p], k[p] * rem[p]], axis=0)) for p in ps]
        for p, j in enumerate(js):
            s_s[j] = s[p] * jnp.exp(glast[p]) + jnp.where(same_head, upd[p], 0.0)
        yield
        mu = head_sums(y, _split_dot)
        d = [y[p] - mu[p] * (1.0 / n) for p in ps]
        yield
        var = head_sums([x * x for x in d], _bdot)
        yield
        bonus = head_sums([r[p] * k[p] * rk_ref[:, psl[p]] for p in ps], _split_dot)
        yield
        for p in ps:
            yn = d[p] * lax.rsqrt(var[p] * (1.0 / n) + RWKV_GN_EPS) * lnw_ref[:, psl[p]] + lnb_ref[:, psl[p]]
            y_ref[:, sl[p]] = (yn + bonus[p] * v[p]) * g_ref[:, sl[p]]

    groups = [list(range(first, first + group)) for first in range(0, nb * pairs_per_seq, group)]
    _run_staggered([pair_group(js) for js in groups], RWKV_STAGGER)

    for jj in range(nb * pairs_per_seq):
        e, j = divmod(jj, pairs_per_seq)
        sp = s_s[jj]
        sfin_ref[e, 2 * j] = sp[:n, :n]
        sfin_ref[e, 2 * j + 1] = sp[n:, n:]


def _rwkv_scan(acts, s0, prm, bsz, t, nb, group):
    seq = acts[0].shape[0]
    grid = (bsz // nb, seq // t)
    const = lambda shape: pl.BlockSpec(shape, lambda b, i: (0,) * len(shape))
    tm = pl.BlockSpec((t, nb * D_MODEL), lambda b, i: (i, b))
    st = pl.BlockSpec((nb, H_C, N_HEAD_C, N_HEAD_C), lambda b, i: (b, 0, 0, 0))
    vec = const((1, D_MODEL))
    return pl.pallas_call(
        functools.partial(_rwkv_scan_kernel, t=t, nb=nb, group=group),
        grid=grid,
        in_specs=[tm] * 7 + [st, vec, vec, vec],
        out_specs=[tm, st],
        out_shape=[jax.ShapeDtypeStruct((seq, bsz * D_MODEL), F32),
                   jax.ShapeDtypeStruct((bsz, H_C, N_HEAD_C, N_HEAD_C), F32)],
        scratch_shapes=[pltpu.VMEM((nb * H_C // 2, LANES, LANES), F32)],
        compiler_params=_cparams(("parallel", "arbitrary"), 48),
    )(*acts, s0, prm["r_k"], prm["ln_w"], prm["ln_b"])


def _block_diag_groups(w, rows_per_group, cols_per_group):
    g = w.shape[0]
    w = w.reshape(g // 8, 8, rows_per_group, cols_per_group)
    eye = jnp.eye(8, dtype=w.dtype)
    out = jnp.einsum("jgrc,gh->jgrhc", w, eye)
    return out.reshape(g // 8, 8 * rows_per_group, 8 * cols_per_group)


def _pad_to(w, axis, size):
    pad = [(0, 0)] * w.ndim
    pad[axis] = (0, size - w.shape[axis])
    return jnp.pad(w, pad)


def _row(v):
    return v.reshape(1, -1).astype(F32)


def _recurrence_tiling(seq):
    if seq >= GDN_CHUNK:
        return dict(s5_tc=64, gdn_t=GDN_CHUNK, gdn_nb=4, rwkv_t=RWKV_CHUNK, rwkv_nb=2,
                    rwkv_group=2 * RWKV_PAIR_GROUP)
    return dict(s5_tc=seq, gdn_t=seq, gdn_nb=8, rwkv_t=seq, rwkv_nb=4, rwkv_group=2 * RWKV_PAIR_GROUP)


def _trunk(x_bm, bsz, seq, s5_re0, s5_im0, gdn_s0, gdn_conv0, rw_s0, rw_shift0, p, out_bm):
    bv, lv, _ = x_bm.shape
    cfg = _recurrence_tiling(seq)
    u, qkv, z, ba = _ab_in(x_bm, p["norm_mix0"], p["ab_w_in"])
    y_a, hr, hi = _s5(u.reshape(seq, bsz, D_A), s5_re0, s5_im0, p["s5"], cfg["s5_tc"])
    y_b, gdn_s, gdn_conv = _gdn(qkv.reshape(seq, bsz * D_QKV), ba.reshape(seq, bsz * GATE_PAD),
                                z.reshape(seq, bsz * D_B), gdn_conv0, gdn_s0, p["gdn"], bsz,
                                cfg["gdn_t"], cfg["gdn_nb"])
    x1 = _mix_ffn(x_bm, True, [y_a.reshape(lv, bv * D_A), y_b.reshape(lv, bv * D_B)],
                  [p["ab_w_out_a"], p["ab_w_out_b"]], p["norm_ffn0"], p["wg0"], p["wu0"], p["wd0"],
                  None, bv, lv, False)
    acts = _rwkv_proj(x1, rw_shift0.reshape(bv, bsz // bv, D_MODEL), p["rw"])
    shift = acts[7].reshape(bsz, D_MODEL)
    yg, rw_s = _rwkv_scan([a.reshape(seq, bsz * D_MODEL) for a in acts[:7]], rw_s0, p["rw"], bsz,
                          cfg["rwkv_t"], cfg["rwkv_nb"], cfg["rwkv_group"])
    y = _mix_ffn(x1, False, [yg.reshape(lv, bv * D_MODEL)], [p["rw"]["wo"]], p["norm_ffn1"], p["wg1"],
                 p["wu1"], p["wd1"], p["norm_final"], bv, lv, out_bm)
    return y, hr, hi, gdn_s, gdn_conv, rw_s, shift


def kernel(x_prompt, x_sample, state_s5_re, state_s5_im, state_gdn, state_gdn_conv, state_rwkv, state_rwkv_shift, norm_mix, norm_ffn, norm_final, ffn_w_gate, ffn_w_up, ffn_w_down, ab_w_in, ab_w_out, s5_lambda_re, s5_lambda_im, s5_log_step, s5_B_re, s5_B_im, s5_C_re, s5_C_im, s5_D, s5_w_glu, gdn_conv_w, gdn_A_log, gdn_dt_bias, gdn_norm_w, rw_maa, rw_w_r, rw_w_k, rw_w_v, rw_w_o, rw_w0, rw_w1, rw_w2, rw_a0, rw_a1, rw_a2, rw_g1, rw_g2, rw_k_k, rw_k_a, rw_r_k, rw_ln_w, rw_ln_b):
    bsz_p, seq_p, _ = x_prompt.shape
    bsz_s, seq_s, _ = x_sample.shape

    w_in = ab_w_in[0]
    n_main = D_A + D_QKV
    w_in = jnp.concatenate([w_in[:, :n_main], w_in[:, n_main + 2 * H_B:], w_in[:, n_main:n_main + 2 * H_B]],
                           axis=1)
    w_in = _pad_to(w_in, 1, D_IN_PAD).astype(BF16)

    ff = lambda w, layer: w[layer].astype(BF16)

    gate_lanes = lambda v: _pad_to(jnp.concatenate([jnp.zeros((H_B,), F32), v.astype(F32)]), 0,
                                   GATE_PAD).reshape(1, GATE_PAD)
    p = dict(
        norm_mix0=_row(norm_mix[0]), norm_ffn0=_row(norm_ffn[0]), norm_ffn1=_row(norm_ffn[1]),
        norm_final=_row(norm_final), ab_w_in=w_in,
        ab_w_out_a=ab_w_out[0][:D_A].astype(BF16), ab_w_out_b=ab_w_out[0][D_A:].astype(BF16),
        wg0=ff(ffn_w_gate, 0), wu0=ff(ffn_w_up, 0),
        wd0=ff(ffn_w_down, 0),
        wg1=ff(ffn_w_gate, 1), wu1=ff(ffn_w_up, 1),
        wd1=ff(ffn_w_down, 1),
        s5=dict(
            lr=_row(s5_lambda_re[0]), li=_row(s5_lambda_im[0]),
            ls=_row(jnp.repeat(s5_log_step[0], P_STATE)),
            bre=_block_diag_groups(jnp.swapaxes(s5_B_re[0], 1, 2), S5_GROUP, P_STATE).astype(BF16),
            bim=_block_diag_groups(jnp.swapaxes(s5_B_im[0], 1, 2), S5_GROUP, P_STATE).astype(BF16),
            cre=_block_diag_groups(jnp.swapaxes(s5_C_re[0], 1, 2), P_STATE, S5_GROUP).astype(BF16),
            cim=_block_diag_groups(jnp.swapaxes(s5_C_im[0], 1, 2), P_STATE, S5_GROUP).astype(BF16),
            d=_row(s5_D[0]), wglu=s5_w_glu[0].astype(BF16)),
        gdn=dict(conv_w=gdn_conv_w[0].astype(F32), alog=gate_lanes(gdn_A_log[0]),
                 dtb=gate_lanes(gdn_dt_bias[0]), norm_w=_row(gdn_norm_w[0])),
        rw=dict(
            norm_w=_row(norm_mix[1]), maa=rw_maa[0].astype(F32),
            wr=rw_w_r[0].astype(BF16), wk=rw_w_k[0].astype(BF16), wv=rw_w_v[0].astype(BF16),
            wo=rw_w_o[0].astype(BF16),
            w0=_row(rw_w0[0]), w1=_pad_to(rw_w1[0], 1, LANES).astype(BF16),
            w2=_pad_to(rw_w2[0], 0, LANES).astype(BF16),
            a0=_row(rw_a0[0]), a1=_pad_to(rw_a1[0], 1, LANES).astype(BF16),
            a2=_pad_to(rw_a2[0], 0, LANES).astype(BF16),
            g1=_pad_to(rw_g1[0], 1, 2 * LANES).astype(BF16), g2=_pad_to(rw_g2[0], 0, 2 * LANES).astype(BF16),
            k_k=_row(rw_k_k[0]), k_a=_row(rw_k_a[0]), r_k=_row(rw_r_k[0]),
            ln_w=_row(rw_ln_w[0]), ln_b=_row(rw_ln_b[0])),
    )

    zeros = lambda *shape: jnp.zeros(shape, F32)
    yp, p_hr, p_hi, p_gdn, p_conv, p_rw, p_shift = _trunk(
        x_prompt, bsz_p, seq_p, zeros(bsz_p, N_S5), zeros(bsz_p, N_S5), zeros(bsz_p, H_B, DK_B, DV_B),
        zeros(bsz_p, CONV_W - 1, D_QKV), zeros(bsz_p, H_C, N_HEAD_C, N_HEAD_C), zeros(bsz_p, D_MODEL), p,
        True)

    xs_tm = jnp.transpose(x_sample, (1, 0, 2)).reshape(1, seq_s * bsz_s, D_MODEL)
    ys, s_hr, s_hi, s_gdn, s_conv, s_rw, s_shift = _trunk(
        xs_tm, bsz_s, seq_s, state_s5_re[0].reshape(bsz_s, N_S5), state_s5_im[0].reshape(bsz_s, N_S5),
        state_gdn[0], state_gdn_conv[0], state_rwkv[0], state_rwkv_shift[0], p, True)
    y_sample = jnp.transpose(ys.reshape(seq_s, bsz_s, D_MODEL), (1, 0, 2))

    s5_shape = lambda b: (1, b, G_A, P_STATE)
    return (yp, y_sample,
            p_hr.reshape(s5_shape(bsz_p)), p_hi.reshape(s5_shape(bsz_p)), p_gdn[None], p_conv[None],
            p_rw[None], p_shift[None],
            s_hr.reshape(s5_shape(bsz_s)), s_hi.reshape(s5_shape(bsz_s)), s_gdn[None], s_conv[None],
            s_rw[None], s_shift[None])
```

```python
import functools
import math

import jax
import jax.numpy as jnp
from jax import lax
from jax.experimental import pallas as pl
from jax.experimental.pallas import tpu as pltpu

F32 = jnp.float32
BF16 = jnp.bfloat16

D_MODEL = 1024
D_A = 512
S5_GROUP = 16
G_A = 32
P_STATE = 64
N_S5 = G_A * P_STATE
D_B = 512
H_B = 4
DK_B = 128
DV_B = 128
D_QKV = 2 * H_B * DK_B + D_B
CONV_W = 4
GDN_CHUNK = 64
N_HEAD_C = 64
H_C = 16
RWKV_CHUNK = 64
RWKV_PAIR_GROUP = 8
RWKV_STAGGER = 7
RWKV_GN_EPS = 64e-5
N_RWKV_ACTS = 6
D_FF = 2816
NORM_EPS = 1e-6

LANES = 128
SUBLANES = 8
FF_TILE = 256
N_FF_TILES = D_FF // FF_TILE
GATE_PAD = LANES
D_IN_PAD = 4 * 512 + 512 + GATE_PAD
S5_LANE_BLOCK = 512
N_S5_BLOCKS = N_S5 // S5_LANE_BLOCK
TOKEN_TILE = 512
RWKV_PROJ_TILE = 256
MIB = 1024 * 1024


def _cparams(semantics, vmem_mib):
    return pltpu.CompilerParams(dimension_semantics=semantics, vmem_limit_bytes=vmem_mib * MIB)


def _rmsnorm(x, w):
    return x * lax.rsqrt(jnp.mean(x * x, axis=-1, keepdims=True) + NORM_EPS) * w


def _sigmoid(x):
    return 0.5 + 0.5 * jnp.tanh(0.5 * x)


def _silu(x):
    half = 0.5 * x
    return half + half * jnp.tanh(half)


def _softplus(x):
    return jnp.maximum(x, 0.0) + jnp.log1p(jnp.exp(-jnp.abs(x)))


def _bdot(a, b):
    return jnp.dot(a.astype(BF16), b.astype(BF16), preferred_element_type=F32)


def _bdot_nt(a, b):
    return lax.dot_general(a.astype(BF16), b.astype(BF16), (((1,), (1,)), ((), ())),
                           preferred_element_type=F32)


def _bdot_tn(a, b):
    return lax.dot_general(a.astype(BF16), b.astype(BF16), (((0,), (0,)), ((), ())),
                           preferred_element_type=F32)


def _split(a):
    hi = a.astype(BF16)
    return hi, (a - hi.astype(F32)).astype(BF16)


def _split3(a):
    p1 = a.astype(BF16)
    r1 = a - p1.astype(F32)
    p2 = r1.astype(BF16)
    return p1, p2, (r1 - p2.astype(F32)).astype(BF16)


def _split_dot(a, exact_bf16):
    hi, lo = _split(a)
    return (jnp.dot(hi, exact_bf16, preferred_element_type=F32)
            + jnp.dot(lo, exact_bf16, preferred_element_type=F32))


def _cumsum_rows(tri_bf16, w, parts=3):
    split = _split3(w) if parts == 3 else _split(w)
    return sum(jnp.dot(tri_bf16, part, preferred_element_type=F32) for part in split)


def _iota2(shape):
    return (lax.broadcasted_iota(jnp.int32, shape, 0), lax.broadcasted_iota(jnp.int32, shape, 1))


def _run_staggered(gens, lag):
    pending, active, tick = list(gens), [], 0
    while pending or active:
        if pending and tick % lag == 0:
            active.append(pending.pop(0))
        for g in list(active):
            try:
                next(g)
            except StopIteration:
                active.remove(g)
        tick += 1


def _inv_identity_plus_wide(lmws, t):
    gen = _inv_identity_plus_wide_stages(lmws, t)
    while True:
        try:
            next(gen)
        except StopIteration as done:
            return done.value


def _inv_identity_plus_wide_stages(lmws, t):
    row, col = _iota2((t, 2 * t))
    left = col < t
    col = jnp.where(left, col, col - t)
    eye = (row == col).astype(F32)
    base = min(t, 16)

    def blockdiag(xw):
        return jnp.concatenate([jnp.where(left, xw, 0.0), jnp.where(left, 0.0, xw)], axis=0)

    if t > base:
        same_base = (row >> 4) == (col >> 4)
        ns = [jnp.where(same_base, -lmw, 0.0) for lmw in lmws]
    else:
        ns = [-lmw for lmw in lmws]
    xs = [eye + n for n in ns]
    ps = [_bdot(n, blockdiag(n)) for n in ns]
    yield
    k = 2
    while k < base:
        bds = [blockdiag(p) for p in ps]
        if 2 * k < base:
            both = [_bdot(jnp.concatenate([p, x], axis=0), bd) for p, x, bd in zip(ps, xs, bds)]
            ps = [r[:t] for r in both]
            xs = [x + r[t:] for x, r in zip(xs, both)]
        else:
            xs = [x + _bdot(x, bd) for x, bd in zip(xs, bds)]
        yield
        k *= 2
    shift = 4
    blk = base
    while blk < t:
        same_big = (row >> (shift + 1)) == (col >> (shift + 1))
        same_small = (row >> shift) == (col >> shift)
        off = [jnp.where(same_big, jnp.where(same_small, 0.0, lmw), 0.0) for lmw in lmws]
        mids = [_bdot(o, blockdiag(x)) for o, x in zip(off, xs)]
        yield
        xs = [x - _bdot(x, blockdiag(m)) for x, m in zip(xs, mids)]
        yield
        blk *= 2
        shift += 1
    return xs


def _ab_in_kernel(x_ref, nw_ref, w_ref, u_ref, qkv_ref, z_ref, ba_ref):
    h = _rmsnorm(x_ref[...], nw_ref[...])
    p = jnp.dot(h.astype(BF16), w_ref[...], preferred_element_type=F32)
    u_ref[...] = p[:, 0:D_A]
    qkv_ref[...] = p[:, D_A:D_A + D_QKV]
    z_ref[...] = p[:, D_A + D_QKV:D_A + D_QKV + D_B]
    ba_ref[...] = p[:, D_A + D_QKV + D_B:D_IN_PAD]


def _ab_in(x_bm, norm_w, w_in):
    bv, lv, _ = x_bm.shape
    tl = min(TOKEN_TILE, lv)
    grid = (bv, lv // tl)
    tm = lambda n: pl.BlockSpec((tl, n), lambda b, i: (i, b))
    const = lambda shape: pl.BlockSpec(shape, lambda b, i: (0,) * len(shape))
    return pl.pallas_call(
        _ab_in_kernel,
        grid=grid,
        in_specs=[pl.BlockSpec((None, tl, D_MODEL), lambda b, i: (b, i, 0)),
                  const((1, D_MODEL)), const((D_MODEL, D_IN_PAD))],
        out_specs=[tm(D_A), tm(D_QKV), tm(D_B), tm(GATE_PAD)],
        out_shape=[jax.ShapeDtypeStruct((lv, bv * D_A), F32),
                   jax.ShapeDtypeStruct((lv, bv * D_QKV), F32),
                   jax.ShapeDtypeStruct((lv, bv * D_B), F32),
                   jax.ShapeDtypeStruct((lv, bv * GATE_PAD), F32)],
        compiler_params=_cparams(("parallel", "parallel"), 48),
    )(x_bm, norm_w, w_in)


def _s5_kernel(u_ref, h0r_ref, h0i_ref, lr_ref, li_ref, ls_ref, bre_ref, bim_ref, cre_ref, cim_ref,
               d_ref, wglu_ref, y_ref, hr_out, hi_out,
               ar_s, ai_s, cr_s, ci_s, hr_s, hi_s, bur_s, bui_s, yg_s, *, tc):
    rows = tc * SUBLANES

    @pl.when(pl.program_id(1) == 0)
    def _():
        lr = lr_ref[...]
        li = li_ref[...]
        dt = jnp.exp(ls_ref[...])
        mag = jnp.exp(lr * dt)
        ar = mag * jnp.cos(li * dt)
        ai = mag * jnp.sin(li * dt)
        den = lr * lr + li * li
        nr = ar - 1.0
        cr = (nr * lr + ai * li) / den
        ci = (ai * lr - nr * li) / den
        ar_s[...] = jnp.broadcast_to(ar, (SUBLANES, N_S5))
        ai_s[...] = jnp.broadcast_to(ai, (SUBLANES, N_S5))
        cr_s[...] = jnp.broadcast_to(cr, (SUBLANES, N_S5))
        ci_s[...] = jnp.broadcast_to(ci, (SUBLANES, N_S5))
        hr_s[...] = h0r_ref[...]
        hi_s[...] = h0i_ref[...]

    u = u_ref[...].reshape(rows, D_A)
    ub = u.astype(BF16)
    for j in range(N_S5_BLOCKS):
        sl = slice(j * S5_LANE_BLOCK, (j + 1) * S5_LANE_BLOCK)
        uj = ub[:, j * LANES:(j + 1) * LANES]
        pr = jnp.dot(uj, bre_ref[j], preferred_element_type=F32)
        pi = jnp.dot(uj, bim_ref[j], preferred_element_type=F32)
        crj = cr_s[0:1, sl]
        cij = ci_s[0:1, sl]
        bur_s[:, :, sl] = (crj * pr - cij * pi).reshape(tc, SUBLANES, S5_LANE_BLOCK)
        bui_s[:, :, sl] = (crj * pi + cij * pr).reshape(tc, SUBLANES, S5_LANE_BLOCK)

    for j in range(N_S5_BLOCKS):
        sl = slice(j * S5_LANE_BLOCK, (j + 1) * S5_LANE_BLOCK)
        ar = ar_s[:, sl]
        ai = ai_s[:, sl]

        def step(t, carry, sl=sl, ar=ar, ai=ai):
            hr, hi = carry
            nr = ar * hr - ai * hi + bur_s[t, :, sl]
            ni = ar * hi + ai * hr + bui_s[t, :, sl]
            bur_s[t, :, sl] = nr
            bui_s[t, :, sl] = ni
            return nr, ni

        hr, hi = lax.fori_loop(0, tc, step, (hr_s[:, sl], hi_s[:, sl]), unroll=min(tc, 8))
        hr_s[:, sl] = hr
        hi_s[:, sl] = hi

    for j in range(N_S5_BLOCKS):
        sl = slice(j * S5_LANE_BLOCK, (j + 1) * S5_LANE_BLOCK)
        cl = slice(j * LANES, (j + 1) * LANES)
        xr = bur_s[:, :, sl].reshape(rows, S5_LANE_BLOCK).astype(BF16)
        xi = bui_s[:, :, sl].reshape(rows, S5_LANE_BLOCK).astype(BF16)
        yj = (jnp.dot(xr, cre_ref[j], preferred_element_type=F32)
              - jnp.dot(xi, cim_ref[j], preferred_element_type=F32)
              + d_ref[:, cl] * u[:, cl])
        yg_s[:, cl] = jax.nn.gelu(yj)

    yg = yg_s[...]
    out = yg * _sigmoid(jnp.dot(yg.astype(BF16), wglu_ref[...], preferred_element_type=F32))
    y_ref[...] = out.reshape(tc, SUBLANES, D_A)
    hr_out[...] = hr_s[...]
    hi_out[...] = hi_s[...]


def _s5(u_tm, h0r, h0i, prm, tc):
    seq, bsz, _ = u_tm.shape
    grid = (bsz // SUBLANES, seq // tc)
    const = lambda shape: pl.BlockSpec(shape, lambda b, i: (0,) * len(shape))
    act = pl.BlockSpec((tc, SUBLANES, D_A), lambda b, i: (i, b, 0))
    st = pl.BlockSpec((SUBLANES, N_S5), lambda b, i: (b, 0))
    vec = const((1, N_S5))
    small = pltpu.VMEM((SUBLANES, N_S5), F32)
    big = pltpu.VMEM((tc, SUBLANES, N_S5), F32)
    return pl.pallas_call(
        functools.partial(_s5_kernel, tc=tc),
        grid=grid,
        in_specs=[act, st, st, vec, vec, vec,
                  const((N_S5_BLOCKS, LANES, S5_LANE_BLOCK)), const((N_S5_BLOCKS, LANES, S5_LANE_BLOCK)),
                  const((N_S5_BLOCKS, S5_LANE_BLOCK, LANES)), const((N_S5_BLOCKS, S5_LANE_BLOCK, LANES)),
                  const((1, D_A)), const((D_A, D_A))],
        out_specs=[act, st, st],
        out_shape=[jax.ShapeDtypeStruct((seq, bsz, D_A), F32),
                   jax.ShapeDtypeStruct((bsz, N_S5), F32),
                   jax.ShapeDtypeStruct((bsz, N_S5), F32)],
        scratch_shapes=[small, small, small, small, small, small, big, big,
                        pltpu.VMEM((tc * SUBLANES, D_A), F32)],
        compiler_params=_cparams(("parallel", "arbitrary"), 48),
    )(u_tm, h0r, h0i, prm["lr"], prm["li"], prm["ls"], prm["bre"], prm["bim"], prm["cre"], prm["cim"],
      prm["d"], prm["wglu"])


def _gdn_kernel(qkv_ref, ba_ref, z_ref, conv0_ref, s0_ref, cw_ref, alog_ref, dtb_ref, nw_ref,
                o_ref, sfin_ref, convn_ref, xbuf, s_s, *, t, nb):
    pad = SUBLANES
    hist = CONV_W - 1

    @pl.when(pl.program_id(1) == 0)
    def _():
        for e in range(nb):
            xbuf[pad - hist:pad, e * D_QKV:(e + 1) * D_QKV] = conv0_ref[e]
        s_s[...] = s0_ref[...].reshape(nb * H_B, DK_B, DV_B)

    row, col = _iota2((t, t))
    causal = row >= col
    strict = row > col
    tri = causal.astype(BF16)
    lane = lax.broadcasted_iota(jnp.int32, (t, LANES), 1)
    nw = nw_ref[...]
    cw = cw_ref[...]
    ys, betas, gcs = [], [], []
    for e in range(nb):
        cols = slice(e * D_QKV, (e + 1) * D_QKV)
        x = qkv_ref[:, cols]
        xbuf[pad:pad + t, cols] = x
        acc = x * cw[hist:hist + 1, :]
        for j in range(hist):
            acc = acc + xbuf[pad - hist + j:pad - hist + j + t, cols] * cw[j:j + 1, :]
        last = xbuf[pad + t - hist:pad + t, cols]
        convn_ref[e] = last
        xbuf[pad - hist:pad, cols] = last
        ys.append(_silu(acc))
        ba = ba_ref[:, e * GATE_PAD:(e + 1) * GATE_PAD]
        betas.append(_sigmoid(ba))
        g = -jnp.exp(alog_ref[...]) * _softplus(ba + dtb_ref[...])
        gcs.append(_cumsum_rows(tri, g))

    chains = [(e, h) for e in range(nb) for h in range(H_B)]
    heads = range(len(chains))
    bcol = [betas[e][:, h:h + 1] for e, h in chains]
    gcol = [gcs[e][:, H_B + h:H_B + h + 1] for e, h in chains]
    y_of = [ys[e] for e, h in chains]
    h_of = [h for e, h in chains]
    decay = []
    gc_parts = [[p.astype(F32) for p in _split3(gc)] for gc in gcs]
    for e, hh in chains:
        p1, p2, p3 = [p[:, H_B + hh:H_B + hh + 1] for p in gc_parts[e]]
        dl = jnp.where(lane == 0, p1, jnp.where(lane == 1, p2, jnp.where(lane == 2, p3,
                                                                         jnp.where(lane < 6, 1.0, 0.0))))
        dr = jnp.where(lane < 3, 1.0, jnp.where(lane == 3, -p1, jnp.where(lane == 4, -p2,
                                                                        jnp.where(lane == 5, -p3, 0.0))))
        decay.append(jnp.exp(jnp.where(causal, _bdot_nt(dl, dr), -jnp.inf)))
    eg = [jnp.exp(gcol[h]) for h in heads]
    glast = [gcol[h][t - 1:t, :] for h in heads]
    q = [y_of[c][:, h_of[c] * DK_B:(h_of[c] + 1) * DK_B] for c in heads]
    k = [y_of[c][:, (H_B + h_of[c]) * DK_B:(H_B + h_of[c] + 1) * DK_B] for c in heads]
    v = [y_of[c][:, 2 * H_B * DK_B + h_of[c] * DV_B:2 * H_B * DK_B + (h_of[c] + 1) * DV_B] for c in heads]
    ones_sq = jnp.ones((DK_B, DK_B), BF16)
    q = [x * (lax.rsqrt(_bdot(x * x, ones_sq) + NORM_EPS) * (DK_B ** -0.5)) for x in q]
    k = [x * lax.rsqrt(_bdot(x * x, ones_sq) + NORM_EPS) for x in k]
    kb = [k[h] * bcol[h] for h in heads]
    vb = [v[h] * bcol[h] for h in heads]
    lm = [jnp.where(strict, _bdot_nt(kb[h], k[h]) * decay[h], 0.0) for h in heads]
    attn = [_bdot_nt(q[h], k[h]) * decay[h] for h in heads]
    s = [s_s[h] for h in heads]
    qs = [_bdot(q[h] * eg[h], s[h]) for h in heads]
    pairs = [(c, c + 1) for c in range(0, len(chains), 2)]
    zeros_wide = jnp.zeros((t, DK_B + DV_B), F32)
    tinv = _inv_identity_plus_wide([jnp.concatenate([lm[c0], lm[c1]], axis=1) for c0, c1 in pairs], t)
    kbg = [kb[h] * eg[h] for h in heads]
    uw = [_bdot(tinv[i], jnp.concatenate(
        [jnp.concatenate([vb[c0], kbg[c0], zeros_wide], axis=1),
         jnp.concatenate([zeros_wide, vb[c1], kbg[c1]], axis=1)], axis=0)) for i, (c0, c1) in enumerate(pairs)]
    u = [uw[c // 2][:, (c % 2) * (DK_B + DV_B):(c % 2) * (DK_B + DV_B) + DV_B] for c in heads]
    w = [uw[c // 2][:, (c % 2) * (DK_B + DV_B) + DV_B:(c % 2 + 1) * (DK_B + DV_B)] for c in heads]
    v_new = [u[h] - _bdot(w[h], s[h]) for h in heads]
    zeros_v = jnp.zeros((t, DV_B), F32)
    av = [_bdot(jnp.concatenate([attn[c0], attn[c1]], axis=1), jnp.concatenate(
        [jnp.concatenate([v_new[c0], zeros_v], axis=1),
         jnp.concatenate([zeros_v, v_new[c1]], axis=1)], axis=0)) for c0, c1 in pairs]
    o = [qs[c] + av[c // 2][:, (c % 2) * DV_B:(c % 2 + 1) * DV_B] for c in heads]
    upd = [_bdot_tn(k[h] * jnp.exp(glast[h] - gcol[h]), v_new[h]) for h in heads]
    for c, (e, h) in enumerate(chains):
        s_new = s[c] * jnp.exp(glast[c]) + upd[c]
        s_s[c] = s_new
        sfin_ref[e, h] = s_new
        cols = slice(e * D_B + h * DV_B, e * D_B + (h + 1) * DV_B)
        o_ref[:, cols] = _rmsnorm(o[c], nw) * _silu(z_ref[:, cols])


def _gdn(qkv_tm, ba_tm, z_tm, conv0, s0, prm, bsz, t, nb):
    seq = qkv_tm.shape[0]
    grid = (bsz // nb, seq // t)
    const = lambda shape: pl.BlockSpec(shape, lambda b, i: (0,) * len(shape))
    tm = lambda n: pl.BlockSpec((t, nb * n), lambda b, i: (i, b))
    state = pl.BlockSpec((nb, H_B, DK_B, DV_B), lambda b, i: (b, 0, 0, 0))
    conv = pl.BlockSpec((nb, CONV_W - 1, D_QKV), lambda b, i: (b, 0, 0))
    return pl.pallas_call(
        functools.partial(_gdn_kernel, t=t, nb=nb),
        grid=grid,
        in_specs=[tm(D_QKV), tm(GATE_PAD), tm(D_B), conv, state,
                  const((CONV_W, D_QKV)), const((1, GATE_PAD)), const((1, GATE_PAD)), const((1, DV_B))],
        out_specs=[tm(D_B), state, conv],
        out_shape=[jax.ShapeDtypeStruct((seq, bsz * D_B), F32),
                   jax.ShapeDtypeStruct((bsz, H_B, DK_B, DV_B), F32),
                   jax.ShapeDtypeStruct((bsz, CONV_W - 1, D_QKV), F32)],
        scratch_shapes=[pltpu.VMEM((t + SUBLANES, nb * D_QKV), F32),
                        pltpu.VMEM((nb * H_B, DK_B, DV_B), F32)],
        compiler_params=_cparams(("parallel", "arbitrary"), 48),
    )(qkv_tm, ba_tm, z_tm, conv0, s0, prm["conv_w"], prm["alog"], prm["dtb"], prm["norm_w"])


def _mix_ffn_kernel(*refs, n_mix, final_norm):
    x_ref = refs[0]
    mix_refs = refs[1:1 + n_mix]
    wout_refs = refs[1 + n_mix:1 + 2 * n_mix]
    nffn_ref, wg_ref, wu_ref, wd_ref = refs[1 + 2 * n_mix:5 + 2 * n_mix]
    rest = refs[5 + 2 * n_mix:]
    if final_norm:
        nfin_ref, out_ref = rest
    else:
        (out_ref,) = rest

    x1 = x_ref[...]
    for m_ref, w_ref in zip(mix_refs, wout_refs):
        x1 = x1 + jnp.dot(m_ref[...].astype(BF16), w_ref[...], preferred_element_type=F32)
    h = _rmsnorm(x1, nffn_ref[...]).astype(BF16)
    out_ref[...] = x1

    for c in range(N_FF_TILES):
        cols = slice(c * FF_TILE, (c + 1) * FF_TILE)
        gate = jnp.dot(h, wg_ref[:, cols], preferred_element_type=F32)
        up = jnp.dot(h, wu_ref[:, cols], preferred_element_type=F32)
        act = (_silu(gate) * up).astype(BF16)
        out_ref[...] += jnp.dot(act, wd_ref[cols, :], preferred_element_type=F32)
    if final_norm:
        out_ref[...] = _rmsnorm(out_ref[...], nfin_ref[...])


def _mix_ffn(x, x_bm, mixes, wouts, nffn, wg, wu, wd, nfin, bv, lv, out_bm):
    tl = min(TOKEN_TILE, lv)
    grid = (bv, lv // tl)
    const = lambda shape: pl.BlockSpec(shape, lambda b, i: (0,) * len(shape),
                                       pipeline_mode=pl.Buffered(1))
    tm = lambda n: pl.BlockSpec((tl, n), lambda b, i: (i, b))
    bm = pl.BlockSpec((None, tl, D_MODEL), lambda b, i: (b, i, 0))
    in_specs = [bm if x_bm else tm(D_MODEL)]
    in_specs += [tm(m.shape[1] // bv) for m in mixes]
    in_specs += [const(w.shape) for w in wouts]
    in_specs += [const((1, D_MODEL)), const(wg.shape), const(wu.shape), const(wd.shape)]
    args = [x, *mixes, *wouts, nffn, wg, wu, wd]
    if nfin is not None:
        in_specs.append(const((1, D_MODEL)))
        args.append(nfin)
    if out_bm:
        out_spec, out_shape = bm, jax.ShapeDtypeStruct((bv, lv, D_MODEL), F32)
    else:
        out_spec, out_shape = tm(D_MODEL), jax.ShapeDtypeStruct((lv, bv * D_MODEL), F32)
    return pl.pallas_call(
        functools.partial(_mix_ffn_kernel, n_mix=len(mixes), final_norm=nfin is not None),
        grid=grid,
        in_specs=in_specs,
        out_specs=out_spec,
        out_shape=out_shape,
        compiler_params=_cparams(("parallel", "parallel"), 56),
    )(*args)


def _rwkv_proj_kernel(x_ref, shift0_ref, nw_ref, maa_ref, wr_ref, wk_ref, wv_ref, w0_ref, w1_ref, w2_ref,
                      a0_ref, a1_ref, a2_ref, g1_ref, g2_ref,
                      r_out, w_out, k_out, v_out, a_out, g_out, shift_out, hbuf, *, shift):
    pad = -(-shift // SUBLANES) * SUBLANES

    @pl.when(pl.program_id(1) == 0)
    def _():
        hbuf[pad - shift:pad, :] = shift0_ref[...]

    h = _rmsnorm(x_ref[...], nw_ref[...])
    rows = h.shape[0]
    hbuf[pad:pad + rows, :] = h
    prev = hbuf[pad - shift:pad - shift + rows, :]
    last = h[rows - shift:]
    hbuf[pad - shift:pad, :] = last
    shift_out[...] = last
    xx = prev - h
    maa = maa_ref[...]
    xr, xw, xk, xv, xa, xg = [h + xx * maa[j:j + 1, :] for j in range(6)]
    r = _bdot(xr, wr_ref[...])
    k = _bdot(xk, wk_ref[...])
    v = _bdot(xv, wv_ref[...])
    u = w0_ref[...] + _bdot(jnp.tanh(_bdot(xw, w1_ref[...])), w2_ref[...])
    a = _sigmoid(a0_ref[...] + _bdot(_bdot(xa, a1_ref[...]), a2_ref[...]))
    g = _bdot(_sigmoid(_bdot(xg, g1_ref[...])), g2_ref[...])
    r_out[...] = r
    w_out[...] = -math.exp(-0.5) * _sigmoid(u)
    k_out[...] = k
    v_out[...] = v
    a_out[...] = a
    g_out[...] = g


def _rwkv_proj(x_tm, shift0, prm):
    bv, shift, _ = shift0.shape
    lv = x_tm.shape[0]
    tr = max(shift, min(RWKV_PROJ_TILE, lv))
    grid = (bv, lv // tr)
    const = lambda shape: pl.BlockSpec(shape, lambda b, i: (0,) * len(shape))
    act = pl.BlockSpec((tr, D_MODEL), lambda b, i: (i, b))
    carry = pl.BlockSpec((None, shift, D_MODEL), lambda b, i: (b, 0, 0))
    vec = const((1, D_MODEL))
    weights = [prm[n] for n in ("wr", "wk", "wv")]
    in_specs = [act, carry, vec, const((6, D_MODEL))]
    in_specs += [const(w.shape) for w in weights]
    in_specs += [vec, const(prm["w1"].shape), const(prm["w2"].shape),
                 vec, const(prm["a1"].shape), const(prm["a2"].shape),
                 const(prm["g1"].shape), const(prm["g2"].shape)]
    out_act = jax.ShapeDtypeStruct((lv, bv * D_MODEL), F32)
    pad = -(-shift // SUBLANES) * SUBLANES
    return pl.pallas_call(
        functools.partial(_rwkv_proj_kernel, shift=shift),
        grid=grid,
        in_specs=in_specs,
        out_specs=[act] * N_RWKV_ACTS + [carry],
        out_shape=[out_act] * N_RWKV_ACTS + [jax.ShapeDtypeStruct((bv, shift, D_MODEL), F32)],
        scratch_shapes=[pltpu.VMEM((pad + tr, D_MODEL), F32)],
        compiler_params=_cparams(("parallel", "arbitrary"), 56),
    )(x_tm, shift0, prm["norm_w"], prm["maa"], prm["wr"], prm["wk"], prm["wv"], prm["w0"], prm["w1"],
      prm["w2"], prm["a0"], prm["a1"], prm["a2"], prm["g1"], prm["g2"])


def _rwkv_scan_kernel(r_ref, w_ref, k_ref, v_ref, a_ref, g_ref, s0_ref, kkw_ref, kaw_ref, rk_ref, lnw_ref, lnb_ref,
                      y_ref, sfin_ref, s_s, *, t, nb, group):
    n = N_HEAD_C
    zero_blk = jnp.zeros((n, n), F32)

    pairs_per_seq = H_C // 2

    @pl.when(pl.program_id(1) == 0)
    def _():
        for jj in range(nb * pairs_per_seq):
            e, j = divmod(jj, pairs_per_seq)
            top = jnp.concatenate([s0_ref[e, 2 * j], zero_blk], axis=1)
            bot = jnp.concatenate([zero_blk, s0_ref[e, 2 * j + 1]], axis=1)
            s_s[jj] = jnp.concatenate([top, bot], axis=0)

    row, col = _iota2((t, t))
    tri = (row >= col).astype(BF16)
    roww, colw = _iota2((t, 2 * t))
    colw = jnp.where(colw >= t, colw - t, colw)
    incl_w = roww >= colw
    strict_w = roww > colw
    lane = lax.broadcasted_iota(jnp.int32, (t, LANES), 1)
    head0 = lane < n

    def by_head(x):
        return jnp.concatenate([jnp.where(head0, x, 0.0), jnp.where(head0, 0.0, x)], axis=0)

    prow, pcol = _iota2((LANES, LANES))
    same_head = (prow >> 6) == (pcol >> 6)
    ones_blk = same_head.astype(BF16)

    def head_sums(xs, dot):
        tot = dot(jnp.concatenate(xs, axis=0), ones_blk)
        return [tot[i * t:(i + 1) * t] for i in range(len(xs))]

    def pair_group(js):
        ps = range(len(js))
        sl = [slice(j * LANES, (j + 1) * LANES) for j in js]
        psl = [slice((j % pairs_per_seq) * LANES, (j % pairs_per_seq + 1) * LANES) for j in js]
        r = [r_ref[:, x] for x in sl]
        w = [w_ref[:, x] for x in sl]
        kraw = [k_ref[:, x] for x in sl]
        v = [v_ref[:, x] for x in sl]
        a = [a_ref[:, x] for x in sl]
        kkr = [kraw[p] * kkw_ref[:, psl[p]] for p in ps]
        k = [kraw[p] * (1.0 + (a[p] - 1.0) * kaw_ref[:, psl[p]]) for p in ps]
        kk = [x * lax.rsqrt(ss + NORM_EPS) for x, ss in zip(kkr, head_sums([x * x for x in kkr], _bdot))]
        bv = [kk[p] * a[p] for p in ps]
        yield
        gc = [_cumsum_rows(tri, x, parts=2) for x in w]
        glast = [x[t - 1:t, :] for x in gc]
        yield
        pinv = [jnp.exp(-x) for x in gc]
        at = [-kk[p] * jnp.exp(gc[p] - w[p]) for p in ps]
        bt = [bv[p] * pinv[p] for p in ps]
        kt = [k[p] * pinv[p] for p in ps]
        rt = [r[p] * jnp.exp(gc[p]) for p in ps]
        rem = [jnp.exp(glast[p] - gc[p]) for p in ps]
        yield
        lhs = [jnp.concatenate([at[p], rt[p]], axis=0) for p in ps]
        gm = [_bdot_nt(lhs[p], jnp.concatenate([by_head(bt[p]), by_head(kt[p])], axis=0)) for p in ps]
        yield
        s = [s_s[j] for j in js]
        ars = [_bdot_nt(lhs[p], s[p]) for p in ps]
        v2 = [by_head(x) for x in v]
        yield
        aab = [jnp.where(strict_w, gm[p][:t, :2 * t], 0.0) for p in ps]
        aak = [jnp.where(strict_w, gm[p][:t, 2 * t:], 0.0) for p in ps]
        rkm = [jnp.where(incl_w, gm[p][t:, 2 * t:], 0.0) for p in ps]
        rb = [jnp.where(incl_w, gm[p][t:, :2 * t], 0.0) for p in ps]
        yield
        kv = [_bdot(jnp.concatenate([aak[p], rkm[p]], axis=0), v2[p]) for p in ps]
        rhs = [ars[p][:t] + kv[p][:t] for p in ps]
        yield
        tinv = yield from _inv_identity_plus_wide_stages([-x for x in aab], t)
        u = [_bdot(tinv[p], by_head(rhs[p])) for p in ps]
        yield
        y = [ars[p][t:] + _bdot(rb[p], by_head(u[p])) + kv[p][t:] for p in ps]
        yield
        upd = [_bdot_tn(jnp.concatenate([u[p], v[p]], axis=0),
                        jnp.concatenate([bv[p] * rem[p], k[p] * rem[p]], axis=0)) for p in ps]
        for p, j in enumerate(js):
            s_s[j] = s[p] * jnp.exp(glast[p]) + jnp.where(same_head, upd[p], 0.0)
        yield
        mu = head_sums(y, _split_dot)
        d = [y[p] - mu[p] * (1.0 / n) for p in ps]
        yield
        var = head_sums([x * x for x in d], _bdot)
        yield
        bonus = head_sums([r[p] * k[p] * rk_ref[:, psl[p]] for p in ps], _split_dot)
        yield
        for p in ps:
            yn = d[p] * lax.rsqrt(var[p] * (1.0 / n) + RWKV_GN_EPS) * lnw_ref[:, psl[p]] + lnb_ref[:, psl[p]]
            y_ref[:, sl[p]] = (yn + bonus[p] * v[p]) * g_ref[:, sl[p]]

    groups = [list(range(first, first + group)) for first in range(0, nb * pairs_per_seq, group)]
    _run_staggered([pair_group(js) for js in groups], RWKV_STAGGER)

    for jj in range(nb * pairs_per_seq):
        e, j = divmod(jj, pairs_per_seq)
        sp = s_s[jj]
        sfin_ref[e, 2 * j] = sp[:n, :n]
        sfin_ref[e, 2 * j + 1] = sp[n:, n:]


def _rwkv_scan(acts, s0, prm, bsz, t, nb, group):
    seq = acts[0].shape[0]
    grid = (bsz // nb, seq // t)
    const = lambda shape: pl.BlockSpec(shape, lambda b, i: (0,) * len(shape))
    tm = pl.BlockSpec((t, nb * D_MODEL), lambda b, i: (i, b))
    st = pl.BlockSpec((nb, H_C, N_HEAD_C, N_HEAD_C), lambda b, i: (b, 0, 0, 0))
    vec = const((1, D_MODEL))
    return pl.pallas_call(
        functools.partial(_rwkv_scan_kernel, t=t, nb=nb, group=group),
        grid=grid,
        in_specs=[tm] * N_RWKV_ACTS + [st, vec, vec, vec, vec, vec],
        out_specs=[tm, st],
        out_shape=[jax.ShapeDtypeStruct((seq, bsz * D_MODEL), F32),
                   jax.ShapeDtypeStruct((bsz, H_C, N_HEAD_C, N_HEAD_C), F32)],
        scratch_shapes=[pltpu.VMEM((nb * H_C // 2, LANES, LANES), F32)],
        compiler_params=_cparams(("parallel", "arbitrary"), 48),
    )(*acts, s0, prm["k_k"], prm["k_a"], prm["r_k"], prm["ln_w"], prm["ln_b"])


def _block_diag_groups(w, rows_per_group, cols_per_group):
    g = w.shape[0]
    w = w.reshape(g // 8, 8, rows_per_group, cols_per_group)
    eye = jnp.eye(8, dtype=w.dtype)
    out = jnp.einsum("jgrc,gh->jgrhc", w, eye)
    return out.reshape(g // 8, 8 * rows_per_group, 8 * cols_per_group)


def _pad_to(w, axis, size):
    pad = [(0, 0)] * w.ndim
    pad[axis] = (0, size - w.shape[axis])
    return jnp.pad(w, pad)


def _row(v):
    return v.reshape(1, -1).astype(F32)


def _recurrence_tiling(seq):
    if seq >= GDN_CHUNK:
        return dict(s5_tc=64, gdn_t=GDN_CHUNK, gdn_nb=4, rwkv_t=RWKV_CHUNK, rwkv_nb=2,
                    rwkv_group=2 * RWKV_PAIR_GROUP)
    return dict(s5_tc=seq, gdn_t=seq, gdn_nb=8, rwkv_t=seq, rwkv_nb=4, rwkv_group=2 * RWKV_PAIR_GROUP)


def _trunk(x_bm, bsz, seq, s5_re0, s5_im0, gdn_s0, gdn_conv0, rw_s0, rw_shift0, p, out_bm):
    bv, lv, _ = x_bm.shape
    cfg = _recurrence_tiling(seq)
    u, qkv, z, ba = _ab_in(x_bm, p["norm_mix0"], p["ab_w_in"])
    y_a, hr, hi = _s5(u.reshape(seq, bsz, D_A), s5_re0, s5_im0, p["s5"], cfg["s5_tc"])
    y_b, gdn_s, gdn_conv = _gdn(qkv.reshape(seq, bsz * D_QKV), ba.reshape(seq, bsz * GATE_PAD),
                                z.reshape(seq, bsz * D_B), gdn_conv0, gdn_s0, p["gdn"], bsz,
                                cfg["gdn_t"], cfg["gdn_nb"])
    x1 = _mix_ffn(x_bm, True, [y_a.reshape(lv, bv * D_A), y_b.reshape(lv, bv * D_B)],
                  [p["ab_w_out_a"], p["ab_w_out_b"]], p["norm_ffn0"], p["wg0"], p["wu0"], p["wd0"],
                  None, bv, lv, False)
    acts = _rwkv_proj(x1, rw_shift0.reshape(bv, bsz // bv, D_MODEL), p["rw"])
    shift = acts[N_RWKV_ACTS].reshape(bsz, D_MODEL)
    yg, rw_s = _rwkv_scan([a.reshape(seq, bsz * D_MODEL) for a in acts[:N_RWKV_ACTS]], rw_s0, p["rw"], bsz,
                          cfg["rwkv_t"], cfg["rwkv_nb"], cfg["rwkv_group"])
    y = _mix_ffn(x1, False, [yg.reshape(lv, bv * D_MODEL)], [p["rw"]["wo"]], p["norm_ffn1"], p["wg1"],
                 p["wu1"], p["wd1"], p["norm_final"], bv, lv, out_bm)
    return y, hr, hi, gdn_s, gdn_conv, rw_s, shift


def kernel(x_prompt, x_sample, state_s5_re, state_s5_im, state_gdn, state_gdn_conv, state_rwkv, state_rwkv_shift, norm_mix, norm_ffn, norm_final, ffn_w_gate, ffn_w_up, ffn_w_down, ab_w_in, ab_w_out, s5_lambda_re, s5_lambda_im, s5_log_step, s5_B_re, s5_B_im, s5_C_re, s5_C_im, s5_D, s5_w_glu, gdn_conv_w, gdn_A_log, gdn_dt_bias, gdn_norm_w, rw_maa, rw_w_r, rw_w_k, rw_w_v, rw_w_o, rw_w0, rw_w1, rw_w2, rw_a0, rw_a1, rw_a2, rw_g1, rw_g2, rw_k_k, rw_k_a, rw_r_k, rw_ln_w, rw_ln_b):
    bsz_p, seq_p, _ = x_prompt.shape
    bsz_s, seq_s, _ = x_sample.shape

    w_in = ab_w_in[0]
    n_main = D_A + D_QKV
    w_in = jnp.concatenate([w_in[:, :n_main], w_in[:, n_main + 2 * H_B:], w_in[:, n_main:n_main + 2 * H_B]],
                           axis=1)
    w_in = _pad_to(w_in, 1, D_IN_PAD).astype(BF16)

    ff = lambda w, layer: w[layer].astype(BF16)

    gate_lanes = lambda v: _pad_to(jnp.concatenate([jnp.zeros((H_B,), F32), v.astype(F32)]), 0,
                                   GATE_PAD).reshape(1, GATE_PAD)
    p = dict(
        norm_mix0=_row(norm_mix[0]), norm_ffn0=_row(norm_ffn[0]), norm_ffn1=_row(norm_ffn[1]),
        norm_final=_row(norm_final), ab_w_in=w_in,
        ab_w_out_a=ab_w_out[0][:D_A].astype(BF16), ab_w_out_b=ab_w_out[0][D_A:].astype(BF16),
        wg0=ff(ffn_w_gate, 0), wu0=ff(ffn_w_up, 0),
        wd0=ff(ffn_w_down, 0),
        wg1=ff(ffn_w_gate, 1), wu1=ff(ffn_w_up, 1),
        wd1=ff(ffn_w_down, 1),
        s5=dict(
            lr=_row(s5_lambda_re[0]), li=_row(s5_lambda_im[0]),
            ls=_row(jnp.repeat(s5_log_step[0], P_STATE)),
            bre=_block_diag_groups(jnp.swapaxes(s5_B_re[0], 1, 2), S5_GROUP, P_STATE).astype(BF16),
            bim=_block_diag_groups(jnp.swapaxes(s5_B_im[0], 1, 2), S5_GROUP, P_STATE).astype(BF16),
            cre=_block_diag_groups(jnp.swapaxes(s5_C_re[0], 1, 2), P_STATE, S5_GROUP).astype(BF16),
            cim=_block_diag_groups(jnp.swapaxes(s5_C_im[0], 1, 2), P_STATE, S5_GROUP).astype(BF16),
            d=_row(s5_D[0]), wglu=s5_w_glu[0].astype(BF16)),
        gdn=dict(conv_w=gdn_conv_w[0].astype(F32), alog=gate_lanes(gdn_A_log[0]),
                 dtb=gate_lanes(gdn_dt_bias[0]), norm_w=_row(gdn_norm_w[0])),
        rw=dict(
            norm_w=_row(norm_mix[1]), maa=rw_maa[0].astype(F32),
            wr=rw_w_r[0].astype(BF16), wk=rw_w_k[0].astype(BF16), wv=rw_w_v[0].astype(BF16),
            wo=rw_w_o[0].astype(BF16),
            w0=_row(rw_w0[0]), w1=_pad_to(rw_w1[0], 1, LANES).astype(BF16),
            w2=_pad_to(rw_w2[0], 0, LANES).astype(BF16),
            a0=_row(rw_a0[0]), a1=_pad_to(rw_a1[0], 1, LANES).astype(BF16),
            a2=_pad_to(rw_a2[0], 0, LANES).astype(BF16),
            g1=_pad_to(rw_g1[0], 1, 2 * LANES).astype(BF16), g2=_pad_to(rw_g2[0], 0, 2 * LANES).astype(BF16),
            k_k=_row(rw_k_k[0]), k_a=_row(rw_k_a[0]), r_k=_row(rw_r_k[0]),
            ln_w=_row(rw_ln_w[0]), ln_b=_row(rw_ln_b[0])),
    )

    zeros = lambda *shape: jnp.zeros(shape, F32)
    yp, p_hr, p_hi, p_gdn, p_conv, p_rw, p_shift = _trunk(
        x_prompt, bsz_p, seq_p, zeros(bsz_p, N_S5), zeros(bsz_p, N_S5), zeros(bsz_p, H_B, DK_B, DV_B),
        zeros(bsz_p, CONV_W - 1, D_QKV), zeros(bsz_p, H_C, N_HEAD_C, N_HEAD_C), zeros(bsz_p, D_MODEL), p,
        True)

    xs_tm = jnp.transpose(x_sample, (1, 0, 2)).reshape(1, seq_s * bsz_s, D_MODEL)
    ys, s_hr, s_hi, s_gdn, s_conv, s_rw, s_shift = _trunk(
        xs_tm, bsz_s, seq_s, state_s5_re[0].reshape(bsz_s, N_S5), state_s5_im[0].reshape(bsz_s, N_S5),
        state_gdn[0], state_gdn_conv[0], state_rwkv[0], state_rwkv_shift[0], p, True)
    y_sample = jnp.transpose(ys.reshape(seq_s, bsz_s, D_MODEL), (1, 0, 2))

    s5_shape = lambda b: (1, b, G_A, P_STATE)
    return (yp, y_sample,
            p_hr.reshape(s5_shape(bsz_p)), p_hi.reshape(s5_shape(bsz_p)), p_gdn[None], p_conv[None],
            p_rw[None], p_shift[None],
            s_hr.reshape(s5_shape(bsz_s)), s_hi.reshape(s5_shape(bsz_s)), s_gdn[None], s_conv[None],
            s_rw[None], s_shift[None])
```

```python
import functools
import math

import jax
import jax.numpy as jnp
from jax import lax
from jax.experimental import pallas as pl
from jax.experimental.pallas import tpu as pltpu

F32 = jnp.float32
BF16 = jnp.bfloat16

D_MODEL = 1024
D_A = 512
S5_GROUP = 16
G_A = 32
P_STATE = 64
N_S5 = G_A * P_STATE
D_B = 512
H_B = 4
DK_B = 128
DV_B = 128
D_QKV = 2 * H_B * DK_B + D_B
CONV_W = 4
GDN_CHUNK = 64
N_HEAD_C = 64
H_C = 16
RWKV_CHUNK = 64
RWKV_PAIR_GROUP = 8
RWKV_STAGGER = 7
RWKV_GN_EPS = 64e-5
N_RWKV_ACTS = 6
D_FF = 2816
NORM_EPS = 1e-6

LANES = 128
SUBLANES = 8
FF_TILE = 256
N_FF_TILES = D_FF // FF_TILE
GATE_PAD = LANES
D_IN_PAD = 4 * 512 + 512 + GATE_PAD
S5_LANE_BLOCK = 512
N_S5_BLOCKS = N_S5 // S5_LANE_BLOCK
TOKEN_TILE = 512
RWKV_PROJ_TILE = 512
MIB = 1024 * 1024


def _cparams(semantics, vmem_mib):
    return pltpu.CompilerParams(dimension_semantics=semantics, vmem_limit_bytes=vmem_mib * MIB)


def _rmsnorm(x, w):
    return x * lax.rsqrt(jnp.mean(x * x, axis=-1, keepdims=True) + NORM_EPS) * w


def _sigmoid(x):
    return 0.5 + 0.5 * jnp.tanh(0.5 * x)


def _silu(x):
    half = 0.5 * x
    return half + half * jnp.tanh(half)


def _softplus(x):
    return jnp.maximum(x, 0.0) + jnp.log1p(jnp.exp(-jnp.abs(x)))


def _bdot(a, b):
    return jnp.dot(a.astype(BF16), b.astype(BF16), preferred_element_type=F32)


def _bdot_nt(a, b):
    return lax.dot_general(a.astype(BF16), b.astype(BF16), (((1,), (1,)), ((), ())),
                           preferred_element_type=F32)


def _bdot_tn(a, b):
    return lax.dot_general(a.astype(BF16), b.astype(BF16), (((0,), (0,)), ((), ())),
                           preferred_element_type=F32)


def _split(a):
    hi = a.astype(BF16)
    return hi, (a - hi.astype(F32)).astype(BF16)


def _split3(a):
    p1 = a.astype(BF16)
    r1 = a - p1.astype(F32)
    p2 = r1.astype(BF16)
    return p1, p2, (r1 - p2.astype(F32)).astype(BF16)


def _split_dot(a, exact_bf16):
    hi, lo = _split(a)
    return (jnp.dot(hi, exact_bf16, preferred_element_type=F32)
            + jnp.dot(lo, exact_bf16, preferred_element_type=F32))


def _cumsum_rows(tri_bf16, w, parts=3):
    split = _split3(w) if parts == 3 else _split(w)
    return sum(jnp.dot(tri_bf16, part, preferred_element_type=F32) for part in split)


def _iota2(shape):
    return (lax.broadcasted_iota(jnp.int32, shape, 0), lax.broadcasted_iota(jnp.int32, shape, 1))


def _run_staggered(gens, lag):
    pending, active, tick = list(gens), [], 0
    while pending or active:
        if pending and tick % lag == 0:
            active.append(pending.pop(0))
        for g in list(active):
            try:
                next(g)
            except StopIteration:
                active.remove(g)
        tick += 1


def _inv_identity_plus_wide(lmws, t):
    gen = _inv_identity_plus_wide_stages(lmws, t)
    while True:
        try:
            next(gen)
        except StopIteration as done:
            return done.value


def _inv_identity_plus_wide_stages(lmws, t):
    row, col = _iota2((t, 2 * t))
    left = col < t
    col = jnp.where(left, col, col - t)
    eye = (row == col).astype(F32)
    base = min(t, 16)

    def blockdiag(xw):
        return jnp.concatenate([jnp.where(left, xw, 0.0), jnp.where(left, 0.0, xw)], axis=0)

    if t > base:
        same_base = (row >> 4) == (col >> 4)
        ns = [jnp.where(same_base, -lmw, 0.0) for lmw in lmws]
    else:
        ns = [-lmw for lmw in lmws]
    xs = [eye + n for n in ns]
    ps = [_bdot(n, blockdiag(n)) for n in ns]
    yield
    k = 2
    while k < base:
        bds = [blockdiag(p) for p in ps]
        if 2 * k < base:
            both = [_bdot(jnp.concatenate([p, x], axis=0), bd) for p, x, bd in zip(ps, xs, bds)]
            ps = [r[:t] for r in both]
            xs = [x + r[t:] for x, r in zip(xs, both)]
        else:
            xs = [x + _bdot(x, bd) for x, bd in zip(xs, bds)]
        yield
        k *= 2
    shift = 4
    blk = base
    while blk < t:
        same_big = (row >> (shift + 1)) == (col >> (shift + 1))
        same_small = (row >> shift) == (col >> shift)
        off = [jnp.where(same_big, jnp.where(same_small, 0.0, lmw), 0.0) for lmw in lmws]
        mids = [_bdot(o, blockdiag(x)) for o, x in zip(off, xs)]
        yield
        xs = [x - _bdot(x, blockdiag(m)) for x, m in zip(xs, mids)]
        yield
        blk *= 2
        shift += 1
    return xs


def _ab_in_kernel(x_ref, nw_ref, w_ref, u_ref, qkv_ref, z_ref, ba_ref):
    h = _rmsnorm(x_ref[...], nw_ref[...])
    p = jnp.dot(h.astype(BF16), w_ref[...], preferred_element_type=F32)
    u_ref[...] = p[:, 0:D_A]
    qkv_ref[...] = p[:, D_A:D_A + D_QKV]
    z_ref[...] = p[:, D_A + D_QKV:D_A + D_QKV + D_B]
    ba_ref[...] = p[:, D_A + D_QKV + D_B:D_IN_PAD]


def _ab_in(x_bm, norm_w, w_in):
    bv, lv, _ = x_bm.shape
    tl = min(TOKEN_TILE, lv)
    grid = (bv, lv // tl)
    tm = lambda n: pl.BlockSpec((tl, n), lambda b, i: (i, b))
    const = lambda shape: pl.BlockSpec(shape, lambda b, i: (0,) * len(shape))
    return pl.pallas_call(
        _ab_in_kernel,
        grid=grid,
        in_specs=[pl.BlockSpec((None, tl, D_MODEL), lambda b, i: (b, i, 0)),
                  const((1, D_MODEL)), const((D_MODEL, D_IN_PAD))],
        out_specs=[tm(D_A), tm(D_QKV), tm(D_B), tm(GATE_PAD)],
        out_shape=[jax.ShapeDtypeStruct((lv, bv * D_A), F32),
                   jax.ShapeDtypeStruct((lv, bv * D_QKV), F32),
                   jax.ShapeDtypeStruct((lv, bv * D_B), F32),
                   jax.ShapeDtypeStruct((lv, bv * GATE_PAD), F32)],
        compiler_params=_cparams(("parallel", "parallel"), 48),
    )(x_bm, norm_w, w_in)


def _s5_kernel(u_ref, h0r_ref, h0i_ref, lr_ref, li_ref, ls_ref, bre_ref, bim_ref, cre_ref, cim_ref,
               d_ref, wglu_ref, y_ref, hr_out, hi_out,
               ar_s, ai_s, bbr_s, bbi_s, hr_s, hi_s, bur_s, bui_s, yg_s, *, tc):
    rows = tc * SUBLANES

    @pl.when(pl.program_id(1) == 0)
    def _():
        lr = lr_ref[...]
        li = li_ref[...]
        dt = jnp.exp(ls_ref[...])
        mag = jnp.exp(lr * dt)
        ar = mag * jnp.cos(li * dt)
        ai = mag * jnp.sin(li * dt)
        den = lr * lr + li * li
        nr = ar - 1.0
        cr = (nr * lr + ai * li) / den
        ci = (ai * lr - nr * li) / den
        ar_s[...] = jnp.broadcast_to(ar, (SUBLANES, N_S5))
        ai_s[...] = jnp.broadcast_to(ai, (SUBLANES, N_S5))
        for j in range(N_S5_BLOCKS):
            sl = slice(j * S5_LANE_BLOCK, (j + 1) * S5_LANE_BLOCK)
            bbr_s[j] = (cr[:, sl] * bre_ref[j] - ci[:, sl] * bim_ref[j]).astype(BF16)
            bbi_s[j] = (cr[:, sl] * bim_ref[j] + ci[:, sl] * bre_ref[j]).astype(BF16)
        hr_s[...] = h0r_ref[...]
        hi_s[...] = h0i_ref[...]

    u = u_ref[...].reshape(rows, D_A)
    ub = u.astype(BF16)
    for j in range(N_S5_BLOCKS):
        sl = slice(j * S5_LANE_BLOCK, (j + 1) * S5_LANE_BLOCK)
        uj = ub[:, j * LANES:(j + 1) * LANES]
        bur_s[:, :, sl] = jnp.dot(uj, bbr_s[j], preferred_element_type=F32).reshape(tc, SUBLANES, S5_LANE_BLOCK)
        bui_s[:, :, sl] = jnp.dot(uj, bbi_s[j], preferred_element_type=F32).reshape(tc, SUBLANES, S5_LANE_BLOCK)

    for j in range(N_S5_BLOCKS):
        sl = slice(j * S5_LANE_BLOCK, (j + 1) * S5_LANE_BLOCK)
        ar = ar_s[:, sl]
        ai = ai_s[:, sl]

        def step(t, carry, sl=sl, ar=ar, ai=ai):
            hr, hi = carry
            nr = ar * hr - ai * hi + bur_s[t, :, sl]
            ni = ar * hi + ai * hr + bui_s[t, :, sl]
            bur_s[t, :, sl] = nr
            bui_s[t, :, sl] = ni
            return nr, ni

        hr, hi = lax.fori_loop(0, tc, step, (hr_s[:, sl], hi_s[:, sl]), unroll=min(tc, 8))
        hr_s[:, sl] = hr
        hi_s[:, sl] = hi

    for j in range(N_S5_BLOCKS):
        sl = slice(j * S5_LANE_BLOCK, (j + 1) * S5_LANE_BLOCK)
        cl = slice(j * LANES, (j + 1) * LANES)
        xr = bur_s[:, :, sl].reshape(rows, S5_LANE_BLOCK).astype(BF16)
        xi = bui_s[:, :, sl].reshape(rows, S5_LANE_BLOCK).astype(BF16)
        yj = (jnp.dot(xr, cre_ref[j], preferred_element_type=F32)
              - jnp.dot(xi, cim_ref[j], preferred_element_type=F32)
              + d_ref[:, cl] * u[:, cl])
        yg_s[:, cl] = jax.nn.gelu(yj)

    yg = yg_s[...]
    out = yg * _sigmoid(jnp.dot(yg.astype(BF16), wglu_ref[...], preferred_element_type=F32))
    y_ref[...] = out.reshape(tc, SUBLANES, D_A)
    hr_out[...] = hr_s[...]
    hi_out[...] = hi_s[...]


def _s5(u_tm, h0r, h0i, prm, tc):
    seq, bsz, _ = u_tm.shape
    grid = (bsz // SUBLANES, seq // tc)
    const = lambda shape: pl.BlockSpec(shape, lambda b, i: (0,) * len(shape))
    act = pl.BlockSpec((tc, SUBLANES, D_A), lambda b, i: (i, b, 0))
    st = pl.BlockSpec((SUBLANES, N_S5), lambda b, i: (b, 0))
    vec = const((1, N_S5))
    small = pltpu.VMEM((SUBLANES, N_S5), F32)
    big = pltpu.VMEM((tc, SUBLANES, N_S5), F32)
    return pl.pallas_call(
        functools.partial(_s5_kernel, tc=tc),
        grid=grid,
        in_specs=[act, st, st, vec, vec, vec,
                  const((N_S5_BLOCKS, LANES, S5_LANE_BLOCK)), const((N_S5_BLOCKS, LANES, S5_LANE_BLOCK)),
                  const((N_S5_BLOCKS, S5_LANE_BLOCK, LANES)), const((N_S5_BLOCKS, S5_LANE_BLOCK, LANES)),
                  const((1, D_A)), const((D_A, D_A))],
        out_specs=[act, st, st],
        out_shape=[jax.ShapeDtypeStruct((seq, bsz, D_A), F32),
                   jax.ShapeDtypeStruct((bsz, N_S5), F32),
                   jax.ShapeDtypeStruct((bsz, N_S5), F32)],
        scratch_shapes=[small, small, pltpu.VMEM((N_S5_BLOCKS, LANES, S5_LANE_BLOCK), BF16),
                        pltpu.VMEM((N_S5_BLOCKS, LANES, S5_LANE_BLOCK), BF16), small, small, big, big,
                        pltpu.VMEM((tc * SUBLANES, D_A), F32)],
        compiler_params=_cparams(("parallel", "arbitrary"), 48),
    )(u_tm, h0r, h0i, prm["lr"], prm["li"], prm["ls"], prm["bre"], prm["bim"], prm["cre"], prm["cim"],
      prm["d"], prm["wglu"])


def _gdn_kernel(qkv_ref, ba_ref, z_ref, conv0_ref, s0_ref, cw_ref, alog_ref, dtb_ref, nw_ref,
                o_ref, sfin_ref, convn_ref, xbuf, s_s, *, t, nb):
    pad = SUBLANES
    hist = CONV_W - 1

    @pl.when(pl.program_id(1) == 0)
    def _():
        for e in range(nb):
            xbuf[pad - hist:pad, e * D_QKV:(e + 1) * D_QKV] = conv0_ref[e]
        s_s[...] = s0_ref[...].reshape(nb * H_B, DK_B, DV_B)

    row, col = _iota2((t, t))
    causal = row >= col
    strict = row > col
    tri = causal.astype(BF16)
    lane = lax.broadcasted_iota(jnp.int32, (t, LANES), 1)
    nw = nw_ref[...]
    cw = cw_ref[...]
    ones_sq = jnp.ones((DK_B, DK_B), BF16)
    zeros_wide = jnp.zeros((t, DK_B + DV_B), F32)
    zeros_v = jnp.zeros((t, DV_B), F32)

    def seq_group(es):
        ys, betas, gcs = [], [], []
        for e in es:
            cols = slice(e * D_QKV, (e + 1) * D_QKV)
            x = qkv_ref[:, cols]
            xbuf[pad:pad + t, cols] = x
            acc = x * cw[hist:hist + 1, :]
            for j in range(hist):
                acc = acc + xbuf[pad - hist + j:pad - hist + j + t, cols] * cw[j:j + 1, :]
            last = xbuf[pad + t - hist:pad + t, cols]
            convn_ref[e] = last
            xbuf[pad - hist:pad, cols] = last
            ys.append(_silu(acc))
            ba = ba_ref[:, e * GATE_PAD:(e + 1) * GATE_PAD]
            betas.append(_sigmoid(ba))
            g = -jnp.exp(alog_ref[...]) * _softplus(ba + dtb_ref[...])
            gcs.append(_cumsum_rows(tri, g))
            yield

        chains = [(i, h) for i in range(len(es)) for h in range(H_B)]
        heads = range(len(chains))
        bcol = [betas[i][:, h:h + 1] for i, h in chains]
        gcol = [gcs[i][:, H_B + h:H_B + h + 1] for i, h in chains]
        y_of = [ys[i] for i, h in chains]
        h_of = [h for i, h in chains]
        decay = []
        gc_parts = [[p.astype(F32) for p in _split3(gc)] for gc in gcs]
        for i, hh in chains:
            p1, p2, p3 = [p[:, H_B + hh:H_B + hh + 1] for p in gc_parts[i]]
            dl = jnp.where(lane == 0, p1, jnp.where(lane == 1, p2, jnp.where(lane == 2, p3,
                                                                             jnp.where(lane < 6, 1.0, 0.0))))
            dr = jnp.where(lane < 3, 1.0, jnp.where(lane == 3, -p1, jnp.where(lane == 4, -p2,
                                                                            jnp.where(lane == 5, -p3, 0.0))))
            decay.append(jnp.exp(jnp.where(causal, _bdot_nt(dl, dr), -jnp.inf)))
        eg = [jnp.exp(gcol[h]) for h in heads]
        glast = [gcol[h][t - 1:t, :] for h in heads]
        yield
        q = [y_of[c][:, h_of[c] * DK_B:(h_of[c] + 1) * DK_B] for c in heads]
        k = [y_of[c][:, (H_B + h_of[c]) * DK_B:(H_B + h_of[c] + 1) * DK_B] for c in heads]
        v = [y_of[c][:, 2 * H_B * DK_B + h_of[c] * DV_B:2 * H_B * DK_B + (h_of[c] + 1) * DV_B] for c in heads]
        q = [x * (lax.rsqrt(_bdot(x * x, ones_sq) + NORM_EPS) * (DK_B ** -0.5)) for x in q]
        k = [x * lax.rsqrt(_bdot(x * x, ones_sq) + NORM_EPS) for x in k]
        kb = [k[h] * bcol[h] for h in heads]
        vb = [v[h] * bcol[h] for h in heads]
        yield
        lm = [jnp.where(strict, _bdot_nt(kb[h], k[h]) * decay[h], 0.0) for h in heads]
        attn = [_bdot_nt(q[h], k[h]) * decay[h] for h in heads]
        s = [s_s[es[i] * H_B + h] for i, h in chains]
        qs = [_bdot(q[h] * eg[h], s[h]) for h in heads]
        yield
        pairs = [(c, c + 1) for c in range(0, len(chains), 2)]
        tinv = yield from _inv_identity_plus_wide_stages(
            [jnp.concatenate([lm[c0], lm[c1]], axis=1) for c0, c1 in pairs], t)
        kbg = [kb[h] * eg[h] for h in heads]
        uw = [_bdot(tinv[i], jnp.concatenate(
            [jnp.concatenate([vb[c0], kbg[c0], zeros_wide], axis=1),
             jnp.concatenate([zeros_wide, vb[c1], kbg[c1]], axis=1)], axis=0)) for i, (c0, c1) in enumerate(pairs)]
        u = [uw[c // 2][:, (c % 2) * (DK_B + DV_B):(c % 2) * (DK_B + DV_B) + DV_B] for c in heads]
        w = [uw[c // 2][:, (c % 2) * (DK_B + DV_B) + DV_B:(c % 2 + 1) * (DK_B + DV_B)] for c in heads]
        yield
        v_new = [u[h] - _bdot(w[h], s[h]) for h in heads]
        yield
        av = [_bdot(jnp.concatenate([attn[c0], attn[c1]], axis=1), jnp.concatenate(
            [jnp.concatenate([v_new[c0], zeros_v], axis=1),
             jnp.concatenate([zeros_v, v_new[c1]], axis=1)], axis=0)) for c0, c1 in pairs]
        o = [qs[c] + av[c // 2][:, (c % 2) * DV_B:(c % 2 + 1) * DV_B] for c in heads]
        upd = [_bdot_tn(k[h] * jnp.exp(glast[h] - gcol[h]), v_new[h]) for h in heads]
        yield
        for c, (i, h) in enumerate(chains):
            e = es[i]
            s_new = s[c] * jnp.exp(glast[c]) + upd[c]
            s_s[e * H_B + h] = s_new
            sfin_ref[e, h] = s_new
            cols = slice(e * D_B + h * DV_B, e * D_B + (h + 1) * DV_B)
            o_ref[:, cols] = _rmsnorm(o[c], nw) * _silu(z_ref[:, cols])

    _run_staggered([seq_group(list(range(nb)))], 1)


def _gdn(qkv_tm, ba_tm, z_tm, conv0, s0, prm, bsz, t, nb):
    seq = qkv_tm.shape[0]
    grid = (bsz // nb, seq // t)
    const = lambda shape: pl.BlockSpec(shape, lambda b, i: (0,) * len(shape))
    tm = lambda n: pl.BlockSpec((t, nb * n), lambda b, i: (i, b))
    state = pl.BlockSpec((nb, H_B, DK_B, DV_B), lambda b, i: (b, 0, 0, 0))
    conv = pl.BlockSpec((nb, CONV_W - 1, D_QKV), lambda b, i: (b, 0, 0))
    return pl.pallas_call(
        functools.partial(_gdn_kernel, t=t, nb=nb),
        grid=grid,
        in_specs=[tm(D_QKV), tm(GATE_PAD), tm(D_B), conv, state,
                  const((CONV_W, D_QKV)), const((1, GATE_PAD)), const((1, GATE_PAD)), const((1, DV_B))],
        out_specs=[tm(D_B), state, conv],
        out_shape=[jax.ShapeDtypeStruct((seq, bsz * D_B), F32),
                   jax.ShapeDtypeStruct((bsz, H_B, DK_B, DV_B), F32),
                   jax.ShapeDtypeStruct((bsz, CONV_W - 1, D_QKV), F32)],
        scratch_shapes=[pltpu.VMEM((t + SUBLANES, nb * D_QKV), F32),
                        pltpu.VMEM((nb * H_B, DK_B, DV_B), F32)],
        compiler_params=_cparams(("parallel", "arbitrary"), 48),
    )(qkv_tm, ba_tm, z_tm, conv0, s0, prm["conv_w"], prm["alog"], prm["dtb"], prm["norm_w"])


def _mix_ffn_kernel(*refs, n_mix, final_norm):
    x_ref = refs[0]
    mix_refs = refs[1:1 + n_mix]
    wout_refs = refs[1 + n_mix:1 + 2 * n_mix]
    nffn_ref, wg_ref, wu_ref, wd_ref = refs[1 + 2 * n_mix:5 + 2 * n_mix]
    rest = refs[5 + 2 * n_mix:]
    if final_norm:
        nfin_ref, out_ref = rest
    else:
        (out_ref,) = rest

    x1 = x_ref[...]
    for m_ref, w_ref in zip(mix_refs, wout_refs):
        x1 = x1 + jnp.dot(m_ref[...].astype(BF16), w_ref[...], preferred_element_type=F32)
    h = _rmsnorm(x1, nffn_ref[...]).astype(BF16)
    out_ref[...] = x1

    for c in range(N_FF_TILES):
        cols = slice(c * FF_TILE, (c + 1) * FF_TILE)
        gate = jnp.dot(h, wg_ref[:, cols], preferred_element_type=F32)
        up = jnp.dot(h, wu_ref[:, cols], preferred_element_type=F32)
        act = (_silu(gate) * up).astype(BF16)
        out_ref[...] += jnp.dot(act, wd_ref[cols, :], preferred_element_type=F32)
    if final_norm:
        out_ref[...] = _rmsnorm(out_ref[...], nfin_ref[...])


def _mix_ffn(x, x_bm, mixes, wouts, nffn, wg, wu, wd, nfin, bv, lv, out_bm):
    tl = min(TOKEN_TILE, lv)
    grid = (bv, lv // tl)
    const = lambda shape: pl.BlockSpec(shape, lambda b, i: (0,) * len(shape),
                                       pipeline_mode=pl.Buffered(1))
    tm = lambda n: pl.BlockSpec((tl, n), lambda b, i: (i, b))
    bm = pl.BlockSpec((None, tl, D_MODEL), lambda b, i: (b, i, 0))
    in_specs = [bm if x_bm else tm(D_MODEL)]
    in_specs += [tm(m.shape[1] // bv) for m in mixes]
    in_specs += [const(w.shape) for w in wouts]
    in_specs += [const((1, D_MODEL)), const(wg.shape), const(wu.shape), const(wd.shape)]
    args = [x, *mixes, *wouts, nffn, wg, wu, wd]
    if nfin is not None:
        in_specs.append(const((1, D_MODEL)))
        args.append(nfin)
    if out_bm:
        out_spec, out_shape = bm, jax.ShapeDtypeStruct((bv, lv, D_MODEL), F32)
    else:
        out_spec, out_shape = tm(D_MODEL), jax.ShapeDtypeStruct((lv, bv * D_MODEL), F32)
    return pl.pallas_call(
        functools.partial(_mix_ffn_kernel, n_mix=len(mixes), final_norm=nfin is not None),
        grid=grid,
        in_specs=in_specs,
        out_specs=out_spec,
        out_shape=out_shape,
        compiler_params=_cparams(("parallel", "parallel"), 56),
    )(*args)


def _rwkv_proj_kernel(x_ref, shift0_ref, nw_ref, maa_ref, wr_ref, wk_ref, wv_ref, w0_ref, w1_ref, w2_ref,
                      a0_ref, a1_ref, a2_ref, g1_ref, g2_ref,
                      r_out, w_out, k_out, v_out, a_out, g_out, shift_out, hbuf, *, shift):
    pad = -(-shift // SUBLANES) * SUBLANES

    @pl.when(pl.program_id(1) == 0)
    def _():
        hbuf[pad - shift:pad, :] = shift0_ref[...]

    h = _rmsnorm(x_ref[...], nw_ref[...])
    rows = h.shape[0]
    hbuf[pad:pad + rows, :] = h
    prev = hbuf[pad - shift:pad - shift + rows, :]
    last = h[rows - shift:]
    hbuf[pad - shift:pad, :] = last
    shift_out[...] = last
    xx = prev - h
    maa = maa_ref[...]
    xr, xw, xk, xv, xa, xg = [h + xx * maa[j:j + 1, :] for j in range(6)]
    r = _bdot(xr, wr_ref[...])
    k = _bdot(xk, wk_ref[...])
    v = _bdot(xv, wv_ref[...])
    u = w0_ref[...] + _bdot(jnp.tanh(_bdot(xw, w1_ref[...])), w2_ref[...])
    a = _sigmoid(a0_ref[...] + _bdot(_bdot(xa, a1_ref[...]), a2_ref[...]))
    g = _bdot(_sigmoid(_bdot(xg, g1_ref[...])), g2_ref[...])
    r_out[...] = r
    w_out[...] = -math.exp(-0.5) * _sigmoid(u)
    k_out[...] = k
    v_out[...] = v
    a_out[...] = a
    g_out[...] = g


def _rwkv_proj(x_tm, shift0, prm):
    bv, shift, _ = shift0.shape
    lv = x_tm.shape[0]
    tr = max(shift, min(RWKV_PROJ_TILE, lv))
    grid = (bv, lv // tr)
    const = lambda shape: pl.BlockSpec(shape, lambda b, i: (0,) * len(shape),
                                       pipeline_mode=pl.Buffered(1))
    act = pl.BlockSpec((tr, D_MODEL), lambda b, i: (i, b))
    carry = pl.BlockSpec((None, shift, D_MODEL), lambda b, i: (b, 0, 0))
    vec = const((1, D_MODEL))
    weights = [prm[n] for n in ("wr", "wk", "wv")]
    in_specs = [act, carry, vec, const((6, D_MODEL))]
    in_specs += [const(w.shape) for w in weights]
    in_specs += [vec, const(prm["w1"].shape), const(prm["w2"].shape),
                 vec, const(prm["a1"].shape), const(prm["a2"].shape),
                 const(prm["g1"].shape), const(prm["g2"].shape)]
    out_act = jax.ShapeDtypeStruct((lv, bv * D_MODEL), F32)
    pad = -(-shift // SUBLANES) * SUBLANES
    return pl.pallas_call(
        functools.partial(_rwkv_proj_kernel, shift=shift),
        grid=grid,
        in_specs=in_specs,
        out_specs=[act] * N_RWKV_ACTS + [carry],
        out_shape=[out_act] * N_RWKV_ACTS + [jax.ShapeDtypeStruct((bv, shift, D_MODEL), F32)],
        scratch_shapes=[pltpu.VMEM((pad + tr, D_MODEL), F32)],
        compiler_params=_cparams(("parallel", "arbitrary"), 56),
    )(x_tm, shift0, prm["norm_w"], prm["maa"], prm["wr"], prm["wk"], prm["wv"], prm["w0"], prm["w1"],
      prm["w2"], prm["a0"], prm["a1"], prm["a2"], prm["g1"], prm["g2"])


def _rwkv_scan_kernel(r_ref, w_ref, k_ref, v_ref, a_ref, g_ref, s0_ref, kkw_ref, kaw_ref, rk_ref, lnw_ref, lnb_ref,
                      y_ref, sfin_ref, s_s, *, t, nb, group):
    n = N_HEAD_C
    zero_blk = jnp.zeros((n, n), F32)

    pairs_per_seq = H_C // 2

    @pl.when(pl.program_id(1) == 0)
    def _():
        for jj in range(nb * pairs_per_seq):
            e, j = divmod(jj, pairs_per_seq)
            top = jnp.concatenate([s0_ref[e, 2 * j], zero_blk], axis=1)
            bot = jnp.concatenate([zero_blk, s0_ref[e, 2 * j + 1]], axis=1)
            s_s[jj] = jnp.concatenate([top, bot], axis=0)

    row, col = _iota2((t, t))
    tri = (row >= col).astype(BF16)
    roww, colw = _iota2((t, 2 * t))
    colw = jnp.where(colw >= t, colw - t, colw)
    incl_w = roww >= colw
    strict_w = roww > colw
    lane = lax.broadcasted_iota(jnp.int32, (t, LANES), 1)
    head0 = lane < n

    def by_head(x):
        return jnp.concatenate([jnp.where(head0, x, 0.0), jnp.where(head0, 0.0, x)], axis=0)

    prow, pcol = _iota2((LANES, LANES))
    same_head = (prow >> 6) == (pcol >> 6)
    ones_blk = same_head.astype(BF16)

    def head_sums(xs, dot):
        tot = dot(jnp.concatenate(xs, axis=0), ones_blk)
        return [tot[i * t:(i + 1) * t] for i in range(len(xs))]

    def pair_group(js):
        ps = range(len(js))
        sl = [slice(j * LANES, (j + 1) * LANES) for j in js]
        psl = [slice((j % pairs_per_seq) * LANES, (j % pairs_per_seq + 1) * LANES) for j in js]
        r = [r_ref[:, x] for x in sl]
        w = [w_ref[:, x] for x in sl]
        kraw = [k_ref[:, x] for x in sl]
        v = [v_ref[:, x] for x in sl]
        a = [a_ref[:, x] for x in sl]
        kkr = [kraw[p] * kkw_ref[:, psl[p]] for p in ps]
        k = [kraw[p] * (1.0 + (a[p] - 1.0) * kaw_ref[:, psl[p]]) for p in ps]
        kk = [x * lax.rsqrt(ss + NORM_EPS) for x, ss in zip(kkr, head_sums([x * x for x in kkr], _bdot))]
        bv = [kk[p] * a[p] for p in ps]
        yield
        gc = [_cumsum_rows(tri, x, parts=2) for x in w]
        glast = [x[t - 1:t, :] for x in gc]
        yield
        pinv = [jnp.exp(-x) for x in gc]
        at = [-kk[p] * jnp.exp(gc[p] - w[p]) for p in ps]
        bt = [bv[p] * pinv[p] for p in ps]
        kt = [k[p] * pinv[p] for p in ps]
        rt = [r[p] * jnp.exp(gc[p]) for p in ps]
        rem = [jnp.exp(glast[p] - gc[p]) for p in ps]
        yield
        lhs = [jnp.concatenate([at[p], rt[p]], axis=0) for p in ps]
        gm = [_bdot_nt(lhs[p], jnp.concatenate([by_head(bt[p]), by_head(kt[p])], axis=0)) for p in ps]
        yield
        s = [s_s[j] for j in js]
        ars = [_bdot_nt(lhs[p], s[p]) for p in ps]
        v2 = [by_head(x) for x in v]
        yield
        aab = [jnp.where(strict_w, gm[p][:t, :2 * t], 0.0) for p in ps]
        aak = [jnp.where(strict_w, gm[p][:t, 2 * t:], 0.0) for p in ps]
        rkm = [jnp.where(incl_w, gm[p][t:, 2 * t:], 0.0) for p in ps]
        rb = [jnp.where(incl_w, gm[p][t:, :2 * t], 0.0) for p in ps]
        yield
        kv = [_bdot(jnp.concatenate([aak[p], rkm[p]], axis=0), v2[p]) for p in ps]
        rhs = [ars[p][:t] + kv[p][:t] for p in ps]
        yield
        tinv = yield from _inv_identity_plus_wide_stages([-x for x in aab], t)
        u = [_bdot(tinv[p], by_head(rhs[p])) for p in ps]
        yield
        y = [ars[p][t:] + _bdot(rb[p], by_head(u[p])) + kv[p][t:] for p in ps]
        yield
        upd = [_bdot_tn(jnp.concatenate([u[p], v[p]], axis=0),
                        jnp.concatenate([bv[p] * rem[p], k[p] * rem[p]], axis=0)) for p in ps]
        for p, j in enumerate(js):
            s_s[j] = s[p] * jnp.exp(glast[p]) + jnp.where(same_head, upd[p], 0.0)
        yield
        mu = head_sums(y, _split_dot)
        d = [y[p] - mu[p] * (1.0 / n) for p in ps]
        yield
        var = head_sums([x * x for x in d], _bdot)
        yield
        bonus = head_sums([r[p] * k[p] * rk_ref[:, psl[p]] for p in ps], _split_dot)
        yield
        for p in ps:
            yn = d[p] * lax.rsqrt(var[p] * (1.0 / n) + RWKV_GN_EPS) * lnw_ref[:, psl[p]] + lnb_ref[:, psl[p]]
            y_ref[:, sl[p]] = (yn + bonus[p] * v[p]) * g_ref[:, sl[p]]

    groups = [list(range(first, first + group)) for first in range(0, nb * pairs_per_seq, group)]
    _run_staggered([pair_group(js) for js in groups], RWKV_STAGGER)

    for jj in range(nb * pairs_per_seq):
        e, j = divmod(jj, pairs_per_seq)
        sp = s_s[jj]
        sfin_ref[e, 2 * j] = sp[:n, :n]
        sfin_ref[e, 2 * j + 1] = sp[n:, n:]


def _rwkv_scan(acts, s0, prm, bsz, t, nb, group):
    seq = acts[0].shape[0]
    grid = (bsz // nb, seq // t)
    const = lambda shape: pl.BlockSpec(shape, lambda b, i: (0,) * len(shape))
    tm = pl.BlockSpec((t, nb * D_MODEL), lambda b, i: (i, b))
    st = pl.BlockSpec((nb, H_C, N_HEAD_C, N_HEAD_C), lambda b, i: (b, 0, 0, 0))
    vec = const((1, D_MODEL))
    return pl.pallas_call(
        functools.partial(_rwkv_scan_kernel, t=t, nb=nb, group=group),
        grid=grid,
        in_specs=[tm] * N_RWKV_ACTS + [st, vec, vec, vec, vec, vec],
        out_specs=[tm, st],
        out_shape=[jax.ShapeDtypeStruct((seq, bsz * D_MODEL), F32),
                   jax.ShapeDtypeStruct((bsz, H_C, N_HEAD_C, N_HEAD_C), F32)],
        scratch_shapes=[pltpu.VMEM((nb * H_C // 2, LANES, LANES), F32)],
        compiler_params=_cparams(("parallel", "arbitrary"), 48),
    )(*acts, s0, prm["k_k"], prm["k_a"], prm["r_k"], prm["ln_w"], prm["ln_b"])


def _block_diag_groups(w, rows_per_group, cols_per_group):
    g = w.shape[0]
    w = w.reshape(g // 8, 8, rows_per_group, cols_per_group)
    eye = jnp.eye(8, dtype=w.dtype)
    out = jnp.einsum("jgrc,gh->jgrhc", w, eye)
    return out.reshape(g // 8, 8 * rows_per_group, 8 * cols_per_group)


def _pad_to(w, axis, size):
    pad = [(0, 0)] * w.ndim
    pad[axis] = (0, size - w.shape[axis])
    return jnp.pad(w, pad)


def _row(v):
    return v.reshape(1, -1).astype(F32)


def _recurrence_tiling(seq):
    if seq >= GDN_CHUNK:
        return dict(s5_tc=128, gdn_t=GDN_CHUNK, gdn_nb=4, rwkv_t=RWKV_CHUNK, rwkv_nb=2,
                    rwkv_group=2 * RWKV_PAIR_GROUP)
    return dict(s5_tc=seq, gdn_t=seq, gdn_nb=8, rwkv_t=seq, rwkv_nb=4, rwkv_group=2 * RWKV_PAIR_GROUP)


def _trunk(x_bm, bsz, seq, s5_re0, s5_im0, gdn_s0, gdn_conv0, rw_s0, rw_shift0, p, out_bm):
    bv, lv, _ = x_bm.shape
    cfg = _recurrence_tiling(seq)
    u, qkv, z, ba = _ab_in(x_bm, p["norm_mix0"], p["ab_w_in"])
    y_a, hr, hi = _s5(u.reshape(seq, bsz, D_A), s5_re0, s5_im0, p["s5"], cfg["s5_tc"])
    y_b, gdn_s, gdn_conv = _gdn(qkv.reshape(seq, bsz * D_QKV), ba.reshape(seq, bsz * GATE_PAD),
                                z.reshape(seq, bsz * D_B), gdn_conv0, gdn_s0, p["gdn"], bsz,
                                cfg["gdn_t"], cfg["gdn_nb"])
    x1 = _mix_ffn(x_bm, True, [y_a.reshape(lv, bv * D_A), y_b.reshape(lv, bv * D_B)],
                  [p["ab_w_out_a"], p["ab_w_out_b"]], p["norm_ffn0"], p["wg0"], p["wu0"], p["wd0"],
                  None, bv, lv, False)
    acts = _rwkv_proj(x1, rw_shift0.reshape(bv, bsz // bv, D_MODEL), p["rw"])
    shift = acts[N_RWKV_ACTS].reshape(bsz, D_MODEL)
    yg, rw_s = _rwkv_scan([a.reshape(seq, bsz * D_MODEL) for a in acts[:N_RWKV_ACTS]], rw_s0, p["rw"], bsz,
                          cfg["rwkv_t"], cfg["rwkv_nb"], cfg["rwkv_group"])
    y = _mix_ffn(x1, False, [yg.reshape(lv, bv * D_MODEL)], [p["rw"]["wo"]], p["norm_ffn1"], p["wg1"],
                 p["wu1"], p["wd1"], p["norm_final"], bv, lv, out_bm)
    return y, hr, hi, gdn_s, gdn_conv, rw_s, shift


def kernel(x_prompt, x_sample, state_s5_re, state_s5_im, state_gdn, state_gdn_conv, state_rwkv, state_rwkv_shift, norm_mix, norm_ffn, norm_final, ffn_w_gate, ffn_w_up, ffn_w_down, ab_w_in, ab_w_out, s5_lambda_re, s5_lambda_im, s5_log_step, s5_B_re, s5_B_im, s5_C_re, s5_C_im, s5_D, s5_w_glu, gdn_conv_w, gdn_A_log, gdn_dt_bias, gdn_norm_w, rw_maa, rw_w_r, rw_w_k, rw_w_v, rw_w_o, rw_w0, rw_w1, rw_w2, rw_a0, rw_a1, rw_a2, rw_g1, rw_g2, rw_k_k, rw_k_a, rw_r_k, rw_ln_w, rw_ln_b):
    bsz_p, seq_p, _ = x_prompt.shape
    bsz_s, seq_s, _ = x_sample.shape

    w_in = ab_w_in[0]
    n_main = D_A + D_QKV
    w_in = jnp.concatenate([w_in[:, :n_main], w_in[:, n_main + 2 * H_B:], w_in[:, n_main:n_main + 2 * H_B]],
                           axis=1)
    w_in = _pad_to(w_in, 1, D_IN_PAD).astype(BF16)

    ff = lambda w, layer: w[layer].astype(BF16)

    gate_lanes = lambda v: _pad_to(jnp.concatenate([jnp.zeros((H_B,), F32), v.astype(F32)]), 0,
                                   GATE_PAD).reshape(1, GATE_PAD)
    p = dict(
        norm_mix0=_row(norm_mix[0]), norm_ffn0=_row(norm_ffn[0]), norm_ffn1=_row(norm_ffn[1]),
        norm_final=_row(norm_final), ab_w_in=w_in,
        ab_w_out_a=ab_w_out[0][:D_A].astype(BF16), ab_w_out_b=ab_w_out[0][D_A:].astype(BF16),
        wg0=ff(ffn_w_gate, 0), wu0=ff(ffn_w_up, 0),
        wd0=ff(ffn_w_down, 0),
        wg1=ff(ffn_w_gate, 1), wu1=ff(ffn_w_up, 1),
        wd1=ff(ffn_w_down, 1),
        s5=dict(
            lr=_row(s5_lambda_re[0]), li=_row(s5_lambda_im[0]),
            ls=_row(jnp.repeat(s5_log_step[0], P_STATE)),
            bre=_block_diag_groups(jnp.swapaxes(s5_B_re[0], 1, 2), S5_GROUP, P_STATE).astype(F32),
            bim=_block_diag_groups(jnp.swapaxes(s5_B_im[0], 1, 2), S5_GROUP, P_STATE).astype(F32),
            cre=_block_diag_groups(jnp.swapaxes(s5_C_re[0], 1, 2), P_STATE, S5_GROUP).astype(BF16),
            cim=_block_diag_groups(jnp.swapaxes(s5_C_im[0], 1, 2), P_STATE, S5_GROUP).astype(BF16),
            d=_row(s5_D[0]), wglu=s5_w_glu[0].astype(BF16)),
        gdn=dict(conv_w=gdn_conv_w[0].astype(F32), alog=gate_lanes(gdn_A_log[0]),
                 dtb=gate_lanes(gdn_dt_bias[0]), norm_w=_row(gdn_norm_w[0])),
        rw=dict(
            norm_w=_row(norm_mix[1]), maa=rw_maa[0].astype(F32),
            wr=rw_w_r[0].astype(BF16), wk=rw_w_k[0].astype(BF16), wv=rw_w_v[0].astype(BF16),
            wo=rw_w_o[0].astype(BF16),
            w0=_row(rw_w0[0]), w1=_pad_to(rw_w1[0], 1, LANES).astype(BF16),
            w2=_pad_to(rw_w2[0], 0, LANES).astype(BF16),
            a0=_row(rw_a0[0]), a1=_pad_to(rw_a1[0], 1, LANES).astype(BF16),
            a2=_pad_to(rw_a2[0], 0, LANES).astype(BF16),
            g1=_pad_to(rw_g1[0], 1, 2 * LANES).astype(BF16), g2=_pad_to(rw_g2[0], 0, 2 * LANES).astype(BF16),
            k_k=_row(rw_k_k[0]), k_a=_row(rw_k_a[0]), r_k=_row(rw_r_k[0]),
            ln_w=_row(rw_ln_w[0]), ln_b=_row(rw_ln_b[0])),
    )

    zeros = lambda *shape: jnp.zeros(shape, F32)
    yp, p_hr, p_hi, p_gdn, p_conv, p_rw, p_shift = _trunk(
        x_prompt, bsz_p, seq_p, zeros(bsz_p, N_S5), zeros(bsz_p, N_S5), zeros(bsz_p, H_B, DK_B, DV_B),
        zeros(bsz_p, CONV_W - 1, D_QKV), zeros(bsz_p, H_C, N_HEAD_C, N_HEAD_C), zeros(bsz_p, D_MODEL), p,
        True)

    xs_tm = jnp.transpose(x_sample, (1, 0, 2)).reshape(1, seq_s * bsz_s, D_MODEL)
    ys, s_hr, s_hi, s_gdn, s_conv, s_rw, s_shift = _trunk(
        xs_tm, bsz_s, seq_s, state_s5_re[0].reshape(bsz_s, N_S5), state_s5_im[0].reshape(bsz_s, N_S5),
        state_gdn[0], state_gdn_conv[0], state_rwkv[0], state_rwkv_shift[0], p, True)
    y_sample = jnp.transpose(ys.reshape(seq_s, bsz_s, D_MODEL), (1, 0, 2))

    s5_shape = lambda b: (1, b, G_A, P_STATE)
    return (yp, y_sample,
            p_hr.reshape(s5_shape(bsz_p)), p_hi.reshape(s5_shape(bsz_p)), p_gdn[None], p_conv[None],
            p_rw[None], p_shift[None],
            s_hr.reshape(s5_shape(bsz_s)), s_hi.reshape(s5_shape(bsz_s)), s_gdn[None], s_conv[None],
            s_rw[None], s_shift[None])
```

```python
import functools
import math

import jax
import jax.numpy as jnp
from jax import lax
from jax.experimental import pallas as pl
from jax.experimental.pallas import tpu as pltpu

F32 = jnp.float32
BF16 = jnp.bfloat16

D_MODEL = 1024
D_A = 512
S5_GROUP = 16
G_A = 32
P_STATE = 64
N_S5 = G_A * P_STATE
D_B = 512
H_B = 4
DK_B = 128
DV_B = 128
D_QKV = 2 * H_B * DK_B + D_B
CONV_W = 4
GDN_CHUNK = 64
N_HEAD_C = 64
H_C = 16
RWKV_CHUNK = 64
RWKV_PAIR_GROUP = 8
RWKV_STAGGER = 7
RWKV_GN_EPS = 64e-5
N_RWKV_ACTS = 6
D_FF = 2816
NORM_EPS = 1e-6

LANES = 128
SUBLANES = 8
FF_TILE = 256
N_FF_TILES = D_FF // FF_TILE
GATE_PAD = LANES
D_IN_PAD = 4 * 512 + 512 + GATE_PAD
S5_LANE_BLOCK = 512
N_S5_BLOCKS = N_S5 // S5_LANE_BLOCK
TOKEN_TILE = 512
FFN_TOKEN_TILE = 512
RWKV_PROJ_TILE = 512
MIB = 1024 * 1024


def _cparams(semantics, vmem_mib):
    return pltpu.CompilerParams(dimension_semantics=semantics, vmem_limit_bytes=vmem_mib * MIB)


def _rmsnorm(x, w):
    return x * lax.rsqrt(jnp.mean(x * x, axis=-1, keepdims=True) + NORM_EPS) * w


def _sigmoid(x):
    return 0.5 + 0.5 * jnp.tanh(0.5 * x)


def _silu(x):
    half = 0.5 * x
    return half + half * jnp.tanh(half)


def _softplus(x):
    return jnp.maximum(x, 0.0) + jnp.log1p(jnp.exp(-jnp.abs(x)))


def _bdot(a, b):
    return jnp.dot(a.astype(BF16), b.astype(BF16), preferred_element_type=F32)


def _bdot_nt(a, b):
    return lax.dot_general(a.astype(BF16), b.astype(BF16), (((1,), (1,)), ((), ())),
                           preferred_element_type=F32)


def _bdot_tn(a, b):
    return lax.dot_general(a.astype(BF16), b.astype(BF16), (((0,), (0,)), ((), ())),
                           preferred_element_type=F32)


def _split(a):
    hi = a.astype(BF16)
    return hi, (a - hi.astype(F32)).astype(BF16)


def _split3(a):
    p1 = a.astype(BF16)
    r1 = a - p1.astype(F32)
    p2 = r1.astype(BF16)
    return p1, p2, (r1 - p2.astype(F32)).astype(BF16)


def _split_dot(a, exact_bf16):
    hi, lo = _split(a)
    return (jnp.dot(hi, exact_bf16, preferred_element_type=F32)
            + jnp.dot(lo, exact_bf16, preferred_element_type=F32))


def _cumsum_rows(tri_bf16, w, parts=3):
    split = _split3(w) if parts == 3 else _split(w)
    return sum(jnp.dot(tri_bf16, part, preferred_element_type=F32) for part in split)


def _iota2(shape):
    return (lax.broadcasted_iota(jnp.int32, shape, 0), lax.broadcasted_iota(jnp.int32, shape, 1))


def _run_staggered(gens, lag):
    pending, active, tick = list(gens), [], 0
    while pending or active:
        if pending and tick % lag == 0:
            active.append(pending.pop(0))
        for g in list(active):
            try:
                next(g)
            except StopIteration:
                active.remove(g)
        tick += 1


def _inv_identity_plus_wide(lmws, t):
    gen = _inv_identity_plus_wide_stages(lmws, t)
    while True:
        try:
            next(gen)
        except StopIteration as done:
            return done.value


def _inv_identity_plus_wide_stages(lmws, t):
    row, col = _iota2((t, 2 * t))
    left = col < t
    col = jnp.where(left, col, col - t)
    eye = (row == col).astype(F32)
    base = min(t, 16)

    def blockdiag(xw):
        return jnp.concatenate([jnp.where(left, xw, 0.0), jnp.where(left, 0.0, xw)], axis=0)

    if t > base:
        same_base = (row >> 4) == (col >> 4)
        ns = [jnp.where(same_base, -lmw, 0.0) for lmw in lmws]
    else:
        ns = [-lmw for lmw in lmws]
    xs = [eye + n for n in ns]
    ps = [_bdot(n, blockdiag(n)) for n in ns]
    yield
    k = 2
    while k < base:
        bds = [blockdiag(p) for p in ps]
        if 2 * k < base:
            both = [_bdot(jnp.concatenate([p, x], axis=0), bd) for p, x, bd in zip(ps, xs, bds)]
            ps = [r[:t] for r in both]
            xs = [x + r[t:] for x, r in zip(xs, both)]
        else:
            xs = [x + _bdot(x, bd) for x, bd in zip(xs, bds)]
        yield
        k *= 2
    shift = 4
    blk = base
    while blk < t:
        same_big = (row >> (shift + 1)) == (col >> (shift + 1))
        same_small = (row >> shift) == (col >> shift)
        off = [jnp.where(same_big, jnp.where(same_small, 0.0, lmw), 0.0) for lmw in lmws]
        mids = [_bdot(o, blockdiag(x)) for o, x in zip(off, xs)]
        yield
        xs = [x - _bdot(x, blockdiag(m)) for x, m in zip(xs, mids)]
        yield
        blk *= 2
        shift += 1
    return xs


def _ab_in_kernel(x_ref, nw_ref, w_ref, u_ref, qkv_ref, z_ref, ba_ref):
    h = _rmsnorm(x_ref[...], nw_ref[...])
    p = jnp.dot(h.astype(BF16), w_ref[...], preferred_element_type=F32)
    u_ref[...] = p[:, 0:D_A]
    qkv_ref[...] = p[:, D_A:D_A + D_QKV]
    z_ref[...] = p[:, D_A + D_QKV:D_A + D_QKV + D_B]
    ba_ref[...] = p[:, D_A + D_QKV + D_B:D_IN_PAD]


def _ab_in(x_bm, norm_w, w_in):
    bv, lv, _ = x_bm.shape
    tl = min(TOKEN_TILE, lv)
    grid = (bv, lv // tl)
    tm = lambda n: pl.BlockSpec((tl, n), lambda b, i: (i, b))
    const = lambda shape: pl.BlockSpec(shape, lambda b, i: (0,) * len(shape))
    return pl.pallas_call(
        _ab_in_kernel,
        grid=grid,
        in_specs=[pl.BlockSpec((None, tl, D_MODEL), lambda b, i: (b, i, 0)),
                  const((1, D_MODEL)), const((D_MODEL, D_IN_PAD))],
        out_specs=[tm(D_A), tm(D_QKV), tm(D_B), tm(GATE_PAD)],
        out_shape=[jax.ShapeDtypeStruct((lv, bv * D_A), F32),
                   jax.ShapeDtypeStruct((lv, bv * D_QKV), F32),
                   jax.ShapeDtypeStruct((lv, bv * D_B), F32),
                   jax.ShapeDtypeStruct((lv, bv * GATE_PAD), F32)],
        compiler_params=_cparams(("parallel", "parallel"), 48),
    )(x_bm, norm_w, w_in)


def _s5_kernel(u_ref, h0r_ref, h0i_ref, lr_ref, li_ref, ls_ref, bre_ref, bim_ref, cre_ref, cim_ref,
               d_ref, wglu_ref, y_ref, hr_out, hi_out,
               ar_s, ai_s, bbr_s, bbi_s, hr_s, hi_s, bur_s, bui_s, yg_s, *, tc, seq_on_lanes):
    rows = tc * SUBLANES

    @pl.when(pl.program_id(1) == 0)
    def _():
        lr = lr_ref[...]
        li = li_ref[...]
        dt = jnp.exp(ls_ref[...])
        mag = jnp.exp(lr * dt)
        ar = mag * jnp.cos(li * dt)
        ai = mag * jnp.sin(li * dt)
        den = lr * lr + li * li
        nr = ar - 1.0
        cr = (nr * lr + ai * li) / den
        ci = (ai * lr - nr * li) / den
        ar_s[...] = jnp.broadcast_to(ar, (SUBLANES, N_S5))
        ai_s[...] = jnp.broadcast_to(ai, (SUBLANES, N_S5))
        for j in range(N_S5_BLOCKS):
            sl = slice(j * S5_LANE_BLOCK, (j + 1) * S5_LANE_BLOCK)
            bbr_s[j] = (cr[:, sl] * bre_ref[j] - ci[:, sl] * bim_ref[j]).astype(BF16)
            bbi_s[j] = (cr[:, sl] * bim_ref[j] + ci[:, sl] * bre_ref[j]).astype(BF16)
        hr_s[...] = h0r_ref[...]
        hi_s[...] = h0i_ref[...]

    if seq_on_lanes:
        per_seq = jnp.stack([u_ref[:, b * D_A:(b + 1) * D_A] for b in range(SUBLANES)], axis=0)
        u = jnp.swapaxes(per_seq, 0, 1).reshape(rows, D_A)
    else:
        u = u_ref[...].reshape(rows, D_A)
    ub = u.astype(BF16)
    for j in range(N_S5_BLOCKS):
        sl = slice(j * S5_LANE_BLOCK, (j + 1) * S5_LANE_BLOCK)
        uj = ub[:, j * LANES:(j + 1) * LANES]
        bur_s[:, :, sl] = jnp.dot(uj, bbr_s[j], preferred_element_type=F32).reshape(tc, SUBLANES, S5_LANE_BLOCK)
        bui_s[:, :, sl] = jnp.dot(uj, bbi_s[j], preferred_element_type=F32).reshape(tc, SUBLANES, S5_LANE_BLOCK)

    for j in range(N_S5_BLOCKS):
        sl = slice(j * S5_LANE_BLOCK, (j + 1) * S5_LANE_BLOCK)
        ar = ar_s[:, sl]
        ai = ai_s[:, sl]

        def step(t, carry, sl=sl, ar=ar, ai=ai):
            hr, hi = carry
            nr = ar * hr - ai * hi + bur_s[t, :, sl]
            ni = ar * hi + ai * hr + bui_s[t, :, sl]
            bur_s[t, :, sl] = nr
            bui_s[t, :, sl] = ni
            return nr, ni

        hr, hi = lax.fori_loop(0, tc, step, (hr_s[:, sl], hi_s[:, sl]), unroll=min(tc, 8))
        hr_s[:, sl] = hr
        hi_s[:, sl] = hi

    for j in range(N_S5_BLOCKS):
        sl = slice(j * S5_LANE_BLOCK, (j + 1) * S5_LANE_BLOCK)
        cl = slice(j * LANES, (j + 1) * LANES)
        xr = bur_s[:, :, sl].reshape(rows, S5_LANE_BLOCK).astype(BF16)
        xi = bui_s[:, :, sl].reshape(rows, S5_LANE_BLOCK).astype(BF16)
        yj = (jnp.dot(xr, cre_ref[j], preferred_element_type=F32)
              - jnp.dot(xi, cim_ref[j], preferred_element_type=F32)
              + d_ref[:, cl] * u[:, cl])
        yg_s[:, cl] = jax.nn.gelu(yj)

    yg = yg_s[...]
    out = yg * _sigmoid(jnp.dot(yg.astype(BF16), wglu_ref[...], preferred_element_type=F32))
    if seq_on_lanes:
        per_seq = jnp.swapaxes(out.reshape(tc, SUBLANES, D_A), 0, 1)
        for b in range(SUBLANES):
            y_ref[:, b * D_A:(b + 1) * D_A] = per_seq[b]
    else:
        y_ref[...] = out.reshape(tc, SUBLANES, D_A)
    hr_out[...] = hr_s[...]
    hi_out[...] = hi_s[...]


def _s5(u_tm, h0r, h0i, prm, tc):
    seq_on_lanes = u_tm.ndim == 2
    seq, bsz = u_tm.shape[0], h0r.shape[0]
    grid = (bsz // SUBLANES, seq // tc)
    const = lambda shape: pl.BlockSpec(shape, lambda b, i: (0,) * len(shape))
    if seq_on_lanes:
        act = pl.BlockSpec((tc, SUBLANES * D_A), lambda b, i: (i, b))
    else:
        act = pl.BlockSpec((tc, SUBLANES, D_A), lambda b, i: (i, b, 0))
    st = pl.BlockSpec((SUBLANES, N_S5), lambda b, i: (b, 0))
    vec = const((1, N_S5))
    small = pltpu.VMEM((SUBLANES, N_S5), F32)
    big = pltpu.VMEM((tc, SUBLANES, N_S5), F32)
    return pl.pallas_call(
        functools.partial(_s5_kernel, tc=tc, seq_on_lanes=seq_on_lanes),
        grid=grid,
        in_specs=[act, st, st, vec, vec, vec,
                  const((N_S5_BLOCKS, LANES, S5_LANE_BLOCK)), const((N_S5_BLOCKS, LANES, S5_LANE_BLOCK)),
                  const((N_S5_BLOCKS, S5_LANE_BLOCK, LANES)), const((N_S5_BLOCKS, S5_LANE_BLOCK, LANES)),
                  const((1, D_A)), const((D_A, D_A))],
        out_specs=[act, st, st],
        out_shape=[jax.ShapeDtypeStruct(u_tm.shape, F32),
                   jax.ShapeDtypeStruct((bsz, N_S5), F32),
                   jax.ShapeDtypeStruct((bsz, N_S5), F32)],
        scratch_shapes=[small, small, pltpu.VMEM((N_S5_BLOCKS, LANES, S5_LANE_BLOCK), BF16),
                        pltpu.VMEM((N_S5_BLOCKS, LANES, S5_LANE_BLOCK), BF16), small, small, big, big,
                        pltpu.VMEM((tc * SUBLANES, D_A), F32)],
        compiler_params=_cparams(("parallel", "arbitrary"), 48),
    )(u_tm, h0r, h0i, prm["lr"], prm["li"], prm["ls"], prm["bre"], prm["bim"], prm["cre"], prm["cim"],
      prm["d"], prm["wglu"])


def _gdn_kernel(qkv_ref, ba_ref, z_ref, conv0_ref, s0_ref, cw_ref, alog_ref, dtb_ref, nw_ref,
                o_ref, sfin_ref, convn_ref, xbuf, s_s, *, t, nb):
    pad = SUBLANES
    hist = CONV_W - 1

    @pl.when(pl.program_id(1) == 0)
    def _():
        for e in range(nb):
            xbuf[pad - hist:pad, e * D_QKV:(e + 1) * D_QKV] = conv0_ref[e]
        s_s[...] = s0_ref[...].reshape(nb * H_B, DK_B, DV_B)

    row, col = _iota2((t, t))
    causal = row >= col
    strict = row > col
    tri = causal.astype(BF16)
    lane = lax.broadcasted_iota(jnp.int32, (t, LANES), 1)
    nw = nw_ref[...]
    cw = cw_ref[...]
    ones_sq = jnp.ones((DK_B, DK_B), BF16)
    zeros_wide = jnp.zeros((t, DK_B + DV_B), F32)
    zeros_v = jnp.zeros((t, DV_B), F32)

    def seq_group(es):
        ys, betas, gcs = [], [], []
        for e in es:
            cols = slice(e * D_QKV, (e + 1) * D_QKV)
            x = qkv_ref[:, cols]
            xbuf[pad:pad + t, cols] = x
            acc = x * cw[hist:hist + 1, :]
            for j in range(hist):
                acc = acc + xbuf[pad - hist + j:pad - hist + j + t, cols] * cw[j:j + 1, :]
            last = xbuf[pad + t - hist:pad + t, cols]
            convn_ref[e] = last
            xbuf[pad - hist:pad, cols] = last
            ys.append(_silu(acc))
            ba = ba_ref[:, e * GATE_PAD:(e + 1) * GATE_PAD]
            betas.append(_sigmoid(ba))
            g = -jnp.exp(alog_ref[...]) * _softplus(ba + dtb_ref[...])
            gcs.append(_cumsum_rows(tri, g))
            yield

        chains = [(i, h) for i in range(len(es)) for h in range(H_B)]
        heads = range(len(chains))
        bcol = [betas[i][:, h:h + 1] for i, h in chains]
        gcol = [gcs[i][:, H_B + h:H_B + h + 1] for i, h in chains]
        y_of = [ys[i] for i, h in chains]
        h_of = [h for i, h in chains]
        decay = []
        gc_parts = [[p.astype(F32) for p in _split3(gc)] for gc in gcs]
        for i, hh in chains:
            p1, p2, p3 = [p[:, H_B + hh:H_B + hh + 1] for p in gc_parts[i]]
            dl = jnp.where(lane == 0, p1, jnp.where(lane == 1, p2, jnp.where(lane == 2, p3,
                                                                             jnp.where(lane < 6, 1.0, 0.0))))
            dr = jnp.where(lane < 3, 1.0, jnp.where(lane == 3, -p1, jnp.where(lane == 4, -p2,
                                                                            jnp.where(lane == 5, -p3, 0.0))))
            decay.append(jnp.exp(jnp.where(causal, _bdot_nt(dl, dr), -jnp.inf)))
        eg = [jnp.exp(gcol[h]) for h in heads]
        glast = [gcol[h][t - 1:t, :] for h in heads]
        yield
        q = [y_of[c][:, h_of[c] * DK_B:(h_of[c] + 1) * DK_B] for c in heads]
        k = [y_of[c][:, (H_B + h_of[c]) * DK_B:(H_B + h_of[c] + 1) * DK_B] for c in heads]
        v = [y_of[c][:, 2 * H_B * DK_B + h_of[c] * DV_B:2 * H_B * DK_B + (h_of[c] + 1) * DV_B] for c in heads]
        q = [x * (lax.rsqrt(_bdot(x * x, ones_sq) + NORM_EPS) * (DK_B ** -0.5)) for x in q]
        k = [x * lax.rsqrt(_bdot(x * x, ones_sq) + NORM_EPS) for x in k]
        kb = [k[h] * bcol[h] for h in heads]
        vb = [v[h] * bcol[h] for h in heads]
        yield
        lm = [jnp.where(strict, _bdot_nt(kb[h], k[h]) * decay[h], 0.0) for h in heads]
        attn = [_bdot_nt(q[h], k[h]) * decay[h] for h in heads]
        s = [s_s[es[i] * H_B + h] for i, h in chains]
        qs = [_bdot(q[h] * eg[h], s[h]) for h in heads]
        yield
        pairs = [(c, c + 1) for c in range(0, len(chains), 2)]
        tinv = yield from _inv_identity_plus_wide_stages(
            [jnp.concatenate([lm[c0], lm[c1]], axis=1) for c0, c1 in pairs], t)
        kbg = [kb[h] * eg[h] for h in heads]
        uw = [_bdot(tinv[i], jnp.concatenate(
            [jnp.concatenate([vb[c0], kbg[c0], zeros_wide], axis=1),
             jnp.concatenate([zeros_wide, vb[c1], kbg[c1]], axis=1)], axis=0)) for i, (c0, c1) in enumerate(pairs)]
        u = [uw[c // 2][:, (c % 2) * (DK_B + DV_B):(c % 2) * (DK_B + DV_B) + DV_B] for c in heads]
        w = [uw[c // 2][:, (c % 2) * (DK_B + DV_B) + DV_B:(c % 2 + 1) * (DK_B + DV_B)] for c in heads]
        yield
        v_new = [u[h] - _bdot(w[h], s[h]) for h in heads]
        yield
        av = [_bdot(jnp.concatenate([attn[c0], attn[c1]], axis=1), jnp.concatenate(
            [jnp.concatenate([v_new[c0], zeros_v], axis=1),
             jnp.concatenate([zeros_v, v_new[c1]], axis=1)], axis=0)) for c0, c1 in pairs]
        o = [qs[c] + av[c // 2][:, (c % 2) * DV_B:(c % 2 + 1) * DV_B] for c in heads]
        upd = [_bdot_tn(k[h] * jnp.exp(glast[h] - gcol[h]), v_new[h]) for h in heads]
        yield
        for c, (i, h) in enumerate(chains):
            e = es[i]
            s_new = s[c] * jnp.exp(glast[c]) + upd[c]
            s_s[e * H_B + h] = s_new
            sfin_ref[e, h] = s_new
            cols = slice(e * D_B + h * DV_B, e * D_B + (h + 1) * DV_B)
            o_ref[:, cols] = _rmsnorm(o[c], nw) * _silu(z_ref[:, cols])

    _run_staggered([seq_group(list(range(nb)))], 1)


def _gdn(qkv_tm, ba_tm, z_tm, conv0, s0, prm, bsz, t, nb):
    seq = qkv_tm.shape[0]
    grid = (bsz // nb, seq // t)
    const = lambda shape: pl.BlockSpec(shape, lambda b, i: (0,) * len(shape))
    tm = lambda n: pl.BlockSpec((t, nb * n), lambda b, i: (i, b))
    state = pl.BlockSpec((nb, H_B, DK_B, DV_B), lambda b, i: (b, 0, 0, 0))
    conv = pl.BlockSpec((nb, CONV_W - 1, D_QKV), lambda b, i: (b, 0, 0))
    return pl.pallas_call(
        functools.partial(_gdn_kernel, t=t, nb=nb),
        grid=grid,
        in_specs=[tm(D_QKV), tm(GATE_PAD), tm(D_B), conv, state,
                  const((CONV_W, D_QKV)), const((1, GATE_PAD)), const((1, GATE_PAD)), const((1, DV_B))],
        out_specs=[tm(D_B), state, conv],
        out_shape=[jax.ShapeDtypeStruct((seq, bsz * D_B), F32),
                   jax.ShapeDtypeStruct((bsz, H_B, DK_B, DV_B), F32),
                   jax.ShapeDtypeStruct((bsz, CONV_W - 1, D_QKV), F32)],
        scratch_shapes=[pltpu.VMEM((t + SUBLANES, nb * D_QKV), F32),
                        pltpu.VMEM((nb * H_B, DK_B, DV_B), F32)],
        compiler_params=_cparams(("parallel", "arbitrary"), 48),
    )(qkv_tm, ba_tm, z_tm, conv0, s0, prm["conv_w"], prm["alog"], prm["dtb"], prm["norm_w"])


def _mix_ffn_kernel(*refs, n_mix, final_norm):
    x_ref = refs[0]
    mix_refs = refs[1:1 + n_mix]
    wout_refs = refs[1 + n_mix:1 + 2 * n_mix]
    nffn_ref, wg_ref, wu_ref, wd_ref = refs[1 + 2 * n_mix:5 + 2 * n_mix]
    rest = refs[5 + 2 * n_mix:]
    if final_norm:
        nfin_ref, out_ref = rest
    else:
        (out_ref,) = rest

    x1 = x_ref[...]
    for m_ref, w_ref in zip(mix_refs, wout_refs):
        x1 = x1 + jnp.dot(m_ref[...].astype(BF16), w_ref[...], preferred_element_type=F32)
    h = _rmsnorm(x1, nffn_ref[...]).astype(BF16)
    out_ref[...] = x1

    for c in range(N_FF_TILES):
        cols = slice(c * FF_TILE, (c + 1) * FF_TILE)
        gate = jnp.dot(h, wg_ref[:, cols], preferred_element_type=F32)
        up = jnp.dot(h, wu_ref[:, cols], preferred_element_type=F32)
        act = (_silu(gate) * up).astype(BF16)
        out_ref[...] += jnp.dot(act, wd_ref[cols, :], preferred_element_type=F32)
    if final_norm:
        out_ref[...] = _rmsnorm(out_ref[...], nfin_ref[...])


def _mix_ffn(x, x_bm, mixes, wouts, nffn, wg, wu, wd, nfin, bv, lv, out_bm):
    tl = min(FFN_TOKEN_TILE, lv)
    grid = (bv, lv // tl)
    const = lambda shape: pl.BlockSpec(shape, lambda b, i: (0,) * len(shape),
                                       pipeline_mode=pl.Buffered(1))
    tm = lambda n: pl.BlockSpec((tl, n), lambda b, i: (i, b))
    bm = pl.BlockSpec((None, tl, D_MODEL), lambda b, i: (b, i, 0))
    in_specs = [bm if x_bm else tm(D_MODEL)]
    in_specs += [tm(m.shape[1] // bv) for m in mixes]
    in_specs += [const(w.shape) for w in wouts]
    in_specs += [const((1, D_MODEL)), const(wg.shape), const(wu.shape), const(wd.shape)]
    args = [x, *mixes, *wouts, nffn, wg, wu, wd]
    if nfin is not None:
        in_specs.append(const((1, D_MODEL)))
        args.append(nfin)
    if out_bm:
        out_spec, out_shape = bm, jax.ShapeDtypeStruct((bv, lv, D_MODEL), F32)
    else:
        out_spec, out_shape = tm(D_MODEL), jax.ShapeDtypeStruct((lv, bv * D_MODEL), F32)
    return pl.pallas_call(
        functools.partial(_mix_ffn_kernel, n_mix=len(mixes), final_norm=nfin is not None),
        grid=grid,
        in_specs=in_specs,
        out_specs=out_spec,
        out_shape=out_shape,
        compiler_params=_cparams(("parallel", "parallel"), 56),
    )(*args)


def _rwkv_proj_kernel(x_ref, shift0_ref, nw_ref, maa_ref, wr_ref, wk_ref, wv_ref, w0_ref, w1_ref, w2_ref,
                      a0_ref, a1_ref, a2_ref, g1_ref, g2_ref,
                      r_out, w_out, k_out, v_out, a_out, g_out, shift_out, hbuf, *, shift):
    pad = -(-shift // SUBLANES) * SUBLANES

    @pl.when(pl.program_id(1) == 0)
    def _():
        hbuf[pad - shift:pad, :] = shift0_ref[...]

    h = _rmsnorm(x_ref[...], nw_ref[...])
    rows = h.shape[0]
    hbuf[pad:pad + rows, :] = h
    prev = hbuf[pad - shift:pad - shift + rows, :]
    last = h[rows - shift:]
    hbuf[pad - shift:pad, :] = last
    shift_out[...] = last
    xx = prev - h
    maa = maa_ref[...]
    xr, xw, xk, xv, xa, xg = [h + xx * maa[j:j + 1, :] for j in range(6)]
    r = _bdot(xr, wr_ref[...])
    k = _bdot(xk, wk_ref[...])
    v = _bdot(xv, wv_ref[...])
    u = w0_ref[...] + _bdot(jnp.tanh(_bdot(xw, w1_ref[...])), w2_ref[...])
    a = _sigmoid(a0_ref[...] + _bdot(_bdot(xa, a1_ref[...]), a2_ref[...]))
    g = _bdot(_sigmoid(_bdot(xg, g1_ref[...])), g2_ref[...])
    r_out[...] = r
    w_out[...] = -math.exp(-0.5) * _sigmoid(u)
    k_out[...] = k
    v_out[...] = v
    a_out[...] = a
    g_out[...] = g


def _rwkv_proj(x_tm, shift0, prm):
    bv, shift, _ = shift0.shape
    lv = x_tm.shape[0]
    tr = max(shift, min(RWKV_PROJ_TILE, lv))
    grid = (bv, lv // tr)
    const = lambda shape: pl.BlockSpec(shape, lambda b, i: (0,) * len(shape),
                                       pipeline_mode=pl.Buffered(1))
    act = pl.BlockSpec((tr, D_MODEL), lambda b, i: (i, b))
    carry = pl.BlockSpec((None, shift, D_MODEL), lambda b, i: (b, 0, 0))
    vec = const((1, D_MODEL))
    weights = [prm[n] for n in ("wr", "wk", "wv")]
    in_specs = [act, carry, vec, const((6, D_MODEL))]
    in_specs += [const(w.shape) for w in weights]
    in_specs += [vec, const(prm["w1"].shape), const(prm["w2"].shape),
                 vec, const(prm["a1"].shape), const(prm["a2"].shape),
                 const(prm["g1"].shape), const(prm["g2"].shape)]
    out_act = jax.ShapeDtypeStruct((lv, bv * D_MODEL), F32)
    pad = -(-shift // SUBLANES) * SUBLANES
    return pl.pallas_call(
        functools.partial(_rwkv_proj_kernel, shift=shift),
        grid=grid,
        in_specs=in_specs,
        out_specs=[act] * N_RWKV_ACTS + [carry],
        out_shape=[out_act] * N_RWKV_ACTS + [jax.ShapeDtypeStruct((bv, shift, D_MODEL), F32)],
        scratch_shapes=[pltpu.VMEM((pad + tr, D_MODEL), F32)],
        compiler_params=_cparams(("parallel", "arbitrary"), 56),
    )(x_tm, shift0, prm["norm_w"], prm["maa"], prm["wr"], prm["wk"], prm["wv"], prm["w0"], prm["w1"],
      prm["w2"], prm["a0"], prm["a1"], prm["a2"], prm["g1"], prm["g2"])


def _rwkv_scan_kernel(r_ref, w_ref, k_ref, v_ref, a_ref, g_ref, s0_ref, kkw_ref, kaw_ref, rk_ref, lnw_ref, lnb_ref,
                      y_ref, sfin_ref, s_s, *, t, nb, group):
    n = N_HEAD_C
    zero_blk = jnp.zeros((n, n), F32)

    pairs_per_seq = H_C // 2

    @pl.when(pl.program_id(1) == 0)
    def _():
        for jj in range(nb * pairs_per_seq):
            e, j = divmod(jj, pairs_per_seq)
            top = jnp.concatenate([s0_ref[e, 2 * j], zero_blk], axis=1)
            bot = jnp.concatenate([zero_blk, s0_ref[e, 2 * j + 1]], axis=1)
            s_s[jj] = jnp.concatenate([top, bot], axis=0)

    row, col = _iota2((t, t))
    tri = (row >= col).astype(BF16)
    roww, colw = _iota2((t, 2 * t))
    colw = jnp.where(colw >= t, colw - t, colw)
    incl_w = roww >= colw
    strict_w = roww > colw
    lane = lax.broadcasted_iota(jnp.int32, (t, LANES), 1)
    head0 = lane < n

    def by_head(x):
        return jnp.concatenate([jnp.where(head0, x, 0.0), jnp.where(head0, 0.0, x)], axis=0)

    prow, pcol = _iota2((LANES, LANES))
    same_head = (prow >> 6) == (pcol >> 6)
    ones_blk = same_head.astype(BF16)

    def head_sums(xs, dot):
        tot = dot(jnp.concatenate(xs, axis=0), ones_blk)
        return [tot[i * t:(i + 1) * t] for i in range(len(xs))]

    def pair_group(js):
        ps = range(len(js))
        sl = [slice(j * LANES, (j + 1) * LANES) for j in js]
        psl = [slice((j % pairs_per_seq) * LANES, (j % pairs_per_seq + 1) * LANES) for j in js]
        r = [r_ref[:, x] for x in sl]
        w = [w_ref[:, x] for x in sl]
        kraw = [k_ref[:, x] for x in sl]
        v = [v_ref[:, x] for x in sl]
        a = [a_ref[:, x] for x in sl]
        kkr = [kraw[p] * kkw_ref[:, psl[p]] for p in ps]
        k = [kraw[p] * (1.0 + (a[p] - 1.0) * kaw_ref[:, psl[p]]) for p in ps]
        kk = [x * lax.rsqrt(ss + NORM_EPS) for x, ss in zip(kkr, head_sums([x * x for x in kkr], _bdot))]
        bv = [kk[p] * a[p] for p in ps]
        yield
        gc = [_cumsum_rows(tri, x, parts=2) for x in w]
        glast = [x[t - 1:t, :] for x in gc]
        yield
        pinv = [jnp.exp(-x) for x in gc]
        at = [-kk[p] * jnp.exp(gc[p] - w[p]) for p in ps]
        bt = [bv[p] * pinv[p] for p in ps]
        kt = [k[p] * pinv[p] for p in ps]
        rt = [r[p] * jnp.exp(gc[p]) for p in ps]
        rem = [jnp.exp(glast[p] - gc[p]) for p in ps]
        yield
        lhs = [jnp.concatenate([at[p], rt[p]], axis=0) for p in ps]
        gm = [_bdot_nt(lhs[p], jnp.concatenate([by_head(bt[p]), by_head(kt[p])], axis=0)) for p in ps]
        yield
        s = [s_s[j] for j in js]
        ars = [_bdot_nt(lhs[p], s[p]) for p in ps]
        v2 = [by_head(x) for x in v]
        yield
        aab = [jnp.where(strict_w, gm[p][:t, :2 * t], 0.0) for p in ps]
        aak = [jnp.where(strict_w, gm[p][:t, 2 * t:], 0.0) for p in ps]
        rkm = [jnp.where(incl_w, gm[p][t:, 2 * t:], 0.0) for p in ps]
        rb = [jnp.where(incl_w, gm[p][t:, :2 * t], 0.0) for p in ps]
        yield
        kv = [_bdot(jnp.concatenate([aak[p], rkm[p]], axis=0), v2[p]) for p in ps]
        rhs = [ars[p][:t] + kv[p][:t] for p in ps]
        yield
        tinv = yield from _inv_identity_plus_wide_stages([-x for x in aab], t)
        u = [_bdot(tinv[p], by_head(rhs[p])) for p in ps]
        yield
        y = [ars[p][t:] + _bdot(rb[p], by_head(u[p])) + kv[p][t:] for p in ps]
        yield
        upd = [_bdot_tn(jnp.concatenate([u[p], v[p]], axis=0),
                        jnp.concatenate([bv[p] * rem[p], k[p] * rem[p]], axis=0)) for p in ps]
        for p, j in enumerate(js):
            s_s[j] = s[p] * jnp.exp(glast[p]) + jnp.where(same_head, upd[p], 0.0)
        yield
        mu = head_sums(y, _split_dot)
        d = [y[p] - mu[p] * (1.0 / n) for p in ps]
        yield
        var = head_sums([x * x for x in d], _bdot)
        yield
        bonus = head_sums([r[p] * k[p] * rk_ref[:, psl[p]] for p in ps], _split_dot)
        yield
        for p in ps:
            yn = d[p] * lax.rsqrt(var[p] * (1.0 / n) + RWKV_GN_EPS) * lnw_ref[:, psl[p]] + lnb_ref[:, psl[p]]
            y_ref[:, sl[p]] = (yn + bonus[p] * v[p]) * g_ref[:, sl[p]]

    groups = [list(range(first, first + group)) for first in range(0, nb * pairs_per_seq, group)]
    _run_staggered([pair_group(js) for js in groups], RWKV_STAGGER)

    for jj in range(nb * pairs_per_seq):
        e, j = divmod(jj, pairs_per_seq)
        sp = s_s[jj]
        sfin_ref[e, 2 * j] = sp[:n, :n]
        sfin_ref[e, 2 * j + 1] = sp[n:, n:]


def _rwkv_scan(acts, s0, prm, bsz, t, nb, group):
    seq = acts[0].shape[0]
    grid = (bsz // nb, seq // t)
    const = lambda shape: pl.BlockSpec(shape, lambda b, i: (0,) * len(shape))
    tm = pl.BlockSpec((t, nb * D_MODEL), lambda b, i: (i, b))
    st = pl.BlockSpec((nb, H_C, N_HEAD_C, N_HEAD_C), lambda b, i: (b, 0, 0, 0))
    vec = const((1, D_MODEL))
    return pl.pallas_call(
        functools.partial(_rwkv_scan_kernel, t=t, nb=nb, group=group),
        grid=grid,
        in_specs=[tm] * N_RWKV_ACTS + [st, vec, vec, vec, vec, vec],
        out_specs=[tm, st],
        out_shape=[jax.ShapeDtypeStruct((seq, bsz * D_MODEL), F32),
                   jax.ShapeDtypeStruct((bsz, H_C, N_HEAD_C, N_HEAD_C), F32)],
        scratch_shapes=[pltpu.VMEM((nb * H_C // 2, LANES, LANES), F32)],
        compiler_params=_cparams(("parallel", "arbitrary"), 48),
    )(*acts, s0, prm["k_k"], prm["k_a"], prm["r_k"], prm["ln_w"], prm["ln_b"])


def _block_diag_groups(w, rows_per_group, cols_per_group):
    g = w.shape[0]
    w = w.reshape(g // 8, 8, rows_per_group, cols_per_group)
    eye = jnp.eye(8, dtype=w.dtype)
    out = jnp.einsum("jgrc,gh->jgrhc", w, eye)
    return out.reshape(g // 8, 8 * rows_per_group, 8 * cols_per_group)


def _pad_to(w, axis, size):
    pad = [(0, 0)] * w.ndim
    pad[axis] = (0, size - w.shape[axis])
    return jnp.pad(w, pad)


def _row(v):
    return v.reshape(1, -1).astype(F32)


def _recurrence_tiling(seq):
    if seq >= GDN_CHUNK:
        return dict(s5_tc=128, gdn_t=GDN_CHUNK, gdn_nb=4, rwkv_t=RWKV_CHUNK, rwkv_nb=2,
                    rwkv_group=2 * RWKV_PAIR_GROUP)
    return dict(s5_tc=seq, gdn_t=seq, gdn_nb=8, rwkv_t=seq, rwkv_nb=4, rwkv_group=2 * RWKV_PAIR_GROUP)


def _trunk(x_bm, bsz, seq, s5_re0, s5_im0, gdn_s0, gdn_conv0, rw_s0, rw_shift0, p, out_bm):
    bv, lv, _ = x_bm.shape
    cfg = _recurrence_tiling(seq)
    u, qkv, z, ba = _ab_in(x_bm, p["norm_mix0"], p["ab_w_in"])
    y_a, hr, hi = _s5(u if bv > 1 else u.reshape(seq, bsz, D_A), s5_re0, s5_im0, p["s5"], cfg["s5_tc"])
    y_b, gdn_s, gdn_conv = _gdn(qkv.reshape(seq, bsz * D_QKV), ba.reshape(seq, bsz * GATE_PAD),
                                z.reshape(seq, bsz * D_B), gdn_conv0, gdn_s0, p["gdn"], bsz,
                                cfg["gdn_t"], cfg["gdn_nb"])
    x1 = _mix_ffn(x_bm, True, [y_a.reshape(lv, bv * D_A), y_b.reshape(lv, bv * D_B)],
                  [p["ab_w_out_a"], p["ab_w_out_b"]], p["norm_ffn0"], p["wg0"], p["wu0"], p["wd0"],
                  None, bv, lv, False)
    acts = _rwkv_proj(x1, rw_shift0.reshape(bv, bsz // bv, D_MODEL), p["rw"])
    shift = acts[N_RWKV_ACTS].reshape(bsz, D_MODEL)
    yg, rw_s = _rwkv_scan([a.reshape(seq, bsz * D_MODEL) for a in acts[:N_RWKV_ACTS]], rw_s0, p["rw"], bsz,
                          cfg["rwkv_t"], cfg["rwkv_nb"], cfg["rwkv_group"])
    y = _mix_ffn(x1, False, [yg.reshape(lv, bv * D_MODEL)], [p["rw"]["wo"]], p["norm_ffn1"], p["wg1"],
                 p["wu1"], p["wd1"], p["norm_final"], bv, lv, out_bm)
    return y, hr, hi, gdn_s, gdn_conv, rw_s, shift


def kernel(x_prompt, x_sample, state_s5_re, state_s5_im, state_gdn, state_gdn_conv, state_rwkv, state_rwkv_shift, norm_mix, norm_ffn, norm_final, ffn_w_gate, ffn_w_up, ffn_w_down, ab_w_in, ab_w_out, s5_lambda_re, s5_lambda_im, s5_log_step, s5_B_re, s5_B_im, s5_C_re, s5_C_im, s5_D, s5_w_glu, gdn_conv_w, gdn_A_log, gdn_dt_bias, gdn_norm_w, rw_maa, rw_w_r, rw_w_k, rw_w_v, rw_w_o, rw_w0, rw_w1, rw_w2, rw_a0, rw_a1, rw_a2, rw_g1, rw_g2, rw_k_k, rw_k_a, rw_r_k, rw_ln_w, rw_ln_b):
    bsz_p, seq_p, _ = x_prompt.shape
    bsz_s, seq_s, _ = x_sample.shape

    w_in = ab_w_in[0]
    n_main = D_A + D_QKV
    w_in = jnp.concatenate([w_in[:, :n_main], w_in[:, n_main + 2 * H_B:], w_in[:, n_main:n_main + 2 * H_B]],
                           axis=1)
    w_in = _pad_to(w_in, 1, D_IN_PAD).astype(BF16)

    ff = lambda w, layer: w[layer].astype(BF16)

    gate_lanes = lambda v: _pad_to(jnp.concatenate([jnp.zeros((H_B,), F32), v.astype(F32)]), 0,
                                   GATE_PAD).reshape(1, GATE_PAD)
    p = dict(
        norm_mix0=_row(norm_mix[0]), norm_ffn0=_row(norm_ffn[0]), norm_ffn1=_row(norm_ffn[1]),
        norm_final=_row(norm_final), ab_w_in=w_in,
        ab_w_out_a=ab_w_out[0][:D_A].astype(BF16), ab_w_out_b=ab_w_out[0][D_A:].astype(BF16),
        wg0=ff(ffn_w_gate, 0), wu0=ff(ffn_w_up, 0),
        wd0=ff(ffn_w_down, 0),
        wg1=ff(ffn_w_gate, 1), wu1=ff(ffn_w_up, 1),
        wd1=ff(ffn_w_down, 1),
        s5=dict(
            lr=_row(s5_lambda_re[0]), li=_row(s5_lambda_im[0]),
            ls=_row(jnp.repeat(s5_log_step[0], P_STATE)),
            bre=_block_diag_groups(jnp.swapaxes(s5_B_re[0], 1, 2), S5_GROUP, P_STATE).astype(F32),
            bim=_block_diag_groups(jnp.swapaxes(s5_B_im[0], 1, 2), S5_GROUP, P_STATE).astype(F32),
            cre=_block_diag_groups(jnp.swapaxes(s5_C_re[0], 1, 2), P_STATE, S5_GROUP).astype(BF16),
            cim=_block_diag_groups(jnp.swapaxes(s5_C_im[0], 1, 2), P_STATE, S5_GROUP).astype(BF16),
            d=_row(s5_D[0]), wglu=s5_w_glu[0].astype(BF16)),
        gdn=dict(conv_w=gdn_conv_w[0].astype(F32), alog=gate_lanes(gdn_A_log[0]),
                 dtb=gate_lanes(gdn_dt_bias[0]), norm_w=_row(gdn_norm_w[0])),
        rw=dict(
            norm_w=_row(norm_mix[1]), maa=rw_maa[0].astype(F32),
            wr=rw_w_r[0].astype(BF16), wk=rw_w_k[0].astype(BF16), wv=rw_w_v[0].astype(BF16),
            wo=rw_w_o[0].astype(BF16),
            w0=_row(rw_w0[0]), w1=_pad_to(rw_w1[0], 1, LANES).astype(BF16),
            w2=_pad_to(rw_w2[0], 0, LANES).astype(BF16),
            a0=_row(rw_a0[0]), a1=_pad_to(rw_a1[0], 1, LANES).astype(BF16),
            a2=_pad_to(rw_a2[0], 0, LANES).astype(BF16),
            g1=_pad_to(rw_g1[0], 1, 2 * LANES).astype(BF16), g2=_pad_to(rw_g2[0], 0, 2 * LANES).astype(BF16),
            k_k=_row(rw_k_k[0]), k_a=_row(rw_k_a[0]), r_k=_row(rw_r_k[0]),
            ln_w=_row(rw_ln_w[0]), ln_b=_row(rw_ln_b[0])),
    )

    zeros = lambda *shape: jnp.zeros(shape, F32)
    yp, p_hr, p_hi, p_gdn, p_conv, p_rw, p_shift = _trunk(
        x_prompt, bsz_p, seq_p, zeros(bsz_p, N_S5), zeros(bsz_p, N_S5), zeros(bsz_p, H_B, DK_B, DV_B),
        zeros(bsz_p, CONV_W - 1, D_QKV), zeros(bsz_p, H_C, N_HEAD_C, N_HEAD_C), zeros(bsz_p, D_MODEL), p,
        True)

    xs_tm = jnp.transpose(x_sample, (1, 0, 2)).reshape(1, seq_s * bsz_s, D_MODEL)
    ys, s_hr, s_hi, s_gdn, s_conv, s_rw, s_shift = _trunk(
        xs_tm, bsz_s, seq_s, state_s5_re[0].reshape(bsz_s, N_S5), state_s5_im[0].reshape(bsz_s, N_S5),
        state_gdn[0], state_gdn_conv[0], state_rwkv[0], state_rwkv_shift[0], p, True)
    y_sample = jnp.transpose(ys.reshape(seq_s, bsz_s, D_MODEL), (1, 0, 2))

    s5_shape = lambda b: (1, b, G_A, P_STATE)
    return (yp, y_sample,
            p_hr.reshape(s5_shape(bsz_p)), p_hi.reshape(s5_shape(bsz_p)), p_gdn[None], p_conv[None],
            p_rw[None], p_shift[None],
            s_hr.reshape(s5_shape(bsz_s)), s_hi.reshape(s5_shape(bsz_s)), s_gdn[None], s_conv[None],
            s_rw[None], s_shift[None])
```

```python
import functools
import math

import jax
import jax.numpy as jnp
from jax import lax
from jax.experimental import pallas as pl
from jax.experimental.pallas import tpu as pltpu

F32 = jnp.float32
BF16 = jnp.bfloat16

D_MODEL = 1024
D_A = 512
S5_GROUP = 16
G_A = 32
P_STATE = 64
N_S5 = G_A * P_STATE
D_B = 512
H_B = 4
DK_B = 128
DV_B = 128
D_QKV = 2 * H_B * DK_B + D_B
CONV_W = 4
GDN_CHUNK = 64
N_HEAD_C = 64
H_C = 16
RWKV_CHUNK = 64
RWKV_PAIR_GROUP = 8
RWKV_STAGGER = 7
RWKV_GN_EPS = 64e-5
N_RWKV_ACTS = 6
D_FF = 2816
NORM_EPS = 1e-6

LANES = 128
SUBLANES = 8
FF_TILE = 256
N_FF_TILES = D_FF // FF_TILE
GATE_PAD = LANES
D_IN_PAD = 4 * 512 + 512 + GATE_PAD
S5_LANE_BLOCK = 512
N_S5_BLOCKS = N_S5 // S5_LANE_BLOCK
TOKEN_TILE = 512
FFN_TOKEN_TILE = 512
RWKV_PROJ_TILE = 512
MIB = 1024 * 1024


def _cparams(semantics, vmem_mib):
    return pltpu.CompilerParams(dimension_semantics=semantics, vmem_limit_bytes=vmem_mib * MIB)


def _rmsnorm(x, w):
    return x * lax.rsqrt(jnp.mean(x * x, axis=-1, keepdims=True) + NORM_EPS) * w


def _sigmoid(x):
    return 0.5 + 0.5 * jnp.tanh(0.5 * x)


def _silu(x):
    half = 0.5 * x
    return half + half * jnp.tanh(half)


def _softplus(x):
    return jnp.maximum(x, 0.0) + jnp.log1p(jnp.exp(-jnp.abs(x)))


def _bdot(a, b):
    return jnp.dot(a.astype(BF16), b.astype(BF16), preferred_element_type=F32)


def _bdot_nt(a, b):
    return lax.dot_general(a.astype(BF16), b.astype(BF16), (((1,), (1,)), ((), ())),
                           preferred_element_type=F32)


def _bdot_tn(a, b):
    return lax.dot_general(a.astype(BF16), b.astype(BF16), (((0,), (0,)), ((), ())),
                           preferred_element_type=F32)


def _split(a):
    hi = a.astype(BF16)
    return hi, (a - hi.astype(F32)).astype(BF16)


def _split3(a):
    p1 = a.astype(BF16)
    r1 = a - p1.astype(F32)
    p2 = r1.astype(BF16)
    return p1, p2, (r1 - p2.astype(F32)).astype(BF16)


def _split_dot(a, exact_bf16):
    hi, lo = _split(a)
    return (jnp.dot(hi, exact_bf16, preferred_element_type=F32)
            + jnp.dot(lo, exact_bf16, preferred_element_type=F32))


def _cumsum_rows(tri_bf16, w, parts=3):
    split = _split3(w) if parts == 3 else _split(w)
    return sum(jnp.dot(tri_bf16, part, preferred_element_type=F32) for part in split)


def _iota2(shape):
    return (lax.broadcasted_iota(jnp.int32, shape, 0), lax.broadcasted_iota(jnp.int32, shape, 1))


def _run_staggered(gens, lag):
    pending, active, tick = list(gens), [], 0
    while pending or active:
        if pending and tick % lag == 0:
            active.append(pending.pop(0))
        for g in list(active):
            try:
                next(g)
            except StopIteration:
                active.remove(g)
        tick += 1


def _inv_identity_plus_wide(lmws, t):
    gen = _inv_identity_plus_wide_stages(lmws, t)
    while True:
        try:
            next(gen)
        except StopIteration as done:
            return done.value


def _inv_identity_plus_wide_stages(lmws, t):
    row, col = _iota2((t, 2 * t))
    left = col < t
    col = jnp.where(left, col, col - t)
    eye = (row == col).astype(F32)
    base = min(t, 16)

    def blockdiag(xw):
        return jnp.concatenate([jnp.where(left, xw, 0.0), jnp.where(left, 0.0, xw)], axis=0)

    if t > base:
        same_base = (row >> 4) == (col >> 4)
        ns = [jnp.where(same_base, -lmw, 0.0) for lmw in lmws]
    else:
        ns = [-lmw for lmw in lmws]
    xs = [eye + n for n in ns]
    ps = [_bdot(n, blockdiag(n)) for n in ns]
    yield
    k = 2
    while k < base:
        bds = [blockdiag(p) for p in ps]
        if 2 * k < base:
            both = [_bdot(jnp.concatenate([p, x], axis=0), bd) for p, x, bd in zip(ps, xs, bds)]
            ps = [r[:t] for r in both]
            xs = [x + r[t:] for x, r in zip(xs, both)]
        else:
            xs = [x + _bdot(x, bd) for x, bd in zip(xs, bds)]
        yield
        k *= 2
    shift = 4
    blk = base
    while blk < t:
        same_big = (row >> (shift + 1)) == (col >> (shift + 1))
        same_small = (row >> shift) == (col >> shift)
        off = [jnp.where(same_big, jnp.where(same_small, 0.0, lmw), 0.0) for lmw in lmws]
        mids = [_bdot(o, blockdiag(x)) for o, x in zip(off, xs)]
        yield
        xs = [x - _bdot(x, blockdiag(m)) for x, m in zip(xs, mids)]
        yield
        blk *= 2
        shift += 1
    return xs


def _ab_in_kernel(x_ref, nw_ref, w_ref, u_ref, qkv_ref, z_ref, ba_ref):
    h = _rmsnorm(x_ref[...], nw_ref[...])
    p = jnp.dot(h.astype(BF16), w_ref[...], preferred_element_type=F32)
    u_ref[...] = p[:, 0:D_A]
    qkv_ref[...] = p[:, D_A:D_A + D_QKV]
    z_ref[...] = p[:, D_A + D_QKV:D_A + D_QKV + D_B]
    ba_ref[...] = p[:, D_A + D_QKV + D_B:D_IN_PAD]


def _ab_in(x_bm, norm_w, w_in):
    bv, lv, _ = x_bm.shape
    tl = min(TOKEN_TILE, lv)
    grid = (bv, lv // tl)
    tm = lambda n: pl.BlockSpec((tl, n), lambda b, i: (i, b))
    const = lambda shape: pl.BlockSpec(shape, lambda b, i: (0,) * len(shape))
    return pl.pallas_call(
        _ab_in_kernel,
        grid=grid,
        in_specs=[pl.BlockSpec((None, tl, D_MODEL), lambda b, i: (b, i, 0)),
                  const((1, D_MODEL)), const((D_MODEL, D_IN_PAD))],
        out_specs=[tm(D_A), tm(D_QKV), tm(D_B), tm(GATE_PAD)],
        out_shape=[jax.ShapeDtypeStruct((lv, bv * D_A), F32),
                   jax.ShapeDtypeStruct((lv, bv * D_QKV), F32),
                   jax.ShapeDtypeStruct((lv, bv * D_B), F32),
                   jax.ShapeDtypeStruct((lv, bv * GATE_PAD), F32)],
        compiler_params=_cparams(("parallel", "parallel"), 48),
    )(x_bm, norm_w, w_in)


def _s5_kernel(u_ref, h0r_ref, h0i_ref, lr_ref, li_ref, ls_ref, bre_ref, bim_ref, cre_ref, cim_ref,
               d_ref, wglu_ref, y_ref, hr_out, hi_out,
               ar_s, ai_s, bbr_s, bbi_s, hr_s, hi_s, bur_s, bui_s, yg_s, *, tc, seq_on_lanes):
    rows = tc * SUBLANES

    @pl.when(pl.program_id(1) == 0)
    def _():
        lr = lr_ref[...]
        li = li_ref[...]
        dt = jnp.exp(ls_ref[...])
        mag = jnp.exp(lr * dt)
        ar = mag * jnp.cos(li * dt)
        ai = mag * jnp.sin(li * dt)
        den = lr * lr + li * li
        nr = ar - 1.0
        cr = (nr * lr + ai * li) / den
        ci = (ai * lr - nr * li) / den
        ar_s[...] = jnp.broadcast_to(ar, (SUBLANES, N_S5))
        ai_s[...] = jnp.broadcast_to(ai, (SUBLANES, N_S5))
        for j in range(N_S5_BLOCKS):
            sl = slice(j * S5_LANE_BLOCK, (j + 1) * S5_LANE_BLOCK)
            bbr_s[j] = (cr[:, sl] * bre_ref[j] - ci[:, sl] * bim_ref[j]).astype(BF16)
            bbi_s[j] = (cr[:, sl] * bim_ref[j] + ci[:, sl] * bre_ref[j]).astype(BF16)
        hr_s[...] = h0r_ref[...]
        hi_s[...] = h0i_ref[...]

    if seq_on_lanes:
        per_seq = jnp.stack([u_ref[:, b * D_A:(b + 1) * D_A] for b in range(SUBLANES)], axis=0)
        u = jnp.swapaxes(per_seq, 0, 1).reshape(rows, D_A)
    else:
        u = u_ref[...].reshape(rows, D_A)
    ub = u.astype(BF16)
    for j in range(N_S5_BLOCKS):
        sl = slice(j * S5_LANE_BLOCK, (j + 1) * S5_LANE_BLOCK)
        uj = ub[:, j * LANES:(j + 1) * LANES]
        bur_s[:, :, sl] = jnp.dot(uj, bbr_s[j], preferred_element_type=F32).reshape(tc, SUBLANES, S5_LANE_BLOCK)
        bui_s[:, :, sl] = jnp.dot(uj, bbi_s[j], preferred_element_type=F32).reshape(tc, SUBLANES, S5_LANE_BLOCK)

    for j in range(N_S5_BLOCKS):
        sl = slice(j * S5_LANE_BLOCK, (j + 1) * S5_LANE_BLOCK)
        ar = ar_s[:, sl]
        ai = ai_s[:, sl]

        def step(t, carry, sl=sl, ar=ar, ai=ai):
            hr, hi = carry
            nr = ar * hr - ai * hi + bur_s[t, :, sl]
            ni = ar * hi + ai * hr + bui_s[t, :, sl]
            bur_s[t, :, sl] = nr
            bui_s[t, :, sl] = ni
            return nr, ni

        hr, hi = lax.fori_loop(0, tc, step, (hr_s[:, sl], hi_s[:, sl]), unroll=min(tc, 8))
        hr_s[:, sl] = hr
        hi_s[:, sl] = hi

    for j in range(N_S5_BLOCKS):
        sl = slice(j * S5_LANE_BLOCK, (j + 1) * S5_LANE_BLOCK)
        cl = slice(j * LANES, (j + 1) * LANES)
        xr = bur_s[:, :, sl].reshape(rows, S5_LANE_BLOCK).astype(BF16)
        xi = bui_s[:, :, sl].reshape(rows, S5_LANE_BLOCK).astype(BF16)
        yj = (jnp.dot(xr, cre_ref[j], preferred_element_type=F32)
              - jnp.dot(xi, cim_ref[j], preferred_element_type=F32)
              + d_ref[:, cl] * u[:, cl])
        yg_s[:, cl] = jax.nn.gelu(yj)

    yg = yg_s[...]
    out = yg * _sigmoid(jnp.dot(yg.astype(BF16), wglu_ref[...], preferred_element_type=F32))
    if seq_on_lanes:
        per_seq = jnp.swapaxes(out.reshape(tc, SUBLANES, D_A), 0, 1)
        for b in range(SUBLANES):
            y_ref[:, b * D_A:(b + 1) * D_A] = per_seq[b]
    else:
        y_ref[...] = out.reshape(tc, SUBLANES, D_A)
    hr_out[...] = hr_s[...]
    hi_out[...] = hi_s[...]


def _s5(u_tm, h0r, h0i, prm, tc):
    seq_on_lanes = u_tm.ndim == 2
    seq, bsz = u_tm.shape[0], h0r.shape[0]
    grid = (bsz // SUBLANES, seq // tc)
    const = lambda shape: pl.BlockSpec(shape, lambda b, i: (0,) * len(shape))
    if seq_on_lanes:
        act = pl.BlockSpec((tc, SUBLANES * D_A), lambda b, i: (i, b))
    else:
        act = pl.BlockSpec((tc, SUBLANES, D_A), lambda b, i: (i, b, 0))
    st = pl.BlockSpec((SUBLANES, N_S5), lambda b, i: (b, 0))
    vec = const((1, N_S5))
    small = pltpu.VMEM((SUBLANES, N_S5), F32)
    big = pltpu.VMEM((tc, SUBLANES, N_S5), F32)
    return pl.pallas_call(
        functools.partial(_s5_kernel, tc=tc, seq_on_lanes=seq_on_lanes),
        grid=grid,
        in_specs=[act, st, st, vec, vec, vec,
                  const((N_S5_BLOCKS, LANES, S5_LANE_BLOCK)), const((N_S5_BLOCKS, LANES, S5_LANE_BLOCK)),
                  const((N_S5_BLOCKS, S5_LANE_BLOCK, LANES)), const((N_S5_BLOCKS, S5_LANE_BLOCK, LANES)),
                  const((1, D_A)), const((D_A, D_A))],
        out_specs=[act, st, st],
        out_shape=[jax.ShapeDtypeStruct(u_tm.shape, F32),
                   jax.ShapeDtypeStruct((bsz, N_S5), F32),
                   jax.ShapeDtypeStruct((bsz, N_S5), F32)],
        scratch_shapes=[small, small, pltpu.VMEM((N_S5_BLOCKS, LANES, S5_LANE_BLOCK), BF16),
                        pltpu.VMEM((N_S5_BLOCKS, LANES, S5_LANE_BLOCK), BF16), small, small, big, big,
                        pltpu.VMEM((tc * SUBLANES, D_A), F32)],
        compiler_params=_cparams(("parallel", "arbitrary"), 48),
    )(u_tm, h0r, h0i, prm["lr"], prm["li"], prm["ls"], prm["bre"], prm["bim"], prm["cre"], prm["cim"],
      prm["d"], prm["wglu"])


def _gdn_kernel(qkv_ref, ba_ref, z_ref, conv0_ref, s0_ref, cw_ref, alog_ref, dtb_ref, nw_ref,
                o_ref, sfin_ref, convn_ref, xbuf, s_s, *, t, nb, seq_on_sublanes):
    if seq_on_sublanes:
        qkv_seq = jnp.swapaxes(qkv_ref[...], 0, 1)
        ba_seq = jnp.swapaxes(ba_ref[...], 0, 1)
        z_seq = jnp.swapaxes(z_ref[...], 0, 1)
        read_qkv = lambda e: qkv_seq[e]
        read_ba = lambda e: ba_seq[e]
        read_z = lambda e, h: z_seq[e][:, h * DV_B:(h + 1) * DV_B]
    else:
        read_qkv = lambda e: qkv_ref[:, e * D_QKV:(e + 1) * D_QKV]
        read_ba = lambda e: ba_ref[:, e * GATE_PAD:(e + 1) * GATE_PAD]
        read_z = lambda e, h: z_ref[:, e * D_B + h * DV_B:e * D_B + (h + 1) * DV_B]
    o_parts = {}
    pad = SUBLANES
    hist = CONV_W - 1

    @pl.when(pl.program_id(1) == 0)
    def _():
        for e in range(nb):
            xbuf[pad - hist:pad, e * D_QKV:(e + 1) * D_QKV] = conv0_ref[e]
        s_s[...] = s0_ref[...].reshape(nb * H_B, DK_B, DV_B)

    row, col = _iota2((t, t))
    causal = row >= col
    strict = row > col
    tri = causal.astype(BF16)
    lane = lax.broadcasted_iota(jnp.int32, (t, LANES), 1)
    nw = nw_ref[...]
    cw = cw_ref[...]
    ones_sq = jnp.ones((DK_B, DK_B), BF16)
    zeros_wide = jnp.zeros((t, DK_B + DV_B), F32)
    zeros_v = jnp.zeros((t, DV_B), F32)

    def seq_group(es):
        ys, betas, gcs = [], [], []
        for e in es:
            cols = slice(e * D_QKV, (e + 1) * D_QKV)
            x = read_qkv(e)
            xbuf[pad:pad + t, cols] = x
            acc = x * cw[hist:hist + 1, :]
            for j in range(hist):
                acc = acc + xbuf[pad - hist + j:pad - hist + j + t, cols] * cw[j:j + 1, :]
            last = xbuf[pad + t - hist:pad + t, cols]
            convn_ref[e] = last
            xbuf[pad - hist:pad, cols] = last
            ys.append(_silu(acc))
            ba = read_ba(e)
            betas.append(_sigmoid(ba))
            g = -jnp.exp(alog_ref[...]) * _softplus(ba + dtb_ref[...])
            gcs.append(_cumsum_rows(tri, g))
            yield

        chains = [(i, h) for i in range(len(es)) for h in range(H_B)]
        heads = range(len(chains))
        bcol = [betas[i][:, h:h + 1] for i, h in chains]
        gcol = [gcs[i][:, H_B + h:H_B + h + 1] for i, h in chains]
        y_of = [ys[i] for i, h in chains]
        h_of = [h for i, h in chains]
        decay = []
        gc_parts = [[p.astype(F32) for p in _split3(gc)] for gc in gcs]
        for i, hh in chains:
            p1, p2, p3 = [p[:, H_B + hh:H_B + hh + 1] for p in gc_parts[i]]
            dl = jnp.where(lane == 0, p1, jnp.where(lane == 1, p2, jnp.where(lane == 2, p3,
                                                                             jnp.where(lane < 6, 1.0, 0.0))))
            dr = jnp.where(lane < 3, 1.0, jnp.where(lane == 3, -p1, jnp.where(lane == 4, -p2,
                                                                            jnp.where(lane == 5, -p3, 0.0))))
            decay.append(jnp.exp(jnp.where(causal, _bdot_nt(dl, dr), -jnp.inf)))
        eg = [jnp.exp(gcol[h]) for h in heads]
        glast = [gcol[h][t - 1:t, :] for h in heads]
        yield
        q = [y_of[c][:, h_of[c] * DK_B:(h_of[c] + 1) * DK_B] for c in heads]
        k = [y_of[c][:, (H_B + h_of[c]) * DK_B:(H_B + h_of[c] + 1) * DK_B] for c in heads]
        v = [y_of[c][:, 2 * H_B * DK_B + h_of[c] * DV_B:2 * H_B * DK_B + (h_of[c] + 1) * DV_B] for c in heads]
        q = [x * (lax.rsqrt(_bdot(x * x, ones_sq) + NORM_EPS) * (DK_B ** -0.5)) for x in q]
        k = [x * lax.rsqrt(_bdot(x * x, ones_sq) + NORM_EPS) for x in k]
        kb = [k[h] * bcol[h] for h in heads]
        vb = [v[h] * bcol[h] for h in heads]
        yield
        lm = [jnp.where(strict, _bdot_nt(kb[h], k[h]) * decay[h], 0.0) for h in heads]
        attn = [_bdot_nt(q[h], k[h]) * decay[h] for h in heads]
        s = [s_s[es[i] * H_B + h] for i, h in chains]
        qs = [_bdot(q[h] * eg[h], s[h]) for h in heads]
        yield
        pairs = [(c, c + 1) for c in range(0, len(chains), 2)]
        tinv = yield from _inv_identity_plus_wide_stages(
            [jnp.concatenate([lm[c0], lm[c1]], axis=1) for c0, c1 in pairs], t)
        kbg = [kb[h] * eg[h] for h in heads]
        uw = [_bdot(tinv[i], jnp.concatenate(
            [jnp.concatenate([vb[c0], kbg[c0], zeros_wide], axis=1),
             jnp.concatenate([zeros_wide, vb[c1], kbg[c1]], axis=1)], axis=0)) for i, (c0, c1) in enumerate(pairs)]
        u = [uw[c // 2][:, (c % 2) * (DK_B + DV_B):(c % 2) * (DK_B + DV_B) + DV_B] for c in heads]
        w = [uw[c // 2][:, (c % 2) * (DK_B + DV_B) + DV_B:(c % 2 + 1) * (DK_B + DV_B)] for c in heads]
        yield
        v_new = [u[h] - _bdot(w[h], s[h]) for h in heads]
        yield
        av = [_bdot(jnp.concatenate([attn[c0], attn[c1]], axis=1), jnp.concatenate(
            [jnp.concatenate([v_new[c0], zeros_v], axis=1),
             jnp.concatenate([zeros_v, v_new[c1]], axis=1)], axis=0)) for c0, c1 in pairs]
        o = [qs[c] + av[c // 2][:, (c % 2) * DV_B:(c % 2 + 1) * DV_B] for c in heads]
        upd = [_bdot_tn(k[h] * jnp.exp(glast[h] - gcol[h]), v_new[h]) for h in heads]
        yield
        for c, (i, h) in enumerate(chains):
            e = es[i]
            s_new = s[c] * jnp.exp(glast[c]) + upd[c]
            s_s[e * H_B + h] = s_new
            sfin_ref[e, h] = s_new
            o_parts[e, h] = _rmsnorm(o[c], nw) * _silu(read_z(e, h))
            if not seq_on_sublanes:
                o_ref[:, e * D_B + h * DV_B:e * D_B + (h + 1) * DV_B] = o_parts[e, h]

    _run_staggered([seq_group(list(range(nb)))], 1)
    if seq_on_sublanes:
        per_seq = [jnp.concatenate([o_parts[e, h] for h in range(H_B)], axis=1) for e in range(nb)]
        o_ref[...] = jnp.swapaxes(jnp.stack(per_seq, axis=0), 0, 1)


def _gdn(qkv_tm, ba_tm, z_tm, conv0, s0, prm, bsz, t, nb):
    seq = qkv_tm.shape[0]
    seq_on_sublanes = qkv_tm.ndim == 3
    grid = (bsz // nb, seq // t)
    const = lambda shape: pl.BlockSpec(shape, lambda b, i: (0,) * len(shape))
    if seq_on_sublanes:
        tm = lambda n: pl.BlockSpec((t, nb, n), lambda b, i: (i, b, 0))
        out_shape = (seq, bsz, D_B)
    else:
        tm = lambda n: pl.BlockSpec((t, nb * n), lambda b, i: (i, b))
        out_shape = (seq, bsz * D_B)
    state = pl.BlockSpec((nb, H_B, DK_B, DV_B), lambda b, i: (b, 0, 0, 0))
    conv = pl.BlockSpec((nb, CONV_W - 1, D_QKV), lambda b, i: (b, 0, 0))
    return pl.pallas_call(
        functools.partial(_gdn_kernel, t=t, nb=nb, seq_on_sublanes=seq_on_sublanes),
        grid=grid,
        in_specs=[tm(D_QKV), tm(GATE_PAD), tm(D_B), conv, state,
                  const((CONV_W, D_QKV)), const((1, GATE_PAD)), const((1, GATE_PAD)), const((1, DV_B))],
        out_specs=[tm(D_B), state, conv],
        out_shape=[jax.ShapeDtypeStruct(out_shape, F32),
                   jax.ShapeDtypeStruct((bsz, H_B, DK_B, DV_B), F32),
                   jax.ShapeDtypeStruct((bsz, CONV_W - 1, D_QKV), F32)],
        scratch_shapes=[pltpu.VMEM((t + SUBLANES, nb * D_QKV), F32),
                        pltpu.VMEM((nb * H_B, DK_B, DV_B), F32)],
        compiler_params=_cparams(("parallel", "arbitrary"), 48),
    )(qkv_tm, ba_tm, z_tm, conv0, s0, prm["conv_w"], prm["alog"], prm["dtb"], prm["norm_w"])


def _mix_ffn_kernel(*refs, n_mix, final_norm):
    x_ref = refs[0]
    mix_refs = refs[1:1 + n_mix]
    wout_refs = refs[1 + n_mix:1 + 2 * n_mix]
    nffn_ref, wg_ref, wu_ref, wd_ref = refs[1 + 2 * n_mix:5 + 2 * n_mix]
    rest = refs[5 + 2 * n_mix:]
    if final_norm:
        nfin_ref, out_ref = rest
    else:
        (out_ref,) = rest

    x1 = x_ref[...]
    for m_ref, w_ref in zip(mix_refs, wout_refs):
        x1 = x1 + jnp.dot(m_ref[...].astype(BF16), w_ref[...], preferred_element_type=F32)
    h = _rmsnorm(x1, nffn_ref[...]).astype(BF16)
    out_ref[...] = x1

    for c in range(N_FF_TILES):
        cols = slice(c * FF_TILE, (c + 1) * FF_TILE)
        gate = jnp.dot(h, wg_ref[:, cols], preferred_element_type=F32)
        up = jnp.dot(h, wu_ref[:, cols], preferred_element_type=F32)
        act = (_silu(gate) * up).astype(BF16)
        out_ref[...] += jnp.dot(act, wd_ref[cols, :], preferred_element_type=F32)
    if final_norm:
        out_ref[...] = _rmsnorm(out_ref[...], nfin_ref[...])


def _mix_ffn(x, x_bm, mixes, wouts, nffn, wg, wu, wd, nfin, bv, lv, out_bm):
    tl = min(FFN_TOKEN_TILE, lv)
    grid = (bv, lv // tl)
    const = lambda shape: pl.BlockSpec(shape, lambda b, i: (0,) * len(shape),
                                       pipeline_mode=pl.Buffered(1))
    tm = lambda n: pl.BlockSpec((tl, n), lambda b, i: (i, b))
    bm = pl.BlockSpec((None, tl, D_MODEL), lambda b, i: (b, i, 0))
    in_specs = [bm if x_bm else tm(D_MODEL)]
    in_specs += [tm(m.shape[1] // bv) for m in mixes]
    in_specs += [const(w.shape) for w in wouts]
    in_specs += [const((1, D_MODEL)), const(wg.shape), const(wu.shape), const(wd.shape)]
    args = [x, *mixes, *wouts, nffn, wg, wu, wd]
    if nfin is not None:
        in_specs.append(const((1, D_MODEL)))
        args.append(nfin)
    if out_bm:
        out_spec, out_shape = bm, jax.ShapeDtypeStruct((bv, lv, D_MODEL), F32)
    else:
        out_spec, out_shape = tm(D_MODEL), jax.ShapeDtypeStruct((lv, bv * D_MODEL), F32)
    return pl.pallas_call(
        functools.partial(_mix_ffn_kernel, n_mix=len(mixes), final_norm=nfin is not None),
        grid=grid,
        in_specs=in_specs,
        out_specs=out_spec,
        out_shape=out_shape,
        compiler_params=_cparams(("parallel", "parallel"), 56),
    )(*args)


def _rwkv_proj_kernel(x_ref, shift0_ref, nw_ref, maa_ref, wr_ref, wk_ref, wv_ref, w0_ref, w1_ref, w2_ref,
                      a0_ref, a1_ref, a2_ref, g1_ref, g2_ref,
                      r_out, w_out, k_out, v_out, a_out, g_out, shift_out, hbuf, *, shift):
    pad = -(-shift // SUBLANES) * SUBLANES

    @pl.when(pl.program_id(1) == 0)
    def _():
        hbuf[pad - shift:pad, :] = shift0_ref[...]

    h = _rmsnorm(x_ref[...], nw_ref[...])
    rows = h.shape[0]
    hbuf[pad:pad + rows, :] = h
    prev = hbuf[pad - shift:pad - shift + rows, :]
    last = h[rows - shift:]
    hbuf[pad - shift:pad, :] = last
    shift_out[...] = last
    xx = prev - h
    maa = maa_ref[...]
    xr, xw, xk, xv, xa, xg = [h + xx * maa[j:j + 1, :] for j in range(6)]
    r = _bdot(xr, wr_ref[...])
    k = _bdot(xk, wk_ref[...])
    v = _bdot(xv, wv_ref[...])
    u = w0_ref[...] + _bdot(jnp.tanh(_bdot(xw, w1_ref[...])), w2_ref[...])
    a = _sigmoid(a0_ref[...] + _bdot(_bdot(xa, a1_ref[...]), a2_ref[...]))
    g = _bdot(_sigmoid(_bdot(xg, g1_ref[...])), g2_ref[...])
    r_out[...] = r
    w_out[...] = -math.exp(-0.5) * _sigmoid(u)
    k_out[...] = k
    v_out[...] = v
    a_out[...] = a
    g_out[...] = g


def _rwkv_proj(x_tm, shift0, prm):
    bv, shift, _ = shift0.shape
    lv = x_tm.shape[0]
    tr = max(shift, min(RWKV_PROJ_TILE, lv))
    grid = (bv, lv // tr)
    const = lambda shape: pl.BlockSpec(shape, lambda b, i: (0,) * len(shape),
                                       pipeline_mode=pl.Buffered(1))
    act = pl.BlockSpec((tr, D_MODEL), lambda b, i: (i, b))
    carry = pl.BlockSpec((None, shift, D_MODEL), lambda b, i: (b, 0, 0))
    vec = const((1, D_MODEL))
    weights = [prm[n] for n in ("wr", "wk", "wv")]
    in_specs = [act, carry, vec, const((6, D_MODEL))]
    in_specs += [const(w.shape) for w in weights]
    in_specs += [vec, const(prm["w1"].shape), const(prm["w2"].shape),
                 vec, const(prm["a1"].shape), const(prm["a2"].shape),
                 const(prm["g1"].shape), const(prm["g2"].shape)]
    out_act = jax.ShapeDtypeStruct((lv, bv * D_MODEL), F32)
    pad = -(-shift // SUBLANES) * SUBLANES
    return pl.pallas_call(
        functools.partial(_rwkv_proj_kernel, shift=shift),
        grid=grid,
        in_specs=in_specs,
        out_specs=[act] * N_RWKV_ACTS + [carry],
        out_shape=[out_act] * N_RWKV_ACTS + [jax.ShapeDtypeStruct((bv, shift, D_MODEL), F32)],
        scratch_shapes=[pltpu.VMEM((pad + tr, D_MODEL), F32)],
        compiler_params=_cparams(("parallel", "arbitrary"), 56),
    )(x_tm, shift0, prm["norm_w"], prm["maa"], prm["wr"], prm["wk"], prm["wv"], prm["w0"], prm["w1"],
      prm["w2"], prm["a0"], prm["a1"], prm["a2"], prm["g1"], prm["g2"])


def _rwkv_scan_kernel(r_ref, w_ref, k_ref, v_ref, a_ref, g_ref, s0_ref, kkw_ref, kaw_ref, rk_ref, lnw_ref, lnb_ref,
                      y_ref, sfin_ref, s_s, *, t, nb, group, seq_on_sublanes):
    if seq_on_sublanes:
        act_refs = (r_ref, w_ref, k_ref, v_ref, a_ref, g_ref)
        per_seq = [jnp.swapaxes(ref[...], 0, 1) for ref in act_refs]

        def slab(ref, jj):
            e, j = divmod(jj, H_C // 2)
            which = [i for i, x in enumerate(act_refs) if x is ref][0]
            return per_seq[which][e][:, j * LANES:(j + 1) * LANES]
    else:
        slab = lambda ref, jj: ref[:, jj * LANES:(jj + 1) * LANES]
    y_parts = {}
    n = N_HEAD_C
    zero_blk = jnp.zeros((n, n), F32)

    pairs_per_seq = H_C // 2

    @pl.when(pl.program_id(1) == 0)
    def _():
        for jj in range(nb * pairs_per_seq):
            e, j = divmod(jj, pairs_per_seq)
            top = jnp.concatenate([s0_ref[e, 2 * j], zero_blk], axis=1)
            bot = jnp.concatenate([zero_blk, s0_ref[e, 2 * j + 1]], axis=1)
            s_s[jj] = jnp.concatenate([top, bot], axis=0)

    row, col = _iota2((t, t))
    tri = (row >= col).astype(BF16)
    roww, colw = _iota2((t, 2 * t))
    colw = jnp.where(colw >= t, colw - t, colw)
    incl_w = roww >= colw
    strict_w = roww > colw
    lane = lax.broadcasted_iota(jnp.int32, (t, LANES), 1)
    head0 = lane < n

    def by_head(x):
        return jnp.concatenate([jnp.where(head0, x, 0.0), jnp.where(head0, 0.0, x)], axis=0)

    prow, pcol = _iota2((LANES, LANES))
    same_head = (prow >> 6) == (pcol >> 6)
    ones_blk = same_head.astype(BF16)

    def head_sums(xs, dot):
        tot = dot(jnp.concatenate(xs, axis=0), ones_blk)
        return [tot[i * t:(i + 1) * t] for i in range(len(xs))]

    def pair_group(js):
        ps = range(len(js))
        sl = [slice(j * LANES, (j + 1) * LANES) for j in js]
        psl = [slice((j % pairs_per_seq) * LANES, (j % pairs_per_seq + 1) * LANES) for j in js]
        r = [slab(r_ref, j) for j in js]
        w = [slab(w_ref, j) for j in js]
        kraw = [slab(k_ref, j) for j in js]
        v = [slab(v_ref, j) for j in js]
        a = [slab(a_ref, j) for j in js]
        kkr = [kraw[p] * kkw_ref[:, psl[p]] for p in ps]
        k = [kraw[p] * (1.0 + (a[p] - 1.0) * kaw_ref[:, psl[p]]) for p in ps]
        kk = [x * lax.rsqrt(ss + NORM_EPS) for x, ss in zip(kkr, head_sums([x * x for x in kkr], _bdot))]
        bv = [kk[p] * a[p] for p in ps]
        yield
        gc = [_cumsum_rows(tri, x, parts=2) for x in w]
        glast = [x[t - 1:t, :] for x in gc]
        yield
        pinv = [jnp.exp(-x) for x in gc]
        at = [-kk[p] * jnp.exp(gc[p] - w[p]) for p in ps]
        bt = [bv[p] * pinv[p] for p in ps]
        kt = [k[p] * pinv[p] for p in ps]
        rt = [r[p] * jnp.exp(gc[p]) for p in ps]
        rem = [jnp.exp(glast[p] - gc[p]) for p in ps]
        yield
        lhs = [jnp.concatenate([at[p], rt[p]], axis=0) for p in ps]
        gm = [_bdot_nt(lhs[p], jnp.concatenate([by_head(bt[p]), by_head(kt[p])], axis=0)) for p in ps]
        yield
        s = [s_s[j] for j in js]
        ars = [_bdot_nt(lhs[p], s[p]) for p in ps]
        v2 = [by_head(x) for x in v]
        yield
        aab = [jnp.where(strict_w, gm[p][:t, :2 * t], 0.0) for p in ps]
        aak = [jnp.where(strict_w, gm[p][:t, 2 * t:], 0.0) for p in ps]
        rkm = [jnp.where(incl_w, gm[p][t:, 2 * t:], 0.0) for p in ps]
        rb = [jnp.where(incl_w, gm[p][t:, :2 * t], 0.0) for p in ps]
        yield
        kv = [_bdot(jnp.concatenate([aak[p], rkm[p]], axis=0), v2[p]) for p in ps]
        rhs = [ars[p][:t] + kv[p][:t] for p in ps]
        yield
        tinv = yield from _inv_identity_plus_wide_stages([-x for x in aab], t)
        u = [_bdot(tinv[p], by_head(rhs[p])) for p in ps]
        yield
        y = [ars[p][t:] + _bdot(rb[p], by_head(u[p])) + kv[p][t:] for p in ps]
        yield
        upd = [_bdot_tn(jnp.concatenate([u[p], v[p]], axis=0),
                        jnp.concatenate([bv[p] * rem[p], k[p] * rem[p]], axis=0)) for p in ps]
        for p, j in enumerate(js):
            s_s[j] = s[p] * jnp.exp(glast[p]) + jnp.where(same_head, upd[p], 0.0)
        yield
        mu = head_sums(y, _split_dot)
        d = [y[p] - mu[p] * (1.0 / n) for p in ps]
        yield
        var = head_sums([x * x for x in d], _bdot)
        yield
        bonus = head_sums([r[p] * k[p] * rk_ref[:, psl[p]] for p in ps], _split_dot)
        yield
        for p in ps:
            yn = d[p] * lax.rsqrt(var[p] * (1.0 / n) + RWKV_GN_EPS) * lnw_ref[:, psl[p]] + lnb_ref[:, psl[p]]
            y_parts[js[p]] = (yn + bonus[p] * v[p]) * slab(g_ref, js[p])
            if not seq_on_sublanes:
                y_ref[:, sl[p]] = y_parts[js[p]]

    groups = [list(range(first, first + group)) for first in range(0, nb * pairs_per_seq, group)]
    _run_staggered([pair_group(js) for js in groups], RWKV_STAGGER)
    if seq_on_sublanes:
        rows = [jnp.concatenate([y_parts[e * pairs_per_seq + j] for j in range(pairs_per_seq)], axis=1)
                for e in range(nb)]
        y_ref[...] = jnp.swapaxes(jnp.stack(rows, axis=0), 0, 1)

    for jj in range(nb * pairs_per_seq):
        e, j = divmod(jj, pairs_per_seq)
        sp = s_s[jj]
        sfin_ref[e, 2 * j] = sp[:n, :n]
        sfin_ref[e, 2 * j + 1] = sp[n:, n:]


def _rwkv_scan(acts, s0, prm, bsz, t, nb, group):
    seq = acts[0].shape[0]
    seq_on_sublanes = acts[0].ndim == 3
    grid = (bsz // nb, seq // t)
    const = lambda shape: pl.BlockSpec(shape, lambda b, i: (0,) * len(shape))
    if seq_on_sublanes:
        tm = pl.BlockSpec((t, nb, D_MODEL), lambda b, i: (i, b, 0))
    else:
        tm = pl.BlockSpec((t, nb * D_MODEL), lambda b, i: (i, b))
    st = pl.BlockSpec((nb, H_C, N_HEAD_C, N_HEAD_C), lambda b, i: (b, 0, 0, 0))
    vec = const((1, D_MODEL))
    return pl.pallas_call(
        functools.partial(_rwkv_scan_kernel, t=t, nb=nb, group=group, seq_on_sublanes=seq_on_sublanes),
        grid=grid,
        in_specs=[tm] * N_RWKV_ACTS + [st, vec, vec, vec, vec, vec],
        out_specs=[tm, st],
        out_shape=[jax.ShapeDtypeStruct(acts[0].shape, F32),
                   jax.ShapeDtypeStruct((bsz, H_C, N_HEAD_C, N_HEAD_C), F32)],
        scratch_shapes=[pltpu.VMEM((nb * H_C // 2, LANES, LANES), F32)],
        compiler_params=_cparams(("parallel", "arbitrary"), 48),
    )(*acts, s0, prm["k_k"], prm["k_a"], prm["r_k"], prm["ln_w"], prm["ln_b"])


def _block_diag_groups(w, rows_per_group, cols_per_group):
    g = w.shape[0]
    w = w.reshape(g // 8, 8, rows_per_group, cols_per_group)
    eye = jnp.eye(8, dtype=w.dtype)
    out = jnp.einsum("jgrc,gh->jgrhc", w, eye)
    return out.reshape(g // 8, 8 * rows_per_group, 8 * cols_per_group)


def _pad_to(w, axis, size):
    pad = [(0, 0)] * w.ndim
    pad[axis] = (0, size - w.shape[axis])
    return jnp.pad(w, pad)


def _row(v):
    return v.reshape(1, -1).astype(F32)


def _recurrence_tiling(seq):
    if seq >= GDN_CHUNK:
        return dict(s5_tc=128, gdn_t=GDN_CHUNK, gdn_nb=4, rwkv_t=RWKV_CHUNK, rwkv_nb=2,
                    rwkv_group=2 * RWKV_PAIR_GROUP)
    return dict(s5_tc=seq, gdn_t=seq, gdn_nb=SUBLANES, rwkv_t=seq, rwkv_nb=SUBLANES,
                rwkv_group=2 * RWKV_PAIR_GROUP)


def _trunk(x_bm, bsz, seq, s5_re0, s5_im0, gdn_s0, gdn_conv0, rw_s0, rw_shift0, p, out_bm):
    bv, lv, _ = x_bm.shape
    cfg = _recurrence_tiling(seq)
    u, qkv, z, ba = _ab_in(x_bm, p["norm_mix0"], p["ab_w_in"])
    view = (lambda a, n: a.reshape(seq, bsz, n)) if bv == 1 else (lambda a, n: a)
    y_a, hr, hi = _s5(view(u, D_A), s5_re0, s5_im0, p["s5"], cfg["s5_tc"])
    y_b, gdn_s, gdn_conv = _gdn(view(qkv, D_QKV), view(ba, GATE_PAD), view(z, D_B), gdn_conv0, gdn_s0,
                                p["gdn"], bsz, cfg["gdn_t"], cfg["gdn_nb"])
    x1 = _mix_ffn(x_bm, True, [y_a.reshape(lv, bv * D_A), y_b.reshape(lv, bv * D_B)],
                  [p["ab_w_out_a"], p["ab_w_out_b"]], p["norm_ffn0"], p["wg0"], p["wu0"], p["wd0"],
                  None, bv, lv, False)
    acts = _rwkv_proj(x1, rw_shift0.reshape(bv, bsz // bv, D_MODEL), p["rw"])
    shift = acts[N_RWKV_ACTS].reshape(bsz, D_MODEL)
    yg, rw_s = _rwkv_scan([view(a, D_MODEL) for a in acts[:N_RWKV_ACTS]], rw_s0, p["rw"], bsz,
                          cfg["rwkv_t"], cfg["rwkv_nb"], cfg["rwkv_group"])
    y = _mix_ffn(x1, False, [yg.reshape(lv, bv * D_MODEL)], [p["rw"]["wo"]], p["norm_ffn1"], p["wg1"],
                 p["wu1"], p["wd1"], p["norm_final"], bv, lv, out_bm)
    return y, hr, hi, gdn_s, gdn_conv, rw_s, shift


def kernel(x_prompt, x_sample, state_s5_re, state_s5_im, state_gdn, state_gdn_conv, state_rwkv, state_rwkv_shift, norm_mix, norm_ffn, norm_final, ffn_w_gate, ffn_w_up, ffn_w_down, ab_w_in, ab_w_out, s5_lambda_re, s5_lambda_im, s5_log_step, s5_B_re, s5_B_im, s5_C_re, s5_C_im, s5_D, s5_w_glu, gdn_conv_w, gdn_A_log, gdn_dt_bias, gdn_norm_w, rw_maa, rw_w_r, rw_w_k, rw_w_v, rw_w_o, rw_w0, rw_w1, rw_w2, rw_a0, rw_a1, rw_a2, rw_g1, rw_g2, rw_k_k, rw_k_a, rw_r_k, rw_ln_w, rw_ln_b):
    bsz_p, seq_p, _ = x_prompt.shape
    bsz_s, seq_s, _ = x_sample.shape

    w_in = ab_w_in[0]
    n_main = D_A + D_QKV
    w_in = jnp.concatenate([w_in[:, :n_main], w_in[:, n_main + 2 * H_B:], w_in[:, n_main:n_main + 2 * H_B]],
                           axis=1)
    w_in = _pad_to(w_in, 1, D_IN_PAD).astype(BF16)

    ff = lambda w, layer: w[layer].astype(BF16)

    gate_lanes = lambda v: _pad_to(jnp.concatenate([jnp.zeros((H_B,), F32), v.astype(F32)]), 0,
                                   GATE_PAD).reshape(1, GATE_PAD)
    p = dict(
        norm_mix0=_row(norm_mix[0]), norm_ffn0=_row(norm_ffn[0]), norm_ffn1=_row(norm_ffn[1]),
        norm_final=_row(norm_final), ab_w_in=w_in,
        ab_w_out_a=ab_w_out[0][:D_A].astype(BF16), ab_w_out_b=ab_w_out[0][D_A:].astype(BF16),
        wg0=ff(ffn_w_gate, 0), wu0=ff(ffn_w_up, 0),
        wd0=ff(ffn_w_down, 0),
        wg1=ff(ffn_w_gate, 1), wu1=ff(ffn_w_up, 1),
        wd1=ff(ffn_w_down, 1),
        s5=dict(
            lr=_row(s5_lambda_re[0]), li=_row(s5_lambda_im[0]),
            ls=_row(jnp.repeat(s5_log_step[0], P_STATE)),
            bre=_block_diag_groups(jnp.swapaxes(s5_B_re[0], 1, 2), S5_GROUP, P_STATE).astype(F32),
            bim=_block_diag_groups(jnp.swapaxes(s5_B_im[0], 1, 2), S5_GROUP, P_STATE).astype(F32),
            cre=_block_diag_groups(jnp.swapaxes(s5_C_re[0], 1, 2), P_STATE, S5_GROUP).astype(BF16),
            cim=_block_diag_groups(jnp.swapaxes(s5_C_im[0], 1, 2), P_STATE, S5_GROUP).astype(BF16),
            d=_row(s5_D[0]), wglu=s5_w_glu[0].astype(BF16)),
        gdn=dict(conv_w=gdn_conv_w[0].astype(F32), alog=gate_lanes(gdn_A_log[0]),
                 dtb=gate_lanes(gdn_dt_bias[0]), norm_w=_row(gdn_norm_w[0])),
        rw=dict(
            norm_w=_row(norm_mix[1]), maa=rw_maa[0].astype(F32),
            wr=rw_w_r[0].astype(BF16), wk=rw_w_k[0].astype(BF16), wv=rw_w_v[0].astype(BF16),
            wo=rw_w_o[0].astype(BF16),
            w0=_row(rw_w0[0]), w1=_pad_to(rw_w1[0], 1, LANES).astype(BF16),
            w2=_pad_to(rw_w2[0], 0, LANES).astype(BF16),
            a0=_row(rw_a0[0]), a1=_pad_to(rw_a1[0], 1, LANES).astype(BF16),
            a2=_pad_to(rw_a2[0], 0, LANES).astype(BF16),
            g1=_pad_to(rw_g1[0], 1, 2 * LANES).astype(BF16), g2=_pad_to(rw_g2[0], 0, 2 * LANES).astype(BF16),
            k_k=_row(rw_k_k[0]), k_a=_row(rw_k_a[0]), r_k=_row(rw_r_k[0]),
            ln_w=_row(rw_ln_w[0]), ln_b=_row(rw_ln_b[0])),
    )

    zeros = lambda *shape: jnp.zeros(shape, F32)
    yp, p_hr, p_hi, p_gdn, p_conv, p_rw, p_shift = _trunk(
        x_prompt, bsz_p, seq_p, zeros(bsz_p, N_S5), zeros(bsz_p, N_S5), zeros(bsz_p, H_B, DK_B, DV_B),
        zeros(bsz_p, CONV_W - 1, D_QKV), zeros(bsz_p, H_C, N_HEAD_C, N_HEAD_C), zeros(bsz_p, D_MODEL), p,
        True)

    xs_tm = jnp.transpose(x_sample, (1, 0, 2)).reshape(1, seq_s * bsz_s, D_MODEL)
    ys, s_hr, s_hi, s_gdn, s_conv, s_rw, s_shift = _trunk(
        xs_tm, bsz_s, seq_s, state_s5_re[0].reshape(bsz_s, N_S5), state_s5_im[0].reshape(bsz_s, N_S5),
        state_gdn[0], state_gdn_conv[0], state_rwkv[0], state_rwkv_shift[0], p, True)
    y_sample = jnp.transpose(ys.reshape(seq_s, bsz_s, D_MODEL), (1, 0, 2))

    s5_shape = lambda b: (1, b, G_A, P_STATE)
    return (yp, y_sample,
            p_hr.reshape(s5_shape(bsz_p)), p_hi.reshape(s5_shape(bsz_p)), p_gdn[None], p_conv[None],
            p_rw[None], p_shift[None],
            s_hr.reshape(s5_shape(bsz_s)), s_hi.reshape(s5_shape(bsz_s)), s_gdn[None], s_conv[None],
            s_rw[None], s_shift[None])
```

```python
import functools
import math

import jax
import jax.numpy as jnp
from jax import lax
from jax.experimental import pallas as pl
from jax.experimental.pallas import tpu as pltpu

F32 = jnp.float32
BF16 = jnp.bfloat16

D_MODEL = 1024
D_A = 512
S5_GROUP = 16
G_A = 32
P_STATE = 64
N_S5 = G_A * P_STATE
D_B = 512
H_B = 4
DK_B = 128
DV_B = 128
D_QKV = 2 * H_B * DK_B + D_B
CONV_W = 4
GDN_CHUNK = 64
N_HEAD_C = 64
H_C = 16
RWKV_CHUNK = 64
RWKV_PAIR_GROUP = 8
RWKV_STAGGER = 7
RWKV_GN_EPS = 64e-5
N_RWKV_ACTS = 6
D_FF = 2816
NORM_EPS = 1e-6

LANES = 128
SUBLANES = 8
FF_TILE = 256
N_FF_TILES = D_FF // FF_TILE
GATE_PAD = LANES
D_IN_PAD = 4 * 512 + 512 + GATE_PAD
S5_LANE_BLOCK = 512
N_S5_BLOCKS = N_S5 // S5_LANE_BLOCK
TOKEN_TILE = 512
FFN_TOKEN_TILE = 512
RWKV_PROJ_TILE = 512
MIB = 1024 * 1024


def _cparams(semantics, vmem_mib):
    return pltpu.CompilerParams(dimension_semantics=semantics, vmem_limit_bytes=vmem_mib * MIB)


def _rmsnorm(x, w):
    return x * lax.rsqrt(jnp.mean(x * x, axis=-1, keepdims=True) + NORM_EPS) * w


def _sigmoid(x):
    return 0.5 + 0.5 * jnp.tanh(0.5 * x)


def _silu(x):
    half = 0.5 * x
    return half + half * jnp.tanh(half)


def _softplus(x):
    return jnp.maximum(x, 0.0) + jnp.log1p(jnp.exp(-jnp.abs(x)))


def _bdot(a, b):
    return jnp.dot(a.astype(BF16), b.astype(BF16), preferred_element_type=F32)


def _bdot_nt(a, b):
    return lax.dot_general(a.astype(BF16), b.astype(BF16), (((1,), (1,)), ((), ())),
                           preferred_element_type=F32)


def _bdot_tn(a, b):
    return lax.dot_general(a.astype(BF16), b.astype(BF16), (((0,), (0,)), ((), ())),
                           preferred_element_type=F32)


def _split(a):
    hi = a.astype(BF16)
    return hi, (a - hi.astype(F32)).astype(BF16)


def _split3(a):
    p1 = a.astype(BF16)
    r1 = a - p1.astype(F32)
    p2 = r1.astype(BF16)
    return p1, p2, (r1 - p2.astype(F32)).astype(BF16)


def _split_dot(a, exact_bf16):
    hi, lo = _split(a)
    return (jnp.dot(hi, exact_bf16, preferred_element_type=F32)
            + jnp.dot(lo, exact_bf16, preferred_element_type=F32))


def _cumsum_rows(tri_bf16, w, parts=3):
    split = _split3(w) if parts == 3 else _split(w)
    return sum(jnp.dot(tri_bf16, part, preferred_element_type=F32) for part in split)


def _iota2(shape):
    return (lax.broadcasted_iota(jnp.int32, shape, 0), lax.broadcasted_iota(jnp.int32, shape, 1))


def _run_staggered(gens, lag):
    pending, active, tick = list(gens), [], 0
    while pending or active:
        if pending and tick % lag == 0:
            active.append(pending.pop(0))
        for g in list(active):
            try:
                next(g)
            except StopIteration:
                active.remove(g)
        tick += 1


def _inv_identity_plus_wide(lmws, t):
    gen = _inv_identity_plus_wide_stages(lmws, t)
    while True:
        try:
            next(gen)
        except StopIteration as done:
            return done.value


def _inv_identity_plus_wide_stages(lmws, t):
    row, col = _iota2((t, 2 * t))
    left = col < t
    col = jnp.where(left, col, col - t)
    eye = (row == col).astype(F32)
    base = min(t, 16)

    def blockdiag(xw):
        return jnp.concatenate([jnp.where(left, xw, 0.0), jnp.where(left, 0.0, xw)], axis=0)

    if t > base:
        same_base = (row >> 4) == (col >> 4)
        ns = [jnp.where(same_base, -lmw, 0.0) for lmw in lmws]
    else:
        ns = [-lmw for lmw in lmws]
    xs = [eye + n for n in ns]
    ps = [_bdot(n, blockdiag(n)) for n in ns]
    yield
    k = 2
    while k < base:
        bds = [blockdiag(p) for p in ps]
        if 2 * k < base:
            both = [_bdot(jnp.concatenate([p, x], axis=0), bd) for p, x, bd in zip(ps, xs, bds)]
            ps = [r[:t] for r in both]
            xs = [x + r[t:] for x, r in zip(xs, both)]
        else:
            xs = [x + _bdot(x, bd) for x, bd in zip(xs, bds)]
        yield
        k *= 2
    shift = 4
    blk = base
    while blk < t:
        same_big = (row >> (shift + 1)) == (col >> (shift + 1))
        same_small = (row >> shift) == (col >> shift)
        off = [jnp.where(same_big, jnp.where(same_small, 0.0, lmw), 0.0) for lmw in lmws]
        mids = [_bdot(o, blockdiag(x)) for o, x in zip(off, xs)]
        yield
        xs = [x - _bdot(x, blockdiag(m)) for x, m in zip(xs, mids)]
        yield
        blk *= 2
        shift += 1
    return xs


def _ab_in_kernel(x_ref, nw_ref, w_ref, u_ref, qkv_ref, z_ref, ba_ref):
    h = _rmsnorm(x_ref[...], nw_ref[...])
    p = jnp.dot(h.astype(BF16), w_ref[...], preferred_element_type=F32)
    u_ref[...] = p[:, 0:D_A]
    qkv_ref[...] = p[:, D_A:D_A + D_QKV]
    z_ref[...] = p[:, D_A + D_QKV:D_A + D_QKV + D_B]
    ba_ref[...] = p[:, D_A + D_QKV + D_B:D_IN_PAD]


def _ab_in(x_bm, norm_w, w_in):
    bv, lv, _ = x_bm.shape
    tl = min(TOKEN_TILE, lv)
    grid = (bv, lv // tl)
    tm = lambda n: pl.BlockSpec((tl, n), lambda b, i: (i, b))
    const = lambda shape: pl.BlockSpec(shape, lambda b, i: (0,) * len(shape))
    return pl.pallas_call(
        _ab_in_kernel,
        grid=grid,
        in_specs=[pl.BlockSpec((None, tl, D_MODEL), lambda b, i: (b, i, 0)),
                  const((1, D_MODEL)), const((D_MODEL, D_IN_PAD))],
        out_specs=[tm(D_A), tm(D_QKV), tm(D_B), tm(GATE_PAD)],
        out_shape=[jax.ShapeDtypeStruct((lv, bv * D_A), F32),
                   jax.ShapeDtypeStruct((lv, bv * D_QKV), F32),
                   jax.ShapeDtypeStruct((lv, bv * D_B), F32),
                   jax.ShapeDtypeStruct((lv, bv * GATE_PAD), F32)],
        compiler_params=_cparams(("parallel", "parallel"), 48),
    )(x_bm, norm_w, w_in)


def _s5_kernel(u_ref, h0r_ref, h0i_ref, lr_ref, li_ref, ls_ref, bre_ref, bim_ref, cre_ref, cim_ref,
               d_ref, wglu_ref, y_ref, hr_out, hi_out,
               ar_s, ai_s, bbr_s, bbi_s, hr_s, hi_s, bur_s, bui_s, yg_s, *, tc, seq_on_lanes):
    rows = tc * SUBLANES

    @pl.when(pl.program_id(1) == 0)
    def _():
        lr = lr_ref[...]
        li = li_ref[...]
        dt = jnp.exp(ls_ref[...])
        mag = jnp.exp(lr * dt)
        ar = mag * jnp.cos(li * dt)
        ai = mag * jnp.sin(li * dt)
        den = lr * lr + li * li
        nr = ar - 1.0
        cr = (nr * lr + ai * li) / den
        ci = (ai * lr - nr * li) / den
        ar_s[...] = jnp.broadcast_to(ar, (SUBLANES, N_S5))
        ai_s[...] = jnp.broadcast_to(ai, (SUBLANES, N_S5))
        for j in range(N_S5_BLOCKS):
            sl = slice(j * S5_LANE_BLOCK, (j + 1) * S5_LANE_BLOCK)
            bbr_s[j] = (cr[:, sl] * bre_ref[j] - ci[:, sl] * bim_ref[j]).astype(BF16)
            bbi_s[j] = (cr[:, sl] * bim_ref[j] + ci[:, sl] * bre_ref[j]).astype(BF16)
        hr_s[...] = h0r_ref[...]
        hi_s[...] = h0i_ref[...]

    if seq_on_lanes:
        per_seq = jnp.stack([u_ref[:, b * D_A:(b + 1) * D_A] for b in range(SUBLANES)], axis=0)
        u = jnp.swapaxes(per_seq, 0, 1).reshape(rows, D_A)
    else:
        u = u_ref[...].reshape(rows, D_A)
    ub = u.astype(BF16)
    for j in range(N_S5_BLOCKS):
        sl = slice(j * S5_LANE_BLOCK, (j + 1) * S5_LANE_BLOCK)
        uj = ub[:, j * LANES:(j + 1) * LANES]
        bur_s[:, :, sl] = jnp.dot(uj, bbr_s[j], preferred_element_type=F32).reshape(tc, SUBLANES, S5_LANE_BLOCK)
        bui_s[:, :, sl] = jnp.dot(uj, bbi_s[j], preferred_element_type=F32).reshape(tc, SUBLANES, S5_LANE_BLOCK)

    for j in range(N_S5_BLOCKS):
        sl = slice(j * S5_LANE_BLOCK, (j + 1) * S5_LANE_BLOCK)
        ar = ar_s[:, sl]
        ai = ai_s[:, sl]

        def step(t, carry, sl=sl, ar=ar, ai=ai):
            hr, hi = carry
            nr = ar * hr - ai * hi + bur_s[t, :, sl]
            ni = ar * hi + ai * hr + bui_s[t, :, sl]
            bur_s[t, :, sl] = nr
            bui_s[t, :, sl] = ni
            return nr, ni

        hr, hi = lax.fori_loop(0, tc, step, (hr_s[:, sl], hi_s[:, sl]), unroll=min(tc, 8))
        hr_s[:, sl] = hr
        hi_s[:, sl] = hi

    for j in range(N_S5_BLOCKS):
        sl = slice(j * S5_LANE_BLOCK, (j + 1) * S5_LANE_BLOCK)
        cl = slice(j * LANES, (j + 1) * LANES)
        xr = bur_s[:, :, sl].reshape(rows, S5_LANE_BLOCK).astype(BF16)
        xi = bui_s[:, :, sl].reshape(rows, S5_LANE_BLOCK).astype(BF16)
        yj = (jnp.dot(xr, cre_ref[j], preferred_element_type=F32)
              - jnp.dot(xi, cim_ref[j], preferred_element_type=F32)
              + d_ref[:, cl] * u[:, cl])
        yg_s[:, cl] = jax.nn.gelu(yj)

    yg = yg_s[...]
    out = yg * _sigmoid(jnp.dot(yg.astype(BF16), wglu_ref[...], preferred_element_type=F32))
    if seq_on_lanes:
        per_seq = jnp.swapaxes(out.reshape(tc, SUBLANES, D_A), 0, 1)
        for b in range(SUBLANES):
            y_ref[:, b * D_A:(b + 1) * D_A] = per_seq[b]
    else:
        y_ref[...] = out.reshape(tc, SUBLANES, D_A)
    hr_out[...] = hr_s[...]
    hi_out[...] = hi_s[...]


def _s5(u_tm, h0r, h0i, prm, tc):
    seq_on_lanes = u_tm.ndim == 2
    seq, bsz = u_tm.shape[0], h0r.shape[0]
    grid = (bsz // SUBLANES, seq // tc)
    const = lambda shape: pl.BlockSpec(shape, lambda b, i: (0,) * len(shape))
    if seq_on_lanes:
        act = pl.BlockSpec((tc, SUBLANES * D_A), lambda b, i: (i, b))
    else:
        act = pl.BlockSpec((tc, SUBLANES, D_A), lambda b, i: (i, b, 0))
    st = pl.BlockSpec((SUBLANES, N_S5), lambda b, i: (b, 0))
    vec = const((1, N_S5))
    small = pltpu.VMEM((SUBLANES, N_S5), F32)
    big = pltpu.VMEM((tc, SUBLANES, N_S5), F32)
    return pl.pallas_call(
        functools.partial(_s5_kernel, tc=tc, seq_on_lanes=seq_on_lanes),
        grid=grid,
        in_specs=[act, st, st, vec, vec, vec,
                  const((N_S5_BLOCKS, LANES, S5_LANE_BLOCK)), const((N_S5_BLOCKS, LANES, S5_LANE_BLOCK)),
                  const((N_S5_BLOCKS, S5_LANE_BLOCK, LANES)), const((N_S5_BLOCKS, S5_LANE_BLOCK, LANES)),
                  const((1, D_A)), const((D_A, D_A))],
        out_specs=[act, st, st],
        out_shape=[jax.ShapeDtypeStruct(u_tm.shape, F32),
                   jax.ShapeDtypeStruct((bsz, N_S5), F32),
                   jax.ShapeDtypeStruct((bsz, N_S5), F32)],
        scratch_shapes=[small, small, pltpu.VMEM((N_S5_BLOCKS, LANES, S5_LANE_BLOCK), BF16),
                        pltpu.VMEM((N_S5_BLOCKS, LANES, S5_LANE_BLOCK), BF16), small, small, big, big,
                        pltpu.VMEM((tc * SUBLANES, D_A), F32)],
        compiler_params=_cparams(("parallel", "arbitrary"), 48),
    )(u_tm, h0r, h0i, prm["lr"], prm["li"], prm["ls"], prm["bre"], prm["bim"], prm["cre"], prm["cim"],
      prm["d"], prm["wglu"])


def _gdn_kernel(qkv_ref, ba_ref, z_ref, conv0_ref, s0_ref, cw_ref, alog_ref, dtb_ref, nw_ref,
                o_ref, sfin_ref, convn_ref, xbuf, s_s, *, t, nb, seq_on_sublanes):
    if seq_on_sublanes:
        qkv_seq = jnp.swapaxes(qkv_ref[...], 0, 1)
        ba_seq = jnp.swapaxes(ba_ref[...], 0, 1)
        z_seq = jnp.swapaxes(z_ref[...], 0, 1)
        read_qkv = lambda e: qkv_seq[e]
        read_ba = lambda e: ba_seq[e]
        read_z = lambda e, h: z_seq[e][:, h * DV_B:(h + 1) * DV_B]
    else:
        read_qkv = lambda e: qkv_ref[:, e * D_QKV:(e + 1) * D_QKV]
        read_ba = lambda e: ba_ref[:, e * GATE_PAD:(e + 1) * GATE_PAD]
        read_z = lambda e, h: z_ref[:, e * D_B + h * DV_B:e * D_B + (h + 1) * DV_B]
    o_parts = {}
    pad = SUBLANES
    hist = CONV_W - 1

    @pl.when(pl.program_id(1) == 0)
    def _():
        for e in range(nb):
            xbuf[pad - hist:pad, e * D_QKV:(e + 1) * D_QKV] = conv0_ref[e]
        s_s[...] = s0_ref[...].reshape(nb * H_B, DK_B, DV_B)

    row, col = _iota2((t, t))
    causal = row >= col
    strict = row > col
    tri = causal.astype(BF16)
    lane = lax.broadcasted_iota(jnp.int32, (t, LANES), 1)
    nw = nw_ref[...]
    cw = cw_ref[...]
    ones_sq = jnp.ones((DK_B, DK_B), BF16)
    zeros_wide = jnp.zeros((t, DK_B + DV_B), F32)
    zeros_v = jnp.zeros((t, DV_B), F32)

    def seq_group(es):
        ys, betas, gcs = [], [], []
        for e in es:
            cols = slice(e * D_QKV, (e + 1) * D_QKV)
            x = read_qkv(e)
            xbuf[pad:pad + t, cols] = x
            acc = x * cw[hist:hist + 1, :]
            for j in range(hist):
                acc = acc + xbuf[pad - hist + j:pad - hist + j + t, cols] * cw[j:j + 1, :]
            last = xbuf[pad + t - hist:pad + t, cols]
            convn_ref[e] = last
            xbuf[pad - hist:pad, cols] = last
            ys.append(_silu(acc))
            ba = read_ba(e)
            betas.append(_sigmoid(ba))
            g = -jnp.exp(alog_ref[...]) * _softplus(ba + dtb_ref[...])
            gcs.append(_cumsum_rows(tri, g))
            yield

        chains = [(i, h) for i in range(len(es)) for h in range(H_B)]
        heads = range(len(chains))
        bcol = [betas[i][:, h:h + 1] for i, h in chains]
        gcol = [gcs[i][:, H_B + h:H_B + h + 1] for i, h in chains]
        y_of = [ys[i] for i, h in chains]
        h_of = [h for i, h in chains]
        decay = []
        gc_parts = [[p.astype(F32) for p in _split3(gc)] for gc in gcs]
        for i, hh in chains:
            p1, p2, p3 = [p[:, H_B + hh:H_B + hh + 1] for p in gc_parts[i]]
            dl = jnp.where(lane == 0, p1, jnp.where(lane == 1, p2, jnp.where(lane == 2, p3,
                                                                             jnp.where(lane < 6, 1.0, 0.0))))
            dr = jnp.where(lane < 3, 1.0, jnp.where(lane == 3, -p1, jnp.where(lane == 4, -p2,
                                                                            jnp.where(lane == 5, -p3, 0.0))))
            decay.append(jnp.exp(jnp.where(causal, _bdot_nt(dl, dr), -jnp.inf)))
        eg = [jnp.exp(gcol[h]) for h in heads]
        glast = [gcol[h][t - 1:t, :] for h in heads]
        yield
        q = [y_of[c][:, h_of[c] * DK_B:(h_of[c] + 1) * DK_B] for c in heads]
        k = [y_of[c][:, (H_B + h_of[c]) * DK_B:(H_B + h_of[c] + 1) * DK_B] for c in heads]
        v = [y_of[c][:, 2 * H_B * DK_B + h_of[c] * DV_B:2 * H_B * DK_B + (h_of[c] + 1) * DV_B] for c in heads]
        sq = _bdot(jnp.concatenate([x * x for x in q + k], axis=0), ones_sq)
        nq = len(q)
        q = [x * (lax.rsqrt(sq[i * t:(i + 1) * t] + NORM_EPS) * (DK_B ** -0.5)) for i, x in enumerate(q)]
        k = [x * lax.rsqrt(sq[(nq + i) * t:(nq + i + 1) * t] + NORM_EPS) for i, x in enumerate(k)]
        kb = [k[h] * bcol[h] for h in heads]
        vb = [v[h] * bcol[h] for h in heads]
        yield
        lm = [jnp.where(strict, _bdot_nt(kb[h], k[h]) * decay[h], 0.0) for h in heads]
        attn = [_bdot_nt(q[h], k[h]) * decay[h] for h in heads]
        s = [s_s[es[i] * H_B + h] for i, h in chains]
        qs = [_bdot(q[h] * eg[h], s[h]) for h in heads]
        yield
        pairs = [(c, c + 1) for c in range(0, len(chains), 2)]
        tinv = yield from _inv_identity_plus_wide_stages(
            [jnp.concatenate([lm[c0], lm[c1]], axis=1) for c0, c1 in pairs], t)
        kbg = [kb[h] * eg[h] for h in heads]
        uw = [_bdot(tinv[i], jnp.concatenate(
            [jnp.concatenate([vb[c0], kbg[c0], zeros_wide], axis=1),
             jnp.concatenate([zeros_wide, vb[c1], kbg[c1]], axis=1)], axis=0)) for i, (c0, c1) in enumerate(pairs)]
        u = [uw[c // 2][:, (c % 2) * (DK_B + DV_B):(c % 2) * (DK_B + DV_B) + DV_B] for c in heads]
        w = [uw[c // 2][:, (c % 2) * (DK_B + DV_B) + DV_B:(c % 2 + 1) * (DK_B + DV_B)] for c in heads]
        yield
        v_new = [u[h] - _bdot(w[h], s[h]) for h in heads]
        yield
        av = [_bdot(jnp.concatenate([attn[c0], attn[c1]], axis=1), jnp.concatenate(
            [jnp.concatenate([v_new[c0], zeros_v], axis=1),
             jnp.concatenate([zeros_v, v_new[c1]], axis=1)], axis=0)) for c0, c1 in pairs]
        o = [qs[c] + av[c // 2][:, (c % 2) * DV_B:(c % 2 + 1) * DV_B] for c in heads]
        upd = [_bdot_tn(k[h] * jnp.exp(glast[h] - gcol[h]), v_new[h]) for h in heads]
        yield
        for c, (i, h) in enumerate(chains):
            e = es[i]
            s_new = s[c] * jnp.exp(glast[c]) + upd[c]
            s_s[e * H_B + h] = s_new
            sfin_ref[e, h] = s_new
            o_parts[e, h] = _rmsnorm(o[c], nw) * _silu(read_z(e, h))
            if not seq_on_sublanes:
                o_ref[:, e * D_B + h * DV_B:e * D_B + (h + 1) * DV_B] = o_parts[e, h]

    _run_staggered([seq_group(list(range(nb)))], 1)
    if seq_on_sublanes:
        per_seq = [jnp.concatenate([o_parts[e, h] for h in range(H_B)], axis=1) for e in range(nb)]
        o_ref[...] = jnp.swapaxes(jnp.stack(per_seq, axis=0), 0, 1)


def _gdn(qkv_tm, ba_tm, z_tm, conv0, s0, prm, bsz, t, nb):
    seq = qkv_tm.shape[0]
    seq_on_sublanes = qkv_tm.ndim == 3
    grid = (bsz // nb, seq // t)
    const = lambda shape: pl.BlockSpec(shape, lambda b, i: (0,) * len(shape))
    if seq_on_sublanes:
        tm = lambda n: pl.BlockSpec((t, nb, n), lambda b, i: (i, b, 0))
        out_shape = (seq, bsz, D_B)
    else:
        tm = lambda n: pl.BlockSpec((t, nb * n), lambda b, i: (i, b))
        out_shape = (seq, bsz * D_B)
    state = pl.BlockSpec((nb, H_B, DK_B, DV_B), lambda b, i: (b, 0, 0, 0))
    conv = pl.BlockSpec((nb, CONV_W - 1, D_QKV), lambda b, i: (b, 0, 0))
    return pl.pallas_call(
        functools.partial(_gdn_kernel, t=t, nb=nb, seq_on_sublanes=seq_on_sublanes),
        grid=grid,
        in_specs=[tm(D_QKV), tm(GATE_PAD), tm(D_B), conv, state,
                  const((CONV_W, D_QKV)), const((1, GATE_PAD)), const((1, GATE_PAD)), const((1, DV_B))],
        out_specs=[tm(D_B), state, conv],
        out_shape=[jax.ShapeDtypeStruct(out_shape, F32),
                   jax.ShapeDtypeStruct((bsz, H_B, DK_B, DV_B), F32),
                   jax.ShapeDtypeStruct((bsz, CONV_W - 1, D_QKV), F32)],
        scratch_shapes=[pltpu.VMEM((t + SUBLANES, nb * D_QKV), F32),
                        pltpu.VMEM((nb * H_B, DK_B, DV_B), F32)],
        compiler_params=_cparams(("parallel", "arbitrary"), 48),
    )(qkv_tm, ba_tm, z_tm, conv0, s0, prm["conv_w"], prm["alog"], prm["dtb"], prm["norm_w"])


def _mix_ffn_kernel(*refs, n_mix, final_norm):
    x_ref = refs[0]
    mix_refs = refs[1:1 + n_mix]
    wout_refs = refs[1 + n_mix:1 + 2 * n_mix]
    nffn_ref, wg_ref, wu_ref, wd_ref = refs[1 + 2 * n_mix:5 + 2 * n_mix]
    rest = refs[5 + 2 * n_mix:]
    if final_norm:
        nfin_ref, out_ref = rest
    else:
        (out_ref,) = rest

    x1 = x_ref[...]
    for m_ref, w_ref in zip(mix_refs, wout_refs):
        x1 = x1 + jnp.dot(m_ref[...].astype(BF16), w_ref[...], preferred_element_type=F32)
    h = _rmsnorm(x1, nffn_ref[...]).astype(BF16)
    out_ref[...] = x1

    for c in range(N_FF_TILES):
        cols = slice(c * FF_TILE, (c + 1) * FF_TILE)
        gate = jnp.dot(h, wg_ref[:, cols], preferred_element_type=F32)
        up = jnp.dot(h, wu_ref[:, cols], preferred_element_type=F32)
        act = (_silu(gate) * up).astype(BF16)
        out_ref[...] += jnp.dot(act, wd_ref[cols, :], preferred_element_type=F32)
    if final_norm:
        out_ref[...] = _rmsnorm(out_ref[...], nfin_ref[...])


def _mix_ffn(x, x_bm, mixes, wouts, nffn, wg, wu, wd, nfin, bv, lv, out_bm):
    tl = min(FFN_TOKEN_TILE, lv)
    grid = (bv, lv // tl)
    const = lambda shape: pl.BlockSpec(shape, lambda b, i: (0,) * len(shape),
                                       pipeline_mode=pl.Buffered(1))
    tm = lambda n: pl.BlockSpec((tl, n), lambda b, i: (i, b))
    bm = pl.BlockSpec((None, tl, D_MODEL), lambda b, i: (b, i, 0))
    in_specs = [bm if x_bm else tm(D_MODEL)]
    in_specs += [tm(m.shape[1] // bv) for m in mixes]
    in_specs += [const(w.shape) for w in wouts]
    in_specs += [const((1, D_MODEL)), const(wg.shape), const(wu.shape), const(wd.shape)]
    args = [x, *mixes, *wouts, nffn, wg, wu, wd]
    if nfin is not None:
        in_specs.append(const((1, D_MODEL)))
        args.append(nfin)
    if out_bm:
        out_spec, out_shape = bm, jax.ShapeDtypeStruct((bv, lv, D_MODEL), F32)
    else:
        out_spec, out_shape = tm(D_MODEL), jax.ShapeDtypeStruct((lv, bv * D_MODEL), F32)
    return pl.pallas_call(
        functools.partial(_mix_ffn_kernel, n_mix=len(mixes), final_norm=nfin is not None),
        grid=grid,
        in_specs=in_specs,
        out_specs=out_spec,
        out_shape=out_shape,
        compiler_params=_cparams(("parallel", "parallel"), 56),
    )(*args)


def _rwkv_proj_kernel(x_ref, shift0_ref, nw_ref, maa_ref, wr_ref, wk_ref, wv_ref, w0_ref, w1_ref, w2_ref,
                      a0_ref, a1_ref, a2_ref, g1_ref, g2_ref,
                      r_out, w_out, k_out, v_out, a_out, g_out, shift_out, hbuf, *, shift):
    pad = -(-shift // SUBLANES) * SUBLANES

    @pl.when(pl.program_id(1) == 0)
    def _():
        hbuf[pad - shift:pad, :] = shift0_ref[...]

    h = _rmsnorm(x_ref[...], nw_ref[...])
    rows = h.shape[0]
    hbuf[pad:pad + rows, :] = h
    prev = hbuf[pad - shift:pad - shift + rows, :]
    last = h[rows - shift:]
    hbuf[pad - shift:pad, :] = last
    shift_out[...] = last
    xx = prev - h
    maa = maa_ref[...]
    xr, xw, xk, xv, xa, xg = [h + xx * maa[j:j + 1, :] for j in range(6)]
    r = _bdot(xr, wr_ref[...])
    k = _bdot(xk, wk_ref[...])
    v = _bdot(xv, wv_ref[...])
    u = w0_ref[...] + _bdot(jnp.tanh(_bdot(xw, w1_ref[...])), w2_ref[...])
    a = _sigmoid(a0_ref[...] + _bdot(_bdot(xa, a1_ref[...]), a2_ref[...]))
    g = _bdot(_sigmoid(_bdot(xg, g1_ref[...])), g2_ref[...])
    r_out[...] = r
    w_out[...] = -math.exp(-0.5) * _sigmoid(u)
    k_out[...] = k
    v_out[...] = v
    a_out[...] = a
    g_out[...] = g


def _rwkv_proj(x_tm, shift0, prm):
    bv, shift, _ = shift0.shape
    lv = x_tm.shape[0]
    tr = max(shift, min(RWKV_PROJ_TILE, lv))
    grid = (bv, lv // tr)
    const = lambda shape: pl.BlockSpec(shape, lambda b, i: (0,) * len(shape),
                                       pipeline_mode=pl.Buffered(1))
    act = pl.BlockSpec((tr, D_MODEL), lambda b, i: (i, b))
    carry = pl.BlockSpec((None, shift, D_MODEL), lambda b, i: (b, 0, 0))
    vec = const((1, D_MODEL))
    weights = [prm[n] for n in ("wr", "wk", "wv")]
    in_specs = [act, carry, vec, const((6, D_MODEL))]
    in_specs += [const(w.shape) for w in weights]
    in_specs += [vec, const(prm["w1"].shape), const(prm["w2"].shape),
                 vec, const(prm["a1"].shape), const(prm["a2"].shape),
                 const(prm["g1"].shape), const(prm["g2"].shape)]
    out_act = jax.ShapeDtypeStruct((lv, bv * D_MODEL), F32)
    pad = -(-shift // SUBLANES) * SUBLANES
    return pl.pallas_call(
        functools.partial(_rwkv_proj_kernel, shift=shift),
        grid=grid,
        in_specs=in_specs,
        out_specs=[act] * N_RWKV_ACTS + [carry],
        out_shape=[out_act] * N_RWKV_ACTS + [jax.ShapeDtypeStruct((bv, shift, D_MODEL), F32)],
        scratch_shapes=[pltpu.VMEM((pad + tr, D_MODEL), F32)],
        compiler_params=_cparams(("parallel", "arbitrary"), 56),
    )(x_tm, shift0, prm["norm_w"], prm["maa"], prm["wr"], prm["wk"], prm["wv"], prm["w0"], prm["w1"],
      prm["w2"], prm["a0"], prm["a1"], prm["a2"], prm["g1"], prm["g2"])


def _rwkv_scan_kernel(r_ref, w_ref, k_ref, v_ref, a_ref, g_ref, s0_ref, kkw_ref, kaw_ref, rk_ref, lnw_ref, lnb_ref,
                      y_ref, sfin_ref, s_s, *, t, nb, group, seq_on_sublanes):
    if seq_on_sublanes:
        act_refs = (r_ref, w_ref, k_ref, v_ref, a_ref, g_ref)
        per_seq = [jnp.swapaxes(ref[...], 0, 1) for ref in act_refs]

        def slab(ref, jj):
            e, j = divmod(jj, H_C // 2)
            which = [i for i, x in enumerate(act_refs) if x is ref][0]
            return per_seq[which][e][:, j * LANES:(j + 1) * LANES]
    else:
        slab = lambda ref, jj: ref[:, jj * LANES:(jj + 1) * LANES]
    y_parts = {}
    n = N_HEAD_C
    zero_blk = jnp.zeros((n, n), F32)

    pairs_per_seq = H_C // 2

    @pl.when(pl.program_id(1) == 0)
    def _():
        for jj in range(nb * pairs_per_seq):
            e, j = divmod(jj, pairs_per_seq)
            top = jnp.concatenate([s0_ref[e, 2 * j], zero_blk], axis=1)
            bot = jnp.concatenate([zero_blk, s0_ref[e, 2 * j + 1]], axis=1)
            s_s[jj] = jnp.concatenate([top, bot], axis=0)

    row, col = _iota2((t, t))
    tri = (row >= col).astype(BF16)
    roww, colw = _iota2((t, 2 * t))
    colw = jnp.where(colw >= t, colw - t, colw)
    incl_w = roww >= colw
    strict_w = roww > colw
    lane = lax.broadcasted_iota(jnp.int32, (t, LANES), 1)
    head0 = lane < n

    def by_head(x):
        return jnp.concatenate([jnp.where(head0, x, 0.0), jnp.where(head0, 0.0, x)], axis=0)

    prow, pcol = _iota2((LANES, LANES))
    same_head = (prow >> 6) == (pcol >> 6)
    ones_blk = same_head.astype(BF16)

    def head_sums(xs, dot):
        tot = dot(jnp.concatenate(xs, axis=0), ones_blk)
        return [tot[i * t:(i + 1) * t] for i in range(len(xs))]

    def pair_group(js):
        ps = range(len(js))
        sl = [slice(j * LANES, (j + 1) * LANES) for j in js]
        psl = [slice((j % pairs_per_seq) * LANES, (j % pairs_per_seq + 1) * LANES) for j in js]
        r = [slab(r_ref, j) for j in js]
        w = [slab(w_ref, j) for j in js]
        kraw = [slab(k_ref, j) for j in js]
        v = [slab(v_ref, j) for j in js]
        a = [slab(a_ref, j) for j in js]
        kkr = [kraw[p] * kkw_ref[:, psl[p]] for p in ps]
        k = [kraw[p] * (1.0 + (a[p] - 1.0) * kaw_ref[:, psl[p]]) for p in ps]
        kk = [x * lax.rsqrt(ss + NORM_EPS) for x, ss in zip(kkr, head_sums([x * x for x in kkr], _bdot))]
        bv = [kk[p] * a[p] for p in ps]
        yield
        gc = [_cumsum_rows(tri, x, parts=2) for x in w]
        glast = [x[t - 1:t, :] for x in gc]
        yield
        pinv = [jnp.exp(-x) for x in gc]
        at = [-kk[p] * jnp.exp(gc[p] - w[p]) for p in ps]
        bt = [bv[p] * pinv[p] for p in ps]
        kt = [k[p] * pinv[p] for p in ps]
        rt = [r[p] * jnp.exp(gc[p]) for p in ps]
        rem = [jnp.exp(glast[p] - gc[p]) for p in ps]
        yield
        lhs = [jnp.concatenate([at[p], rt[p]], axis=0) for p in ps]
        gm = [_bdot_nt(lhs[p], jnp.concatenate([by_head(bt[p]), by_head(kt[p])], axis=0)) for p in ps]
        yield
        s = [s_s[j] for j in js]
        ars = [_bdot_nt(lhs[p], s[p]) for p in ps]
        v2 = [by_head(x) for x in v]
        yield
        aab = [jnp.where(strict_w, gm[p][:t, :2 * t], 0.0) for p in ps]
        aak = [jnp.where(strict_w, gm[p][:t, 2 * t:], 0.0) for p in ps]
        rkm = [jnp.where(incl_w, gm[p][t:, 2 * t:], 0.0) for p in ps]
        rb = [jnp.where(incl_w, gm[p][t:, :2 * t], 0.0) for p in ps]
        yield
        kv = [_bdot(jnp.concatenate([aak[p], rkm[p]], axis=0), v2[p]) for p in ps]
        rhs = [ars[p][:t] + kv[p][:t] for p in ps]
        yield
        tinv = yield from _inv_identity_plus_wide_stages([-x for x in aab], t)
        u = [_bdot(tinv[p], by_head(rhs[p])) for p in ps]
        yield
        y = [ars[p][t:] + _bdot(rb[p], by_head(u[p])) + kv[p][t:] for p in ps]
        yield
        upd = [_bdot_tn(jnp.concatenate([u[p], v[p]], axis=0),
                        jnp.concatenate([bv[p] * rem[p], k[p] * rem[p]], axis=0)) for p in ps]
        for p, j in enumerate(js):
            s_s[j] = s[p] * jnp.exp(glast[p]) + jnp.where(same_head, upd[p], 0.0)
        yield
        mu = head_sums(y, _split_dot)
        d = [y[p] - mu[p] * (1.0 / n) for p in ps]
        yield
        var = head_sums([x * x for x in d], _bdot)
        yield
        bonus = head_sums([r[p] * k[p] * rk_ref[:, psl[p]] for p in ps], _split_dot)
        yield
        for p in ps:
            yn = d[p] * lax.rsqrt(var[p] * (1.0 / n) + RWKV_GN_EPS) * lnw_ref[:, psl[p]] + lnb_ref[:, psl[p]]
            y_parts[js[p]] = (yn + bonus[p] * v[p]) * slab(g_ref, js[p])
            if not seq_on_sublanes:
                y_ref[:, sl[p]] = y_parts[js[p]]

    groups = [list(range(first, first + group)) for first in range(0, nb * pairs_per_seq, group)]
    _run_staggered([pair_group(js) for js in groups], RWKV_STAGGER)
    if seq_on_sublanes:
        rows = [jnp.concatenate([y_parts[e * pairs_per_seq + j] for j in range(pairs_per_seq)], axis=1)
                for e in range(nb)]
        y_ref[...] = jnp.swapaxes(jnp.stack(rows, axis=0), 0, 1)

    for jj in range(nb * pairs_per_seq):
        e, j = divmod(jj, pairs_per_seq)
        sp = s_s[jj]
        sfin_ref[e, 2 * j] = sp[:n, :n]
        sfin_ref[e, 2 * j + 1] = sp[n:, n:]


def _rwkv_scan(acts, s0, prm, bsz, t, nb, group):
    seq = acts[0].shape[0]
    seq_on_sublanes = acts[0].ndim == 3
    grid = (bsz // nb, seq // t)
    const = lambda shape: pl.BlockSpec(shape, lambda b, i: (0,) * len(shape))
    if seq_on_sublanes:
        tm = pl.BlockSpec((t, nb, D_MODEL), lambda b, i: (i, b, 0))
    else:
        tm = pl.BlockSpec((t, nb * D_MODEL), lambda b, i: (i, b))
    st = pl.BlockSpec((nb, H_C, N_HEAD_C, N_HEAD_C), lambda b, i: (b, 0, 0, 0))
    vec = const((1, D_MODEL))
    return pl.pallas_call(
        functools.partial(_rwkv_scan_kernel, t=t, nb=nb, group=group, seq_on_sublanes=seq_on_sublanes),
        grid=grid,
        in_specs=[tm] * N_RWKV_ACTS + [st, vec, vec, vec, vec, vec],
        out_specs=[tm, st],
        out_shape=[jax.ShapeDtypeStruct(acts[0].shape, F32),
                   jax.ShapeDtypeStruct((bsz, H_C, N_HEAD_C, N_HEAD_C), F32)],
        scratch_shapes=[pltpu.VMEM((nb * H_C // 2, LANES, LANES), F32)],
        compiler_params=_cparams(("parallel", "arbitrary"), 48),
    )(*acts, s0, prm["k_k"], prm["k_a"], prm["r_k"], prm["ln_w"], prm["ln_b"])


def _block_diag_groups(w, rows_per_group, cols_per_group):
    g = w.shape[0]
    w = w.reshape(g // 8, 8, rows_per_group, cols_per_group)
    eye = jnp.eye(8, dtype=w.dtype)
    out = jnp.einsum("jgrc,gh->jgrhc", w, eye)
    return out.reshape(g // 8, 8 * rows_per_group, 8 * cols_per_group)


def _pad_to(w, axis, size):
    pad = [(0, 0)] * w.ndim
    pad[axis] = (0, size - w.shape[axis])
    return jnp.pad(w, pad)


def _row(v):
    return v.reshape(1, -1).astype(F32)


def _recurrence_tiling(seq):
    if seq >= GDN_CHUNK:
        return dict(s5_tc=128, gdn_t=GDN_CHUNK, gdn_nb=4, rwkv_t=RWKV_CHUNK, rwkv_nb=4,
                    rwkv_group=2 * RWKV_PAIR_GROUP)
    return dict(s5_tc=seq, gdn_t=seq, gdn_nb=SUBLANES, rwkv_t=seq, rwkv_nb=SUBLANES,
                rwkv_group=2 * RWKV_PAIR_GROUP)


def _trunk(x_bm, bsz, seq, s5_re0, s5_im0, gdn_s0, gdn_conv0, rw_s0, rw_shift0, p, out_bm):
    bv, lv, _ = x_bm.shape
    cfg = _recurrence_tiling(seq)
    u, qkv, z, ba = _ab_in(x_bm, p["norm_mix0"], p["ab_w_in"])
    view = (lambda a, n: a.reshape(seq, bsz, n)) if bv == 1 else (lambda a, n: a)
    y_a, hr, hi = _s5(view(u, D_A), s5_re0, s5_im0, p["s5"], cfg["s5_tc"])
    y_b, gdn_s, gdn_conv = _gdn(view(qkv, D_QKV), view(ba, GATE_PAD), view(z, D_B), gdn_conv0, gdn_s0,
                                p["gdn"], bsz, cfg["gdn_t"], cfg["gdn_nb"])
    x1 = _mix_ffn(x_bm, True, [y_a.reshape(lv, bv * D_A), y_b.reshape(lv, bv * D_B)],
                  [p["ab_w_out_a"], p["ab_w_out_b"]], p["norm_ffn0"], p["wg0"], p["wu0"], p["wd0"],
                  None, bv, lv, False)
    acts = _rwkv_proj(x1, rw_shift0.reshape(bv, bsz // bv, D_MODEL), p["rw"])
    shift = acts[N_RWKV_ACTS].reshape(bsz, D_MODEL)
    yg, rw_s = _rwkv_scan([view(a, D_MODEL) for a in acts[:N_RWKV_ACTS]], rw_s0, p["rw"], bsz,
                          cfg["rwkv_t"], cfg["rwkv_nb"], cfg["rwkv_group"])
    y = _mix_ffn(x1, False, [yg.reshape(lv, bv * D_MODEL)], [p["rw"]["wo"]], p["norm_ffn1"], p["wg1"],
                 p["wu1"], p["wd1"], p["norm_final"], bv, lv, out_bm)
    return y, hr, hi, gdn_s, gdn_conv, rw_s, shift


def kernel(x_prompt, x_sample, state_s5_re, state_s5_im, state_gdn, state_gdn_conv, state_rwkv, state_rwkv_shift, norm_mix, norm_ffn, norm_final, ffn_w_gate, ffn_w_up, ffn_w_down, ab_w_in, ab_w_out, s5_lambda_re, s5_lambda_im, s5_log_step, s5_B_re, s5_B_im, s5_C_re, s5_C_im, s5_D, s5_w_glu, gdn_conv_w, gdn_A_log, gdn_dt_bias, gdn_norm_w, rw_maa, rw_w_r, rw_w_k, rw_w_v, rw_w_o, rw_w0, rw_w1, rw_w2, rw_a0, rw_a1, rw_a2, rw_g1, rw_g2, rw_k_k, rw_k_a, rw_r_k, rw_ln_w, rw_ln_b):
    bsz_p, seq_p, _ = x_prompt.shape
    bsz_s, seq_s, _ = x_sample.shape

    w_in = ab_w_in[0]
    n_main = D_A + D_QKV
    w_in = jnp.concatenate([w_in[:, :n_main], w_in[:, n_main + 2 * H_B:], w_in[:, n_main:n_main + 2 * H_B]],
                           axis=1)
    w_in = _pad_to(w_in, 1, D_IN_PAD).astype(BF16)

    ff = lambda w, layer: w[layer].astype(BF16)

    gate_lanes = lambda v: _pad_to(jnp.concatenate([jnp.zeros((H_B,), F32), v.astype(F32)]), 0,
                                   GATE_PAD).reshape(1, GATE_PAD)
    p = dict(
        norm_mix0=_row(norm_mix[0]), norm_ffn0=_row(norm_ffn[0]), norm_ffn1=_row(norm_ffn[1]),
        norm_final=_row(norm_final), ab_w_in=w_in,
        ab_w_out_a=ab_w_out[0][:D_A].astype(BF16), ab_w_out_b=ab_w_out[0][D_A:].astype(BF16),
        wg0=ff(ffn_w_gate, 0), wu0=ff(ffn_w_up, 0),
        wd0=ff(ffn_w_down, 0),
        wg1=ff(ffn_w_gate, 1), wu1=ff(ffn_w_up, 1),
        wd1=ff(ffn_w_down, 1),
        s5=dict(
            lr=_row(s5_lambda_re[0]), li=_row(s5_lambda_im[0]),
            ls=_row(jnp.repeat(s5_log_step[0], P_STATE)),
            bre=_block_diag_groups(jnp.swapaxes(s5_B_re[0], 1, 2), S5_GROUP, P_STATE).astype(F32),
            bim=_block_diag_groups(jnp.swapaxes(s5_B_im[0], 1, 2), S5_GROUP, P_STATE).astype(F32),
            cre=_block_diag_groups(jnp.swapaxes(s5_C_re[0], 1, 2), P_STATE, S5_GROUP).astype(BF16),
            cim=_block_diag_groups(jnp.swapaxes(s5_C_im[0], 1, 2), P_STATE, S5_GROUP).astype(BF16),
            d=_row(s5_D[0]), wglu=s5_w_glu[0].astype(BF16)),
        gdn=dict(conv_w=gdn_conv_w[0].astype(F32), alog=gate_lanes(gdn_A_log[0]),
                 dtb=gate_lanes(gdn_dt_bias[0]), norm_w=_row(gdn_norm_w[0])),
        rw=dict(
            norm_w=_row(norm_mix[1]), maa=rw_maa[0].astype(F32),
            wr=rw_w_r[0].astype(BF16), wk=rw_w_k[0].astype(BF16), wv=rw_w_v[0].astype(BF16),
            wo=rw_w_o[0].astype(BF16),
            w0=_row(rw_w0[0]), w1=_pad_to(rw_w1[0], 1, LANES).astype(BF16),
            w2=_pad_to(rw_w2[0], 0, LANES).astype(BF16),
            a0=_row(rw_a0[0]), a1=_pad_to(rw_a1[0], 1, LANES).astype(BF16),
            a2=_pad_to(rw_a2[0], 0, LANES).astype(BF16),
            g1=_pad_to(rw_g1[0], 1, 2 * LANES).astype(BF16), g2=_pad_to(rw_g2[0], 0, 2 * LANES).astype(BF16),
            k_k=_row(rw_k_k[0]), k_a=_row(rw_k_a[0]), r_k=_row(rw_r_k[0]),
            ln_w=_row(rw_ln_w[0]), ln_b=_row(rw_ln_b[0])),
    )

    zeros = lambda *shape: jnp.zeros(shape, F32)
    yp, p_hr, p_hi, p_gdn, p_conv, p_rw, p_shift = _trunk(
        x_prompt, bsz_p, seq_p, zeros(bsz_p, N_S5), zeros(bsz_p, N_S5), zeros(bsz_p, H_B, DK_B, DV_B),
        zeros(bsz_p, CONV_W - 1, D_QKV), zeros(bsz_p, H_C, N_HEAD_C, N_HEAD_C), zeros(bsz_p, D_MODEL), p,
        True)

    xs_tm = jnp.transpose(x_sample, (1, 0, 2)).reshape(1, seq_s * bsz_s, D_MODEL)
    ys, s_hr, s_hi, s_gdn, s_conv, s_rw, s_shift = _trunk(
        xs_tm, bsz_s, seq_s, state_s5_re[0].reshape(bsz_s, N_S5), state_s5_im[0].reshape(bsz_s, N_S5),
        state_gdn[0], state_gdn_conv[0], state_rwkv[0], state_rwkv_shift[0], p, True)
    y_sample = jnp.transpose(ys.reshape(seq_s, bsz_s, D_MODEL), (1, 0, 2))

    s5_shape = lambda b: (1, b, G_A, P_STATE)
    return (yp, y_sample,
            p_hr.reshape(s5_shape(bsz_p)), p_hi.reshape(s5_shape(bsz_p)), p_gdn[None], p_conv[None],
            p_rw[None], p_shift[None],
            s_hr.reshape(s5_shape(bsz_s)), s_hi.reshape(s5_shape(bsz_s)), s_gdn[None], s_conv[None],
            s_rw[None], s_shift[None])
```

```python
import functools
import math

import jax
import jax.numpy as jnp
from jax import lax
from jax.experimental import pallas as pl
from jax.experimental.pallas import tpu as pltpu

F32 = jnp.float32
BF16 = jnp.bfloat16

D_MODEL = 1024
D_A = 512
S5_GROUP = 16
G_A = 32
P_STATE = 64
N_S5 = G_A * P_STATE
D_B = 512
H_B = 4
DK_B = 128
DV_B = 128
D_QKV = 2 * H_B * DK_B + D_B
CONV_W = 4
GDN_CHUNK = 64
N_HEAD_C = 64
H_C = 16
RWKV_CHUNK = 64
RWKV_PAIR_GROUP = 8
RWKV_STAGGER = 7
RWKV_GN_EPS = 64e-5
N_RWKV_ACTS = 6
D_FF = 2816
NORM_EPS = 1e-6

LANES = 128
SUBLANES = 8
FF_TILE = 256
N_FF_TILES = D_FF // FF_TILE
GATE_PAD = LANES
D_IN_PAD = 4 * 512 + 512 + GATE_PAD
S5_LANE_BLOCK = 512
N_S5_BLOCKS = N_S5 // S5_LANE_BLOCK
TOKEN_TILE = 512
FFN_TOKEN_TILE = 512
RWKV_PROJ_TILE = 512
MIB = 1024 * 1024


def _cparams(semantics, vmem_mib):
    return pltpu.CompilerParams(dimension_semantics=semantics, vmem_limit_bytes=vmem_mib * MIB)


def _rmsnorm(x, w):
    return x * lax.rsqrt(jnp.mean(x * x, axis=-1, keepdims=True) + NORM_EPS) * w


def _sigmoid(x):
    return 0.5 + 0.5 * jnp.tanh(0.5 * x)


def _silu(x):
    half = 0.5 * x
    return half + half * jnp.tanh(half)


def _softplus(x):
    return jnp.maximum(x, 0.0) + jnp.log1p(jnp.exp(-jnp.abs(x)))


def _bdot(a, b):
    return jnp.dot(a.astype(BF16), b.astype(BF16), preferred_element_type=F32)


def _bdot_nt(a, b):
    return lax.dot_general(a.astype(BF16), b.astype(BF16), (((1,), (1,)), ((), ())),
                           preferred_element_type=F32)


def _bdot_tn(a, b):
    return lax.dot_general(a.astype(BF16), b.astype(BF16), (((0,), (0,)), ((), ())),
                           preferred_element_type=F32)


def _split(a):
    hi = a.astype(BF16)
    return hi, (a - hi.astype(F32)).astype(BF16)


def _split3(a):
    p1 = a.astype(BF16)
    r1 = a - p1.astype(F32)
    p2 = r1.astype(BF16)
    return p1, p2, (r1 - p2.astype(F32)).astype(BF16)


def _split_dot(a, exact_bf16):
    hi, lo = _split(a)
    return (jnp.dot(hi, exact_bf16, preferred_element_type=F32)
            + jnp.dot(lo, exact_bf16, preferred_element_type=F32))


def _cumsum_rows(tri_bf16, w, parts=3):
    split = _split3(w) if parts == 3 else _split(w)
    return sum(jnp.dot(tri_bf16, part, preferred_element_type=F32) for part in split)


def _iota2(shape):
    return (lax.broadcasted_iota(jnp.int32, shape, 0), lax.broadcasted_iota(jnp.int32, shape, 1))


def _run_staggered(gens, lag):
    pending, active, tick = list(gens), [], 0
    while pending or active:
        if pending and tick % lag == 0:
            active.append(pending.pop(0))
        for g in list(active):
            try:
                next(g)
            except StopIteration:
                active.remove(g)
        tick += 1


def _inv_identity_plus_wide(lmws, t):
    gen = _inv_identity_plus_wide_stages(lmws, t)
    while True:
        try:
            next(gen)
        except StopIteration as done:
            return done.value


def _inv_identity_plus_wide_stages(lmws, t):
    row, col = _iota2((t, 2 * t))
    left = col < t
    col = jnp.where(left, col, col - t)
    eye = (row == col).astype(F32)
    base = min(t, 16)

    def blockdiag(xw):
        return jnp.concatenate([jnp.where(left, xw, 0.0), jnp.where(left, 0.0, xw)], axis=0)

    if t > base:
        same_base = (row >> 4) == (col >> 4)
        ns = [jnp.where(same_base, -lmw, 0.0) for lmw in lmws]
    else:
        ns = [-lmw for lmw in lmws]
    xs = [eye + n for n in ns]
    ps = [_bdot(n, blockdiag(n)) for n in ns]
    yield
    k = 2
    while k < base:
        bds = [blockdiag(p) for p in ps]
        if 2 * k < base:
            both = [_bdot(jnp.concatenate([p, x], axis=0), bd) for p, x, bd in zip(ps, xs, bds)]
            ps = [r[:t] for r in both]
            xs = [x + r[t:] for x, r in zip(xs, both)]
        else:
            xs = [x + _bdot(x, bd) for x, bd in zip(xs, bds)]
        yield
        k *= 2
    shift = 4
    blk = base
    while blk < t:
        same_big = (row >> (shift + 1)) == (col >> (shift + 1))
        same_small = (row >> shift) == (col >> shift)
        off = [jnp.where(same_big, jnp.where(same_small, 0.0, lmw), 0.0) for lmw in lmws]
        mids = [_bdot(o, blockdiag(x)) for o, x in zip(off, xs)]
        yield
        xs = [x - _bdot(x, blockdiag(m)) for x, m in zip(xs, mids)]
        yield
        blk *= 2
        shift += 1
    return xs


def _ab_in_kernel(x_ref, nw_ref, w_ref, u_ref, qkv_ref, z_ref, ba_ref):
    h = _rmsnorm(x_ref[...], nw_ref[...])
    p = jnp.dot(h.astype(BF16), w_ref[...], preferred_element_type=F32)
    u_ref[...] = p[:, 0:D_A]
    qkv_ref[...] = p[:, D_A:D_A + D_QKV]
    z_ref[...] = p[:, D_A + D_QKV:D_A + D_QKV + D_B]
    ba_ref[...] = p[:, D_A + D_QKV + D_B:D_IN_PAD]


def _ab_in(x_bm, norm_w, w_in):
    bv, lv, _ = x_bm.shape
    tl = min(TOKEN_TILE, lv)
    grid = (bv, lv // tl)
    tm = lambda n: pl.BlockSpec((tl, n), lambda b, i: (i, b))
    const = lambda shape: pl.BlockSpec(shape, lambda b, i: (0,) * len(shape))
    return pl.pallas_call(
        _ab_in_kernel,
        grid=grid,
        in_specs=[pl.BlockSpec((None, tl, D_MODEL), lambda b, i: (b, i, 0)),
                  const((1, D_MODEL)), const((D_MODEL, D_IN_PAD))],
        out_specs=[tm(D_A), tm(D_QKV), tm(D_B), tm(GATE_PAD)],
        out_shape=[jax.ShapeDtypeStruct((lv, bv * D_A), F32),
                   jax.ShapeDtypeStruct((lv, bv * D_QKV), F32),
                   jax.ShapeDtypeStruct((lv, bv * D_B), F32),
                   jax.ShapeDtypeStruct((lv, bv * GATE_PAD), F32)],
        compiler_params=_cparams(("parallel", "parallel"), 48),
    )(x_bm, norm_w, w_in)


def _s5_kernel(u_ref, h0r_ref, h0i_ref, lr_ref, li_ref, ls_ref, bre_ref, bim_ref, cre_ref, cim_ref,
               d_ref, wglu_ref, y_ref, hr_out, hi_out,
               ar_s, ai_s, bbr_s, bbi_s, hr_s, hi_s, bur_s, bui_s, yg_s, *, tc, seq_on_lanes):
    rows = tc * SUBLANES

    @pl.when(pl.program_id(1) == 0)
    def _():
        lr = lr_ref[...]
        li = li_ref[...]
        dt = jnp.exp(ls_ref[...])
        mag = jnp.exp(lr * dt)
        ar = mag * jnp.cos(li * dt)
        ai = mag * jnp.sin(li * dt)
        den = lr * lr + li * li
        nr = ar - 1.0
        cr = (nr * lr + ai * li) / den
        ci = (ai * lr - nr * li) / den
        ar_s[...] = jnp.broadcast_to(ar, (SUBLANES, N_S5))
        ai_s[...] = jnp.broadcast_to(ai, (SUBLANES, N_S5))
        for j in range(N_S5_BLOCKS):
            sl = slice(j * S5_LANE_BLOCK, (j + 1) * S5_LANE_BLOCK)
            bbr_s[j] = (cr[:, sl] * bre_ref[j] - ci[:, sl] * bim_ref[j]).astype(BF16)
            bbi_s[j] = (cr[:, sl] * bim_ref[j] + ci[:, sl] * bre_ref[j]).astype(BF16)
        hr_s[...] = h0r_ref[...]
        hi_s[...] = h0i_ref[...]

    if seq_on_lanes:
        per_seq = jnp.stack([u_ref[:, b * D_A:(b + 1) * D_A] for b in range(SUBLANES)], axis=0)
        u = jnp.swapaxes(per_seq, 0, 1).reshape(rows, D_A)
    else:
        u = u_ref[...].reshape(rows, D_A)
    ub = u.astype(BF16)
    for j in range(N_S5_BLOCKS):
        sl = slice(j * S5_LANE_BLOCK, (j + 1) * S5_LANE_BLOCK)
        uj = ub[:, j * LANES:(j + 1) * LANES]
        bur_s[:, :, sl] = jnp.dot(uj, bbr_s[j], preferred_element_type=F32).reshape(tc, SUBLANES, S5_LANE_BLOCK)
        bui_s[:, :, sl] = jnp.dot(uj, bbi_s[j], preferred_element_type=F32).reshape(tc, SUBLANES, S5_LANE_BLOCK)

    for j in range(N_S5_BLOCKS):
        sl = slice(j * S5_LANE_BLOCK, (j + 1) * S5_LANE_BLOCK)
        ar = ar_s[:, sl]
        ai = ai_s[:, sl]

        def step(t, carry, sl=sl, ar=ar, ai=ai):
            hr, hi = carry
            nr = ar * hr - ai * hi + bur_s[t, :, sl]
            ni = ar * hi + ai * hr + bui_s[t, :, sl]
            bur_s[t, :, sl] = nr
            bui_s[t, :, sl] = ni
            return nr, ni

        hr, hi = lax.fori_loop(0, tc, step, (hr_s[:, sl], hi_s[:, sl]), unroll=min(tc, 8))
        hr_s[:, sl] = hr
        hi_s[:, sl] = hi

    for j in range(N_S5_BLOCKS):
        sl = slice(j * S5_LANE_BLOCK, (j + 1) * S5_LANE_BLOCK)
        cl = slice(j * LANES, (j + 1) * LANES)
        xr = bur_s[:, :, sl].reshape(rows, S5_LANE_BLOCK).astype(BF16)
        xi = bui_s[:, :, sl].reshape(rows, S5_LANE_BLOCK).astype(BF16)
        yj = (jnp.dot(xr, cre_ref[j], preferred_element_type=F32)
              - jnp.dot(xi, cim_ref[j], preferred_element_type=F32)
              + d_ref[:, cl] * u[:, cl])
        yg_s[:, cl] = jax.nn.gelu(yj)

    yg = yg_s[...]
    out = yg * _sigmoid(jnp.dot(yg.astype(BF16), wglu_ref[...], preferred_element_type=F32))
    if seq_on_lanes:
        per_seq = jnp.swapaxes(out.reshape(tc, SUBLANES, D_A), 0, 1)
        for b in range(SUBLANES):
            y_ref[:, b * D_A:(b + 1) * D_A] = per_seq[b]
    else:
        y_ref[...] = out.reshape(tc, SUBLANES, D_A)
    hr_out[...] = hr_s[...]
    hi_out[...] = hi_s[...]


def _s5(u_tm, h0r, h0i, prm, tc):
    seq_on_lanes = u_tm.ndim == 2
    seq, bsz = u_tm.shape[0], h0r.shape[0]
    grid = (bsz // SUBLANES, seq // tc)
    const = lambda shape: pl.BlockSpec(shape, lambda b, i: (0,) * len(shape))
    if seq_on_lanes:
        act = pl.BlockSpec((tc, SUBLANES * D_A), lambda b, i: (i, b))
    else:
        act = pl.BlockSpec((tc, SUBLANES, D_A), lambda b, i: (i, b, 0))
    st = pl.BlockSpec((SUBLANES, N_S5), lambda b, i: (b, 0))
    vec = const((1, N_S5))
    small = pltpu.VMEM((SUBLANES, N_S5), F32)
    big = pltpu.VMEM((tc, SUBLANES, N_S5), F32)
    return pl.pallas_call(
        functools.partial(_s5_kernel, tc=tc, seq_on_lanes=seq_on_lanes),
        grid=grid,
        in_specs=[act, st, st, vec, vec, vec,
                  const((N_S5_BLOCKS, LANES, S5_LANE_BLOCK)), const((N_S5_BLOCKS, LANES, S5_LANE_BLOCK)),
                  const((N_S5_BLOCKS, S5_LANE_BLOCK, LANES)), const((N_S5_BLOCKS, S5_LANE_BLOCK, LANES)),
                  const((1, D_A)), const((D_A, D_A))],
        out_specs=[act, st, st],
        out_shape=[jax.ShapeDtypeStruct(u_tm.shape, F32),
                   jax.ShapeDtypeStruct((bsz, N_S5), F32),
                   jax.ShapeDtypeStruct((bsz, N_S5), F32)],
        scratch_shapes=[small, small, pltpu.VMEM((N_S5_BLOCKS, LANES, S5_LANE_BLOCK), BF16),
                        pltpu.VMEM((N_S5_BLOCKS, LANES, S5_LANE_BLOCK), BF16), small, small, big, big,
                        pltpu.VMEM((tc * SUBLANES, D_A), F32)],
        compiler_params=_cparams(("parallel", "arbitrary"), 48),
    )(u_tm, h0r, h0i, prm["lr"], prm["li"], prm["ls"], prm["bre"], prm["bim"], prm["cre"], prm["cim"],
      prm["d"], prm["wglu"])


def _gdn_kernel(qkv_ref, ba_ref, z_ref, conv0_ref, s0_ref, cw_ref, alog_ref, dtb_ref, nw_ref,
                o_ref, sfin_ref, convn_ref, xbuf, s_s, *, t, nb, seq_on_sublanes):
    if seq_on_sublanes:
        qkv_seq = jnp.swapaxes(qkv_ref[...], 0, 1)
        ba_seq = jnp.swapaxes(ba_ref[...], 0, 1)
        z_seq = jnp.swapaxes(z_ref[...], 0, 1)
        read_qkv = lambda e: qkv_seq[e]
        read_ba = lambda e: ba_seq[e]
        read_z = lambda e, h: z_seq[e][:, h * DV_B:(h + 1) * DV_B]
    else:
        read_qkv = lambda e: qkv_ref[:, e * D_QKV:(e + 1) * D_QKV]
        read_ba = lambda e: ba_ref[:, e * GATE_PAD:(e + 1) * GATE_PAD]
        read_z = lambda e, h: z_ref[:, e * D_B + h * DV_B:e * D_B + (h + 1) * DV_B]
    o_parts = {}
    pad = SUBLANES
    hist = CONV_W - 1

    @pl.when(pl.program_id(1) == 0)
    def _():
        for e in range(nb):
            xbuf[pad - hist:pad, e * D_QKV:(e + 1) * D_QKV] = conv0_ref[e]
        s_s[...] = s0_ref[...].reshape(nb * H_B, DK_B, DV_B)

    row, col = _iota2((t, t))
    causal = row >= col
    strict = row > col
    tri = causal.astype(BF16)
    lane = lax.broadcasted_iota(jnp.int32, (t, LANES), 1)
    nw = nw_ref[...]
    cw = cw_ref[...]
    ones_sq = jnp.ones((DK_B, DK_B), BF16)
    zeros_wide = jnp.zeros((t, DK_B + DV_B), F32)
    zeros_v = jnp.zeros((t, DV_B), F32)

    def seq_group(es):
        ys, betas, gcs = [], [], []
        for e in es:
            cols = slice(e * D_QKV, (e + 1) * D_QKV)
            x = read_qkv(e)
            xbuf[pad:pad + t, cols] = x
            acc = x * cw[hist:hist + 1, :]
            for j in range(hist):
                acc = acc + xbuf[pad - hist + j:pad - hist + j + t, cols] * cw[j:j + 1, :]
            last = xbuf[pad + t - hist:pad + t, cols]
            convn_ref[e] = last
            xbuf[pad - hist:pad, cols] = last
            ys.append(_silu(acc))
            ba = read_ba(e)
            betas.append(_sigmoid(ba))
            g = -jnp.exp(alog_ref[...]) * _softplus(ba + dtb_ref[...])
            gcs.append(_cumsum_rows(tri, g))
            yield

        chains = [(i, h) for i in range(len(es)) for h in range(H_B)]
        heads = range(len(chains))
        bcol = [betas[i][:, h:h + 1] for i, h in chains]
        gcol = [gcs[i][:, H_B + h:H_B + h + 1] for i, h in chains]
        y_of = [ys[i] for i, h in chains]
        h_of = [h for i, h in chains]
        decay = []
        gc_parts = [[p.astype(F32) for p in _split3(gc)] for gc in gcs]
        for i, hh in chains:
            p1, p2, p3 = [p[:, H_B + hh:H_B + hh + 1] for p in gc_parts[i]]
            dl = jnp.where(lane == 0, p1, jnp.where(lane == 1, p2, jnp.where(lane == 2, p3,
                                                                             jnp.where(lane < 6, 1.0, 0.0))))
            dr = jnp.where(lane < 3, 1.0, jnp.where(lane == 3, -p1, jnp.where(lane == 4, -p2,
                                                                            jnp.where(lane == 5, -p3, 0.0))))
            decay.append(jnp.exp(jnp.where(causal, _bdot_nt(dl, dr), -jnp.inf)))
        eg = [jnp.exp(gcol[h]) for h in heads]
        glast = [gcol[h][t - 1:t, :] for h in heads]
        yield
        q = [y_of[c][:, h_of[c] * DK_B:(h_of[c] + 1) * DK_B] for c in heads]
        k = [y_of[c][:, (H_B + h_of[c]) * DK_B:(H_B + h_of[c] + 1) * DK_B] for c in heads]
        v = [y_of[c][:, 2 * H_B * DK_B + h_of[c] * DV_B:2 * H_B * DK_B + (h_of[c] + 1) * DV_B] for c in heads]
        sq = _bdot(jnp.concatenate([x * x for x in q + k], axis=0), ones_sq)
        nq = len(q)
        q = [x * (lax.rsqrt(sq[i * t:(i + 1) * t] + NORM_EPS) * (DK_B ** -0.5)) for i, x in enumerate(q)]
        k = [x * lax.rsqrt(sq[(nq + i) * t:(nq + i + 1) * t] + NORM_EPS) for i, x in enumerate(k)]
        kb = [k[h] * bcol[h] for h in heads]
        vb = [v[h] * bcol[h] for h in heads]
        yield
        lm = [jnp.where(strict, _bdot_nt(kb[h], k[h]) * decay[h], 0.0) for h in heads]
        attn = [_bdot_nt(q[h], k[h]) * decay[h] for h in heads]
        s = [s_s[es[i] * H_B + h] for i, h in chains]
        qs = [_bdot(q[h] * eg[h], s[h]) for h in heads]
        yield
        pairs = [(c, c + 1) for c in range(0, len(chains), 2)]
        tinv = yield from _inv_identity_plus_wide_stages(
            [jnp.concatenate([lm[c0], lm[c1]], axis=1) for c0, c1 in pairs], t)
        kbg = [kb[h] * eg[h] for h in heads]
        uw = [_bdot(tinv[i], jnp.concatenate(
            [jnp.concatenate([vb[c0], kbg[c0], zeros_wide], axis=1),
             jnp.concatenate([zeros_wide, vb[c1], kbg[c1]], axis=1)], axis=0)) for i, (c0, c1) in enumerate(pairs)]
        u = [uw[c // 2][:, (c % 2) * (DK_B + DV_B):(c % 2) * (DK_B + DV_B) + DV_B] for c in heads]
        w = [uw[c // 2][:, (c % 2) * (DK_B + DV_B) + DV_B:(c % 2 + 1) * (DK_B + DV_B)] for c in heads]
        yield
        v_new = [u[h] - _bdot(w[h], s[h]) for h in heads]
        yield
        av = [_bdot(jnp.concatenate([attn[c0], attn[c1]], axis=1), jnp.concatenate(
            [jnp.concatenate([v_new[c0], zeros_v], axis=1),
             jnp.concatenate([zeros_v, v_new[c1]], axis=1)], axis=0)) for c0, c1 in pairs]
        o = [qs[c] + av[c // 2][:, (c % 2) * DV_B:(c % 2 + 1) * DV_B] for c in heads]
        upd = [_bdot_tn(k[h] * jnp.exp(glast[h] - gcol[h]), v_new[h]) for h in heads]
        yield
        for c, (i, h) in enumerate(chains):
            e = es[i]
            s_new = s[c] * jnp.exp(glast[c]) + upd[c]
            s_s[e * H_B + h] = s_new
            sfin_ref[e, h] = s_new
            o_parts[e, h] = _rmsnorm(o[c], nw) * _silu(read_z(e, h))
            if not seq_on_sublanes:
                o_ref[:, e * D_B + h * DV_B:e * D_B + (h + 1) * DV_B] = o_parts[e, h]

    _run_staggered([seq_group(list(range(nb)))], 1)
    if seq_on_sublanes:
        per_seq = [jnp.concatenate([o_parts[e, h] for h in range(H_B)], axis=1) for e in range(nb)]
        o_ref[...] = jnp.swapaxes(jnp.stack(per_seq, axis=0), 0, 1)


def _gdn(qkv_tm, ba_tm, z_tm, conv0, s0, prm, bsz, t, nb):
    seq = qkv_tm.shape[0]
    seq_on_sublanes = qkv_tm.ndim == 3
    grid = (bsz // nb, seq // t)
    const = lambda shape: pl.BlockSpec(shape, lambda b, i: (0,) * len(shape))
    if seq_on_sublanes:
        tm = lambda n: pl.BlockSpec((t, nb, n), lambda b, i: (i, b, 0))
        out_shape = (seq, bsz, D_B)
    else:
        tm = lambda n: pl.BlockSpec((t, nb * n), lambda b, i: (i, b))
        out_shape = (seq, bsz * D_B)
    state = pl.BlockSpec((nb, H_B, DK_B, DV_B), lambda b, i: (b, 0, 0, 0))
    conv = pl.BlockSpec((nb, CONV_W - 1, D_QKV), lambda b, i: (b, 0, 0))
    return pl.pallas_call(
        functools.partial(_gdn_kernel, t=t, nb=nb, seq_on_sublanes=seq_on_sublanes),
        grid=grid,
        in_specs=[tm(D_QKV), tm(GATE_PAD), tm(D_B), conv, state,
                  const((CONV_W, D_QKV)), const((1, GATE_PAD)), const((1, GATE_PAD)), const((1, DV_B))],
        out_specs=[tm(D_B), state, conv],
        out_shape=[jax.ShapeDtypeStruct(out_shape, F32),
                   jax.ShapeDtypeStruct((bsz, H_B, DK_B, DV_B), F32),
                   jax.ShapeDtypeStruct((bsz, CONV_W - 1, D_QKV), F32)],
        scratch_shapes=[pltpu.VMEM((t + SUBLANES, nb * D_QKV), F32),
                        pltpu.VMEM((nb * H_B, DK_B, DV_B), F32)],
        compiler_params=_cparams(("parallel", "arbitrary"), 48),
    )(qkv_tm, ba_tm, z_tm, conv0, s0, prm["conv_w"], prm["alog"], prm["dtb"], prm["norm_w"])


def _mix_ffn_kernel(*refs, n_mix, final_norm):
    x_ref = refs[0]
    mix_refs = refs[1:1 + n_mix]
    wout_refs = refs[1 + n_mix:1 + 2 * n_mix]
    nffn_ref, wg_ref, wu_ref, wd_ref = refs[1 + 2 * n_mix:5 + 2 * n_mix]
    rest = refs[5 + 2 * n_mix:]
    if final_norm:
        nfin_ref, out_ref = rest
    else:
        (out_ref,) = rest

    x1 = x_ref[...]
    for m_ref, w_ref in zip(mix_refs, wout_refs):
        x1 = x1 + jnp.dot(m_ref[...].astype(BF16), w_ref[...], preferred_element_type=F32)
    h = _rmsnorm(x1, nffn_ref[...]).astype(BF16)
    out_ref[...] = x1

    for c in range(N_FF_TILES):
        cols = slice(c * FF_TILE, (c + 1) * FF_TILE)
        gate = jnp.dot(h, wg_ref[:, cols], preferred_element_type=F32)
        up = jnp.dot(h, wu_ref[:, cols], preferred_element_type=F32)
        act = (_silu(gate) * up).astype(BF16)
        out_ref[...] += jnp.dot(act, wd_ref[cols, :], preferred_element_type=F32)
    if final_norm:
        out_ref[...] = _rmsnorm(out_ref[...], nfin_ref[...])


def _mix_ffn(x, x_bm, mixes, wouts, nffn, wg, wu, wd, nfin, bv, lv, out_bm):
    tl = min(FFN_TOKEN_TILE, lv)
    grid = (bv, lv // tl)
    const = lambda shape: pl.BlockSpec(shape, lambda b, i: (0,) * len(shape),
                                       pipeline_mode=pl.Buffered(1))
    tm = lambda n: pl.BlockSpec((tl, n), lambda b, i: (i, b))
    bm = pl.BlockSpec((None, tl, D_MODEL), lambda b, i: (b, i, 0))
    in_specs = [bm if x_bm else tm(D_MODEL)]
    in_specs += [tm(m.shape[1] // bv) for m in mixes]
    in_specs += [const(w.shape) for w in wouts]
    in_specs += [const((1, D_MODEL)), const(wg.shape), const(wu.shape), const(wd.shape)]
    args = [x, *mixes, *wouts, nffn, wg, wu, wd]
    if nfin is not None:
        in_specs.append(const((1, D_MODEL)))
        args.append(nfin)
    if out_bm:
        out_spec, out_shape = bm, jax.ShapeDtypeStruct((bv, lv, D_MODEL), F32)
    else:
        out_spec, out_shape = tm(D_MODEL), jax.ShapeDtypeStruct((lv, bv * D_MODEL), F32)
    return pl.pallas_call(
        functools.partial(_mix_ffn_kernel, n_mix=len(mixes), final_norm=nfin is not None),
        grid=grid,
        in_specs=in_specs,
        out_specs=out_spec,
        out_shape=out_shape,
        compiler_params=_cparams(("parallel", "parallel"), 56),
    )(*args)


def _rwkv_proj_kernel(x_ref, shift0_ref, nw_ref, maa_ref, wr_ref, wk_ref, wv_ref, w0_ref, w1_ref, w2_ref,
                      a0_ref, a1_ref, a2_ref, g1_ref, g2_ref,
                      r_out, w_out, k_out, v_out, a_out, g_out, shift_out, hbuf, *, shift):
    pad = -(-shift // SUBLANES) * SUBLANES

    @pl.when(pl.program_id(1) == 0)
    def _():
        hbuf[pad - shift:pad, :] = shift0_ref[...]

    h = _rmsnorm(x_ref[...], nw_ref[...])
    rows = h.shape[0]
    hbuf[pad:pad + rows, :] = h
    prev = hbuf[pad - shift:pad - shift + rows, :]
    last = h[rows - shift:]
    hbuf[pad - shift:pad, :] = last
    shift_out[...] = last
    xx = prev - h
    maa = maa_ref[...]
    xr, xw, xk, xv, xa, xg = [h + xx * maa[j:j + 1, :] for j in range(6)]
    r = _bdot(xr, wr_ref[...])
    k = _bdot(xk, wk_ref[...])
    v = _bdot(xv, wv_ref[...])
    u = w0_ref[...] + _bdot(jnp.tanh(_bdot(xw, w1_ref[...])), w2_ref[...])
    a = _sigmoid(a0_ref[...] + _bdot(_bdot(xa, a1_ref[...]), a2_ref[...]))
    g = _bdot(_sigmoid(_bdot(xg, g1_ref[...])), g2_ref[...])
    r_out[...] = r
    w_out[...] = -math.exp(-0.5) * _sigmoid(u)
    k_out[...] = k
    v_out[...] = v
    a_out[...] = a
    g_out[...] = g


def _rwkv_proj(x_tm, shift0, prm):
    bv, shift, _ = shift0.shape
    lv = x_tm.shape[0]
    tr = max(shift, min(RWKV_PROJ_TILE, lv))
    grid = (bv, lv // tr)
    const = lambda shape: pl.BlockSpec(shape, lambda b, i: (0,) * len(shape),
                                       pipeline_mode=pl.Buffered(1))
    act = pl.BlockSpec((tr, D_MODEL), lambda b, i: (i, b))
    carry = pl.BlockSpec((None, shift, D_MODEL), lambda b, i: (b, 0, 0))
    vec = const((1, D_MODEL))
    weights = [prm[n] for n in ("wr", "wk", "wv")]
    in_specs = [act, carry, vec, const((6, D_MODEL))]
    in_specs += [const(w.shape) for w in weights]
    in_specs += [vec, const(prm["w1"].shape), const(prm["w2"].shape),
                 vec, const(prm["a1"].shape), const(prm["a2"].shape),
                 const(prm["g1"].shape), const(prm["g2"].shape)]
    out_act = jax.ShapeDtypeStruct((lv, bv * D_MODEL), F32)
    pad = -(-shift // SUBLANES) * SUBLANES
    return pl.pallas_call(
        functools.partial(_rwkv_proj_kernel, shift=shift),
        grid=grid,
        in_specs=in_specs,
        out_specs=[act] * N_RWKV_ACTS + [carry],
        out_shape=[out_act] * N_RWKV_ACTS + [jax.ShapeDtypeStruct((bv, shift, D_MODEL), F32)],
        scratch_shapes=[pltpu.VMEM((pad + tr, D_MODEL), F32)],
        compiler_params=_cparams(("parallel", "arbitrary"), 56),
    )(x_tm, shift0, prm["norm_w"], prm["maa"], prm["wr"], prm["wk"], prm["wv"], prm["w0"], prm["w1"],
      prm["w2"], prm["a0"], prm["a1"], prm["a2"], prm["g1"], prm["g2"])


def _rwkv_scan_kernel(r_ref, w_ref, k_ref, v_ref, a_ref, g_ref, s0_ref, kkw_ref, kaw_ref, rk_ref, lnw_ref, lnb_ref,
                      y_ref, sfin_ref, s_s, *, t, nb, group, seq_on_sublanes):
    if seq_on_sublanes:
        act_refs = (r_ref, w_ref, k_ref, v_ref, a_ref, g_ref)
        per_seq = [jnp.swapaxes(ref[...], 0, 1) for ref in act_refs]

        def slab(ref, jj):
            e, j = divmod(jj, H_C // 2)
            which = [i for i, x in enumerate(act_refs) if x is ref][0]
            return per_seq[which][e][:, j * LANES:(j + 1) * LANES]
    else:
        slab = lambda ref, jj: ref[:, jj * LANES:(jj + 1) * LANES]
    y_parts = {}
    n = N_HEAD_C
    zero_blk = jnp.zeros((n, n), F32)

    pairs_per_seq = H_C // 2

    @pl.when(pl.program_id(1) == 0)
    def _():
        for jj in range(nb * pairs_per_seq):
            e, j = divmod(jj, pairs_per_seq)
            top = jnp.concatenate([s0_ref[e, 2 * j], zero_blk], axis=1)
            bot = jnp.concatenate([zero_blk, s0_ref[e, 2 * j + 1]], axis=1)
            s_s[jj] = jnp.concatenate([top, bot], axis=0)

    row, col = _iota2((t, t))
    tri = (row >= col).astype(BF16)
    roww, colw = _iota2((t, 2 * t))
    colw = jnp.where(colw >= t, colw - t, colw)
    incl_w = roww >= colw
    strict_w = roww > colw
    lane = lax.broadcasted_iota(jnp.int32, (t, LANES), 1)
    head0 = lane < n

    def by_head(x):
        return jnp.concatenate([jnp.where(head0, x, 0.0), jnp.where(head0, 0.0, x)], axis=0)

    prow, pcol = _iota2((LANES, LANES))
    same_head = (prow >> 6) == (pcol >> 6)
    ones_blk = same_head.astype(BF16)

    def head_sums(xs, dot):
        tot = dot(jnp.concatenate(xs, axis=0), ones_blk)
        return [tot[i * t:(i + 1) * t] for i in range(len(xs))]

    def pair_group(js):
        ps = range(len(js))
        sl = [slice(j * LANES, (j + 1) * LANES) for j in js]
        psl = [slice((j % pairs_per_seq) * LANES, (j % pairs_per_seq + 1) * LANES) for j in js]
        r = [slab(r_ref, j) for j in js]
        w = [slab(w_ref, j) for j in js]
        kraw = [slab(k_ref, j) for j in js]
        v = [slab(v_ref, j) for j in js]
        a = [slab(a_ref, j) for j in js]
        kkr = [kraw[p] * kkw_ref[:, psl[p]] for p in ps]
        k = [kraw[p] * (1.0 + (a[p] - 1.0) * kaw_ref[:, psl[p]]) for p in ps]
        kk = [x * lax.rsqrt(ss + NORM_EPS) for x, ss in zip(kkr, head_sums([x * x for x in kkr], _bdot))]
        bv = [kk[p] * a[p] for p in ps]
        yield
        gc = [_cumsum_rows(tri, x, parts=2) for x in w]
        glast = [x[t - 1:t, :] for x in gc]
        yield
        pinv = [jnp.exp(-x) for x in gc]
        at = [-kk[p] * jnp.exp(gc[p] - w[p]) for p in ps]
        bt = [bv[p] * pinv[p] for p in ps]
        kt = [k[p] * pinv[p] for p in ps]
        rt = [r[p] * jnp.exp(gc[p]) for p in ps]
        rem = [jnp.exp(glast[p] - gc[p]) for p in ps]
        yield
        lhs = [jnp.concatenate([at[p], rt[p]], axis=0) for p in ps]
        gm = [_bdot_nt(lhs[p], jnp.concatenate([by_head(bt[p]), by_head(kt[p])], axis=0)) for p in ps]
        yield
        s = [s_s[j] for j in js]
        ars = [_bdot_nt(lhs[p], s[p]) for p in ps]
        v2 = [by_head(x) for x in v]
        yield
        aab = [jnp.where(strict_w, gm[p][:t, :2 * t], 0.0) for p in ps]
        aak = [jnp.where(strict_w, gm[p][:t, 2 * t:], 0.0) for p in ps]
        rkm = [jnp.where(incl_w, gm[p][t:, 2 * t:], 0.0) for p in ps]
        rb = [jnp.where(incl_w, gm[p][t:, :2 * t], 0.0) for p in ps]
        yield
        kv = [_bdot(jnp.concatenate([aak[p], rkm[p]], axis=0), v2[p]) for p in ps]
        rhs = [ars[p][:t] + kv[p][:t] for p in ps]
        yield
        tinv = yield from _inv_identity_plus_wide_stages([-x for x in aab], t)
        u = [_bdot(tinv[p], by_head(rhs[p])) for p in ps]
        yield
        y = [ars[p][t:] + _bdot(rb[p], by_head(u[p])) + kv[p][t:] for p in ps]
        yield
        upd = [_bdot_tn(jnp.concatenate([u[p], v[p]], axis=0),
                        jnp.concatenate([bv[p] * rem[p], k[p] * rem[p]], axis=0)) for p in ps]
        for p, j in enumerate(js):
            s_s[j] = s[p] * jnp.exp(glast[p]) + jnp.where(same_head, upd[p], 0.0)
        yield
        mu = head_sums(y, _split_dot)
        d = [y[p] - mu[p] * (1.0 / n) for p in ps]
        yield
        var = head_sums([x * x for x in d], _bdot)
        yield
        bonus = head_sums([r[p] * k[p] * rk_ref[:, psl[p]] for p in ps], _split_dot)
        yield
        for p in ps:
            yn = d[p] * lax.rsqrt(var[p] * (1.0 / n) + RWKV_GN_EPS) * lnw_ref[:, psl[p]] + lnb_ref[:, psl[p]]
            y_parts[js[p]] = (yn + bonus[p] * v[p]) * slab(g_ref, js[p])
            if not seq_on_sublanes:
                y_ref[:, sl[p]] = y_parts[js[p]]

    groups = [list(range(first, first + group)) for first in range(0, nb * pairs_per_seq, group)]
    _run_staggered([pair_group(js) for js in groups], RWKV_STAGGER)
    if seq_on_sublanes:
        rows = [jnp.concatenate([y_parts[e * pairs_per_seq + j] for j in range(pairs_per_seq)], axis=1)
                for e in range(nb)]
        y_ref[...] = jnp.swapaxes(jnp.stack(rows, axis=0), 0, 1)

    for jj in range(nb * pairs_per_seq):
        e, j = divmod(jj, pairs_per_seq)
        sp = s_s[jj]
        sfin_ref[e, 2 * j] = sp[:n, :n]
        sfin_ref[e, 2 * j + 1] = sp[n:, n:]


def _rwkv_scan(acts, s0, prm, bsz, t, nb, group):
    seq = acts[0].shape[0]
    seq_on_sublanes = acts[0].ndim == 3
    grid = (bsz // nb, seq // t)
    const = lambda shape: pl.BlockSpec(shape, lambda b, i: (0,) * len(shape))
    if seq_on_sublanes:
        tm = pl.BlockSpec((t, nb, D_MODEL), lambda b, i: (i, b, 0))
    else:
        tm = pl.BlockSpec((t, nb * D_MODEL), lambda b, i: (i, b))
    st = pl.BlockSpec((nb, H_C, N_HEAD_C, N_HEAD_C), lambda b, i: (b, 0, 0, 0))
    vec = const((1, D_MODEL))
    return pl.pallas_call(
        functools.partial(_rwkv_scan_kernel, t=t, nb=nb, group=group, seq_on_sublanes=seq_on_sublanes),
        grid=grid,
        in_specs=[tm] * N_RWKV_ACTS + [st, vec, vec, vec, vec, vec],
        out_specs=[tm, st],
        out_shape=[jax.ShapeDtypeStruct(acts[0].shape, F32),
                   jax.ShapeDtypeStruct((bsz, H_C, N_HEAD_C, N_HEAD_C), F32)],
        scratch_shapes=[pltpu.VMEM((nb * H_C // 2, LANES, LANES), F32)],
        compiler_params=_cparams(("parallel", "arbitrary"), 48),
    )(*acts, s0, prm["k_k"], prm["k_a"], prm["r_k"], prm["ln_w"], prm["ln_b"])


def _block_diag_groups(w, rows_per_group, cols_per_group):
    g = w.shape[0]
    w = w.reshape(g // 8, 8, rows_per_group, cols_per_group)
    eye = jnp.eye(8, dtype=w.dtype)
    out = jnp.einsum("jgrc,gh->jgrhc", w, eye)
    return out.reshape(g // 8, 8 * rows_per_group, 8 * cols_per_group)


def _pad_to(w, axis, size):
    pad = [(0, 0)] * w.ndim
    pad[axis] = (0, size - w.shape[axis])
    return jnp.pad(w, pad)


def _row(v):
    return v.reshape(1, -1).astype(F32)


def _recurrence_tiling(seq):
    if seq >= GDN_CHUNK:
        return dict(s5_tc=128, gdn_t=GDN_CHUNK, gdn_nb=8, rwkv_t=RWKV_CHUNK, rwkv_nb=4,
                    rwkv_group=2 * RWKV_PAIR_GROUP)
    return dict(s5_tc=seq, gdn_t=seq, gdn_nb=SUBLANES, rwkv_t=seq, rwkv_nb=SUBLANES,
                rwkv_group=2 * RWKV_PAIR_GROUP)


def _trunk(x_bm, bsz, seq, s5_re0, s5_im0, gdn_s0, gdn_conv0, rw_s0, rw_shift0, p, out_bm):
    bv, lv, _ = x_bm.shape
    cfg = _recurrence_tiling(seq)
    u, qkv, z, ba = _ab_in(x_bm, p["norm_mix0"], p["ab_w_in"])
    view = (lambda a, n: a.reshape(seq, bsz, n)) if bv == 1 else (lambda a, n: a)
    y_a, hr, hi = _s5(view(u, D_A), s5_re0, s5_im0, p["s5"], cfg["s5_tc"])
    y_b, gdn_s, gdn_conv = _gdn(view(qkv, D_QKV), view(ba, GATE_PAD), view(z, D_B), gdn_conv0, gdn_s0,
                                p["gdn"], bsz, cfg["gdn_t"], cfg["gdn_nb"])
    x1 = _mix_ffn(x_bm, True, [y_a.reshape(lv, bv * D_A), y_b.reshape(lv, bv * D_B)],
                  [p["ab_w_out_a"], p["ab_w_out_b"]], p["norm_ffn0"], p["wg0"], p["wu0"], p["wd0"],
                  None, bv, lv, False)
    acts = _rwkv_proj(x1, rw_shift0.reshape(bv, bsz // bv, D_MODEL), p["rw"])
    shift = acts[N_RWKV_ACTS].reshape(bsz, D_MODEL)
    yg, rw_s = _rwkv_scan([view(a, D_MODEL) for a in acts[:N_RWKV_ACTS]], rw_s0, p["rw"], bsz,
                          cfg["rwkv_t"], cfg["rwkv_nb"], cfg["rwkv_group"])
    y = _mix_ffn(x1, False, [yg.reshape(lv, bv * D_MODEL)], [p["rw"]["wo"]], p["norm_ffn1"], p["wg1"],
                 p["wu1"], p["wd1"], p["norm_final"], bv, lv, out_bm)
    return y, hr, hi, gdn_s, gdn_conv, rw_s, shift


def kernel(x_prompt, x_sample, state_s5_re, state_s5_im, state_gdn, state_gdn_conv, state_rwkv, state_rwkv_shift, norm_mix, norm_ffn, norm_final, ffn_w_gate, ffn_w_up, ffn_w_down, ab_w_in, ab_w_out, s5_lambda_re, s5_lambda_im, s5_log_step, s5_B_re, s5_B_im, s5_C_re, s5_C_im, s5_D, s5_w_glu, gdn_conv_w, gdn_A_log, gdn_dt_bias, gdn_norm_w, rw_maa, rw_w_r, rw_w_k, rw_w_v, rw_w_o, rw_w0, rw_w1, rw_w2, rw_a0, rw_a1, rw_a2, rw_g1, rw_g2, rw_k_k, rw_k_a, rw_r_k, rw_ln_w, rw_ln_b):
    bsz_p, seq_p, _ = x_prompt.shape
    bsz_s, seq_s, _ = x_sample.shape

    w_in = ab_w_in[0]
    n_main = D_A + D_QKV
    w_in = jnp.concatenate([w_in[:, :n_main], w_in[:, n_main + 2 * H_B:], w_in[:, n_main:n_main + 2 * H_B]],
                           axis=1)
    w_in = _pad_to(w_in, 1, D_IN_PAD).astype(BF16)

    ff = lambda w, layer: w[layer].astype(BF16)

    gate_lanes = lambda v: _pad_to(jnp.concatenate([jnp.zeros((H_B,), F32), v.astype(F32)]), 0,
                                   GATE_PAD).reshape(1, GATE_PAD)
    p = dict(
        norm_mix0=_row(norm_mix[0]), norm_ffn0=_row(norm_ffn[0]), norm_ffn1=_row(norm_ffn[1]),
        norm_final=_row(norm_final), ab_w_in=w_in,
        ab_w_out_a=ab_w_out[0][:D_A].astype(BF16), ab_w_out_b=ab_w_out[0][D_A:].astype(BF16),
        wg0=ff(ffn_w_gate, 0), wu0=ff(ffn_w_up, 0),
        wd0=ff(ffn_w_down, 0),
        wg1=ff(ffn_w_gate, 1), wu1=ff(ffn_w_up, 1),
        wd1=ff(ffn_w_down, 1),
        s5=dict(
            lr=_row(s5_lambda_re[0]), li=_row(s5_lambda_im[0]),
            ls=_row(jnp.repeat(s5_log_step[0], P_STATE)),
            bre=_block_diag_groups(jnp.swapaxes(s5_B_re[0], 1, 2), S5_GROUP, P_STATE).astype(F32),
            bim=_block_diag_groups(jnp.swapaxes(s5_B_im[0], 1, 2), S5_GROUP, P_STATE).astype(F32),
            cre=_block_diag_groups(jnp.swapaxes(s5_C_re[0], 1, 2), P_STATE, S5_GROUP).astype(BF16),
            cim=_block_diag_groups(jnp.swapaxes(s5_C_im[0], 1, 2), P_STATE, S5_GROUP).astype(BF16),
            d=_row(s5_D[0]), wglu=s5_w_glu[0].astype(BF16)),
        gdn=dict(conv_w=gdn_conv_w[0].astype(F32), alog=gate_lanes(gdn_A_log[0]),
                 dtb=gate_lanes(gdn_dt_bias[0]), norm_w=_row(gdn_norm_w[0])),
        rw=dict(
            norm_w=_row(norm_mix[1]), maa=rw_maa[0].astype(F32),
            wr=rw_w_r[0].astype(BF16), wk=rw_w_k[0].astype(BF16), wv=rw_w_v[0].astype(BF16),
            wo=rw_w_o[0].astype(BF16),
            w0=_row(rw_w0[0]), w1=_pad_to(rw_w1[0], 1, LANES).astype(BF16),
            w2=_pad_to(rw_w2[0], 0, LANES).astype(BF16),
            a0=_row(rw_a0[0]), a1=_pad_to(rw_a1[0], 1, LANES).astype(BF16),
            a2=_pad_to(rw_a2[0], 0, LANES).astype(BF16),
            g1=_pad_to(rw_g1[0], 1, 2 * LANES).astype(BF16), g2=_pad_to(rw_g2[0], 0, 2 * LANES).astype(BF16),
            k_k=_row(rw_k_k[0]), k_a=_row(rw_k_a[0]), r_k=_row(rw_r_k[0]),
            ln_w=_row(rw_ln_w[0]), ln_b=_row(rw_ln_b[0])),
    )

    zeros = lambda *shape: jnp.zeros(shape, F32)
    yp, p_hr, p_hi, p_gdn, p_conv, p_rw, p_shift = _trunk(
        x_prompt, bsz_p, seq_p, zeros(bsz_p, N_S5), zeros(bsz_p, N_S5), zeros(bsz_p, H_B, DK_B, DV_B),
        zeros(bsz_p, CONV_W - 1, D_QKV), zeros(bsz_p, H_C, N_HEAD_C, N_HEAD_C), zeros(bsz_p, D_MODEL), p,
        True)

    xs_tm = jnp.transpose(x_sample, (1, 0, 2)).reshape(1, seq_s * bsz_s, D_MODEL)
    ys, s_hr, s_hi, s_gdn, s_conv, s_rw, s_shift = _trunk(
        xs_tm, bsz_s, seq_s, state_s5_re[0].reshape(bsz_s, N_S5), state_s5_im[0].reshape(bsz_s, N_S5),
        state_gdn[0], state_gdn_conv[0], state_rwkv[0], state_rwkv_shift[0], p, True)
    y_sample = jnp.transpose(ys.reshape(seq_s, bsz_s, D_MODEL), (1, 0, 2))

    s5_shape = lambda b: (1, b, G_A, P_STATE)
    return (yp, y_sample,
            p_hr.reshape(s5_shape(bsz_p)), p_hi.reshape(s5_shape(bsz_p)), p_gdn[None], p_conv[None],
            p_rw[None], p_shift[None],
            s_hr.reshape(s5_shape(bsz_s)), s_hi.reshape(s5_shape(bsz_s)), s_gdn[None], s_conv[None],
            s_rw[None], s_shift[None])
```

```python
import functools
import math

import jax
import jax.numpy as jnp
from jax import lax
from jax.experimental import pallas as pl
from jax.experimental.pallas import tpu as pltpu

F32 = jnp.float32
BF16 = jnp.bfloat16

D_MODEL = 1024
D_A = 512
S5_GROUP = 16
G_A = 32
P_STATE = 64
N_S5 = G_A * P_STATE
D_B = 512
H_B = 4
DK_B = 128
DV_B = 128
D_QKV = 2 * H_B * DK_B + D_B
CONV_W = 4
GDN_CHUNK = 64
N_HEAD_C = 64
H_C = 16
RWKV_CHUNK = 64
RWKV_PAIR_GROUP = 8
RWKV_STAGGER = 7
RWKV_GN_EPS = 64e-5
N_RWKV_ACTS = 6
D_FF = 2816
NORM_EPS = 1e-6

LANES = 128
SUBLANES = 8
FF_TILE = 256
N_FF_TILES = D_FF // FF_TILE
GATE_PAD = LANES
D_IN_PAD = 4 * 512 + 512 + GATE_PAD
S5_LANE_BLOCK = 512
N_S5_BLOCKS = N_S5 // S5_LANE_BLOCK
TOKEN_TILE = 512
FFN_TOKEN_TILE = 512
RWKV_PROJ_TILE = 512
MIB = 1024 * 1024


def _cparams(semantics, vmem_mib):
    return pltpu.CompilerParams(dimension_semantics=semantics, vmem_limit_bytes=vmem_mib * MIB)


def _rmsnorm(x, w):
    return x * lax.rsqrt(jnp.mean(x * x, axis=-1, keepdims=True) + NORM_EPS) * w


def _sigmoid(x):
    return 0.5 + 0.5 * jnp.tanh(0.5 * x)


def _silu(x):
    half = 0.5 * x
    return half + half * jnp.tanh(half)


def _softplus(x):
    return jnp.maximum(x, 0.0) + jnp.log1p(jnp.exp(-jnp.abs(x)))


def _bdot(a, b):
    return jnp.dot(a.astype(BF16), b.astype(BF16), preferred_element_type=F32)


def _bdot_nt(a, b):
    return lax.dot_general(a.astype(BF16), b.astype(BF16), (((1,), (1,)), ((), ())),
                           preferred_element_type=F32)


def _bdot_tn(a, b):
    return lax.dot_general(a.astype(BF16), b.astype(BF16), (((0,), (0,)), ((), ())),
                           preferred_element_type=F32)


def _split(a):
    hi = a.astype(BF16)
    return hi, (a - hi.astype(F32)).astype(BF16)


def _split3(a):
    p1 = a.astype(BF16)
    r1 = a - p1.astype(F32)
    p2 = r1.astype(BF16)
    return p1, p2, (r1 - p2.astype(F32)).astype(BF16)


def _split_dot(a, exact_bf16):
    hi, lo = _split(a)
    return (jnp.dot(hi, exact_bf16, preferred_element_type=F32)
            + jnp.dot(lo, exact_bf16, preferred_element_type=F32))


def _cumsum_rows(tri_bf16, w, parts=3):
    split = _split3(w) if parts == 3 else _split(w)
    return sum(jnp.dot(tri_bf16, part, preferred_element_type=F32) for part in split)


def _iota2(shape):
    return (lax.broadcasted_iota(jnp.int32, shape, 0), lax.broadcasted_iota(jnp.int32, shape, 1))


def _run_staggered(gens, lag):
    pending, active, tick = list(gens), [], 0
    while pending or active:
        if pending and tick % lag == 0:
            active.append(pending.pop(0))
        for g in list(active):
            try:
                next(g)
            except StopIteration:
                active.remove(g)
        tick += 1


def _inv_identity_plus_wide(lmws, t):
    gen = _inv_identity_plus_wide_stages(lmws, t)
    while True:
        try:
            next(gen)
        except StopIteration as done:
            return done.value


def _inv_identity_plus_wide_stages(lmws, t):
    row, col = _iota2((t, 2 * t))
    left = col < t
    col = jnp.where(left, col, col - t)
    eye = (row == col).astype(F32)
    base = min(t, 16)

    def blockdiag(xw):
        return jnp.concatenate([jnp.where(left, xw, 0.0), jnp.where(left, 0.0, xw)], axis=0)

    if t > base:
        same_base = (row >> 4) == (col >> 4)
        ns = [jnp.where(same_base, -lmw, 0.0) for lmw in lmws]
    else:
        ns = [-lmw for lmw in lmws]
    xs = [eye + n for n in ns]
    ps = [_bdot(n, blockdiag(n)) for n in ns]
    yield
    k = 2
    while k < base:
        bds = [blockdiag(p) for p in ps]
        if 2 * k < base:
            both = [_bdot(jnp.concatenate([p, x], axis=0), bd) for p, x, bd in zip(ps, xs, bds)]
            ps = [r[:t] for r in both]
            xs = [x + r[t:] for x, r in zip(xs, both)]
        else:
            xs = [x + _bdot(x, bd) for x, bd in zip(xs, bds)]
        yield
        k *= 2
    shift = 4
    blk = base
    while blk < t:
        same_big = (row >> (shift + 1)) == (col >> (shift + 1))
        same_small = (row >> shift) == (col >> shift)
        off = [jnp.where(same_big, jnp.where(same_small, 0.0, lmw), 0.0) for lmw in lmws]
        mids = [_bdot(o, blockdiag(x)) for o, x in zip(off, xs)]
        yield
        xs = [x - _bdot(x, blockdiag(m)) for x, m in zip(xs, mids)]
        yield
        blk *= 2
        shift += 1
    return xs


def _ab_in_kernel(x_ref, nw_ref, w_ref, u_ref, qkv_ref, z_ref, ba_ref):
    h = _rmsnorm(x_ref[...], nw_ref[...])
    p = jnp.dot(h.astype(BF16), w_ref[...], preferred_element_type=F32)
    u_ref[...] = p[:, 0:D_A]
    qkv_ref[...] = p[:, D_A:D_A + D_QKV]
    z_ref[...] = p[:, D_A + D_QKV:D_A + D_QKV + D_B]
    ba_ref[...] = p[:, D_A + D_QKV + D_B:D_IN_PAD]


def _ab_in(x_bm, norm_w, w_in):
    bv, lv, _ = x_bm.shape
    tl = min(TOKEN_TILE, lv)
    grid = (bv, lv // tl)
    tm = lambda n: pl.BlockSpec((tl, n), lambda b, i: (i, b))
    const = lambda shape: pl.BlockSpec(shape, lambda b, i: (0,) * len(shape))
    return pl.pallas_call(
        _ab_in_kernel,
        grid=grid,
        in_specs=[pl.BlockSpec((None, tl, D_MODEL), lambda b, i: (b, i, 0)),
                  const((1, D_MODEL)), const((D_MODEL, D_IN_PAD))],
        out_specs=[tm(D_A), tm(D_QKV), tm(D_B), tm(GATE_PAD)],
        out_shape=[jax.ShapeDtypeStruct((lv, bv * D_A), F32),
                   jax.ShapeDtypeStruct((lv, bv * D_QKV), F32),
                   jax.ShapeDtypeStruct((lv, bv * D_B), F32),
                   jax.ShapeDtypeStruct((lv, bv * GATE_PAD), F32)],
        compiler_params=_cparams(("parallel", "parallel"), 48),
    )(x_bm, norm_w, w_in)


def _s5_kernel(u_ref, h0r_ref, h0i_ref, lr_ref, li_ref, ls_ref, bre_ref, bim_ref, cre_ref, cim_ref,
               d_ref, wglu_ref, y_ref, hr_out, hi_out,
               ar_s, ai_s, bbr_s, bbi_s, hr_s, hi_s, bur_s, bui_s, yg_s, *, tc, seq_on_lanes):
    rows = tc * SUBLANES

    @pl.when(pl.program_id(1) == 0)
    def _():
        lr = lr_ref[...]
        li = li_ref[...]
        dt = jnp.exp(ls_ref[...])
        mag = jnp.exp(lr * dt)
        ar = mag * jnp.cos(li * dt)
        ai = mag * jnp.sin(li * dt)
        den = lr * lr + li * li
        nr = ar - 1.0
        cr = (nr * lr + ai * li) / den
        ci = (ai * lr - nr * li) / den
        ar_s[...] = jnp.broadcast_to(ar, (SUBLANES, N_S5))
        ai_s[...] = jnp.broadcast_to(ai, (SUBLANES, N_S5))
        for j in range(N_S5_BLOCKS):
            sl = slice(j * S5_LANE_BLOCK, (j + 1) * S5_LANE_BLOCK)
            bbr_s[j] = (cr[:, sl] * bre_ref[j] - ci[:, sl] * bim_ref[j]).astype(BF16)
            bbi_s[j] = (cr[:, sl] * bim_ref[j] + ci[:, sl] * bre_ref[j]).astype(BF16)
        hr_s[...] = h0r_ref[...]
        hi_s[...] = h0i_ref[...]

    if seq_on_lanes:
        per_seq = jnp.stack([u_ref[:, b * D_A:(b + 1) * D_A] for b in range(SUBLANES)], axis=0)
        u = jnp.swapaxes(per_seq, 0, 1).reshape(rows, D_A)
    else:
        u = u_ref[...].reshape(rows, D_A)
    ub = u.astype(BF16)
    for j in range(N_S5_BLOCKS):
        sl = slice(j * S5_LANE_BLOCK, (j + 1) * S5_LANE_BLOCK)
        uj = ub[:, j * LANES:(j + 1) * LANES]
        bur_s[:, :, sl] = jnp.dot(uj, bbr_s[j], preferred_element_type=F32).reshape(tc, SUBLANES, S5_LANE_BLOCK)
        bui_s[:, :, sl] = jnp.dot(uj, bbi_s[j], preferred_element_type=F32).reshape(tc, SUBLANES, S5_LANE_BLOCK)

    for j in range(N_S5_BLOCKS):
        sl = slice(j * S5_LANE_BLOCK, (j + 1) * S5_LANE_BLOCK)
        ar = ar_s[:, sl]
        ai = ai_s[:, sl]

        def step(t, carry, sl=sl, ar=ar, ai=ai):
            hr, hi = carry
            nr = ar * hr - ai * hi + bur_s[t, :, sl]
            ni = ar * hi + ai * hr + bui_s[t, :, sl]
            bur_s[t, :, sl] = nr
            bui_s[t, :, sl] = ni
            return nr, ni

        hr, hi = lax.fori_loop(0, tc, step, (hr_s[:, sl], hi_s[:, sl]), unroll=min(tc, 8))
        hr_s[:, sl] = hr
        hi_s[:, sl] = hi

    for j in range(N_S5_BLOCKS):
        sl = slice(j * S5_LANE_BLOCK, (j + 1) * S5_LANE_BLOCK)
        cl = slice(j * LANES, (j + 1) * LANES)
        xr = bur_s[:, :, sl].reshape(rows, S5_LANE_BLOCK).astype(BF16)
        xi = bui_s[:, :, sl].reshape(rows, S5_LANE_BLOCK).astype(BF16)
        yj = (jnp.dot(xr, cre_ref[j], preferred_element_type=F32)
              - jnp.dot(xi, cim_ref[j], preferred_element_type=F32)
              + d_ref[:, cl] * u[:, cl])
        yg_s[:, cl] = jax.nn.gelu(yj)

    yg = yg_s[...]
    out = yg * _sigmoid(jnp.dot(yg.astype(BF16), wglu_ref[...], preferred_element_type=F32))
    if seq_on_lanes:
        per_seq = jnp.swapaxes(out.reshape(tc, SUBLANES, D_A), 0, 1)
        for b in range(SUBLANES):
            y_ref[:, b * D_A:(b + 1) * D_A] = per_seq[b]
    else:
        y_ref[...] = out.reshape(tc, SUBLANES, D_A)
    hr_out[...] = hr_s[...]
    hi_out[...] = hi_s[...]


def _s5(u_tm, h0r, h0i, prm, tc):
    seq_on_lanes = u_tm.ndim == 2
    seq, bsz = u_tm.shape[0], h0r.shape[0]
    grid = (bsz // SUBLANES, seq // tc)
    const = lambda shape: pl.BlockSpec(shape, lambda b, i: (0,) * len(shape))
    if seq_on_lanes:
        act = pl.BlockSpec((tc, SUBLANES * D_A), lambda b, i: (i, b))
    else:
        act = pl.BlockSpec((tc, SUBLANES, D_A), lambda b, i: (i, b, 0))
    st = pl.BlockSpec((SUBLANES, N_S5), lambda b, i: (b, 0))
    vec = const((1, N_S5))
    small = pltpu.VMEM((SUBLANES, N_S5), F32)
    big = pltpu.VMEM((tc, SUBLANES, N_S5), F32)
    return pl.pallas_call(
        functools.partial(_s5_kernel, tc=tc, seq_on_lanes=seq_on_lanes),
        grid=grid,
        in_specs=[act, st, st, vec, vec, vec,
                  const((N_S5_BLOCKS, LANES, S5_LANE_BLOCK)), const((N_S5_BLOCKS, LANES, S5_LANE_BLOCK)),
                  const((N_S5_BLOCKS, S5_LANE_BLOCK, LANES)), const((N_S5_BLOCKS, S5_LANE_BLOCK, LANES)),
                  const((1, D_A)), const((D_A, D_A))],
        out_specs=[act, st, st],
        out_shape=[jax.ShapeDtypeStruct(u_tm.shape, F32),
                   jax.ShapeDtypeStruct((bsz, N_S5), F32),
                   jax.ShapeDtypeStruct((bsz, N_S5), F32)],
        scratch_shapes=[small, small, pltpu.VMEM((N_S5_BLOCKS, LANES, S5_LANE_BLOCK), BF16),
                        pltpu.VMEM((N_S5_BLOCKS, LANES, S5_LANE_BLOCK), BF16), small, small, big, big,
                        pltpu.VMEM((tc * SUBLANES, D_A), F32)],
        compiler_params=_cparams(("parallel", "arbitrary"), 48),
    )(u_tm, h0r, h0i, prm["lr"], prm["li"], prm["ls"], prm["bre"], prm["bim"], prm["cre"], prm["cim"],
      prm["d"], prm["wglu"])


def _gdn_kernel(qkv_ref, ba_ref, z_ref, conv0_ref, s0_ref, cw_ref, alog_ref, dtb_ref, nw_ref,
                o_ref, sfin_ref, convn_ref, xbuf, s_s, *, t, nb, seq_on_sublanes):
    if seq_on_sublanes:
        qkv_seq = jnp.swapaxes(qkv_ref[...], 0, 1)
        ba_seq = jnp.swapaxes(ba_ref[...], 0, 1)
        z_seq = jnp.swapaxes(z_ref[...], 0, 1)
        read_qkv = lambda e: qkv_seq[e]
        read_ba = lambda e: ba_seq[e]
        read_z = lambda e, h: z_seq[e][:, h * DV_B:(h + 1) * DV_B]
    else:
        read_qkv = lambda e: qkv_ref[:, e * D_QKV:(e + 1) * D_QKV]
        read_ba = lambda e: ba_ref[:, e * GATE_PAD:(e + 1) * GATE_PAD]
        read_z = lambda e, h: z_ref[:, e * D_B + h * DV_B:e * D_B + (h + 1) * DV_B]
    o_parts = {}
    pad = SUBLANES
    hist = CONV_W - 1

    @pl.when(pl.program_id(1) == 0)
    def _():
        for e in range(nb):
            xbuf[pad - hist:pad, e * D_QKV:(e + 1) * D_QKV] = conv0_ref[e]
        s_s[...] = s0_ref[...].reshape(nb * H_B, DK_B, DV_B)

    row, col = _iota2((t, t))
    causal = row >= col
    strict = row > col
    tri = causal.astype(BF16)
    lane = lax.broadcasted_iota(jnp.int32, (t, LANES), 1)
    nw = nw_ref[...]
    cw = cw_ref[...]
    ones_sq = jnp.ones((DK_B, DK_B), BF16)
    zeros_wide = jnp.zeros((t, DK_B + DV_B), F32)
    zeros_v = jnp.zeros((t, DV_B), F32)

    def seq_group(es):
        ys, betas, gcs = [], [], []
        for e in es:
            cols = slice(e * D_QKV, (e + 1) * D_QKV)
            x = read_qkv(e)
            xbuf[pad:pad + t, cols] = x
            acc = x * cw[hist:hist + 1, :]
            for j in range(hist):
                acc = acc + xbuf[pad - hist + j:pad - hist + j + t, cols] * cw[j:j + 1, :]
            last = xbuf[pad + t - hist:pad + t, cols]
            convn_ref[e] = last
            xbuf[pad - hist:pad, cols] = last
            ys.append(_silu(acc))
            ba = read_ba(e)
            betas.append(_sigmoid(ba))
            g = -jnp.exp(alog_ref[...]) * _softplus(ba + dtb_ref[...])
            gcs.append(_cumsum_rows(tri, g))
            yield

        chains = [(i, h) for i in range(len(es)) for h in range(H_B)]
        heads = range(len(chains))
        bcol = [betas[i][:, h:h + 1] for i, h in chains]
        gcol = [gcs[i][:, H_B + h:H_B + h + 1] for i, h in chains]
        y_of = [ys[i] for i, h in chains]
        h_of = [h for i, h in chains]
        decay = []
        gc_parts = [[p.astype(F32) for p in _split3(gc)] for gc in gcs]
        for i, hh in chains:
            p1, p2, p3 = [p[:, H_B + hh:H_B + hh + 1] for p in gc_parts[i]]
            dl = jnp.where(lane == 0, p1, jnp.where(lane == 1, p2, jnp.where(lane == 2, p3,
                                                                             jnp.where(lane < 6, 1.0, 0.0))))
            dr = jnp.where(lane < 3, 1.0, jnp.where(lane == 3, -p1, jnp.where(lane == 4, -p2,
                                                                            jnp.where(lane == 5, -p3, 0.0))))
            decay.append(jnp.exp(jnp.where(causal, _bdot_nt(dl, dr), -jnp.inf)))
        eg = [jnp.exp(gcol[h]) for h in heads]
        glast = [gcol[h][t - 1:t, :] for h in heads]
        yield
        q = [y_of[c][:, h_of[c] * DK_B:(h_of[c] + 1) * DK_B] for c in heads]
        k = [y_of[c][:, (H_B + h_of[c]) * DK_B:(H_B + h_of[c] + 1) * DK_B] for c in heads]
        v = [y_of[c][:, 2 * H_B * DK_B + h_of[c] * DV_B:2 * H_B * DK_B + (h_of[c] + 1) * DV_B] for c in heads]
        sq = _bdot(jnp.concatenate([x * x for x in q + k], axis=0), ones_sq)
        nq = len(q)
        q = [x * (lax.rsqrt(sq[i * t:(i + 1) * t] + NORM_EPS) * (DK_B ** -0.5)) for i, x in enumerate(q)]
        k = [x * lax.rsqrt(sq[(nq + i) * t:(nq + i + 1) * t] + NORM_EPS) for i, x in enumerate(k)]
        kb = [k[h] * bcol[h] for h in heads]
        vb = [v[h] * bcol[h] for h in heads]
        yield
        lm = [jnp.where(strict, _bdot_nt(kb[h], k[h]) * decay[h], 0.0) for h in heads]
        attn = [_bdot_nt(q[h], k[h]) * decay[h] for h in heads]
        s = [s_s[es[i] * H_B + h] for i, h in chains]
        qs = [_bdot(q[h] * eg[h], s[h]) for h in heads]
        yield
        pairs = [(c, c + 1) for c in range(0, len(chains), 2)]
        tinv = yield from _inv_identity_plus_wide_stages(
            [jnp.concatenate([lm[c0], lm[c1]], axis=1) for c0, c1 in pairs], t)
        kbg = [kb[h] * eg[h] for h in heads]
        uw = [_bdot(tinv[i], jnp.concatenate(
            [jnp.concatenate([vb[c0], kbg[c0], zeros_wide], axis=1),
             jnp.concatenate([zeros_wide, vb[c1], kbg[c1]], axis=1)], axis=0)) for i, (c0, c1) in enumerate(pairs)]
        u = [uw[c // 2][:, (c % 2) * (DK_B + DV_B):(c % 2) * (DK_B + DV_B) + DV_B] for c in heads]
        w = [uw[c // 2][:, (c % 2) * (DK_B + DV_B) + DV_B:(c % 2 + 1) * (DK_B + DV_B)] for c in heads]
        yield
        v_new = [u[h] - _bdot(w[h], s[h]) for h in heads]
        yield
        av = [_bdot(jnp.concatenate([attn[c0], attn[c1]], axis=1), jnp.concatenate(
            [jnp.concatenate([v_new[c0], zeros_v], axis=1),
             jnp.concatenate([zeros_v, v_new[c1]], axis=1)], axis=0)) for c0, c1 in pairs]
        o = [qs[c] + av[c // 2][:, (c % 2) * DV_B:(c % 2 + 1) * DV_B] for c in heads]
        upd = [_bdot_tn(k[h] * jnp.exp(glast[h] - gcol[h]), v_new[h]) for h in heads]
        yield
        for c, (i, h) in enumerate(chains):
            e = es[i]
            s_new = s[c] * jnp.exp(glast[c]) + upd[c]
            s_s[e * H_B + h] = s_new
            sfin_ref[e, h] = s_new
            o_parts[e, h] = _rmsnorm(o[c], nw) * _silu(read_z(e, h))
            if not seq_on_sublanes:
                o_ref[:, e * D_B + h * DV_B:e * D_B + (h + 1) * DV_B] = o_parts[e, h]

    _run_staggered([seq_group(list(range(nb)))], 1)
    if seq_on_sublanes:
        per_seq = [jnp.concatenate([o_parts[e, h] for h in range(H_B)], axis=1) for e in range(nb)]
        o_ref[...] = jnp.swapaxes(jnp.stack(per_seq, axis=0), 0, 1)


def _gdn(qkv_tm, ba_tm, z_tm, conv0, s0, prm, bsz, t, nb):
    seq = qkv_tm.shape[0]
    seq_on_sublanes = qkv_tm.ndim == 3
    grid = (bsz // nb, seq // t)
    const = lambda shape: pl.BlockSpec(shape, lambda b, i: (0,) * len(shape))
    if seq_on_sublanes:
        tm = lambda n: pl.BlockSpec((t, nb, n), lambda b, i: (i, b, 0))
        out_shape = (seq, bsz, D_B)
    else:
        tm = lambda n: pl.BlockSpec((t, nb * n), lambda b, i: (i, b))
        out_shape = (seq, bsz * D_B)
    state = pl.BlockSpec((nb, H_B, DK_B, DV_B), lambda b, i: (b, 0, 0, 0))
    conv = pl.BlockSpec((nb, CONV_W - 1, D_QKV), lambda b, i: (b, 0, 0))
    return pl.pallas_call(
        functools.partial(_gdn_kernel, t=t, nb=nb, seq_on_sublanes=seq_on_sublanes),
        grid=grid,
        in_specs=[tm(D_QKV), tm(GATE_PAD), tm(D_B), conv, state,
                  const((CONV_W, D_QKV)), const((1, GATE_PAD)), const((1, GATE_PAD)), const((1, DV_B))],
        out_specs=[tm(D_B), state, conv],
        out_shape=[jax.ShapeDtypeStruct(out_shape, F32),
                   jax.ShapeDtypeStruct((bsz, H_B, DK_B, DV_B), F32),
                   jax.ShapeDtypeStruct((bsz, CONV_W - 1, D_QKV), F32)],
        scratch_shapes=[pltpu.VMEM((t + SUBLANES, nb * D_QKV), F32),
                        pltpu.VMEM((nb * H_B, DK_B, DV_B), F32)],
        compiler_params=_cparams(("parallel", "arbitrary"), 48),
    )(qkv_tm, ba_tm, z_tm, conv0, s0, prm["conv_w"], prm["alog"], prm["dtb"], prm["norm_w"])


def _mix_ffn_kernel(*refs, n_mix, final_norm):
    x_ref = refs[0]
    mix_refs = refs[1:1 + n_mix]
    wout_refs = refs[1 + n_mix:1 + 2 * n_mix]
    nffn_ref, wg_ref, wu_ref, wd_ref = refs[1 + 2 * n_mix:5 + 2 * n_mix]
    rest = refs[5 + 2 * n_mix:]
    if final_norm:
        nfin_ref, out_ref = rest
    else:
        (out_ref,) = rest

    x1 = x_ref[...]
    for m_ref, w_ref in zip(mix_refs, wout_refs):
        x1 = x1 + jnp.dot(m_ref[...].astype(BF16), w_ref[...], preferred_element_type=F32)
    h = _rmsnorm(x1, nffn_ref[...]).astype(BF16)
    out_ref[...] = x1

    for c in range(N_FF_TILES):
        cols = slice(c * FF_TILE, (c + 1) * FF_TILE)
        gate = jnp.dot(h, wg_ref[:, cols], preferred_element_type=F32)
        up = jnp.dot(h, wu_ref[:, cols], preferred_element_type=F32)
        act = (_silu(gate) * up).astype(BF16)
        out_ref[...] += jnp.dot(act, wd_ref[cols, :], preferred_element_type=F32)
    if final_norm:
        out_ref[...] = _rmsnorm(out_ref[...], nfin_ref[...])


def _mix_ffn(x, x_bm, mixes, wouts, nffn, wg, wu, wd, nfin, bv, lv, out_bm):
    tl = min(FFN_TOKEN_TILE, lv)
    grid = (bv, lv // tl)
    const = lambda shape: pl.BlockSpec(shape, lambda b, i: (0,) * len(shape),
                                       pipeline_mode=pl.Buffered(1))
    tm = lambda n: pl.BlockSpec((tl, n), lambda b, i: (i, b))
    bm = pl.BlockSpec((None, tl, D_MODEL), lambda b, i: (b, i, 0))
    in_specs = [bm if x_bm else tm(D_MODEL)]
    in_specs += [tm(m.shape[1] // bv) for m in mixes]
    in_specs += [const(w.shape) for w in wouts]
    in_specs += [const((1, D_MODEL)), const(wg.shape), const(wu.shape), const(wd.shape)]
    args = [x, *mixes, *wouts, nffn, wg, wu, wd]
    if nfin is not None:
        in_specs.append(const((1, D_MODEL)))
        args.append(nfin)
    if out_bm:
        out_spec, out_shape = bm, jax.ShapeDtypeStruct((bv, lv, D_MODEL), F32)
    else:
        out_spec, out_shape = tm(D_MODEL), jax.ShapeDtypeStruct((lv, bv * D_MODEL), F32)
    return pl.pallas_call(
        functools.partial(_mix_ffn_kernel, n_mix=len(mixes), final_norm=nfin is not None),
        grid=grid,
        in_specs=in_specs,
        out_specs=out_spec,
        out_shape=out_shape,
        compiler_params=_cparams(("parallel", "parallel"), 56),
    )(*args)


def _rwkv_proj_kernel(x_ref, shift0_ref, nw_ref, maa_ref, wr_ref, wk_ref, wv_ref, w0_ref, w1_ref, w2_ref,
                      a0_ref, a1_ref, a2_ref, g1_ref, g2_ref,
                      r_out, w_out, k_out, v_out, a_out, g_out, shift_out, hbuf, *, shift):
    pad = -(-shift // SUBLANES) * SUBLANES

    @pl.when(pl.program_id(1) == 0)
    def _():
        hbuf[pad - shift:pad, :] = shift0_ref[...]

    h = _rmsnorm(x_ref[...], nw_ref[...])
    rows = h.shape[0]
    hbuf[pad:pad + rows, :] = h
    prev = hbuf[pad - shift:pad - shift + rows, :]
    last = h[rows - shift:]
    hbuf[pad - shift:pad, :] = last
    shift_out[...] = last
    xx = prev - h
    maa = maa_ref[...]
    xr, xw, xk, xv, xa, xg = [h + xx * maa[j:j + 1, :] for j in range(6)]
    r = _bdot(xr, wr_ref[...])
    k = _bdot(xk, wk_ref[...])
    v = _bdot(xv, wv_ref[...])
    u = w0_ref[...] + _bdot(jnp.tanh(_bdot(xw, w1_ref[...])), w2_ref[...])
    a = _sigmoid(a0_ref[...] + _bdot(_bdot(xa, a1_ref[...]), a2_ref[...]))
    g = _bdot(_sigmoid(_bdot(xg, g1_ref[...])), g2_ref[...])
    r_out[...] = r
    w_out[...] = -math.exp(-0.5) * _sigmoid(u)
    k_out[...] = k
    v_out[...] = v
    a_out[...] = a
    g_out[...] = g


def _rwkv_proj(x_tm, shift0, prm):
    bv, shift, _ = shift0.shape
    lv = x_tm.shape[0]
    tr = max(shift, min(RWKV_PROJ_TILE, lv))
    grid = (bv, lv // tr)
    const = lambda shape: pl.BlockSpec(shape, lambda b, i: (0,) * len(shape),
                                       pipeline_mode=pl.Buffered(1))
    act = pl.BlockSpec((tr, D_MODEL), lambda b, i: (i, b))
    carry = pl.BlockSpec((None, shift, D_MODEL), lambda b, i: (b, 0, 0))
    vec = const((1, D_MODEL))
    weights = [prm[n] for n in ("wr", "wk", "wv")]
    in_specs = [act, carry, vec, const((6, D_MODEL))]
    in_specs += [const(w.shape) for w in weights]
    in_specs += [vec, const(prm["w1"].shape), const(prm["w2"].shape),
                 vec, const(prm["a1"].shape), const(prm["a2"].shape),
                 const(prm["g1"].shape), const(prm["g2"].shape)]
    out_act = jax.ShapeDtypeStruct((lv, bv * D_MODEL), F32)
    pad = -(-shift // SUBLANES) * SUBLANES
    return pl.pallas_call(
        functools.partial(_rwkv_proj_kernel, shift=shift),
        grid=grid,
        in_specs=in_specs,
        out_specs=[act] * N_RWKV_ACTS + [carry],
        out_shape=[out_act] * N_RWKV_ACTS + [jax.ShapeDtypeStruct((bv, shift, D_MODEL), F32)],
        scratch_shapes=[pltpu.VMEM((pad + tr, D_MODEL), F32)],
        compiler_params=_cparams(("parallel", "arbitrary"), 56),
    )(x_tm, shift0, prm["norm_w"], prm["maa"], prm["wr"], prm["wk"], prm["wv"], prm["w0"], prm["w1"],
      prm["w2"], prm["a0"], prm["a1"], prm["a2"], prm["g1"], prm["g2"])


def _rwkv_scan_kernel(r_ref, w_ref, k_ref, v_ref, a_ref, g_ref, s0_ref, kkw_ref, kaw_ref, rk_ref, lnw_ref, lnb_ref,
                      y_ref, sfin_ref, s_s, *, t, nb, group, seq_on_sublanes):
    if seq_on_sublanes:
        act_refs = (r_ref, w_ref, k_ref, v_ref, a_ref, g_ref)
        per_seq = [jnp.swapaxes(ref[...], 0, 1) for ref in act_refs]

        def slab(ref, jj):
            e, j = divmod(jj, H_C // 2)
            which = [i for i, x in enumerate(act_refs) if x is ref][0]
            return per_seq[which][e][:, j * LANES:(j + 1) * LANES]
    else:
        slab = lambda ref, jj: ref[:, jj * LANES:(jj + 1) * LANES]
    y_parts = {}
    n = N_HEAD_C
    zero_blk = jnp.zeros((n, n), F32)

    pairs_per_seq = H_C // 2

    @pl.when(pl.program_id(1) == 0)
    def _():
        for jj in range(nb * pairs_per_seq):
            e, j = divmod(jj, pairs_per_seq)
            top = jnp.concatenate([s0_ref[e, 2 * j], zero_blk], axis=1)
            bot = jnp.concatenate([zero_blk, s0_ref[e, 2 * j + 1]], axis=1)
            s_s[jj] = jnp.concatenate([top, bot], axis=0)

    row, col = _iota2((t, t))
    tri = (row >= col).astype(BF16)
    roww, colw = _iota2((t, 2 * t))
    colw = jnp.where(colw >= t, colw - t, colw)
    incl_w = roww >= colw
    strict_w = roww > colw
    lane = lax.broadcasted_iota(jnp.int32, (t, LANES), 1)
    head0 = lane < n

    def by_head(x):
        return jnp.concatenate([jnp.where(head0, x, 0.0), jnp.where(head0, 0.0, x)], axis=0)

    prow, pcol = _iota2((LANES, LANES))
    same_head = (prow >> 6) == (pcol >> 6)
    ones_blk = same_head.astype(BF16)

    def head_sums(xs, dot):
        tot = dot(jnp.concatenate(xs, axis=0), ones_blk)
        return [tot[i * t:(i + 1) * t] for i in range(len(xs))]

    def pair_group(js):
        ps = range(len(js))
        sl = [slice(j * LANES, (j + 1) * LANES) for j in js]
        psl = [slice((j % pairs_per_seq) * LANES, (j % pairs_per_seq + 1) * LANES) for j in js]
        r = [slab(r_ref, j) for j in js]
        w = [slab(w_ref, j) for j in js]
        kraw = [slab(k_ref, j) for j in js]
        v = [slab(v_ref, j) for j in js]
        a = [slab(a_ref, j) for j in js]
        kkr = [kraw[p] * kkw_ref[:, psl[p]] for p in ps]
        k = [kraw[p] * (1.0 + (a[p] - 1.0) * kaw_ref[:, psl[p]]) for p in ps]
        kk = [x * lax.rsqrt(ss + NORM_EPS) for x, ss in zip(kkr, head_sums([x * x for x in kkr], _bdot))]
        bv = [kk[p] * a[p] for p in ps]
        yield
        gc = [_cumsum_rows(tri, x, parts=2) for x in w]
        glast = [x[t - 1:t, :] for x in gc]
        yield
        pinv = [jnp.exp(-x) for x in gc]
        at = [-kk[p] * jnp.exp(gc[p] - w[p]) for p in ps]
        bt = [bv[p] * pinv[p] for p in ps]
        kt = [k[p] * pinv[p] for p in ps]
        rt = [r[p] * jnp.exp(gc[p]) for p in ps]
        rem = [jnp.exp(glast[p] - gc[p]) for p in ps]
        yield
        lhs = [jnp.concatenate([at[p], rt[p]], axis=0) for p in ps]
        gm = [_bdot_nt(lhs[p], jnp.concatenate([by_head(bt[p]), by_head(kt[p])], axis=0)) for p in ps]
        yield
        s = [s_s[j] for j in js]
        ars = [_bdot_nt(lhs[p], s[p]) for p in ps]
        v2 = [by_head(x) for x in v]
        yield
        aab = [jnp.where(strict_w, gm[p][:t, :2 * t], 0.0) for p in ps]
        aak = [jnp.where(strict_w, gm[p][:t, 2 * t:], 0.0) for p in ps]
        rkm = [jnp.where(incl_w, gm[p][t:, 2 * t:], 0.0) for p in ps]
        rb = [jnp.where(incl_w, gm[p][t:, :2 * t], 0.0) for p in ps]
        yield
        kv = [_bdot(jnp.concatenate([aak[p], rkm[p]], axis=0), v2[p]) for p in ps]
        rhs = [ars[p][:t] + kv[p][:t] for p in ps]
        yield
        tinv = yield from _inv_identity_plus_wide_stages([-x for x in aab], t)
        u = [_bdot(tinv[p], by_head(rhs[p])) for p in ps]
        yield
        y = [ars[p][t:] + _bdot(rb[p], by_head(u[p])) + kv[p][t:] for p in ps]
        yield
        upd = [_bdot_tn(jnp.concatenate([u[p], v[p]], axis=0),
                        jnp.concatenate([bv[p] * rem[p], k[p] * rem[p]], axis=0)) for p in ps]
        for p, j in enumerate(js):
            s_s[j] = s[p] * jnp.exp(glast[p]) + jnp.where(same_head, upd[p], 0.0)
        yield
        mu = head_sums(y, _split_dot)
        d = [y[p] - mu[p] * (1.0 / n) for p in ps]
        yield
        var = head_sums([x * x for x in d], _bdot)
        yield
        bonus = head_sums([r[p] * k[p] * rk_ref[:, psl[p]] for p in ps], _split_dot)
        yield
        for p in ps:
            yn = d[p] * lax.rsqrt(var[p] * (1.0 / n) + RWKV_GN_EPS) * lnw_ref[:, psl[p]] + lnb_ref[:, psl[p]]
            y_parts[js[p]] = (yn + bonus[p] * v[p]) * slab(g_ref, js[p])
            if not seq_on_sublanes:
                y_ref[:, sl[p]] = y_parts[js[p]]

    groups = [list(range(first, first + group)) for first in range(0, nb * pairs_per_seq, group)]
    _run_staggered([pair_group(js) for js in groups], RWKV_STAGGER)
    if seq_on_sublanes:
        rows = [jnp.concatenate([y_parts[e * pairs_per_seq + j] for j in range(pairs_per_seq)], axis=1)
                for e in range(nb)]
        y_ref[...] = jnp.swapaxes(jnp.stack(rows, axis=0), 0, 1)

    for jj in range(nb * pairs_per_seq):
        e, j = divmod(jj, pairs_per_seq)
        sp = s_s[jj]
        sfin_ref[e, 2 * j] = sp[:n, :n]
        sfin_ref[e, 2 * j + 1] = sp[n:, n:]


def _rwkv_scan(acts, s0, prm, bsz, t, nb, group):
    seq = acts[0].shape[0]
    seq_on_sublanes = acts[0].ndim == 3
    grid = (bsz // nb, seq // t)
    const = lambda shape: pl.BlockSpec(shape, lambda b, i: (0,) * len(shape))
    if seq_on_sublanes:
        tm = pl.BlockSpec((t, nb, D_MODEL), lambda b, i: (i, b, 0))
    else:
        tm = pl.BlockSpec((t, nb * D_MODEL), lambda b, i: (i, b))
    st = pl.BlockSpec((nb, H_C, N_HEAD_C, N_HEAD_C), lambda b, i: (b, 0, 0, 0))
    vec = const((1, D_MODEL))
    return pl.pallas_call(
        functools.partial(_rwkv_scan_kernel, t=t, nb=nb, group=group, seq_on_sublanes=seq_on_sublanes),
        grid=grid,
        in_specs=[tm] * N_RWKV_ACTS + [st, vec, vec, vec, vec, vec],
        out_specs=[tm, st],
        out_shape=[jax.ShapeDtypeStruct(acts[0].shape, F32),
                   jax.ShapeDtypeStruct((bsz, H_C, N_HEAD_C, N_HEAD_C), F32)],
        scratch_shapes=[pltpu.VMEM((nb * H_C // 2, LANES, LANES), F32)],
        compiler_params=_cparams(("parallel", "arbitrary"), 48),
    )(*acts, s0, prm["k_k"], prm["k_a"], prm["r_k"], prm["ln_w"], prm["ln_b"])


def _block_diag_groups(w, rows_per_group, cols_per_group):
    g = w.shape[0]
    w = w.reshape(g // 8, 8, rows_per_group, cols_per_group)
    eye = jnp.eye(8, dtype=w.dtype)
    out = jnp.einsum("jgrc,gh->jgrhc", w, eye)
    return out.reshape(g // 8, 8 * rows_per_group, 8 * cols_per_group)


def _pad_to(w, axis, size):
    pad = [(0, 0)] * w.ndim
    pad[axis] = (0, size - w.shape[axis])
    return jnp.pad(w, pad)


def _row(v):
    return v.reshape(1, -1).astype(F32)


def _recurrence_tiling(seq):
    if seq >= GDN_CHUNK:
        return dict(s5_tc=128, gdn_t=GDN_CHUNK, gdn_nb=8, rwkv_t=RWKV_CHUNK, rwkv_nb=4,
                    rwkv_group=2 * RWKV_PAIR_GROUP)
    return dict(s5_tc=seq, gdn_t=seq, gdn_nb=2 * SUBLANES, rwkv_t=seq, rwkv_nb=2 * SUBLANES,
                rwkv_group=2 * RWKV_PAIR_GROUP)


def _trunk(x_bm, bsz, seq, s5_re0, s5_im0, gdn_s0, gdn_conv0, rw_s0, rw_shift0, p, out_bm):
    bv, lv, _ = x_bm.shape
    cfg = _recurrence_tiling(seq)
    u, qkv, z, ba = _ab_in(x_bm, p["norm_mix0"], p["ab_w_in"])
    view = (lambda a, n: a.reshape(seq, bsz, n)) if bv == 1 else (lambda a, n: a)
    y_a, hr, hi = _s5(view(u, D_A), s5_re0, s5_im0, p["s5"], cfg["s5_tc"])
    y_b, gdn_s, gdn_conv = _gdn(view(qkv, D_QKV), view(ba, GATE_PAD), view(z, D_B), gdn_conv0, gdn_s0,
                                p["gdn"], bsz, cfg["gdn_t"], cfg["gdn_nb"])
    x1 = _mix_ffn(x_bm, True, [y_a.reshape(lv, bv * D_A), y_b.reshape(lv, bv * D_B)],
                  [p["ab_w_out_a"], p["ab_w_out_b"]], p["norm_ffn0"], p["wg0"], p["wu0"], p["wd0"],
                  None, bv, lv, False)
    acts = _rwkv_proj(x1, rw_shift0.reshape(bv, bsz // bv, D_MODEL), p["rw"])
    shift = acts[N_RWKV_ACTS].reshape(bsz, D_MODEL)
    yg, rw_s = _rwkv_scan([view(a, D_MODEL) for a in acts[:N_RWKV_ACTS]], rw_s0, p["rw"], bsz,
                          cfg["rwkv_t"], cfg["rwkv_nb"], cfg["rwkv_group"])
    y = _mix_ffn(x1, False, [yg.reshape(lv, bv * D_MODEL)], [p["rw"]["wo"]], p["norm_ffn1"], p["wg1"],
                 p["wu1"], p["wd1"], p["norm_final"], bv, lv, out_bm)
    return y, hr, hi, gdn_s, gdn_conv, rw_s, shift


def kernel(x_prompt, x_sample, state_s5_re, state_s5_im, state_gdn, state_gdn_conv, state_rwkv, state_rwkv_shift, norm_mix, norm_ffn, norm_final, ffn_w_gate, ffn_w_up, ffn_w_down, ab_w_in, ab_w_out, s5_lambda_re, s5_lambda_im, s5_log_step, s5_B_re, s5_B_im, s5_C_re, s5_C_im, s5_D, s5_w_glu, gdn_conv_w, gdn_A_log, gdn_dt_bias, gdn_norm_w, rw_maa, rw_w_r, rw_w_k, rw_w_v, rw_w_o, rw_w0, rw_w1, rw_w2, rw_a0, rw_a1, rw_a2, rw_g1, rw_g2, rw_k_k, rw_k_a, rw_r_k, rw_ln_w, rw_ln_b):
    bsz_p, seq_p, _ = x_prompt.shape
    bsz_s, seq_s, _ = x_sample.shape

    w_in = ab_w_in[0]
    n_main = D_A + D_QKV
    w_in = jnp.concatenate([w_in[:, :n_main], w_in[:, n_main + 2 * H_B:], w_in[:, n_main:n_main + 2 * H_B]],
                           axis=1)
    w_in = _pad_to(w_in, 1, D_IN_PAD).astype(BF16)

    ff = lambda w, layer: w[layer].astype(BF16)

    gate_lanes = lambda v: _pad_to(jnp.concatenate([jnp.zeros((H_B,), F32), v.astype(F32)]), 0,
                                   GATE_PAD).reshape(1, GATE_PAD)
    p = dict(
        norm_mix0=_row(norm_mix[0]), norm_ffn0=_row(norm_ffn[0]), norm_ffn1=_row(norm_ffn[1]),
        norm_final=_row(norm_final), ab_w_in=w_in,
        ab_w_out_a=ab_w_out[0][:D_A].astype(BF16), ab_w_out_b=ab_w_out[0][D_A:].astype(BF16),
        wg0=ff(ffn_w_gate, 0), wu0=ff(ffn_w_up, 0),
        wd0=ff(ffn_w_down, 0),
        wg1=ff(ffn_w_gate, 1), wu1=ff(ffn_w_up, 1),
        wd1=ff(ffn_w_down, 1),
        s5=dict(
            lr=_row(s5_lambda_re[0]), li=_row(s5_lambda_im[0]),
            ls=_row(jnp.repeat(s5_log_step[0], P_STATE)),
            bre=_block_diag_groups(jnp.swapaxes(s5_B_re[0], 1, 2), S5_GROUP, P_STATE).astype(F32),
            bim=_block_diag_groups(jnp.swapaxes(s5_B_im[0], 1, 2), S5_GROUP, P_STATE).astype(F32),
            cre=_block_diag_groups(jnp.swapaxes(s5_C_re[0], 1, 2), P_STATE, S5_GROUP).astype(BF16),
            cim=_block_diag_groups(jnp.swapaxes(s5_C_im[0], 1, 2), P_STATE, S5_GROUP).astype(BF16),
            d=_row(s5_D[0]), wglu=s5_w_glu[0].astype(BF16)),
        gdn=dict(conv_w=gdn_conv_w[0].astype(F32), alog=gate_lanes(gdn_A_log[0]),
                 dtb=gate_lanes(gdn_dt_bias[0]), norm_w=_row(gdn_norm_w[0])),
        rw=dict(
            norm_w=_row(norm_mix[1]), maa=rw_maa[0].astype(F32),
            wr=rw_w_r[0].astype(BF16), wk=rw_w_k[0].astype(BF16), wv=rw_w_v[0].astype(BF16),
            wo=rw_w_o[0].astype(BF16),
            w0=_row(rw_w0[0]), w1=_pad_to(rw_w1[0], 1, LANES).astype(BF16),
            w2=_pad_to(rw_w2[0], 0, LANES).astype(BF16),
            a0=_row(rw_a0[0]), a1=_pad_to(rw_a1[0], 1, LANES).astype(BF16),
            a2=_pad_to(rw_a2[0], 0, LANES).astype(BF16),
            g1=_pad_to(rw_g1[0], 1, 2 * LANES).astype(BF16), g2=_pad_to(rw_g2[0], 0, 2 * LANES).astype(BF16),
            k_k=_row(rw_k_k[0]), k_a=_row(rw_k_a[0]), r_k=_row(rw_r_k[0]),
            ln_w=_row(rw_ln_w[0]), ln_b=_row(rw_ln_b[0])),
    )

    zeros = lambda *shape: jnp.zeros(shape, F32)
    yp, p_hr, p_hi, p_gdn, p_conv, p_rw, p_shift = _trunk(
        x_prompt, bsz_p, seq_p, zeros(bsz_p, N_S5), zeros(bsz_p, N_S5), zeros(bsz_p, H_B, DK_B, DV_B),
        zeros(bsz_p, CONV_W - 1, D_QKV), zeros(bsz_p, H_C, N_HEAD_C, N_HEAD_C), zeros(bsz_p, D_MODEL), p,
        True)

    xs_tm = jnp.transpose(x_sample, (1, 0, 2)).reshape(1, seq_s * bsz_s, D_MODEL)
    ys, s_hr, s_hi, s_gdn, s_conv, s_rw, s_shift = _trunk(
        xs_tm, bsz_s, seq_s, state_s5_re[0].reshape(bsz_s, N_S5), state_s5_im[0].reshape(bsz_s, N_S5),
        state_gdn[0], state_gdn_conv[0], state_rwkv[0], state_rwkv_shift[0], p, True)
    y_sample = jnp.transpose(ys.reshape(seq_s, bsz_s, D_MODEL), (1, 0, 2))

    s5_shape = lambda b: (1, b, G_A, P_STATE)
    return (yp, y_sample,
            p_hr.reshape(s5_shape(bsz_p)), p_hi.reshape(s5_shape(bsz_p)), p_gdn[None], p_conv[None],
            p_rw[None], p_shift[None],
            s_hr.reshape(s5_shape(bsz_s)), s_hi.reshape(s5_shape(bsz_s)), s_gdn[None], s_conv[None],
            s_rw[None], s_shift[None])
```
